```python
import math
import jax, jax.numpy as jnp
from jax import lax
import numpy as np

D_MODEL = 1024
BATCH = 8
SEQ = 2048
DEPTH = 4
DEC_BATCH = 1
DEC_SEQ = 16384
PAST_LEN = 128

N_Q_HEADS = 8
N_KV_HEADS = 2
Q_PER_KV = N_Q_HEADS // N_KV_HEADS
HEAD_DIM = D_MODEL // N_Q_HEADS
DIFF_DIM = HEAD_DIM // 2
D_FF = 2816
BLOCK_Q = 128
WINDOW = 128
GRID_W = 64
N_BUCKETS = 32
MAX_DISTANCE = 128
ROPE_THETA = 10000.0
EPS = 1e-6
N_SUBLAYERS = 3
N_BRANCH = 3
N_BIAS_HEADS = 2 * N_Q_HEADS
Q_W = N_Q_HEADS * HEAD_DIM
KV_W = N_KV_HEADS * HEAD_DIM
BRANCH_IN = Q_W + 2 * KV_W
W_IN_COLS = N_BRANCH * BRANCH_IN + N_BRANCH * D_MODEL
NEG = -1e30

kernel_name = "hybrid_axial_window_diff_encoder"

F32 = jnp.float32


def rms_norm(x, g):
    x32 = x.astype(F32)
    y = x32 * lax.rsqrt(jnp.mean(x32 * x32, axis=-1, keepdims=True) + EPS)
    return (y * g.astype(F32)).astype(x.dtype)


def t5_bucket(rel):
    half = N_BUCKETS // 2
    max_exact = half // 2
    ret = jnp.where(rel > 0, half, 0)
    n = jnp.abs(rel)
    large = max_exact + (jnp.log(jnp.maximum(n, 1).astype(F32) / max_exact)
                         / math.log(MAX_DISTANCE / max_exact) * (half - max_exact)).astype(jnp.int32)
    large = jnp.minimum(large, half - 1)
    return ret + jnp.where(n < max_exact, n, large)


def axial_rope_tables(seq):
    rows = seq // GRID_W
    row = jnp.repeat(jnp.arange(rows), GRID_W).astype(F32)
    col = jnp.tile(jnp.arange(GRID_W), rows).astype(F32)
    nfreq = HEAD_DIM // 4
    inv = ROPE_THETA ** (-jnp.arange(nfreq, dtype=F32) / nfreq)
    ang_r = row[:, None] * inv
    ang_c = col[:, None] * inv
    ang = jnp.concatenate([ang_r, ang_r, ang_c, ang_c], axis=-1)
    return jnp.cos(ang), jnp.sin(ang)


def rotate_half(u):
    u1, u2 = jnp.split(u, 2, axis=-1)
    return jnp.concatenate([-u2, u1], axis=-1)


def apply_axial_rope(x, cos, sin):
    x32 = x.astype(F32)
    xr, xc = jnp.split(x32, 2, axis=-1)
    rot = jnp.concatenate([rotate_half(xr), rotate_half(xc)], axis=-1)
    return (x32 * cos[None, :, None, :] + rot * sin[None, :, None, :]).astype(x.dtype)


def to_blocks(q):
    b, s = q.shape[:2]
    return q.reshape(b, s // BLOCK_Q, BLOCK_Q, N_KV_HEADS, Q_PER_KV, -1).transpose(1, 0, 2, 3, 4, 5)


def from_blocks(o):
    nb, b = o.shape[:2]
    return o.transpose(1, 0, 2, 3, 4, 5).reshape(b, nb * BLOCK_Q, -1)


def axial_rope_attention(q, k, v, cos, sin):
    q = apply_axial_rope(q, cos, sin)
    k = apply_axial_rope(k, cos, sin)
    scale = HEAD_DIM ** -0.5

    def block(qb):
        sc = jnp.einsum('bqgrd,bkgd->bgrqk', qb, k).astype(F32) * scale
        p = jax.nn.softmax(sc, axis=-1).astype(v.dtype)
        return jnp.einsum('bgrqk,bkgd->bqgrd', p, v)

    return from_blocks(lax.map(block, to_blocks(q)))


def sink_window_attention(q, k, v, sink, bias_table):
    b, s = q.shape[:2]
    nb = s // BLOCK_Q
    pad = ((0, 0), (BLOCK_Q, BLOCK_Q), (0, 0), (0, 0))

    def band(t):
        tp = jnp.pad(t, pad).reshape(b, nb + 2, BLOCK_Q, N_KV_HEADS, HEAD_DIM)
        return jnp.concatenate([tp[:, :-2], tp[:, 1:-1], tp[:, 2:]], axis=2)

    kb, vb = band(k), band(v)
    qb = q.reshape(b, nb, BLOCK_Q, N_KV_HEADS, Q_PER_KV, HEAD_DIM)
    i = jnp.arange(BLOCK_Q)
    j = jnp.arange(3 * BLOCK_Q)
    rel = j[None, :] - BLOCK_Q - i[:, None]
    kpos = jnp.arange(nb)[:, None] * BLOCK_Q - BLOCK_Q + j[None, :]
    allowed = (jnp.abs(rel) <= WINDOW)[None] & ((kpos >= 0) & (kpos < s))[:, None, :]
    bias = bias_table[:, :N_Q_HEADS][t5_bucket(rel)]
    bias = bias.transpose(2, 0, 1).reshape(N_KV_HEADS, Q_PER_KV, BLOCK_Q, 3 * BLOCK_Q).astype(F32)
    scale = HEAD_DIM ** -0.5
    sc = jnp.einsum('bnqgrd,bnkgd->bngrqk', qb, kb).astype(F32) * scale + bias
    sc = jnp.where(allowed[None, :, None, None], sc, NEG)
    sink_col = jnp.broadcast_to(sink.astype(F32).reshape(N_KV_HEADS, Q_PER_KV, 1, 1), sc.shape[:-1] + (1,))
    p = jax.nn.softmax(jnp.concatenate([sc, sink_col], axis=-1), axis=-1)[..., :-1].astype(v.dtype)
    o = jnp.einsum('bngrqk,bnkgd->bnqgrd', p, vb)
    return o.reshape(b, s, -1)


def diff_attention(q, k, v, lam, lam_init, g_subln, bias_table):
    b, s = q.shape[:2]
    nb = s // BLOCK_Q
    q1, q2 = jnp.split(q, 2, axis=-1)
    k1, k2 = jnp.split(k, 2, axis=-1)
    scale = DIFF_DIM ** -0.5
    keypos = jnp.arange(s)
    table_c = bias_table[:, N_Q_HEADS:]

    def block(args):
        idx, q1b, q2b = args
        qpos = idx * BLOCK_Q + jnp.arange(BLOCK_Q)
        rel = keypos[None, :] - qpos[:, None]
        bias = table_c[t5_bucket(rel)]
        bias = bias.transpose(2, 0, 1).reshape(N_KV_HEADS, Q_PER_KV, BLOCK_Q, s).astype(F32)
        p1 = jax.nn.softmax(jnp.einsum('bqgrd,bkgd->bgrqk', q1b, k1).astype(F32) * scale + bias, axis=-1)
        p2 = jax.nn.softmax(jnp.einsum('bqgrd,bkgd->bgrqk', q2b, k2).astype(F32) * scale + bias, axis=-1)
        w = (p1 - lam * p2).astype(v.dtype)
        return jnp.einsum('bgrqk,bkgd->bqgrd', w, v)

    o = from_blocks(lax.map(block, (jnp.arange(nb), to_blocks(q1), to_blocks(q2))))
    o = rms_norm(o.reshape(b, s, N_Q_HEADS, HEAD_DIM), g_subln) * (1.0 - lam_init)
    return o.reshape(b, s, -1)


def swiglu(h, w_in_ff, w_out_ff):
    gate, up = jnp.split(h @ w_in_ff, 2, axis=-1)
    return (jax.nn.silu(gate) * up) @ w_out_ff


def trunk(x, c, p):
    b, s, _ = x.shape
    cos, sin = axial_rope_tables(s)
    split_idx = [int(v) for v in np.cumsum([Q_W, KV_W, KV_W] * N_BRANCH)]
    for l in range(DEPTH):
        mod = (jax.nn.silu(c) @ p['w_ada'][l] + p['b_ada'][l]).reshape(b, 3 * N_SUBLAYERS, 1, D_MODEL)

        def modulate(h, jj):
            return rms_norm(h, p['g_norm'][l, jj]) * (1.0 + mod[:, 3 * jj + 1]) + mod[:, 3 * jj]

        x = x + 0.5 * mod[:, 2] * swiglu(modulate(x, 0), p['w_ff_in'][l, 0], p['w_ff_out'][l, 0])

        n = modulate(x, 1)
        qa, ka, va, qb, kb, vb, qc, kc, vc, gates = jnp.split(n @ p['w_in'][l], split_idx, axis=-1)
        hq = (b, s, N_Q_HEADS, HEAD_DIM)
        hk = (b, s, N_KV_HEADS, HEAD_DIM)
        qa = rms_norm(qa.reshape(hq), p['g_qa'][l])
        ka = rms_norm(ka.reshape(hk), p['g_ka'][l])
        out_a = axial_rope_attention(qa, ka, va.reshape(hk), cos, sin)

        qb = rms_norm(qb.reshape(hq), p['g_qb'][l])
        kb = rms_norm(kb.reshape(hk), p['g_kb'][l])
        out_b = sink_window_attention(qb, kb, vb.reshape(hk), p['sink'][l], p['rel_bias'])

        qc = rms_norm(qc.reshape(b, s, N_Q_HEADS, 2, DIFF_DIM), p['g_qc'][l]).reshape(hq)
        kc = rms_norm(kc.reshape(b, s, N_KV_HEADS, 2, DIFF_DIM), p['g_kc'][l]).reshape(hk)
        lam_init = 0.8 - 0.6 * math.exp(-0.3 * l)
        lam = (jnp.exp(jnp.sum(p['lam_q1'][l].astype(F32) * p['lam_k1'][l].astype(F32)))
               - jnp.exp(jnp.sum(p['lam_q2'][l].astype(F32) * p['lam_k2'][l].astype(F32))) + lam_init)
        out_c = diff_attention(qc, kc, vc.reshape(hk), lam, lam_init, p['g_subln'][l], p['rel_bias'])

        g_a, g_b, g_c = jnp.split(jax.nn.sigmoid(gates), N_BRANCH, axis=-1)
        merged = g_a * out_a + g_b * out_b + g_c * out_c
        x = x + mod[:, 5] * (merged @ p['w_o'][l])

        x = x + 0.5 * mod[:, 8] * swiglu(modulate(x, 2), p['w_ff_in'][l, 1], p['w_ff_out'][l, 1])
    return x


def setup_inputs(seed: int = 0) -> dict:
    key = jax.random.key(seed)
    ks = jax.random.split(key, 24)

    def nrm(k, shape, scale):
        return jax.random.normal(k, shape, F32) * scale

    return {
        'x_prompt': nrm(ks[0], (BATCH, SEQ, D_MODEL), 1.0),
        'x_sample': nrm(ks[1], (DEC_BATCH, DEC_SEQ, D_MODEL), 1.0),
        'c_prompt': nrm(ks[2], (BATCH, D_MODEL), 1.0),
        'c_sample': nrm(ks[3], (DEC_BATCH, D_MODEL), 1.0),
        'w_ada': nrm(ks[4], (DEPTH, D_MODEL, 3 * N_SUBLAYERS * D_MODEL), 0.5 * D_MODEL ** -0.5),
        'b_ada': nrm(ks[5], (DEPTH, 3 * N_SUBLAYERS * D_MODEL), 0.02),
        'g_norm': 1.0 + nrm(ks[6], (DEPTH, N_SUBLAYERS, D_MODEL), 0.02),
        'w_ff_in': nrm(ks[7], (DEPTH, 2, D_MODEL, 2 * D_FF), D_MODEL ** -0.5),
        'w_ff_out': nrm(ks[8], (DEPTH, 2, D_FF, D_MODEL), D_FF ** -0.5),
        'w_in': nrm(ks[9], (DEPTH, D_MODEL, W_IN_COLS), D_MODEL ** -0.5),
        'w_o': nrm(ks[10], (DEPTH, D_MODEL, D_MODEL), D_MODEL ** -0.5),
        'g_qa': 1.0 + nrm(ks[11], (DEPTH, HEAD_DIM), 0.02),
        'g_ka': 1.0 + nrm(ks[12], (DEPTH, HEAD_DIM), 0.02),
        'g_qb': 1.0 + nrm(ks[13], (DEPTH, HEAD_DIM), 0.02),
        'g_kb': 1.0 + nrm(ks[14], (DEPTH, HEAD_DIM), 0.02),
        'g_qc': 1.0 + nrm(ks[15], (DEPTH, DIFF_DIM), 0.02),
        'g_kc': 1.0 + nrm(ks[16], (DEPTH, DIFF_DIM), 0.02),
        'sink': nrm(ks[17], (DEPTH, N_Q_HEADS), 0.5),
        'lam_q1': nrm(ks[18], (DEPTH, DIFF_DIM), 0.1),
        'lam_k1': nrm(ks[19], (DEPTH, DIFF_DIM), 0.1),
        'lam_q2': nrm(ks[20], (DEPTH, DIFF_DIM), 0.1),
        'lam_k2': nrm(ks[21], (DEPTH, DIFF_DIM), 0.1),
        'g_subln': 1.0 + nrm(ks[22], (DEPTH, HEAD_DIM), 0.02),
        'rel_bias': nrm(ks[23], (N_BUCKETS, N_BIAS_HEADS), 0.5),
    }


def reference(x_prompt, x_sample, c_prompt, c_sample, w_ada, b_ada, g_norm, w_ff_in, w_ff_out,
              w_in, w_o, g_qa, g_ka, g_qb, g_kb, g_qc, g_kc, sink, lam_q1, lam_k1, lam_q2, lam_k2,
              g_subln, rel_bias):
    params = dict(w_ada=w_ada, b_ada=b_ada, g_norm=g_norm, w_ff_in=w_ff_in, w_ff_out=w_ff_out,
                  w_in=w_in, w_o=w_o, g_qa=g_qa, g_ka=g_ka, g_qb=g_qb, g_kb=g_kb, g_qc=g_qc,
                  g_kc=g_kc, sink=sink, lam_q1=lam_q1, lam_k1=lam_k1, lam_q2=lam_q2, lam_k2=lam_k2,
                  g_subln=g_subln, rel_bias=rel_bias)
    y_prompt = trunk(x_prompt, c_prompt, params)
    y_sample = trunk(x_sample, c_sample, params)
    return (y_prompt, y_sample)
```

```python
import functools
import math

import numpy as np
import jax
import jax.numpy as jnp
from jax import lax
from jax.experimental import pallas as pl
from jax.experimental.pallas import tpu as pltpu

F32 = jnp.float32
BF16 = jnp.bfloat16

D_MODEL = 1024
DEPTH = 4
N_Q_HEADS = 8
N_KV_HEADS = 2
Q_PER_KV = N_Q_HEADS // N_KV_HEADS
HEAD_DIM = 128
DIFF_DIM = 64
D_FF = 2816
BLOCK_Q = 128
WINDOW = 128
GRID_W = 64
N_BUCKETS = 32
MAX_DISTANCE = 128
ROPE_THETA = 10000.0
EPS = 1e-6
NEG = -1e30
LOG2E = 1.4426950408889634

Q_W = N_Q_HEADS * HEAD_DIM
KV_W = N_KV_HEADS * HEAD_DIM
BRANCH_IN = Q_W + 2 * KV_W
QKV_W = 3 * BRANCH_IN
FF_CHUNK = 256
N_FF_CHUNKS = D_FF // FF_CHUNK
ADA_ROWS = 16
ADA_TN = 1536
NEAR = 3 * BLOCK_Q

VMEM_LIMIT = 56 * 1024 * 1024


def _tiles(seq):
    return dict(
        tm=min(512, seq),
        tq=BLOCK_Q,
        tk_a=min(2048, seq),
        tk_c=min(1024, seq),
    )


def _cparams(*sem):
    return pltpu.CompilerParams(dimension_semantics=sem, vmem_limit_bytes=VMEM_LIMIT)


def _dot(a, b):
    return jnp.dot(a, b, preferred_element_type=F32)


def _dot_nt(a, b):
    return lax.dot_general(a, b, (((1,), (1,)), ((), ())), preferred_element_type=F32)


def _ada_kernel(c_ref, w_ref, b_ref, o_ref):
    c = c_ref[...]
    a = (c * jax.nn.sigmoid(c)).astype(BF16)
    o_ref[0] = _dot(a, w_ref[0].astype(BF16)) + b_ref[0]


def _ada(c_all, w_ada, b_ada):
    n_out = w_ada.shape[-1]
    return pl.pallas_call(
        _ada_kernel,
        out_shape=jax.ShapeDtypeStruct((DEPTH, ADA_ROWS, n_out), F32),
        grid=(DEPTH, n_out // ADA_TN),
        in_specs=[
            pl.BlockSpec((ADA_ROWS, D_MODEL), lambda l, j: (0, 0)),
            pl.BlockSpec((1, D_MODEL, ADA_TN), lambda l, j: (l, 0, j)),
            pl.BlockSpec((1, 1, ADA_TN), lambda l, j: (l, 0, j)),
        ],
        out_specs=pl.BlockSpec((1, ADA_ROWS, ADA_TN), lambda l, j: (l, 0, j)),
        compiler_params=_cparams("arbitrary", "arbitrary"),
        name="ada",
    )(c_all, w_ada, b_ada.reshape(DEPTH, 1, n_out))


def _modulate(x, mod_ref, g, jj):
    ms = jnp.mean(x * x, axis=-1, keepdims=True)
    y = x * lax.rsqrt(ms + EPS) * g
    return y * (1.0 + mod_ref[0, 3 * jj + 1:3 * jj + 2, :]) + mod_ref[0, 3 * jj:3 * jj + 1, :]


def _ffn_kernel(x_ref, mod_ref, g_ref, wgu_ref, wout_ref, o_ref, nb_ref, acc_ref, *, jj):
    x = x_ref[...]
    nb_ref[...] = _modulate(x, mod_ref, g_ref[jj:jj + 1, :], jj).astype(BF16)
    acc_ref[...] = jnp.zeros_like(acc_ref)

    def body(c, carry):
        h = _dot(nb_ref[...], wgu_ref[c])
        hg = h[:, :FF_CHUNK]
        a = (hg * jax.nn.sigmoid(hg)) * h[:, FF_CHUNK:]
        acc_ref[...] += _dot(a.astype(BF16), wout_ref[c])
        return carry

    lax.fori_loop(0, N_FF_CHUNKS, body, 0)
    o_ref[...] = x + (0.5 * mod_ref[0, 3 * jj + 2:3 * jj + 3, :]) * acc_ref[...]


def _ffn(x, mod, g_norm, wgu, wout, *, layer, which, seq, tm):
    n = x.shape[0]
    jj = 2 * which
    const = dict(pipeline_mode=pl.Buffered(1))
    return pl.pallas_call(
        functools.partial(_ffn_kernel, jj=jj),
        out_shape=jax.ShapeDtypeStruct((n, D_MODEL), F32),
        grid=(n // tm,),
        in_specs=[
            pl.BlockSpec((tm, D_MODEL), lambda i: (i, 0)),
            pl.BlockSpec((None, 1, 9, D_MODEL), lambda i: (layer, (i * tm) // seq, 0, 0)),
            pl.BlockSpec((None, 3, D_MODEL), lambda i: (layer, 0, 0)),
            pl.BlockSpec((None, None, N_FF_CHUNKS, D_MODEL, 2 * FF_CHUNK),
                         lambda i: (layer, which, 0, 0, 0), **const),
            pl.BlockSpec((None, None, N_FF_CHUNKS, FF_CHUNK, D_MODEL),
                         lambda i: (layer, which, 0, 0, 0), **const),
        ],
        out_specs=pl.BlockSpec((tm, D_MODEL), lambda i: (i, 0)),
        scratch_shapes=[pltpu.VMEM((tm, D_MODEL), BF16), pltpu.VMEM((tm, D_MODEL), F32)],
        input_output_aliases={0: 0},
        compiler_params=_cparams("arbitrary"),
        name=f"ffn{which}",
    )(x, mod, g_norm, wgu, wout)


def _head_norm(r, g):
    ms = jnp.mean(r * r, axis=-1, keepdims=True)
    return r * lax.rsqrt(ms + EPS) * g


def _half_norm(r, g2):
    sq = r * r
    lo = lax.broadcasted_iota(jnp.int32, r.shape, 1) < DIFF_DIM
    s_lo = jnp.sum(jnp.where(lo, sq, 0.0), axis=-1, keepdims=True)
    s_hi = jnp.sum(jnp.where(lo, 0.0, sq), axis=-1, keepdims=True)
    ms = jnp.where(lo, s_lo, s_hi) * (1.0 / DIFF_DIM)
    return r * lax.rsqrt(ms + EPS) * g2


def _qkv_kernel(x_ref, mod_ref, g_ref, w_ref, cos_ref, sa_ref, sb_ref, gh_ref,
                qa_ref, ka_ref, va_ref, qb_ref, kb_ref, vb_ref, qc_ref, kc_ref, vc_ref, nb_ref):
    nb_ref[...] = _modulate(x_ref[...], mod_ref, g_ref[1:2, :], 1).astype(BF16)
    cos, sa, sb = cos_ref[...], sa_ref[...], sb_ref[...]

    def rope(r):
        return (r * cos + pltpu.roll(r, HEAD_DIM - 32, 1) * sa + pltpu.roll(r, 32, 1) * sb)

    def proj(col, width):
        return _dot(nb_ref[...], w_ref[:, col:col + width])

    def heads(r, n_heads, fn, out_ref):
        for h in range(n_heads):
            out_ref[h] = fn(r[:, h * HEAD_DIM:(h + 1) * HEAD_DIM]).astype(BF16)

    qa_scale = HEAD_DIM ** -0.5 * LOG2E
    qc_scale = DIFF_DIM ** -0.5 * LOG2E
    g_qa, g_ka, g_qb, g_kb = (gh_ref[i:i + 1, :] for i in range(4))
    g_qc, g_kc = gh_ref[4:5, :], gh_ref[5:6, :]
    ident = lambda r: r
    col = 0
    heads(proj(col, Q_W), N_Q_HEADS, lambda r: rope(_head_norm(r, g_qa)) * qa_scale, qa_ref)
    col += Q_W
    heads(proj(col, KV_W), N_KV_HEADS, lambda r: rope(_head_norm(r, g_ka)), ka_ref)
    col += KV_W
    heads(proj(col, KV_W), N_KV_HEADS, ident, va_ref)
    col += KV_W
    heads(proj(col, Q_W), N_Q_HEADS, lambda r: _head_norm(r, g_qb), qb_ref)
    col += Q_W
    heads(proj(col, KV_W), N_KV_HEADS, lambda r: _head_norm(r, g_kb), kb_ref)
    col += KV_W
    heads(proj(col, KV_W), N_KV_HEADS, ident, vb_ref)
    col += KV_W
    heads(proj(col, Q_W), N_Q_HEADS, lambda r: _half_norm(r, g_qc) * qc_scale, qc_ref)
    col += Q_W
    heads(proj(col, KV_W), N_KV_HEADS, lambda r: _half_norm(r, g_kc), kc_ref)
    col += KV_W
    heads(proj(col, KV_W), N_KV_HEADS, ident, vc_ref)


def _qkv(x, mod, g_norm, wqkv, rope_tabs, gh, *, layer, seq, tm):
    n = x.shape[0]
    tiles_per_seq = seq // tm
    q_shape = jax.ShapeDtypeStruct((N_Q_HEADS, n, HEAD_DIM), BF16)
    kv_shape = jax.ShapeDtypeStruct((N_KV_HEADS, n, HEAD_DIM), BF16)
    q_spec = pl.BlockSpec((N_Q_HEADS, tm, HEAD_DIM), lambda i: (0, i, 0))
    kv_spec = pl.BlockSpec((N_KV_HEADS, tm, HEAD_DIM), lambda i: (0, i, 0))
    tab_spec = pl.BlockSpec((tm, HEAD_DIM), lambda i: (i % tiles_per_seq, 0))
    return pl.pallas_call(
        _qkv_kernel,
        out_shape=[q_shape, kv_shape, kv_shape] * 3,
        grid=(n // tm,),
        in_specs=[
            pl.BlockSpec((tm, D_MODEL), lambda i: (i, 0)),
            pl.BlockSpec((None, 1, 9, D_MODEL), lambda i: (layer, (i * tm) // seq, 0, 0)),
            pl.BlockSpec((None, 3, D_MODEL), lambda i: (layer, 0, 0)),
            pl.BlockSpec((None, D_MODEL, QKV_W), lambda i: (layer, 0, 0), pipeline_mode=pl.Buffered(1)),
            tab_spec, tab_spec, tab_spec,
            pl.BlockSpec((None, 8, HEAD_DIM), lambda i: (layer, 0, 0)),
        ],
        out_specs=[q_spec, kv_spec, kv_spec] * 3,
        scratch_shapes=[pltpu.VMEM((tm, D_MODEL), BF16)],
        compiler_params=_cparams("arbitrary"),
        name="qkv",
    )(x, mod, g_norm, wqkv, *rope_tabs, gh)


def _out_kernel(x_ref, mod_ref, g_ref, oa_ref, ob_ref, oc_ref, wg_ref, wo_ref, o_ref, mg_ref):
    x = x_ref[...]
    nb = _modulate(x, mod_ref, g_ref[1:2, :], 1).astype(BF16)
    for br, br_ref in enumerate((oa_ref, ob_ref, oc_ref)):
        gate = jax.nn.sigmoid(_dot(nb, wg_ref[:, br * D_MODEL:(br + 1) * D_MODEL]))
        for h in range(N_Q_HEADS):
            lanes = slice(h * HEAD_DIM, (h + 1) * HEAD_DIM)
            term = gate[:, lanes] * br_ref[h]
            if br == 0:
                mg_ref[:, lanes] = term
            else:
                mg_ref[:, lanes] += term
    o_ref[...] = x + mod_ref[0, 5:6, :] * _dot(mg_ref[...].astype(BF16), wo_ref[...])


def _out_proj(x, mod, g_norm, oa, ob, oc, wgate, wo, *, layer, seq, tm):
    n = x.shape[0]
    o_spec = pl.BlockSpec((N_Q_HEADS, tm, HEAD_DIM), lambda i: (0, i, 0))
    return pl.pallas_call(
        _out_kernel,
        out_shape=jax.ShapeDtypeStruct((n, D_MODEL), F32),
        grid=(n // tm,),
        in_specs=[
            pl.BlockSpec((tm, D_MODEL), lambda i: (i, 0)),
            pl.BlockSpec((None, 1, 9, D_MODEL), lambda i: (layer, (i * tm) // seq, 0, 0)),
            pl.BlockSpec((None, 3, D_MODEL), lambda i: (layer, 0, 0)),
            o_spec, o_spec, o_spec,
            pl.BlockSpec((None, D_MODEL, 3 * D_MODEL), lambda i: (layer, 0, 0), pipeline_mode=pl.Buffered(1)),
            pl.BlockSpec((None, D_MODEL, D_MODEL), lambda i: (layer, 0, 0), pipeline_mode=pl.Buffered(1)),
        ],
        out_specs=pl.BlockSpec((tm, D_MODEL), lambda i: (i, 0)),
        scratch_shapes=[pltpu.VMEM((tm, D_MODEL), F32)],
        input_output_aliases={0: 0},
        compiler_params=_cparams("arbitrary"),
        name="out_proj",
    )(x, mod, g_norm, oa, ob, oc, wgate, wo)


def _softmax_step(s, v, m_ref, l_ref, acc_ref):
    m_prev = m_ref[...]
    m_new = jnp.maximum(m_prev, jnp.max(s, axis=-1, keepdims=True))
    alpha = jnp.exp2(m_prev - m_new)
    p = jnp.exp2(s - m_new)
    l_ref[...] = alpha * l_ref[...] + jnp.sum(p, axis=-1, keepdims=True)
    acc_ref[...] = alpha * acc_ref[...] + _dot(p.astype(BF16), v)
    m_ref[...] = m_new


def _attn_a_kernel(q_ref, k_ref, v_ref, o_ref, m_ref, l_ref, acc_ref, *, nk):
    j = pl.program_id(3)

    @pl.when(j == 0)
    def _():
        m_ref[...] = jnp.full_like(m_ref, -jnp.inf)
        l_ref[...] = jnp.zeros_like(l_ref)
        acc_ref[...] = jnp.zeros_like(acc_ref)

    rows = acc_ref.shape[0]
    q = q_ref[...].reshape(rows, HEAD_DIM)
    _softmax_step(_dot_nt(q, k_ref[...]), v_ref[...], m_ref, l_ref, acc_ref)

    @pl.when(j == nk - 1)
    def _():
        o_ref[...] = (acc_ref[...] / l_ref[...]).reshape(o_ref.shape)


def _attn_a(q, k, v, *, batch, seq, tq, tk):
    n = q.shape[1]
    nq, nk = seq // tq, seq // tk
    rows = Q_PER_KV * tq
    q_map = lambda b, g, i, j: (g, b * nq + i, 0)
    kv_map = lambda b, g, i, j: (g, b * nk + j, 0)
    return pl.pallas_call(
        functools.partial(_attn_a_kernel, nk=nk),
        out_shape=jax.ShapeDtypeStruct((N_Q_HEADS, n, HEAD_DIM), F32),
        grid=(batch, N_KV_HEADS, nq, nk),
        in_specs=[
            pl.BlockSpec((Q_PER_KV, tq, HEAD_DIM), q_map),
            pl.BlockSpec((None, tk, HEAD_DIM), kv_map),
            pl.BlockSpec((None, tk, HEAD_DIM), kv_map),
        ],
        out_specs=pl.BlockSpec((Q_PER_KV, tq, HEAD_DIM), q_map),
        scratch_shapes=[pltpu.VMEM((rows, 1), F32), pltpu.VMEM((rows, 1), F32),
                        pltpu.VMEM((rows, HEAD_DIM), F32)],
        compiler_params=_cparams("arbitrary", "arbitrary", "arbitrary", "arbitrary"),
        name="attn_axial",
    )(q, k, v)


def _attn_b_kernel(q_ref, kp_ref, kc_ref, kn_ref, vp_ref, vc_ref, vn_ref, bias_ref, sink_ref, o_ref, *, nb):
    i = pl.program_id(2)
    rows = Q_PER_KV * BLOCK_Q
    q = q_ref[...].reshape(rows, HEAD_DIM)
    kcat = jnp.concatenate([kp_ref[...], kc_ref[...], kn_ref[...]], axis=0)
    vcat = jnp.concatenate([vp_ref[...], vc_ref[...], vn_ref[...]], axis=0)
    s = _dot_nt(q, kcat) * (HEAD_DIM ** -0.5)
    s = s.reshape(Q_PER_KV, BLOCK_Q, NEAR) + bias_ref[...]
    r = lax.broadcasted_iota(jnp.int32, (BLOCK_Q, NEAR), 0)
    c = lax.broadcasted_iota(jnp.int32, (BLOCK_Q, NEAR), 1)
    kpos = (i - 1) * BLOCK_Q + c
    allowed = (jnp.abs(c - BLOCK_Q - r) <= WINDOW) & (kpos >= 0) & (kpos < nb * BLOCK_Q)
    s = jnp.where(allowed[None], s, NEG)
    sink = sink_ref[...][:, :, :1]
    m = jnp.maximum(jnp.max(s, axis=-1, keepdims=True), sink)
    e = jnp.exp(s - m)
    den = jnp.sum(e, axis=-1, keepdims=True) + jnp.exp(sink - m)
    p = (e * (1.0 / den)).reshape(rows, NEAR).astype(BF16)
    o_ref[...] = _dot(p, vcat).reshape(o_ref.shape)


def _attn_b(q, k, v, bias, sink, *, layer, batch, seq):
    n = q.shape[1]
    nb = seq // BLOCK_Q
    blk = (None, BLOCK_Q, HEAD_DIM)
    prev_map = lambda b, g, i: (g, b * nb + jnp.maximum(i - 1, 0), 0)
    cur_map = lambda b, g, i: (g, b * nb + i, 0)
    next_map = lambda b, g, i: (g, b * nb + jnp.minimum(i + 1, nb - 1), 0)
    kv_specs = [pl.BlockSpec(blk, prev_map), pl.BlockSpec(blk, cur_map), pl.BlockSpec(blk, next_map)]
    return pl.pallas_call(
        functools.partial(_attn_b_kernel, nb=nb),
        out_shape=jax.ShapeDtypeStruct((N_Q_HEADS, n, HEAD_DIM), F32),
        grid=(batch, N_KV_HEADS, nb),
        in_specs=[pl.BlockSpec((Q_PER_KV, BLOCK_Q, HEAD_DIM), cur_map)] + kv_specs + kv_specs + [
            pl.BlockSpec((Q_PER_KV, BLOCK_Q, NEAR), lambda b, g, i: (g, 0, 0)),
            pl.BlockSpec((None, Q_PER_KV, 1, HEAD_DIM), lambda b, g, i: (layer, g, 0, 0)),
        ],
        out_specs=pl.BlockSpec((Q_PER_KV, BLOCK_Q, HEAD_DIM), cur_map),
        compiler_params=_cparams("arbitrary", "arbitrary", "arbitrary"),
        name="attn_window",
    )(q, k, k, k, v, v, v, bias, sink)


def _attn_c_kernel(q_ref, k_ref, v_ref, adj_ref, cfar_ref, lam_ref, gs_ref, o_ref,
                   s1_ref, s2_ref, m1_ref, l1_ref, a1_ref, m2_ref, l2_ref, a2_ref,
                   *, nk, nblk, tq, tk, out_scale):
    i = pl.program_id(2)
    j = pl.program_id(3)
    rows = Q_PER_KV * tq

    @pl.when(j == 0)
    def _():
        for m_ref, l_ref, a_ref in ((m1_ref, l1_ref, a1_ref), (m2_ref, l2_ref, a2_ref)):
            m_ref[...] = jnp.full_like(m_ref, -jnp.inf)
            l_ref[...] = jnp.zeros_like(l_ref)
            a_ref[...] = jnp.zeros_like(a_ref)

    q = q_ref[...].reshape(rows, HEAD_DIM)
    lo = lax.broadcasted_iota(jnp.int32, q.shape, 1) < DIFF_DIM
    zero = jnp.zeros_like(q)
    k = k_ref[...]
    kidx = j * tk + lax.broadcasted_iota(jnp.int32, (1, 1, tk), 2)
    cfar = jnp.where(kidx < i * tq, cfar_ref[0], cfar_ref[1])
    for s_ref, qm in ((s1_ref, jnp.where(lo, q, zero)), (s2_ref, jnp.where(lo, zero, q))):
        s_ref[...] = (_dot_nt(qm, k).reshape(Q_PER_KV, tq, tk) + cfar).reshape(rows, tk)

    for t in range(NEAR // BLOCK_Q):
        blk = i - 1 + t
        start = jnp.clip(blk, 0, nblk - 1) * BLOCK_Q

        @pl.when((blk >= 0) & (blk < nblk) & (start // tk == j))
        def _():
            off = pl.multiple_of(start % tk, BLOCK_Q)
            adj = adj_ref[:, :, t * BLOCK_Q:(t + 1) * BLOCK_Q].reshape(rows, BLOCK_Q)
            s1_ref[:, pl.ds(off, BLOCK_Q)] += adj
            s2_ref[:, pl.ds(off, BLOCK_Q)] += adj

    v = v_ref[...]
    _softmax_step(s1_ref[...], v, m1_ref, l1_ref, a1_ref)
    _softmax_step(s2_ref[...], v, m2_ref, l2_ref, a2_ref)

    @pl.when(j == nk - 1)
    def _():
        o = a1_ref[...] / l1_ref[...] - lam_ref[...] * (a2_ref[...] / l2_ref[...])
        ms = jnp.mean(o * o, axis=-1, keepdims=True)
        o = o * lax.rsqrt(ms + EPS) * gs_ref[...] * out_scale
        o_ref[...] = o.reshape(o_ref.shape)


def _attn_c(q, k, v, adj, cfar, lam, g_subln, *, layer, batch, seq, tq, tk):
    n = q.shape[1]
    nq, nk = seq // tq, seq // tk
    rows = Q_PER_KV * tq
    lam_init = 0.8 - 0.6 * math.exp(-0.3 * layer)
    q_map = lambda b, g, i, j: (g, b * nq + i, 0)
    kv_map = lambda b, g, i, j: (g, b * nk + j, 0)
    stat = pltpu.VMEM((rows, 1), F32)
    acc = pltpu.VMEM((rows, HEAD_DIM), F32)
    return pl.pallas_call(
        functools.partial(_attn_c_kernel, nk=nk, nblk=seq // BLOCK_Q, tq=tq, tk=tk, out_scale=1.0 - lam_init),
        out_shape=jax.ShapeDtypeStruct((N_Q_HEADS, n, HEAD_DIM), F32),
        grid=(batch, N_KV_HEADS, nq, nk),
        in_specs=[
            pl.BlockSpec((Q_PER_KV, tq, HEAD_DIM), q_map),
            pl.BlockSpec((None, tk, HEAD_DIM), kv_map),
            pl.BlockSpec((None, tk, HEAD_DIM), kv_map),
            pl.BlockSpec((Q_PER_KV, BLOCK_Q, NEAR), lambda b, g, i, j: (g, 0, 0)),
            pl.BlockSpec((2, Q_PER_KV, 1, 1), lambda b, g, i, j: (0, g, 0, 0)),
            pl.BlockSpec((None, 1, HEAD_DIM), lambda b, g, i, j: (layer, 0, 0)),
            pl.BlockSpec((None, 1, HEAD_DIM), lambda b, g, i, j: (layer, 0, 0)),
        ],
        out_specs=pl.BlockSpec((Q_PER_KV, tq, HEAD_DIM), q_map),
        scratch_shapes=[pltpu.VMEM((rows, tk), F32), pltpu.VMEM((rows, tk), F32),
                        stat, stat, acc, stat, stat, acc],
        compiler_params=_cparams("arbitrary", "arbitrary", "arbitrary", "arbitrary"),
        name="attn_diff",
    )(q, k, v, adj, cfar, lam, g_subln)


def _t5_bucket_np(rel):
    half = N_BUCKETS // 2
    max_exact = half // 2
    ret = np.where(rel > 0, half, 0)
    n = np.abs(rel)
    ratio = np.log(np.maximum(n, 1).astype(np.float32) / np.float32(max_exact)) / np.float32(
        math.log(MAX_DISTANCE / max_exact))
    large = max_exact + (ratio * np.float32(half - max_exact)).astype(np.int32)
    large = np.minimum(large, half - 1)
    return (ret + np.where(n < max_exact, n, large)).astype(np.int32)


def _near_bias(table):
    r = np.arange(BLOCK_Q)[:, None]
    c = np.arange(NEAR)[None, :]
    bucket = _t5_bucket_np(c - BLOCK_Q - r)
    return jnp.take(table, jnp.asarray(bucket.reshape(-1)), axis=0).reshape(
        BLOCK_Q, NEAR, table.shape[1]).transpose(2, 0, 1).astype(F32)


def _rope_tables(seq):
    rows = seq // GRID_W
    row = jnp.repeat(jnp.arange(rows), GRID_W).astype(F32)
    col = jnp.tile(jnp.arange(GRID_W), rows).astype(F32)
    nfreq = HEAD_DIM // 4
    inv = ROPE_THETA ** (-jnp.arange(nfreq, dtype=F32) / nfreq)
    ang_r = row[:, None] * inv
    ang_c = col[:, None] * inv
    ang = jnp.concatenate([ang_r, ang_r, ang_c, ang_c], axis=-1)
    cos, sin = jnp.cos(ang), jnp.sin(ang)
    first = (np.arange(HEAD_DIM) % (HEAD_DIM // 2)) < HEAD_DIM // 4
    return cos, jnp.where(first, -sin, 0.0), jnp.where(first, 0.0, sin)


def _trunk(x, mod, p, *, batch, seq):
    t = _tiles(seq)
    tm = t["tm"]
    rope_tabs = _rope_tables(seq)
    for l in range(DEPTH):
        x = _ffn(x, mod, p["g_norm"], p["wgu"], p["wout"], layer=l, which=0, seq=seq, tm=tm)
        qa, ka, va, qb, kb, vb, qc, kc, vc = _qkv(x, mod, p["g_norm"], p["wqkv"], rope_tabs, p["gh"],
                                                  layer=l, seq=seq, tm=tm)
        oa = _attn_a(qa, ka, va, batch=batch, seq=seq, tq=t["tq"], tk=t["tk_a"])
        ob = _attn_b(qb, kb, vb, p["bias_b"], p["sink"], layer=l, batch=batch, seq=seq)
        oc = _attn_c(qc, kc, vc, p["adj_c"], p["cfar_c"], p["lam"], p["g_subln"],
                     layer=l, batch=batch, seq=seq, tq=t["tq"], tk=t["tk_c"])
        x = _out_proj(x, mod, p["g_norm"], oa, ob, oc, p["wgate"], p["wo"], layer=l, seq=seq, tm=tm)
        x = _ffn(x, mod, p["g_norm"], p["wgu"], p["wout"], layer=l, which=1, seq=seq, tm=tm)
    return x


def _prepare(w_ff_in, w_ff_out, w_in, w_o, g_qa, g_ka, g_qb, g_kb, g_qc, g_kc, sink,
             lam_q1, lam_k1, lam_q2, lam_k2, g_subln, rel_bias):
    wg = w_ff_in[..., :D_FF].reshape(DEPTH, 2, D_MODEL, N_FF_CHUNKS, FF_CHUNK)
    wu = w_ff_in[..., D_FF:].reshape(DEPTH, 2, D_MODEL, N_FF_CHUNKS, FF_CHUNK)
    wgu = jnp.concatenate([wg, wu], axis=-1).transpose(0, 1, 3, 2, 4).astype(BF16)
    wout = w_ff_out.reshape(DEPTH, 2, N_FF_CHUNKS, FF_CHUNK, D_MODEL).astype(BF16)
    zeros = jnp.zeros_like(g_qa)
    gh = jnp.stack([g_qa, g_ka, g_qb, g_kb, jnp.tile(g_qc, (1, 2)), jnp.tile(g_kc, (1, 2)), zeros, zeros],
                   axis=1).astype(F32)
    table_b, table_c = rel_bias[:, :N_Q_HEADS], rel_bias[:, N_Q_HEADS:]
    half = N_BUCKETS // 2
    c_left, c_right = table_c[half - 1] * LOG2E, table_c[N_BUCKETS - 1] * LOG2E
    near_c = _near_bias(table_c) * LOG2E
    adj_c = jnp.concatenate([near_c[:, :, :BLOCK_Q] - c_left[:, None, None],
                             near_c[:, :, BLOCK_Q:] - c_right[:, None, None]], axis=-1)
    lam_init = jnp.asarray([0.8 - 0.6 * math.exp(-0.3 * l) for l in range(DEPTH)], F32)
    lam = (jnp.exp(jnp.sum(lam_q1.astype(F32) * lam_k1.astype(F32), axis=-1))
           - jnp.exp(jnp.sum(lam_q2.astype(F32) * lam_k2.astype(F32), axis=-1)) + lam_init)
    return dict(
        wgu=wgu, wout=wout,
        wqkv=w_in[:, :, :QKV_W].astype(BF16), wgate=w_in[:, :, QKV_W:].astype(BF16), wo=w_o.astype(BF16),
        gh=gh,
        bias_b=_near_bias(table_b),
        sink=jnp.broadcast_to(sink.astype(F32)[:, :, None, None], (DEPTH, N_Q_HEADS, 1, HEAD_DIM)),
        adj_c=adj_c,
        cfar_c=jnp.stack([c_left, c_right]).astype(F32)[:, :, None, None],
        lam=jnp.broadcast_to(lam[:, None, None], (DEPTH, 1, HEAD_DIM)),
        g_subln=g_subln.astype(F32)[:, None, :],
    )


def kernel(x_prompt, x_sample, c_prompt, c_sample, w_ada, b_ada, g_norm, w_ff_in, w_ff_out, w_in, w_o,
           g_qa, g_ka, g_qb, g_kb, g_qc, g_kc, sink, lam_q1, lam_k1, lam_q2, lam_k2, g_subln, rel_bias):
    p = _prepare(w_ff_in, w_ff_out, w_in, w_o, g_qa, g_ka, g_qb, g_kb, g_qc, g_kc, sink,
                 lam_q1, lam_k1, lam_q2, lam_k2, g_subln, rel_bias)
    p["g_norm"] = g_norm.astype(F32)
    outs = []
    n_cond = 0
    conds = [c_prompt, c_sample]
    c_all = jnp.concatenate(conds + [jnp.zeros((ADA_ROWS - sum(c.shape[0] for c in conds), D_MODEL), F32)])
    mod_all = _ada(c_all, w_ada, b_ada)
    for x, c in ((x_prompt, c_prompt), (x_sample, c_sample)):
        batch, seq, _ = x.shape
        mod = mod_all[:, n_cond:n_cond + batch].reshape(DEPTH, batch, 9, D_MODEL)
        n_cond += batch
        y = _trunk(x.reshape(batch * seq, D_MODEL), mod, p, batch=batch, seq=seq)
        outs.append(y.reshape(batch, seq, D_MODEL))
    return tuple(outs)
```

```python
import functools
import math

import numpy as np
import jax
import jax.numpy as jnp
from jax import lax
from jax.experimental import pallas as pl
from jax.experimental.pallas import tpu as pltpu

F32 = jnp.float32
BF16 = jnp.bfloat16

D_MODEL = 1024
DEPTH = 4
N_Q_HEADS = 8
N_KV_HEADS = 2
Q_PER_KV = N_Q_HEADS // N_KV_HEADS
HEAD_DIM = 128
DIFF_DIM = 64
D_FF = 2816
BLOCK_Q = 128
WINDOW = 128
GRID_W = 64
N_BUCKETS = 32
MAX_DISTANCE = 128
ROPE_THETA = 10000.0
EPS = 1e-6
NEG = -1e30
LOG2E = 1.4426950408889634

Q_W = N_Q_HEADS * HEAD_DIM
KV_W = N_KV_HEADS * HEAD_DIM
BRANCH_IN = Q_W + 2 * KV_W
QKV_W = 3 * BRANCH_IN
FF_CHUNK = 256
N_FF_CHUNKS = D_FF // FF_CHUNK
ADA_ROWS = 16
ADA_TN = 1536
NEAR = 3 * BLOCK_Q
VT_ROWS = HEAD_DIM + 16

VMEM_LIMIT = 56 * 1024 * 1024


def _tiles(seq):
    tks = min(512, seq)
    return dict(
        tm=min(512, seq),
        tq_a=BLOCK_Q,
        tks=tks,
        unroll=min(4, seq // tks),
    )


def _cparams(*sem):
    return pltpu.CompilerParams(dimension_semantics=sem, vmem_limit_bytes=VMEM_LIMIT)


def _dot(a, b):
    return jnp.dot(a, b, preferred_element_type=F32)


def _dot_nt(a, b):
    return lax.dot_general(a, b, (((1,), (1,)), ((), ())), preferred_element_type=F32)


def _ada_kernel(c_ref, w_ref, b_ref, o_ref):
    c = c_ref[...]
    a = (c * jax.nn.sigmoid(c)).astype(BF16)
    o_ref[0] = _dot(a, w_ref[0].astype(BF16)) + b_ref[0]


def _ada(c_all, w_ada, b_ada):
    n_out = w_ada.shape[-1]
    return pl.pallas_call(
        _ada_kernel,
        out_shape=jax.ShapeDtypeStruct((DEPTH, ADA_ROWS, n_out), F32),
        grid=(DEPTH, n_out // ADA_TN),
        in_specs=[
            pl.BlockSpec((ADA_ROWS, D_MODEL), lambda l, j: (0, 0)),
            pl.BlockSpec((1, D_MODEL, ADA_TN), lambda l, j: (l, 0, j)),
            pl.BlockSpec((1, 1, ADA_TN), lambda l, j: (l, 0, j)),
        ],
        out_specs=pl.BlockSpec((1, ADA_ROWS, ADA_TN), lambda l, j: (l, 0, j)),
        compiler_params=_cparams("arbitrary", "arbitrary"),
        name="ada",
    )(c_all, w_ada, b_ada.reshape(DEPTH, 1, n_out))


def _modulate(x, mod_ref, g, jj):
    ms = jnp.mean(x * x, axis=-1, keepdims=True)
    y = x * lax.rsqrt(ms + EPS) * g
    return y * (1.0 + mod_ref[0, 3 * jj + 1:3 * jj + 2, :]) + mod_ref[0, 3 * jj:3 * jj + 1, :]


def _ffn_kernel(x_ref, mod_ref, g_ref, wgu_ref, wout_ref, o_ref, nb_ref, acc_ref, *, jj):
    x = x_ref[...]
    nb_ref[...] = _modulate(x, mod_ref, g_ref[jj:jj + 1, :], jj).astype(BF16)
    acc_ref[...] = jnp.zeros_like(acc_ref)

    def body(c, carry):
        h = _dot(nb_ref[...], wgu_ref[c])
        hg = h[:, :FF_CHUNK]
        a = (hg * jax.nn.sigmoid(hg)) * h[:, FF_CHUNK:]
        acc_ref[...] += _dot(a.astype(BF16), wout_ref[c])
        return carry

    lax.fori_loop(0, N_FF_CHUNKS, body, 0)
    o_ref[...] = x + (0.5 * mod_ref[0, 3 * jj + 2:3 * jj + 3, :]) * acc_ref[...]


def _ffn(x, mod, g_norm, wgu, wout, *, layer, which, seq, tm):
    n = x.shape[0]
    jj = 2 * which
    const = dict(pipeline_mode=pl.Buffered(1))
    return pl.pallas_call(
        functools.partial(_ffn_kernel, jj=jj),
        out_shape=jax.ShapeDtypeStruct((n, D_MODEL), F32),
        grid=(n // tm,),
        in_specs=[
            pl.BlockSpec((tm, D_MODEL), lambda i: (i, 0)),
            pl.BlockSpec((None, 1, 9, D_MODEL), lambda i: (layer, (i * tm) // seq, 0, 0)),
            pl.BlockSpec((None, 3, D_MODEL), lambda i: (layer, 0, 0)),
            pl.BlockSpec((None, None, N_FF_CHUNKS, D_MODEL, 2 * FF_CHUNK),
                         lambda i: (layer, which, 0, 0, 0), **const),
            pl.BlockSpec((None, None, N_FF_CHUNKS, FF_CHUNK, D_MODEL),
                         lambda i: (layer, which, 0, 0, 0), **const),
        ],
        out_specs=pl.BlockSpec((tm, D_MODEL), lambda i: (i, 0)),
        scratch_shapes=[pltpu.VMEM((tm, D_MODEL), BF16), pltpu.VMEM((tm, D_MODEL), F32)],
        input_output_aliases={0: 0},
        compiler_params=_cparams("arbitrary"),
        name=f"ffn{which}",
    )(x, mod, g_norm, wgu, wout)


def _head_norm(r, g):
    ms = jnp.mean(r * r, axis=-1, keepdims=True)
    return r * lax.rsqrt(ms + EPS) * g


def _half_norm(r, g2):
    sq = r * r
    lo = lax.broadcasted_iota(jnp.int32, r.shape, 1) < DIFF_DIM
    s_lo = jnp.sum(jnp.where(lo, sq, 0.0), axis=-1, keepdims=True)
    s_hi = jnp.sum(jnp.where(lo, 0.0, sq), axis=-1, keepdims=True)
    ms = jnp.where(lo, s_lo, s_hi) * (1.0 / DIFF_DIM)
    return r * lax.rsqrt(ms + EPS) * g2


def _qkv_kernel(x_ref, mod_ref, g_ref, w_ref, cos_ref, sa_ref, sb_ref, gh_ref,
                qa_ref, ka_ref, va_ref, qb_ref, kb_ref, vb_ref, qc_ref, kc_ref, vc_ref, nb_ref):
    nb_ref[...] = _modulate(x_ref[...], mod_ref, g_ref[1:2, :], 1).astype(BF16)
    cos, sa, sb = cos_ref[...], sa_ref[...], sb_ref[...]

    def rope(r):
        return (r * cos + pltpu.roll(r, HEAD_DIM - 32, 1) * sa + pltpu.roll(r, 32, 1) * sb)

    def proj(col, width):
        return _dot(nb_ref[...], w_ref[:, col:col + width])

    def heads(r, n_heads, fn, out_ref):
        for h in range(n_heads):
            out_ref[h] = fn(r[:, h * HEAD_DIM:(h + 1) * HEAD_DIM]).astype(BF16)

    def values_transposed(r, out_ref):
        for h in range(N_KV_HEADS):
            out_ref[h, :HEAD_DIM, :] = r[:, h * HEAD_DIM:(h + 1) * HEAD_DIM].T.astype(BF16)
            out_ref[h, HEAD_DIM:, :] = jnp.ones((VT_ROWS - HEAD_DIM, r.shape[0]), BF16)

    qa_scale = HEAD_DIM ** -0.5 * LOG2E
    qc_scale = DIFF_DIM ** -0.5 * LOG2E
    g_qa, g_ka, g_qb, g_kb = (gh_ref[i:i + 1, :] for i in range(4))
    g_qc, g_kc = gh_ref[4:5, :], gh_ref[5:6, :]
    ident = lambda r: r
    col = 0
    heads(proj(col, Q_W), N_Q_HEADS, lambda r: rope(_head_norm(r, g_qa)) * qa_scale, qa_ref)
    col += Q_W
    heads(proj(col, KV_W), N_KV_HEADS, lambda r: rope(_head_norm(r, g_ka)), ka_ref)
    col += KV_W
    values_transposed(proj(col, KV_W),va_ref)
    col += KV_W
    heads(proj(col, Q_W), N_Q_HEADS, lambda r: _head_norm(r, g_qb), qb_ref)
    col += Q_W
    heads(proj(col, KV_W), N_KV_HEADS, lambda r: _head_norm(r, g_kb), kb_ref)
    col += KV_W
    heads(proj(col, KV_W), N_KV_HEADS, ident, vb_ref)
    col += KV_W
    heads(proj(col, Q_W), N_Q_HEADS, lambda r: _half_norm(r, g_qc) * qc_scale, qc_ref)
    col += Q_W
    heads(proj(col, KV_W), N_KV_HEADS, lambda r: _half_norm(r, g_kc), kc_ref)
    col += KV_W
    values_transposed(proj(col, KV_W),vc_ref)


def _qkv(x, mod, g_norm, wqkv, rope_tabs, gh, *, layer, seq, tm):
    n = x.shape[0]
    tiles_per_seq = seq // tm
    q_shape = jax.ShapeDtypeStruct((N_Q_HEADS, n, HEAD_DIM), BF16)
    kv_shape = jax.ShapeDtypeStruct((N_KV_HEADS, n, HEAD_DIM), BF16)
    q_spec = pl.BlockSpec((N_Q_HEADS, tm, HEAD_DIM), lambda i: (0, i, 0))
    kv_spec = pl.BlockSpec((N_KV_HEADS, tm, HEAD_DIM), lambda i: (0, i, 0))
    tab_spec = pl.BlockSpec((tm, HEAD_DIM), lambda i: (i % tiles_per_seq, 0))
    v1_shape = jax.ShapeDtypeStruct((N_KV_HEADS, VT_ROWS, n), BF16)
    v1_spec = pl.BlockSpec((N_KV_HEADS, VT_ROWS, tm), lambda i: (0, 0, i))
    return pl.pallas_call(
        _qkv_kernel,
        out_shape=[q_shape, kv_shape, v1_shape, q_shape, kv_shape, kv_shape, q_shape, kv_shape, v1_shape],
        grid=(n // tm,),
        in_specs=[
            pl.BlockSpec((tm, D_MODEL), lambda i: (i, 0)),
            pl.BlockSpec((None, 1, 9, D_MODEL), lambda i: (layer, (i * tm) // seq, 0, 0)),
            pl.BlockSpec((None, 3, D_MODEL), lambda i: (layer, 0, 0)),
            pl.BlockSpec((None, D_MODEL, QKV_W), lambda i: (layer, 0, 0), pipeline_mode=pl.Buffered(1)),
            tab_spec, tab_spec, tab_spec,
            pl.BlockSpec((None, 8, HEAD_DIM), lambda i: (layer, 0, 0)),
        ],
        out_specs=[q_spec, kv_spec, v1_spec, q_spec, kv_spec, kv_spec, q_spec, kv_spec, v1_spec],
        scratch_shapes=[pltpu.VMEM((tm, D_MODEL), BF16)],
        compiler_params=_cparams("arbitrary"),
        name="qkv",
    )(x, mod, g_norm, wqkv, *rope_tabs, gh)


def _out_kernel(x_ref, mod_ref, g_ref, oa_ref, ob_ref, oc_ref, wg_ref, wo_ref, o_ref, mg_ref):
    x = x_ref[...]
    nb = _modulate(x, mod_ref, g_ref[1:2, :], 1).astype(BF16)
    for br, br_ref in enumerate((oa_ref, ob_ref, oc_ref)):
        gate = jax.nn.sigmoid(_dot(nb, wg_ref[:, br * D_MODEL:(br + 1) * D_MODEL]))
        for h in range(N_Q_HEADS):
            lanes = slice(h * HEAD_DIM, (h + 1) * HEAD_DIM)
            term = gate[:, lanes] * br_ref[h]
            if br == 0:
                mg_ref[:, lanes] = term
            else:
                mg_ref[:, lanes] += term
    o_ref[...] = x + mod_ref[0, 5:6, :] * _dot(mg_ref[...].astype(BF16), wo_ref[...])


def _out_proj(x, mod, g_norm, oa, ob, oc, wgate, wo, *, layer, seq, tm):
    n = x.shape[0]
    o_spec = pl.BlockSpec((N_Q_HEADS, tm, HEAD_DIM), lambda i: (0, i, 0))
    return pl.pallas_call(
        _out_kernel,
        out_shape=jax.ShapeDtypeStruct((n, D_MODEL), F32),
        grid=(n // tm,),
        in_specs=[
            pl.BlockSpec((tm, D_MODEL), lambda i: (i, 0)),
            pl.BlockSpec((None, 1, 9, D_MODEL), lambda i: (layer, (i * tm) // seq, 0, 0)),
            pl.BlockSpec((None, 3, D_MODEL), lambda i: (layer, 0, 0)),
            o_spec, o_spec, o_spec,
            pl.BlockSpec((None, D_MODEL, 3 * D_MODEL), lambda i: (layer, 0, 0), pipeline_mode=pl.Buffered(1)),
            pl.BlockSpec((None, D_MODEL, D_MODEL), lambda i: (layer, 0, 0), pipeline_mode=pl.Buffered(1)),
        ],
        out_specs=pl.BlockSpec((tm, D_MODEL), lambda i: (i, 0)),
        scratch_shapes=[pltpu.VMEM((tm, D_MODEL), F32)],
        input_output_aliases={0: 0},
        compiler_params=_cparams("arbitrary"),
        name="out_proj",
    )(x, mod, g_norm, oa, ob, oc, wgate, wo)


def _softmax_pv_t(s, v1t, m, acct, lane_const=None):
    top = jnp.max(s, axis=0, keepdims=True)
    if lane_const is not None:
        top = top + lane_const
    m_new = jnp.maximum(m, top)
    alpha = jnp.exp2(m - m_new)
    p = jnp.exp2(s - (m_new if lane_const is None else m_new - lane_const))
    return m_new, alpha * acct + _dot(v1t, p.astype(BF16))


def _flash_iteration(t, last, q, k_ref, v_ref, m_ref, acct_ref, s_ref, *, unroll, tks,
                     lane_const=None, bias_fn=None):
    def sub(idx):
        return pl.ds(pl.multiple_of(idx * tks, tks), tks)

    m, acct = m_ref[...], acct_ref[...]
    s = s_ref[...]
    for u in range(unroll):
        idx = t * unroll + u
        s_next = None if (last and u == unroll - 1) else _dot_nt(k_ref[sub(idx + 1), :], q)
        if bias_fn is not None:
            s = bias_fn(idx, s)
        m, acct = _softmax_pv_t(s, v_ref[:, sub(idx)], m, acct, lane_const)
        s = s_next
    m_ref[...] = m
    acct_ref[...] = acct
    if not last:
        s_ref[...] = s


def _flash_init(q, k_ref, m_ref, acct_ref, s_ref, tks):
    m_ref[...] = jnp.full_like(m_ref, -jnp.inf)
    acct_ref[...] = jnp.zeros_like(acct_ref)
    s_ref[...] = _dot_nt(k_ref[0:tks, :], q)


def _attn_a_kernel(q_ref, k_ref, v_ref, o_ref, m_ref, acct_ref, s_ref, *, n_iter, unroll, tks):
    rows = acct_ref.shape[1]
    q = q_ref[...].reshape(rows, HEAD_DIM)
    _flash_init(q, k_ref, m_ref, acct_ref, s_ref, tks)
    refs = (q, k_ref, v_ref, m_ref, acct_ref, s_ref)

    def body(t, carry):
        _flash_iteration(t, False, *refs, unroll=unroll, tks=tks)
        return carry

    lax.fori_loop(0, n_iter - 1, body, 0)
    _flash_iteration(n_iter - 1, True, *refs, unroll=unroll, tks=tks)
    acct = acct_ref[...]
    o_t = acct[:HEAD_DIM] / acct[HEAD_DIM:HEAD_DIM + 1]
    o_ref[...] = o_t.T.reshape(o_ref.shape)


def _attn_a(q, k, v1t, *, batch, seq, tq, tks, unroll):
    n = q.shape[1]
    nq = seq // tq
    rows = Q_PER_KV * tq
    q_map = lambda b, g, i: (g, b * nq + i, 0)
    return pl.pallas_call(
        functools.partial(_attn_a_kernel, n_iter=seq // (tks * unroll), unroll=unroll, tks=tks),
        out_shape=jax.ShapeDtypeStruct((N_Q_HEADS, n, HEAD_DIM), F32),
        grid=(batch, N_KV_HEADS, nq),
        in_specs=[
            pl.BlockSpec((Q_PER_KV, tq, HEAD_DIM), q_map),
            pl.BlockSpec((None, seq, HEAD_DIM), lambda b, g, i: (g, b, 0)),
            pl.BlockSpec((None, VT_ROWS, seq), lambda b, g, i: (g, 0, b)),
        ],
        out_specs=pl.BlockSpec((Q_PER_KV, tq, HEAD_DIM), q_map),
        scratch_shapes=[pltpu.VMEM((1, rows), F32), pltpu.VMEM((VT_ROWS, rows), F32),
                        pltpu.VMEM((tks, rows), F32)],
        compiler_params=_cparams("arbitrary", "arbitrary", "arbitrary"),
        name="attn_axial",
    )(q, k, v1t)


def _attn_b_kernel(q_ref, kp_ref, kc_ref, kn_ref, vp_ref, vc_ref, vn_ref, bias_ref, sink_ref, o_ref, *, nb):
    i = pl.program_id(2)
    rows = Q_PER_KV * BLOCK_Q
    q = q_ref[...].reshape(rows, HEAD_DIM)
    kcat = jnp.concatenate([kp_ref[...], kc_ref[...], kn_ref[...]], axis=0)
    vcat = jnp.concatenate([vp_ref[...], vc_ref[...], vn_ref[...]], axis=0)
    s = _dot_nt(q, kcat) * (HEAD_DIM ** -0.5)
    s = s.reshape(Q_PER_KV, BLOCK_Q, NEAR) + bias_ref[...]
    r = lax.broadcasted_iota(jnp.int32, (BLOCK_Q, NEAR), 0)
    c = lax.broadcasted_iota(jnp.int32, (BLOCK_Q, NEAR), 1)
    kpos = (i - 1) * BLOCK_Q + c
    allowed = (jnp.abs(c - BLOCK_Q - r) <= WINDOW) & (kpos >= 0) & (kpos < nb * BLOCK_Q)
    s = jnp.where(allowed[None], s, NEG)
    sink = sink_ref[...][:, :, :1]
    m = jnp.maximum(jnp.max(s, axis=-1, keepdims=True), sink)
    e = jnp.exp(s - m)
    den = jnp.sum(e, axis=-1, keepdims=True) + jnp.exp(sink - m)
    p = (e * (1.0 / den)).reshape(rows, NEAR).astype(BF16)
    o_ref[...] = _dot(p, vcat).reshape(o_ref.shape)


def _attn_b(q, k, v, bias, sink, *, layer, batch, seq):
    n = q.shape[1]
    nb = seq // BLOCK_Q
    blk = (None, BLOCK_Q, HEAD_DIM)
    prev_map = lambda b, g, i: (g, b * nb + jnp.maximum(i - 1, 0), 0)
    cur_map = lambda b, g, i: (g, b * nb + i, 0)
    next_map = lambda b, g, i: (g, b * nb + jnp.minimum(i + 1, nb - 1), 0)
    kv_specs = [pl.BlockSpec(blk, prev_map), pl.BlockSpec(blk, cur_map), pl.BlockSpec(blk, next_map)]
    return pl.pallas_call(
        functools.partial(_attn_b_kernel, nb=nb),
        out_shape=jax.ShapeDtypeStruct((N_Q_HEADS, n, HEAD_DIM), F32),
        grid=(batch, N_KV_HEADS, nb),
        in_specs=[pl.BlockSpec((Q_PER_KV, BLOCK_Q, HEAD_DIM), cur_map)] + kv_specs + kv_specs + [
            pl.BlockSpec((Q_PER_KV, BLOCK_Q, NEAR), lambda b, g, i: (g, 0, 0)),
            pl.BlockSpec((None, Q_PER_KV, 1, HEAD_DIM), lambda b, g, i: (layer, g, 0, 0)),
        ],
        out_specs=pl.BlockSpec((Q_PER_KV, BLOCK_Q, HEAD_DIM), cur_map),
        compiler_params=_cparams("arbitrary", "arbitrary", "arbitrary"),
        name="attn_window",
    )(q, k, k, k, v, v, v, bias, sink)


def _attn_c_kernel(q_ref, k_ref, v_ref, near_ref, cfar_ref, lam_ref, gs_ref, o_ref, m_ref, acct_ref, s_ref,
                   *, n_iter, unroll, tks, nblk, out_scale):
    i = pl.program_id(2)
    rows = Q_PER_KV * BLOCK_Q
    blocks_per_sub = tks // BLOCK_Q
    blocks_per_iter = unroll * blocks_per_sub

    q = q_ref[...].reshape(rows, HEAD_DIM)
    lo = lax.broadcasted_iota(jnp.int32, q.shape, 1) < DIFF_DIM
    zero = jnp.zeros_like(q)
    q2 = jnp.concatenate([jnp.where(lo, q, zero), jnp.where(lo, zero, q)], axis=0)
    _flash_init(q2, k_ref, m_ref, acct_ref, s_ref, tks)
    refs = (q2, k_ref, v_ref, m_ref, acct_ref, s_ref)

    c_left, c_right = cfar_ref[0], cfar_ref[1]
    t_lo = jnp.maximum(i - 1, 0) // blocks_per_iter
    t_hi = jnp.minimum(i + 1, nblk - 1) // blocks_per_iter

    def add_bias(idx, s):
        blks = []
        for kb in range(blocks_per_sub):
            d = idx * blocks_per_sub + kb - i
            blk = jnp.broadcast_to(jnp.where(d < 0, c_left, c_right), (BLOCK_Q, rows))
            for e in range(NEAR // BLOCK_Q):
                blk = jnp.where(d == e - 1, near_ref[e], blk)
            blks.append(blk)
        bias = jnp.concatenate(blks, axis=0)
        return jnp.concatenate([s[:, :rows] + bias, s[:, rows:] + bias], axis=1)

    def iteration(t, last):
        is_near = (t >= t_lo) & (t <= t_hi)

        @pl.when(jnp.logical_not(is_near))
        def _():
            side = jnp.where(t < t_lo, c_left, c_right)
            _flash_iteration(t, last, *refs, unroll=unroll, tks=tks,
                             lane_const=jnp.concatenate([side, side], axis=1))

        @pl.when(is_near)
        def _():
            _flash_iteration(t, last, *refs, unroll=unroll, tks=tks, bias_fn=add_bias)

    def body(t, carry):
        iteration(t, False)
        return carry

    lax.fori_loop(0, n_iter - 1, body, 0)
    iteration(n_iter - 1, True)
    acct = acct_ref[...]
    o1_t = acct[:HEAD_DIM, :rows] / acct[HEAD_DIM:HEAD_DIM + 1, :rows]
    o2_t = acct[:HEAD_DIM, rows:] / acct[HEAD_DIM:HEAD_DIM + 1, rows:]
    o = (o1_t - lam_ref[:, :1] * o2_t).T
    ms = jnp.mean(o * o, axis=-1, keepdims=True)
    o = o * lax.rsqrt(ms + EPS) * gs_ref[...] * out_scale
    o_ref[...] = o.reshape(o_ref.shape)


def _attn_c(q, k, v1t, near_t, cfar_t, lam, g_subln, *, layer, batch, seq, tks, unroll):
    n = q.shape[1]
    nq = seq // BLOCK_Q
    rows = Q_PER_KV * BLOCK_Q
    lam_init = 0.8 - 0.6 * math.exp(-0.3 * layer)
    q_map = lambda b, g, i: (g, b * nq + i, 0)
    return pl.pallas_call(
        functools.partial(_attn_c_kernel, n_iter=seq // (tks * unroll), unroll=unroll, tks=tks,
                          nblk=nq, out_scale=1.0 - lam_init),
        out_shape=jax.ShapeDtypeStruct((N_Q_HEADS, n, HEAD_DIM), F32),
        grid=(batch, N_KV_HEADS, nq),
        in_specs=[
            pl.BlockSpec((Q_PER_KV, BLOCK_Q, HEAD_DIM), q_map),
            pl.BlockSpec((None, seq, HEAD_DIM), lambda b, g, i: (g, b, 0)),
            pl.BlockSpec((None, VT_ROWS, seq), lambda b, g, i: (g, 0, b)),
            pl.BlockSpec((None, NEAR // BLOCK_Q, BLOCK_Q, rows), lambda b, g, i: (g, 0, 0, 0)),
            pl.BlockSpec((2, None, 1, rows), lambda b, g, i: (0, g, 0, 0)),
            pl.BlockSpec((None, 1, HEAD_DIM), lambda b, g, i: (layer, 0, 0)),
            pl.BlockSpec((None, 1, HEAD_DIM), lambda b, g, i: (layer, 0, 0)),
        ],
        out_specs=pl.BlockSpec((Q_PER_KV, BLOCK_Q, HEAD_DIM), q_map),
        scratch_shapes=[pltpu.VMEM((1, 2 * rows), F32), pltpu.VMEM((VT_ROWS, 2 * rows), F32),
                        pltpu.VMEM((tks, 2 * rows), F32)],
        compiler_params=_cparams("arbitrary", "arbitrary", "arbitrary"),
        name="attn_diff",
    )(q, k, v1t, near_t, cfar_t, lam, g_subln)


def _t5_bucket_np(rel):
    half = N_BUCKETS // 2
    max_exact = half // 2
    ret = np.where(rel > 0, half, 0)
    n = np.abs(rel)
    ratio = np.log(np.maximum(n, 1).astype(np.float32) / np.float32(max_exact)) / np.float32(
        math.log(MAX_DISTANCE / max_exact))
    large = max_exact + (ratio * np.float32(half - max_exact)).astype(np.int32)
    large = np.minimum(large, half - 1)
    return (ret + np.where(n < max_exact, n, large)).astype(np.int32)


def _near_bias(table):
    r = np.arange(BLOCK_Q)[:, None]
    c = np.arange(NEAR)[None, :]
    bucket = _t5_bucket_np(c - BLOCK_Q - r)
    onehot = (bucket.reshape(-1, 1) == np.arange(N_BUCKETS)[None, :]).astype(np.float32)
    rows = jnp.dot(jnp.asarray(onehot), table.astype(F32), precision=lax.Precision.HIGHEST)
    return rows.reshape(BLOCK_Q, NEAR, table.shape[1]).transpose(2, 0, 1)


def _rope_tables(seq):
    rows = seq // GRID_W
    row = jnp.repeat(jnp.arange(rows), GRID_W).astype(F32)
    col = jnp.tile(jnp.arange(GRID_W), rows).astype(F32)
    nfreq = HEAD_DIM // 4
    inv = ROPE_THETA ** (-jnp.arange(nfreq, dtype=F32) / nfreq)
    ang_r = row[:, None] * inv
    ang_c = col[:, None] * inv
    ang = jnp.concatenate([ang_r, ang_r, ang_c, ang_c], axis=-1)
    cos, sin = jnp.cos(ang), jnp.sin(ang)
    first = (np.arange(HEAD_DIM) % (HEAD_DIM // 2)) < HEAD_DIM // 4
    return cos, jnp.where(first, -sin, 0.0), jnp.where(first, 0.0, sin)


def _trunk(x, mod, p, *, batch, seq):
    t = _tiles(seq)
    tm = t["tm"]
    rope_tabs = _rope_tables(seq)
    for l in range(DEPTH):
        x = _ffn(x, mod, p["g_norm"], p["wgu"], p["wout"], layer=l, which=0, seq=seq, tm=tm)
        qa, ka, va, qb, kb, vb, qc, kc, vc = _qkv(x, mod, p["g_norm"], p["wqkv"], rope_tabs, p["gh"],
                                                  layer=l, seq=seq, tm=tm)
        oa = _attn_a(qa, ka, va, batch=batch, seq=seq, tq=t["tq_a"], tks=t["tks"], unroll=t["unroll"])
        ob = _attn_b(qb, kb, vb, p["bias_b"], p["sink"], layer=l, batch=batch, seq=seq)
        oc = _attn_c(qc, kc, vc, p["near_c"], p["cfar_c"], p["lam"], p["g_subln"],
                     layer=l, batch=batch, seq=seq, tks=t["tks"], unroll=t["unroll"])
        x = _out_proj(x, mod, p["g_norm"], oa, ob, oc, p["wgate"], p["wo"], layer=l, seq=seq, tm=tm)
        x = _ffn(x, mod, p["g_norm"], p["wgu"], p["wout"], layer=l, which=1, seq=seq, tm=tm)
    return x


def _prepare(w_ff_in, w_ff_out, w_in, w_o, g_qa, g_ka, g_qb, g_kb, g_qc, g_kc, sink,
             lam_q1, lam_k1, lam_q2, lam_k2, g_subln, rel_bias):
    wg = w_ff_in[..., :D_FF].reshape(DEPTH, 2, D_MODEL, N_FF_CHUNKS, FF_CHUNK)
    wu = w_ff_in[..., D_FF:].reshape(DEPTH, 2, D_MODEL, N_FF_CHUNKS, FF_CHUNK)
    wgu = jnp.concatenate([wg, wu], axis=-1).transpose(0, 1, 3, 2, 4).astype(BF16)
    wout = w_ff_out.reshape(DEPTH, 2, N_FF_CHUNKS, FF_CHUNK, D_MODEL).astype(BF16)
    zeros = jnp.zeros_like(g_qa)
    gh = jnp.stack([g_qa, g_ka, g_qb, g_kb, jnp.tile(g_qc, (1, 2)), jnp.tile(g_kc, (1, 2)), zeros, zeros],
                   axis=1).astype(F32)
    table_b, table_c = rel_bias[:, :N_Q_HEADS], rel_bias[:, N_Q_HEADS:]
    half = N_BUCKETS // 2
    c_left, c_right = table_c[half - 1] * LOG2E, table_c[N_BUCKETS - 1] * LOG2E
    lam_init = jnp.asarray([0.8 - 0.6 * math.exp(-0.3 * l) for l in range(DEPTH)], F32)
    lam = (jnp.exp(jnp.sum(lam_q1.astype(F32) * lam_k1.astype(F32), axis=-1))
           - jnp.exp(jnp.sum(lam_q2.astype(F32) * lam_k2.astype(F32), axis=-1)) + lam_init)
    return dict(
        wgu=wgu, wout=wout,
        wqkv=w_in[:, :, :QKV_W].astype(BF16), wgate=w_in[:, :, QKV_W:].astype(BF16), wo=w_o.astype(BF16),
        gh=gh,
        bias_b=_near_bias(table_b),
        sink=jnp.broadcast_to(sink.astype(F32)[:, :, None, None], (DEPTH, N_Q_HEADS, 1, HEAD_DIM)),
        near_c=(_near_bias(table_c) * LOG2E).reshape(
            N_KV_HEADS, Q_PER_KV, BLOCK_Q, NEAR // BLOCK_Q, BLOCK_Q).transpose(0, 3, 4, 1, 2).reshape(
            N_KV_HEADS, NEAR // BLOCK_Q, BLOCK_Q, Q_PER_KV * BLOCK_Q),
        cfar_c=jnp.repeat(jnp.stack([c_left, c_right]).astype(F32), BLOCK_Q, axis=1).reshape(
            2, N_KV_HEADS, 1, Q_PER_KV * BLOCK_Q),
        lam=jnp.broadcast_to(lam[:, None, None], (DEPTH, 1, HEAD_DIM)),
        g_subln=g_subln.astype(F32)[:, None, :],
    )


def kernel(x_prompt, x_sample, c_prompt, c_sample, w_ada, b_ada, g_norm, w_ff_in, w_ff_out, w_in, w_o,
           g_qa, g_ka, g_qb, g_kb, g_qc, g_kc, sink, lam_q1, lam_k1, lam_q2, lam_k2, g_subln, rel_bias):
    p = _prepare(w_ff_in, w_ff_out, w_in, w_o, g_qa, g_ka, g_qb, g_kb, g_qc, g_kc, sink,
                 lam_q1, lam_k1, lam_q2, lam_k2, g_subln, rel_bias)
    p["g_norm"] = g_norm.astype(F32)
    outs = []
    n_cond = 0
    conds = [c_prompt, c_sample]
    c_all = jnp.concatenate(conds + [jnp.zeros((ADA_ROWS - sum(c.shape[0] for c in conds), D_MODEL), F32)])
    mod_all = _ada(c_all, w_ada, b_ada)
    for x, c in ((x_prompt, c_prompt), (x_sample, c_sample)):
        batch, seq, _ = x.shape
        mod = mod_all[:, n_cond:n_cond + batch].reshape(DEPTH, batch, 9, D_MODEL)
        n_cond += batch
        y = _trunk(x.reshape(batch * seq, D_MODEL), mod, p, batch=batch, seq=seq)
        outs.append(y.reshape(batch, seq, D_MODEL))
    return tuple(outs)
```

```python
import functools
import math

import numpy as np
import jax
import jax.numpy as jnp
from jax import lax
from jax.experimental import pallas as pl
from jax.experimental.pallas import tpu as pltpu

F32 = jnp.float32
BF16 = jnp.bfloat16

D_MODEL = 1024
DEPTH = 4
N_Q_HEADS = 8
N_KV_HEADS = 2
Q_PER_KV = N_Q_HEADS // N_KV_HEADS
HEAD_DIM = 128
DIFF_DIM = 64
D_FF = 2816
BLOCK_Q = 128
WINDOW = 128
GRID_W = 64
N_BUCKETS = 32
MAX_DISTANCE = 128
ROPE_THETA = 10000.0
EPS = 1e-6
NEG = -1e30
LOG2E = 1.4426950408889634

Q_W = N_Q_HEADS * HEAD_DIM
KV_W = N_KV_HEADS * HEAD_DIM
BRANCH_IN = Q_W + 2 * KV_W
QKV_W = 3 * BRANCH_IN
FF_CHUNK = 256
N_FF_CHUNKS = D_FF // FF_CHUNK
ADA_ROWS = 16
ADA_TN = 1536
NEAR = 3 * BLOCK_Q
VT_ROWS = HEAD_DIM + 16

VMEM_LIMIT = 56 * 1024 * 1024


def _tiles(seq):
    tks = min(512, seq)
    return dict(
        tm=min(512, seq),
        tq_a=BLOCK_Q,
        tks=tks,
        unroll=min(4, seq // tks),
    )


def _cparams(*sem):
    return pltpu.CompilerParams(dimension_semantics=sem, vmem_limit_bytes=VMEM_LIMIT)


def _dot(a, b):
    return jnp.dot(a, b, preferred_element_type=F32)


def _dot_nt(a, b):
    return lax.dot_general(a, b, (((1,), (1,)), ((), ())), preferred_element_type=F32)


def _ada_kernel(c_ref, w_ref, b_ref, o_ref):
    c = c_ref[...]
    a = (c * jax.nn.sigmoid(c)).astype(BF16)
    o_ref[0] = _dot(a, w_ref[0].astype(BF16)) + b_ref[0]


def _ada(c_all, w_ada, b_ada):
    n_out = w_ada.shape[-1]
    return pl.pallas_call(
        _ada_kernel,
        out_shape=jax.ShapeDtypeStruct((DEPTH, ADA_ROWS, n_out), F32),
        grid=(DEPTH, n_out // ADA_TN),
        in_specs=[
            pl.BlockSpec((ADA_ROWS, D_MODEL), lambda l, j: (0, 0)),
            pl.BlockSpec((1, D_MODEL, ADA_TN), lambda l, j: (l, 0, j)),
            pl.BlockSpec((1, 1, ADA_TN), lambda l, j: (l, 0, j)),
        ],
        out_specs=pl.BlockSpec((1, ADA_ROWS, ADA_TN), lambda l, j: (l, 0, j)),
        compiler_params=_cparams("arbitrary", "arbitrary"),
        name="ada",
    )(c_all, w_ada, b_ada.reshape(DEPTH, 1, n_out))


def _modulate(x, mod_ref, g, jj):
    ms = jnp.mean(x * x, axis=-1, keepdims=True)
    y = x * lax.rsqrt(ms + EPS) * g
    return y * (1.0 + mod_ref[0, 3 * jj + 1:3 * jj + 2, :]) + mod_ref[0, 3 * jj:3 * jj + 1, :]


def _ffn_kernel(x_ref, mod_ref, g_ref, wgu_ref, wout_ref, o_ref, nb_ref, acc_ref, *, jj):
    x = x_ref[...]
    nb_ref[...] = _modulate(x, mod_ref, g_ref[jj:jj + 1, :], jj).astype(BF16)
    acc_ref[...] = jnp.zeros_like(acc_ref)

    def body(c, carry):
        h = _dot(nb_ref[...], wgu_ref[c])
        hg = h[:, :FF_CHUNK]
        a = (hg * jax.nn.sigmoid(hg)) * h[:, FF_CHUNK:]
        acc_ref[...] += _dot(a.astype(BF16), wout_ref[c])
        return carry

    lax.fori_loop(0, N_FF_CHUNKS, body, 0)
    o_ref[...] = x + (0.5 * mod_ref[0, 3 * jj + 2:3 * jj + 3, :]) * acc_ref[...]


def _ffn(x, mod, g_norm, wgu, wout, *, layer, which, seq, tm):
    n = x.shape[0]
    jj = 2 * which
    const = dict(pipeline_mode=pl.Buffered(1))
    return pl.pallas_call(
        functools.partial(_ffn_kernel, jj=jj),
        out_shape=jax.ShapeDtypeStruct((n, D_MODEL), F32),
        grid=(n // tm,),
        in_specs=[
            pl.BlockSpec((tm, D_MODEL), lambda i: (i, 0)),
            pl.BlockSpec((None, 1, 9, D_MODEL), lambda i: (layer, (i * tm) // seq, 0, 0)),
            pl.BlockSpec((None, 3, D_MODEL), lambda i: (layer, 0, 0)),
            pl.BlockSpec((None, None, N_FF_CHUNKS, D_MODEL, 2 * FF_CHUNK),
                         lambda i: (layer, which, 0, 0, 0), **const),
            pl.BlockSpec((None, None, N_FF_CHUNKS, FF_CHUNK, D_MODEL),
                         lambda i: (layer, which, 0, 0, 0), **const),
        ],
        out_specs=pl.BlockSpec((tm, D_MODEL), lambda i: (i, 0)),
        scratch_shapes=[pltpu.VMEM((tm, D_MODEL), BF16), pltpu.VMEM((tm, D_MODEL), F32)],
        input_output_aliases={0: 0},
        compiler_params=_cparams("arbitrary"),
        name=f"ffn{which}",
    )(x, mod, g_norm, wgu, wout)


def _head_norm(r, g):
    ms = jnp.mean(r * r, axis=-1, keepdims=True)
    return r * lax.rsqrt(ms + EPS) * g


def _half_norm(r, g2):
    sq = r * r
    lo = lax.broadcasted_iota(jnp.int32, r.shape, 1) < DIFF_DIM
    s_lo = jnp.sum(jnp.where(lo, sq, 0.0), axis=-1, keepdims=True)
    s_hi = jnp.sum(jnp.where(lo, 0.0, sq), axis=-1, keepdims=True)
    ms = jnp.where(lo, s_lo, s_hi) * (1.0 / DIFF_DIM)
    return r * lax.rsqrt(ms + EPS) * g2


def _qkv_kernel(x_ref, mod_ref, g_ref, w_ref, cos_ref, sa_ref, sb_ref, gh_ref,
                qa_ref, ka_ref, va_ref, qb_ref, kb_ref, vb_ref, qc_ref, kc_ref, vc_ref, nb_ref):
    nb_ref[...] = _modulate(x_ref[...], mod_ref, g_ref[1:2, :], 1).astype(BF16)
    cos, sa, sb = cos_ref[...], sa_ref[...], sb_ref[...]

    def rope(r):
        return (r * cos + pltpu.roll(r, HEAD_DIM - 32, 1) * sa + pltpu.roll(r, 32, 1) * sb)

    def proj(col, width):
        return _dot(nb_ref[...], w_ref[:, col:col + width])

    def heads(r, n_heads, fn, out_ref):
        for h in range(n_heads):
            out_ref[h] = fn(r[:, h * HEAD_DIM:(h + 1) * HEAD_DIM]).astype(BF16)

    def values_transposed(r, out_ref):
        for h in range(N_KV_HEADS):
            out_ref[h, :HEAD_DIM, :] = r[:, h * HEAD_DIM:(h + 1) * HEAD_DIM].T.astype(BF16)
            out_ref[h, HEAD_DIM:, :] = jnp.ones((VT_ROWS - HEAD_DIM, r.shape[0]), BF16)

    qa_scale = HEAD_DIM ** -0.5 * LOG2E
    qc_scale = DIFF_DIM ** -0.5 * LOG2E
    g_qa, g_ka, g_qb, g_kb = (gh_ref[i:i + 1, :] for i in range(4))
    g_qc, g_kc = gh_ref[4:5, :], gh_ref[5:6, :]
    ident = lambda r: r
    col = 0
    heads(proj(col, Q_W), N_Q_HEADS, lambda r: rope(_head_norm(r, g_qa)) * qa_scale, qa_ref)
    col += Q_W
    heads(proj(col, KV_W), N_KV_HEADS, lambda r: rope(_head_norm(r, g_ka)), ka_ref)
    col += KV_W
    values_transposed(proj(col, KV_W),va_ref)
    col += KV_W
    heads(proj(col, Q_W), N_Q_HEADS, lambda r: _head_norm(r, g_qb), qb_ref)
    col += Q_W
    heads(proj(col, KV_W), N_KV_HEADS, lambda r: _head_norm(r, g_kb), kb_ref)
    col += KV_W
    heads(proj(col, KV_W), N_KV_HEADS, ident, vb_ref)
    col += KV_W
    heads(proj(col, Q_W), N_Q_HEADS, lambda r: _half_norm(r, g_qc) * qc_scale, qc_ref)
    col += Q_W
    heads(proj(col, KV_W), N_KV_HEADS, lambda r: _half_norm(r, g_kc), kc_ref)
    col += KV_W
    values_transposed(proj(col, KV_W),vc_ref)


def _qkv(x, mod, g_norm, wqkv, rope_tabs, gh, *, layer, seq, tm):
    n = x.shape[0]
    tiles_per_seq = seq // tm
    q_shape = jax.ShapeDtypeStruct((N_Q_HEADS, n, HEAD_DIM), BF16)
    kv_shape = jax.ShapeDtypeStruct((N_KV_HEADS, n, HEAD_DIM), BF16)
    q_spec = pl.BlockSpec((N_Q_HEADS, tm, HEAD_DIM), lambda i: (0, i, 0))
    kv_spec = pl.BlockSpec((N_KV_HEADS, tm, HEAD_DIM), lambda i: (0, i, 0))
    tab_spec = pl.BlockSpec((tm, HEAD_DIM), lambda i: (i % tiles_per_seq, 0))
    v1_shape = jax.ShapeDtypeStruct((N_KV_HEADS, VT_ROWS, n), BF16)
    v1_spec = pl.BlockSpec((N_KV_HEADS, VT_ROWS, tm), lambda i: (0, 0, i))
    return pl.pallas_call(
        _qkv_kernel,
        out_shape=[q_shape, kv_shape, v1_shape, q_shape, kv_shape, kv_shape, q_shape, kv_shape, v1_shape],
        grid=(n // tm,),
        in_specs=[
            pl.BlockSpec((tm, D_MODEL), lambda i: (i, 0)),
            pl.BlockSpec((None, 1, 9, D_MODEL), lambda i: (layer, (i * tm) // seq, 0, 0)),
            pl.BlockSpec((None, 3, D_MODEL), lambda i: (layer, 0, 0)),
            pl.BlockSpec((None, D_MODEL, QKV_W), lambda i: (layer, 0, 0), pipeline_mode=pl.Buffered(1)),
            tab_spec, tab_spec, tab_spec,
            pl.BlockSpec((None, 8, HEAD_DIM), lambda i: (layer, 0, 0)),
        ],
        out_specs=[q_spec, kv_spec, v1_spec, q_spec, kv_spec, kv_spec, q_spec, kv_spec, v1_spec],
        scratch_shapes=[pltpu.VMEM((tm, D_MODEL), BF16)],
        compiler_params=_cparams("arbitrary"),
        name="qkv",
    )(x, mod, g_norm, wqkv, *rope_tabs, gh)


def _out_kernel(x_ref, mod_ref, g_ref, oa_ref, ob_ref, oc_ref, wg_ref, wo_ref, o_ref, mg_ref):
    x = x_ref[...]
    nb = _modulate(x, mod_ref, g_ref[1:2, :], 1).astype(BF16)
    for br, br_ref in enumerate((oa_ref, ob_ref, oc_ref)):
        gate = jax.nn.sigmoid(_dot(nb, wg_ref[:, br * D_MODEL:(br + 1) * D_MODEL]))
        for h in range(N_Q_HEADS):
            lanes = slice(h * HEAD_DIM, (h + 1) * HEAD_DIM)
            term = gate[:, lanes] * br_ref[h]
            if br == 0:
                mg_ref[:, lanes] = term
            else:
                mg_ref[:, lanes] += term
    o_ref[...] = x + mod_ref[0, 5:6, :] * _dot(mg_ref[...].astype(BF16), wo_ref[...])


def _out_proj(x, mod, g_norm, oa, ob, oc, wgate, wo, *, layer, seq, tm):
    n = x.shape[0]
    o_spec = pl.BlockSpec((N_Q_HEADS, tm, HEAD_DIM), lambda i: (0, i, 0))
    return pl.pallas_call(
        _out_kernel,
        out_shape=jax.ShapeDtypeStruct((n, D_MODEL), F32),
        grid=(n // tm,),
        in_specs=[
            pl.BlockSpec((tm, D_MODEL), lambda i: (i, 0)),
            pl.BlockSpec((None, 1, 9, D_MODEL), lambda i: (layer, (i * tm) // seq, 0, 0)),
            pl.BlockSpec((None, 3, D_MODEL), lambda i: (layer, 0, 0)),
            o_spec, o_spec, o_spec,
            pl.BlockSpec((None, D_MODEL, 3 * D_MODEL), lambda i: (layer, 0, 0), pipeline_mode=pl.Buffered(1)),
            pl.BlockSpec((None, D_MODEL, D_MODEL), lambda i: (layer, 0, 0), pipeline_mode=pl.Buffered(1)),
        ],
        out_specs=pl.BlockSpec((tm, D_MODEL), lambda i: (i, 0)),
        scratch_shapes=[pltpu.VMEM((tm, D_MODEL), F32)],
        input_output_aliases={0: 0},
        compiler_params=_cparams("arbitrary"),
        name="out_proj",
    )(x, mod, g_norm, oa, ob, oc, wgate, wo)


MIN_DENOM = 2.0 ** -64


def _key_slice(idx, tks):
    return pl.ds(pl.multiple_of(idx * tks, tks), tks)


def _score_bound(q, key_norm):
    qf = q.astype(F32)
    qn2 = _dot_nt(jnp.ones((8, q.shape[1]), F32), qf * qf)[:1]
    return jnp.sqrt(qn2) * key_norm


def _bounded_iteration(t, last, q, k_ref, v_ref, acct_ref, s_ref, *, unroll, tks, shift, bias_fn=None):
    acct = acct_ref[...]
    s = s_ref[...]
    for u in range(unroll):
        idx = t * unroll + u
        s_next = None if (last and u == unroll - 1) else _dot_nt(k_ref[_key_slice(idx + 1, tks), :], q)
        if bias_fn is not None:
            s = bias_fn(idx, s)
        acct = acct + _dot(v_ref[:, _key_slice(idx, tks)], jnp.exp2(s - shift).astype(BF16))
        s = s_next
    acct_ref[...] = acct
    if not last:
        s_ref[...] = s


def _exact_pass(q, k_ref, v_ref, m_ref, acct_ref, *, n_sub, tks, bias_fn=None):
    m_ref[...] = jnp.full_like(m_ref, -jnp.inf)
    acct_ref[...] = jnp.zeros_like(acct_ref)

    def body(idx, carry):
        s = _dot_nt(k_ref[_key_slice(idx, tks), :], q)
        if bias_fn is not None:
            s = bias_fn(idx, s)
        m = m_ref[...]
        m_new = jnp.maximum(m, jnp.max(s, axis=0, keepdims=True))
        p = jnp.exp2(s - m_new).astype(BF16)
        acct_ref[...] = jnp.exp2(m - m_new) * acct_ref[...] + _dot(v_ref[:, _key_slice(idx, tks)], p)
        m_ref[...] = m_new
        return carry

    lax.fori_loop(0, n_sub, body, 0)


def _denominators_ok(acct_ref):
    return jnp.min(acct_ref[HEAD_DIM:HEAD_DIM + 1, :]) >= MIN_DENOM


def _attn_a_kernel(q_ref, k_ref, v_ref, kn_ref, o_ref, m_ref, acct_ref, s_ref, *, n_iter, unroll, tks):
    rows = acct_ref.shape[1]
    q = q_ref[...].reshape(rows, HEAD_DIM)
    shift = _score_bound(q, kn_ref[:, :1])
    acct_ref[...] = jnp.zeros_like(acct_ref)
    s_ref[...] = _dot_nt(k_ref[0:tks, :], q)
    refs = (q, k_ref, v_ref, acct_ref, s_ref)

    def body(t, carry):
        _bounded_iteration(t, False, *refs, unroll=unroll, tks=tks, shift=shift)
        return carry

    lax.fori_loop(0, n_iter - 1, body, 0)
    _bounded_iteration(n_iter - 1, True, *refs, unroll=unroll, tks=tks, shift=shift)

    @pl.when(jnp.logical_not(_denominators_ok(acct_ref)))
    def _():
        _exact_pass(q, k_ref, v_ref, m_ref, acct_ref, n_sub=n_iter * unroll, tks=tks)

    acct = acct_ref[...]
    o_t = acct[:HEAD_DIM] / acct[HEAD_DIM:HEAD_DIM + 1]
    o_ref[...] = o_t.T.reshape(o_ref.shape)


def _attn_a(q, k, v1t, key_norm, *, layer, batch, seq, tq, tks, unroll):
    n = q.shape[1]
    nq = seq // tq
    rows = Q_PER_KV * tq
    q_map = lambda b, g, i: (g, b * nq + i, 0)
    return pl.pallas_call(
        functools.partial(_attn_a_kernel, n_iter=seq // (tks * unroll), unroll=unroll, tks=tks),
        out_shape=jax.ShapeDtypeStruct((N_Q_HEADS, n, HEAD_DIM), F32),
        grid=(batch, N_KV_HEADS, nq),
        in_specs=[
            pl.BlockSpec((Q_PER_KV, tq, HEAD_DIM), q_map),
            pl.BlockSpec((None, seq, HEAD_DIM), lambda b, g, i: (g, b, 0)),
            pl.BlockSpec((None, VT_ROWS, seq), lambda b, g, i: (g, 0, b)),
            pl.BlockSpec((None, 1, HEAD_DIM), lambda b, g, i: (layer, 0, 0)),
        ],
        out_specs=pl.BlockSpec((Q_PER_KV, tq, HEAD_DIM), q_map),
        scratch_shapes=[pltpu.VMEM((1, rows), F32), pltpu.VMEM((VT_ROWS, rows), F32),
                        pltpu.VMEM((tks, rows), F32)],
        compiler_params=_cparams("arbitrary", "arbitrary", "arbitrary"),
        name="attn_axial",
    )(q, k, v1t, key_norm)


def _attn_b_kernel(q_ref, kp_ref, kc_ref, kn_ref, vp_ref, vc_ref, vn_ref, bias_ref, sink_ref, o_ref, *, nb):
    i = pl.program_id(2)
    rows = Q_PER_KV * BLOCK_Q
    q = q_ref[...].reshape(rows, HEAD_DIM)
    kcat = jnp.concatenate([kp_ref[...], kc_ref[...], kn_ref[...]], axis=0)
    vcat = jnp.concatenate([vp_ref[...], vc_ref[...], vn_ref[...]], axis=0)
    s = _dot_nt(q, kcat) * (HEAD_DIM ** -0.5)
    s = s.reshape(Q_PER_KV, BLOCK_Q, NEAR) + bias_ref[...]
    r = lax.broadcasted_iota(jnp.int32, (BLOCK_Q, NEAR), 0)
    c = lax.broadcasted_iota(jnp.int32, (BLOCK_Q, NEAR), 1)
    kpos = (i - 1) * BLOCK_Q + c
    allowed = (jnp.abs(c - BLOCK_Q - r) <= WINDOW) & (kpos >= 0) & (kpos < nb * BLOCK_Q)
    s = jnp.where(allowed[None], s, NEG)
    sink = sink_ref[...][:, :, :1]
    m = jnp.maximum(jnp.max(s, axis=-1, keepdims=True), sink)
    e = jnp.exp(s - m)
    den = jnp.sum(e, axis=-1, keepdims=True) + jnp.exp(sink - m)
    p = (e * (1.0 / den)).reshape(rows, NEAR).astype(BF16)
    o_ref[...] = _dot(p, vcat).reshape(o_ref.shape)


def _attn_b(q, k, v, bias, sink, *, layer, batch, seq):
    n = q.shape[1]
    nb = seq // BLOCK_Q
    blk = (None, BLOCK_Q, HEAD_DIM)
    prev_map = lambda b, g, i: (g, b * nb + jnp.maximum(i - 1, 0), 0)
    cur_map = lambda b, g, i: (g, b * nb + i, 0)
    next_map = lambda b, g, i: (g, b * nb + jnp.minimum(i + 1, nb - 1), 0)
    kv_specs = [pl.BlockSpec(blk, prev_map), pl.BlockSpec(blk, cur_map), pl.BlockSpec(blk, next_map)]
    return pl.pallas_call(
        functools.partial(_attn_b_kernel, nb=nb),
        out_shape=jax.ShapeDtypeStruct((N_Q_HEADS, n, HEAD_DIM), F32),
        grid=(batch, N_KV_HEADS, nb),
        in_specs=[pl.BlockSpec((Q_PER_KV, BLOCK_Q, HEAD_DIM), cur_map)] + kv_specs + kv_specs + [
            pl.BlockSpec((Q_PER_KV, BLOCK_Q, NEAR), lambda b, g, i: (g, 0, 0)),
            pl.BlockSpec((None, Q_PER_KV, 1, HEAD_DIM), lambda b, g, i: (layer, g, 0, 0)),
        ],
        out_specs=pl.BlockSpec((Q_PER_KV, BLOCK_Q, HEAD_DIM), cur_map),
        compiler_params=_cparams("arbitrary", "arbitrary", "arbitrary"),
        name="attn_window",
    )(q, k, k, k, v, v, v, bias, sink)


def _attn_c_kernel(q_ref, k_ref, v_ref, near_ref, cfar_ref, kn_ref, lam_ref, gs_ref, o_ref,
                   m_ref, acct_ref, s_ref, *, n_iter, unroll, tks, nblk, out_scale):
    i = pl.program_id(2)
    rows = Q_PER_KV * BLOCK_Q
    blocks_per_sub = tks // BLOCK_Q
    blocks_per_iter = unroll * blocks_per_sub

    q = q_ref[...].reshape(rows, HEAD_DIM)
    lo = lax.broadcasted_iota(jnp.int32, q.shape, 1) < DIFF_DIM
    zero = jnp.zeros_like(q)
    q2 = jnp.concatenate([jnp.where(lo, q, zero), jnp.where(lo, zero, q)], axis=0)
    acct_ref[...] = jnp.zeros_like(acct_ref)
    s_ref[...] = _dot_nt(k_ref[0:tks, :], q2)
    refs = (q2, k_ref, v_ref, acct_ref, s_ref)

    c_left, c_right, c_max = cfar_ref[0], cfar_ref[1], cfar_ref[2]
    t_lo = jnp.maximum(i - 1, 0) // blocks_per_iter
    t_hi = jnp.minimum(i + 1, nblk - 1) // blocks_per_iter
    two = lambda x: jnp.concatenate([x, x], axis=1)
    shift = _score_bound(q2, kn_ref[:, :1]) + two(c_max)

    def add_bias(idx, s):
        blks = []
        for kb in range(blocks_per_sub):
            d = idx * blocks_per_sub + kb - i
            blk = jnp.broadcast_to(jnp.where(d < 0, c_left, c_right), (BLOCK_Q, rows))
            for e in range(NEAR // BLOCK_Q):
                blk = jnp.where(d == e - 1, near_ref[e], blk)
            blks.append(blk)
        bias = jnp.concatenate(blks, axis=0)
        return jnp.concatenate([s[:, :rows] + bias, s[:, rows:] + bias], axis=1)

    def iteration(t, last):
        is_near = (t >= t_lo) & (t <= t_hi)

        @pl.when(jnp.logical_not(is_near))
        def _():
            side = jnp.where(t < t_lo, c_left, c_right)
            _bounded_iteration(t, last, *refs, unroll=unroll, tks=tks, shift=shift - two(side))

        @pl.when(is_near)
        def _():
            _bounded_iteration(t, last, *refs, unroll=unroll, tks=tks, shift=shift, bias_fn=add_bias)

    def body(t, carry):
        iteration(t, False)
        return carry

    lax.fori_loop(0, n_iter - 1, body, 0)
    iteration(n_iter - 1, True)

    @pl.when(jnp.logical_not(_denominators_ok(acct_ref)))
    def _():
        _exact_pass(q2, k_ref, v_ref, m_ref, acct_ref, n_sub=n_iter * unroll, tks=tks, bias_fn=add_bias)

    acct = acct_ref[...]
    o1_t = acct[:HEAD_DIM, :rows] / acct[HEAD_DIM:HEAD_DIM + 1, :rows]
    o2_t = acct[:HEAD_DIM, rows:] / acct[HEAD_DIM:HEAD_DIM + 1, rows:]
    o = (o1_t - lam_ref[:, :1] * o2_t).T
    ms = jnp.mean(o * o, axis=-1, keepdims=True)
    o = o * lax.rsqrt(ms + EPS) * gs_ref[...] * out_scale
    o_ref[...] = o.reshape(o_ref.shape)


def _attn_c(q, k, v1t, near_t, cfar_t, key_norm, lam, g_subln, *, layer, batch, seq, tks, unroll):
    n = q.shape[1]
    nq = seq // BLOCK_Q
    rows = Q_PER_KV * BLOCK_Q
    lam_init = 0.8 - 0.6 * math.exp(-0.3 * layer)
    q_map = lambda b, g, i: (g, b * nq + i, 0)
    return pl.pallas_call(
        functools.partial(_attn_c_kernel, n_iter=seq // (tks * unroll), unroll=unroll, tks=tks,
                          nblk=nq, out_scale=1.0 - lam_init),
        out_shape=jax.ShapeDtypeStruct((N_Q_HEADS, n, HEAD_DIM), F32),
        grid=(batch, N_KV_HEADS, nq),
        in_specs=[
            pl.BlockSpec((Q_PER_KV, BLOCK_Q, HEAD_DIM), q_map),
            pl.BlockSpec((None, seq, HEAD_DIM), lambda b, g, i: (g, b, 0)),
            pl.BlockSpec((None, VT_ROWS, seq), lambda b, g, i: (g, 0, b)),
            pl.BlockSpec((None, NEAR // BLOCK_Q, BLOCK_Q, rows), lambda b, g, i: (g, 0, 0, 0)),
            pl.BlockSpec((3, None, 1, rows), lambda b, g, i: (0, g, 0, 0)),
            pl.BlockSpec((None, 1, HEAD_DIM), lambda b, g, i: (layer, 0, 0)),
            pl.BlockSpec((None, 1, HEAD_DIM), lambda b, g, i: (layer, 0, 0)),
            pl.BlockSpec((None, 1, HEAD_DIM), lambda b, g, i: (layer, 0, 0)),
        ],
        out_specs=pl.BlockSpec((Q_PER_KV, BLOCK_Q, HEAD_DIM), q_map),
        scratch_shapes=[pltpu.VMEM((1, 2 * rows), F32), pltpu.VMEM((VT_ROWS, 2 * rows), F32),
                        pltpu.VMEM((tks, 2 * rows), F32)],
        compiler_params=_cparams("arbitrary", "arbitrary", "arbitrary"),
        name="attn_diff",
    )(q, k, v1t, near_t, cfar_t, key_norm, lam, g_subln)


def _t5_bucket_np(rel):
    half = N_BUCKETS // 2
    max_exact = half // 2
    ret = np.where(rel > 0, half, 0)
    n = np.abs(rel)
    ratio = np.log(np.maximum(n, 1).astype(np.float32) / np.float32(max_exact)) / np.float32(
        math.log(MAX_DISTANCE / max_exact))
    large = max_exact + (ratio * np.float32(half - max_exact)).astype(np.int32)
    large = np.minimum(large, half - 1)
    return (ret + np.where(n < max_exact, n, large)).astype(np.int32)


def _near_bias(table):
    r = np.arange(BLOCK_Q)[:, None]
    c = np.arange(NEAR)[None, :]
    bucket = _t5_bucket_np(c - BLOCK_Q - r)
    onehot = (bucket.reshape(-1, 1) == np.arange(N_BUCKETS)[None, :]).astype(np.float32)
    rows = jnp.dot(jnp.asarray(onehot), table.astype(F32), precision=lax.Precision.HIGHEST)
    return rows.reshape(BLOCK_Q, NEAR, table.shape[1]).transpose(2, 0, 1)


def _rope_tables(seq):
    rows = seq // GRID_W
    row = jnp.repeat(jnp.arange(rows), GRID_W).astype(F32)
    col = jnp.tile(jnp.arange(GRID_W), rows).astype(F32)
    nfreq = HEAD_DIM // 4
    inv = ROPE_THETA ** (-jnp.arange(nfreq, dtype=F32) / nfreq)
    ang_r = row[:, None] * inv
    ang_c = col[:, None] * inv
    ang = jnp.concatenate([ang_r, ang_r, ang_c, ang_c], axis=-1)
    cos, sin = jnp.cos(ang), jnp.sin(ang)
    first = (np.arange(HEAD_DIM) % (HEAD_DIM // 2)) < HEAD_DIM // 4
    return cos, jnp.where(first, -sin, 0.0), jnp.where(first, 0.0, sin)


def _trunk(x, mod, p, *, batch, seq):
    t = _tiles(seq)
    tm = t["tm"]
    rope_tabs = _rope_tables(seq)
    for l in range(DEPTH):
        x = _ffn(x, mod, p["g_norm"], p["wgu"], p["wout"], layer=l, which=0, seq=seq, tm=tm)
        qa, ka, va, qb, kb, vb, qc, kc, vc = _qkv(x, mod, p["g_norm"], p["wqkv"], rope_tabs, p["gh"],
                                                  layer=l, seq=seq, tm=tm)
        oa = _attn_a(qa, ka, va, p["kn_a"], layer=l, batch=batch, seq=seq, tq=t["tq_a"], tks=t["tks"],
                     unroll=t["unroll"])
        ob = _attn_b(qb, kb, vb, p["bias_b"], p["sink"], layer=l, batch=batch, seq=seq)
        oc = _attn_c(qc, kc, vc, p["near_c"], p["cfar_c"], p["kn_c"], p["lam"], p["g_subln"],
                     layer=l, batch=batch, seq=seq, tks=t["tks"], unroll=t["unroll"])
        x = _out_proj(x, mod, p["g_norm"], oa, ob, oc, p["wgate"], p["wo"], layer=l, seq=seq, tm=tm)
        x = _ffn(x, mod, p["g_norm"], p["wgu"], p["wout"], layer=l, which=1, seq=seq, tm=tm)
    return x


def _prepare(w_ff_in, w_ff_out, w_in, w_o, g_qa, g_ka, g_qb, g_kb, g_qc, g_kc, sink,
             lam_q1, lam_k1, lam_q2, lam_k2, g_subln, rel_bias):
    wg = w_ff_in[..., :D_FF].reshape(DEPTH, 2, D_MODEL, N_FF_CHUNKS, FF_CHUNK)
    wu = w_ff_in[..., D_FF:].reshape(DEPTH, 2, D_MODEL, N_FF_CHUNKS, FF_CHUNK)
    wgu = jnp.concatenate([wg, wu], axis=-1).transpose(0, 1, 3, 2, 4).astype(BF16)
    wout = w_ff_out.reshape(DEPTH, 2, N_FF_CHUNKS, FF_CHUNK, D_MODEL).astype(BF16)
    zeros = jnp.zeros_like(g_qa)
    gh = jnp.stack([g_qa, g_ka, g_qb, g_kb, jnp.tile(g_qc, (1, 2)), jnp.tile(g_kc, (1, 2)), zeros, zeros],
                   axis=1).astype(F32)
    table_b, table_c = rel_bias[:, :N_Q_HEADS], rel_bias[:, N_Q_HEADS:]
    half = N_BUCKETS // 2
    c_left, c_right = table_c[half - 1] * LOG2E, table_c[N_BUCKETS - 1] * LOG2E
    lam_init = jnp.asarray([0.8 - 0.6 * math.exp(-0.3 * l) for l in range(DEPTH)], F32)
    lam = (jnp.exp(jnp.sum(lam_q1.astype(F32) * lam_k1.astype(F32), axis=-1))
           - jnp.exp(jnp.sum(lam_q2.astype(F32) * lam_k2.astype(F32), axis=-1)) + lam_init)

    def key_norm(g, dim):
        bound = 1.01 * math.sqrt(dim) * jnp.max(jnp.abs(g.astype(F32)), axis=-1)
        return jnp.broadcast_to(bound[:, None, None], (DEPTH, 1, HEAD_DIM))

    return dict(
        wgu=wgu, wout=wout,
        wqkv=w_in[:, :, :QKV_W].astype(BF16), wgate=w_in[:, :, QKV_W:].astype(BF16), wo=w_o.astype(BF16),
        gh=gh,
        bias_b=_near_bias(table_b),
        sink=jnp.broadcast_to(sink.astype(F32)[:, :, None, None], (DEPTH, N_Q_HEADS, 1, HEAD_DIM)),
        near_c=(_near_bias(table_c) * LOG2E).reshape(
            N_KV_HEADS, Q_PER_KV, BLOCK_Q, NEAR // BLOCK_Q, BLOCK_Q).transpose(0, 3, 4, 1, 2).reshape(
            N_KV_HEADS, NEAR // BLOCK_Q, BLOCK_Q, Q_PER_KV * BLOCK_Q),
        cfar_c=jnp.repeat(jnp.stack([c_left, c_right, jnp.max(table_c, axis=0) * LOG2E]).astype(F32),
                          BLOCK_Q, axis=1).reshape(3, N_KV_HEADS, 1, Q_PER_KV * BLOCK_Q),
        kn_a=key_norm(g_ka, HEAD_DIM), kn_c=key_norm(g_kc, DIFF_DIM),
        lam=jnp.broadcast_to(lam[:, None, None], (DEPTH, 1, HEAD_DIM)),
        g_subln=g_subln.astype(F32)[:, None, :],
    )


def kernel(x_prompt, x_sample, c_prompt, c_sample, w_ada, b_ada, g_norm, w_ff_in, w_ff_out, w_in, w_o,
           g_qa, g_ka, g_qb, g_kb, g_qc, g_kc, sink, lam_q1, lam_k1, lam_q2, lam_k2, g_subln, rel_bias):
    p = _prepare(w_ff_in, w_ff_out, w_in, w_o, g_qa, g_ka, g_qb, g_kb, g_qc, g_kc, sink,
                 lam_q1, lam_k1, lam_q2, lam_k2, g_subln, rel_bias)
    p["g_norm"] = g_norm.astype(F32)
    outs = []
    n_cond = 0
    conds = [c_prompt, c_sample]
    c_all = jnp.concatenate(conds + [jnp.zeros((ADA_ROWS - sum(c.shape[0] for c in conds), D_MODEL), F32)])
    mod_all = _ada(c_all, w_ada, b_ada)
    for x, c in ((x_prompt, c_prompt), (x_sample, c_sample)):
        batch, seq, _ = x.shape
        mod = mod_all[:, n_cond:n_cond + batch].reshape(DEPTH, batch, 9, D_MODEL)
        n_cond += batch
        y = _trunk(x.reshape(batch * seq, D_MODEL), mod, p, batch=batch, seq=seq)
        outs.append(y.reshape(batch, seq, D_MODEL))
    return tuple(outs)
```

```python
import functools
import math

import numpy as np
import jax
import jax.numpy as jnp
from jax import lax
from jax.experimental import pallas as pl
from jax.experimental.pallas import tpu as pltpu

F32 = jnp.float32
BF16 = jnp.bfloat16

D_MODEL = 1024
DEPTH = 4
N_Q_HEADS = 8
N_KV_HEADS = 2
Q_PER_KV = N_Q_HEADS // N_KV_HEADS
HEAD_DIM = 128
DIFF_DIM = 64
D_FF = 2816
BLOCK_Q = 128
WINDOW = 128
GRID_W = 64
N_BUCKETS = 32
MAX_DISTANCE = 128
ROPE_THETA = 10000.0
EPS = 1e-6
NEG = -1e30
LOG2E = 1.4426950408889634

Q_W = N_Q_HEADS * HEAD_DIM
KV_W = N_KV_HEADS * HEAD_DIM
BRANCH_IN = Q_W + 2 * KV_W
QKV_W = 3 * BRANCH_IN
FF_CHUNK = 256
N_FF_CHUNKS = D_FF // FF_CHUNK
ADA_ROWS = 16
ADA_TN = 1536
NEAR = 3 * BLOCK_Q
VT_ROWS = HEAD_DIM + 16

VMEM_LIMIT = 56 * 1024 * 1024


def _tiles(seq):
    tks = min(512, seq)
    return dict(
        tm=min(512, seq),
        tq_a=BLOCK_Q,
        tks=tks,
        unroll=min(8, seq // tks),
    )


def _cparams(*sem):
    return pltpu.CompilerParams(dimension_semantics=sem, vmem_limit_bytes=VMEM_LIMIT)


def _dot(a, b):
    return jnp.dot(a, b, preferred_element_type=F32)


def _dot_nt(a, b):
    return lax.dot_general(a, b, (((1,), (1,)), ((), ())), preferred_element_type=F32)


def _ada_kernel(c_ref, w_ref, b_ref, o_ref):
    c = c_ref[...]
    a = (c * jax.nn.sigmoid(c)).astype(BF16)
    o_ref[0] = _dot(a, w_ref[0].astype(BF16)) + b_ref[0]


def _ada(c_all, w_ada, b_ada):
    n_out = w_ada.shape[-1]
    return pl.pallas_call(
        _ada_kernel,
        out_shape=jax.ShapeDtypeStruct((DEPTH, ADA_ROWS, n_out), F32),
        grid=(DEPTH, n_out // ADA_TN),
        in_specs=[
            pl.BlockSpec((ADA_ROWS, D_MODEL), lambda l, j: (0, 0)),
            pl.BlockSpec((1, D_MODEL, ADA_TN), lambda l, j: (l, 0, j)),
            pl.BlockSpec((1, 1, ADA_TN), lambda l, j: (l, 0, j)),
        ],
        out_specs=pl.BlockSpec((1, ADA_ROWS, ADA_TN), lambda l, j: (l, 0, j)),
        compiler_params=_cparams("arbitrary", "arbitrary"),
        name="ada",
    )(c_all, w_ada, b_ada.reshape(DEPTH, 1, n_out))


def _modulate(x, mod_ref, g, jj):
    ms = jnp.mean(x * x, axis=-1, keepdims=True)
    y = x * lax.rsqrt(ms + EPS) * g
    return y * (1.0 + mod_ref[0, 3 * jj + 1:3 * jj + 2, :]) + mod_ref[0, 3 * jj:3 * jj + 1, :]


def _ffn_kernel(x_ref, mod_ref, g_ref, wgu_ref, wout_ref, o_ref, nb_ref, acc_ref, *, jj):
    x = x_ref[...]
    nb_ref[...] = _modulate(x, mod_ref, g_ref[jj:jj + 1, :], jj).astype(BF16)
    acc_ref[...] = jnp.zeros_like(acc_ref)

    def body(c, carry):
        h = _dot(nb_ref[...], wgu_ref[c])
        hg = h[:, :FF_CHUNK]
        a = (hg * jax.nn.sigmoid(hg)) * h[:, FF_CHUNK:]
        acc_ref[...] += _dot(a.astype(BF16), wout_ref[c])
        return carry

    lax.fori_loop(0, N_FF_CHUNKS, body, 0)
    o_ref[...] = x + (0.5 * mod_ref[0, 3 * jj + 2:3 * jj + 3, :]) * acc_ref[...]


def _ffn(x, mod, g_norm, wgu, wout, *, layer, which, seq, tm):
    n = x.shape[0]
    jj = 2 * which
    const = dict(pipeline_mode=pl.Buffered(1))
    return pl.pallas_call(
        functools.partial(_ffn_kernel, jj=jj),
        out_shape=jax.ShapeDtypeStruct((n, D_MODEL), F32),
        grid=(n // tm,),
        in_specs=[
            pl.BlockSpec((tm, D_MODEL), lambda i: (i, 0)),
            pl.BlockSpec((None, 1, 9, D_MODEL), lambda i: (layer, (i * tm) // seq, 0, 0)),
            pl.BlockSpec((None, 3, D_MODEL), lambda i: (layer, 0, 0)),
            pl.BlockSpec((None, None, N_FF_CHUNKS, D_MODEL, 2 * FF_CHUNK),
                         lambda i: (layer, which, 0, 0, 0), **const),
            pl.BlockSpec((None, None, N_FF_CHUNKS, FF_CHUNK, D_MODEL),
                         lambda i: (layer, which, 0, 0, 0), **const),
        ],
        out_specs=pl.BlockSpec((tm, D_MODEL), lambda i: (i, 0)),
        scratch_shapes=[pltpu.VMEM((tm, D_MODEL), BF16), pltpu.VMEM((tm, D_MODEL), F32)],
        input_output_aliases={0: 0},
        compiler_params=_cparams("arbitrary"),
        name=f"ffn{which}",
    )(x, mod, g_norm, wgu, wout)


def _head_norm(r, g):
    ms = jnp.mean(r * r, axis=-1, keepdims=True)
    return r * lax.rsqrt(ms + EPS) * g


def _half_norm(r, g2):
    sq = r * r
    lo = lax.broadcasted_iota(jnp.int32, r.shape, 1) < DIFF_DIM
    s_lo = jnp.sum(jnp.where(lo, sq, 0.0), axis=-1, keepdims=True)
    s_hi = jnp.sum(jnp.where(lo, 0.0, sq), axis=-1, keepdims=True)
    ms = jnp.where(lo, s_lo, s_hi) * (1.0 / DIFF_DIM)
    return r * lax.rsqrt(ms + EPS) * g2


def _qkv_kernel(x_ref, mod_ref, g_ref, w_ref, cos_ref, sa_ref, sb_ref, gh_ref,
                qa_ref, ka_ref, va_ref, qb_ref, kb_ref, vb_ref, qc_ref, kc_ref, vc_ref, nb_ref):
    nb_ref[...] = _modulate(x_ref[...], mod_ref, g_ref[1:2, :], 1).astype(BF16)
    cos, sa, sb = cos_ref[...], sa_ref[...], sb_ref[...]

    def rope(r):
        return (r * cos + pltpu.roll(r, HEAD_DIM - 32, 1) * sa + pltpu.roll(r, 32, 1) * sb)

    def proj(col, width):
        return _dot(nb_ref[...], w_ref[:, col:col + width])

    def heads(r, n_heads, fn, out_ref):
        for h in range(n_heads):
            out_ref[h] = fn(r[:, h * HEAD_DIM:(h + 1) * HEAD_DIM]).astype(BF16)

    def values_transposed(r, out_ref):
        for h in range(N_KV_HEADS):
            out_ref[h, :HEAD_DIM, :] = r[:, h * HEAD_DIM:(h + 1) * HEAD_DIM].T.astype(BF16)
            out_ref[h, HEAD_DIM:, :] = jnp.ones((VT_ROWS - HEAD_DIM, r.shape[0]), BF16)

    qa_scale = HEAD_DIM ** -0.5 * LOG2E
    qc_scale = DIFF_DIM ** -0.5 * LOG2E
    g_qa, g_ka, g_qb, g_kb = (gh_ref[i:i + 1, :] for i in range(4))
    g_qc, g_kc = gh_ref[4:5, :], gh_ref[5:6, :]
    ident = lambda r: r
    col = 0
    heads(proj(col, Q_W), N_Q_HEADS, lambda r: rope(_head_norm(r, g_qa)) * qa_scale, qa_ref)
    col += Q_W
    heads(proj(col, KV_W), N_KV_HEADS, lambda r: rope(_head_norm(r, g_ka)), ka_ref)
    col += KV_W
    values_transposed(proj(col, KV_W),va_ref)
    col += KV_W
    heads(proj(col, Q_W), N_Q_HEADS, lambda r: _head_norm(r, g_qb), qb_ref)
    col += Q_W
    heads(proj(col, KV_W), N_KV_HEADS, lambda r: _head_norm(r, g_kb), kb_ref)
    col += KV_W
    heads(proj(col, KV_W), N_KV_HEADS, ident, vb_ref)
    col += KV_W
    heads(proj(col, Q_W), N_Q_HEADS, lambda r: _half_norm(r, g_qc) * qc_scale, qc_ref)
    col += Q_W
    heads(proj(col, KV_W), N_KV_HEADS, lambda r: _half_norm(r, g_kc), kc_ref)
    col += KV_W
    values_transposed(proj(col, KV_W),vc_ref)


def _qkv(x, mod, g_norm, wqkv, rope_tabs, gh, *, layer, seq, tm):
    n = x.shape[0]
    tiles_per_seq = seq // tm
    q_shape = jax.ShapeDtypeStruct((N_Q_HEADS, n, HEAD_DIM), BF16)
    kv_shape = jax.ShapeDtypeStruct((N_KV_HEADS, n, HEAD_DIM), BF16)
    q_spec = pl.BlockSpec((N_Q_HEADS, tm, HEAD_DIM), lambda i: (0, i, 0))
    kv_spec = pl.BlockSpec((N_KV_HEADS, tm, HEAD_DIM), lambda i: (0, i, 0))
    tab_spec = pl.BlockSpec((tm, HEAD_DIM), lambda i: (i % tiles_per_seq, 0))
    v1_shape = jax.ShapeDtypeStruct((N_KV_HEADS, VT_ROWS, n), BF16)
    v1_spec = pl.BlockSpec((N_KV_HEADS, VT_ROWS, tm), lambda i: (0, 0, i))
    return pl.pallas_call(
        _qkv_kernel,
        out_shape=[q_shape, kv_shape, v1_shape, q_shape, kv_shape, kv_shape, q_shape, kv_shape, v1_shape],
        grid=(n // tm,),
        in_specs=[
            pl.BlockSpec((tm, D_MODEL), lambda i: (i, 0)),
            pl.BlockSpec((None, 1, 9, D_MODEL), lambda i: (layer, (i * tm) // seq, 0, 0)),
            pl.BlockSpec((None, 3, D_MODEL), lambda i: (layer, 0, 0)),
            pl.BlockSpec((None, D_MODEL, QKV_W), lambda i: (layer, 0, 0), pipeline_mode=pl.Buffered(1)),
            tab_spec, tab_spec, tab_spec,
            pl.BlockSpec((None, 8, HEAD_DIM), lambda i: (layer, 0, 0)),
        ],
        out_specs=[q_spec, kv_spec, v1_spec, q_spec, kv_spec, kv_spec, q_spec, kv_spec, v1_spec],
        scratch_shapes=[pltpu.VMEM((tm, D_MODEL), BF16)],
        compiler_params=_cparams("arbitrary"),
        name="qkv",
    )(x, mod, g_norm, wqkv, *rope_tabs, gh)


def _out_kernel(x_ref, mod_ref, g_ref, oa_ref, ob_ref, oc_ref, wg_ref, wo_ref, o_ref, mg_ref):
    x = x_ref[...]
    nb = _modulate(x, mod_ref, g_ref[1:2, :], 1).astype(BF16)
    for br, br_ref in enumerate((oa_ref, ob_ref, oc_ref)):
        gate = jax.nn.sigmoid(_dot(nb, wg_ref[:, br * D_MODEL:(br + 1) * D_MODEL]))
        for h in range(N_Q_HEADS):
            lanes = slice(h * HEAD_DIM, (h + 1) * HEAD_DIM)
            term = gate[:, lanes] * br_ref[h]
            if br == 0:
                mg_ref[:, lanes] = term
            else:
                mg_ref[:, lanes] += term
    o_ref[...] = x + mod_ref[0, 5:6, :] * _dot(mg_ref[...].astype(BF16), wo_ref[...])


def _out_proj(x, mod, g_norm, oa, ob, oc, wgate, wo, *, layer, seq, tm):
    n = x.shape[0]
    o_spec = pl.BlockSpec((N_Q_HEADS, tm, HEAD_DIM), lambda i: (0, i, 0))
    return pl.pallas_call(
        _out_kernel,
        out_shape=jax.ShapeDtypeStruct((n, D_MODEL), F32),
        grid=(n // tm,),
        in_specs=[
            pl.BlockSpec((tm, D_MODEL), lambda i: (i, 0)),
            pl.BlockSpec((None, 1, 9, D_MODEL), lambda i: (layer, (i * tm) // seq, 0, 0)),
            pl.BlockSpec((None, 3, D_MODEL), lambda i: (layer, 0, 0)),
            o_spec, o_spec, o_spec,
            pl.BlockSpec((None, D_MODEL, 3 * D_MODEL), lambda i: (layer, 0, 0), pipeline_mode=pl.Buffered(1)),
            pl.BlockSpec((None, D_MODEL, D_MODEL), lambda i: (layer, 0, 0), pipeline_mode=pl.Buffered(1)),
        ],
        out_specs=pl.BlockSpec((tm, D_MODEL), lambda i: (i, 0)),
        scratch_shapes=[pltpu.VMEM((tm, D_MODEL), F32)],
        input_output_aliases={0: 0},
        compiler_params=_cparams("arbitrary"),
        name="out_proj",
    )(x, mod, g_norm, oa, ob, oc, wgate, wo)


MIN_DENOM = 2.0 ** -64


def _key_slice(idx, tks):
    return pl.ds(pl.multiple_of(idx * tks, tks), tks)


def _score_bound(q, key_norm):
    qf = q.astype(F32)
    qn2 = _dot_nt(jnp.ones((8, q.shape[1]), F32), qf * qf)[:1]
    return jnp.sqrt(qn2) * key_norm


def _bounded_iteration(t, last, q, k_ref, v_ref, acct_ref, s_ref, *, unroll, tks, shift, bias_fn=None):
    acct = acct_ref[...]
    s = s_ref[...]
    for u in range(unroll):
        idx = t * unroll + u
        s_next = None if (last and u == unroll - 1) else _dot_nt(k_ref[_key_slice(idx + 1, tks), :], q)
        if bias_fn is not None:
            s = bias_fn(idx, s)
        acct = acct + _dot(v_ref[:, _key_slice(idx, tks)], jnp.exp2(s - shift).astype(BF16))
        s = s_next
    acct_ref[...] = acct
    if not last:
        s_ref[...] = s


def _exact_pass(q, k_ref, v_ref, m_ref, acct_ref, *, n_sub, tks, bias_fn=None):
    m_ref[...] = jnp.full_like(m_ref, -jnp.inf)
    acct_ref[...] = jnp.zeros_like(acct_ref)

    def body(idx, carry):
        s = _dot_nt(k_ref[_key_slice(idx, tks), :], q)
        if bias_fn is not None:
            s = bias_fn(idx, s)
        m = m_ref[...]
        m_new = jnp.maximum(m, jnp.max(s, axis=0, keepdims=True))
        p = jnp.exp2(s - m_new).astype(BF16)
        acct_ref[...] = jnp.exp2(m - m_new) * acct_ref[...] + _dot(v_ref[:, _key_slice(idx, tks)], p)
        m_ref[...] = m_new
        return carry

    lax.fori_loop(0, n_sub, body, 0)


def _denominators_ok(acct_ref):
    return jnp.min(acct_ref[HEAD_DIM:HEAD_DIM + 1, :]) >= MIN_DENOM


def _attn_a_kernel(q_ref, k_ref, v_ref, kn_ref, o_ref, m_ref, acct_ref, s_ref, *, n_iter, unroll, tks):
    rows = acct_ref.shape[1]
    q = q_ref[...].reshape(rows, HEAD_DIM)
    shift = _score_bound(q, kn_ref[:, :1])
    acct_ref[...] = jnp.zeros_like(acct_ref)
    s_ref[...] = _dot_nt(k_ref[0:tks, :], q)
    refs = (q, k_ref, v_ref, acct_ref, s_ref)

    def body(t, carry):
        _bounded_iteration(t, False, *refs, unroll=unroll, tks=tks, shift=shift)
        return carry

    lax.fori_loop(0, n_iter - 1, body, 0)
    _bounded_iteration(n_iter - 1, True, *refs, unroll=unroll, tks=tks, shift=shift)

    @pl.when(jnp.logical_not(_denominators_ok(acct_ref)))
    def _():
        _exact_pass(q, k_ref, v_ref, m_ref, acct_ref, n_sub=n_iter * unroll, tks=tks)

    acct = acct_ref[...]
    o_t = acct[:HEAD_DIM] / acct[HEAD_DIM:HEAD_DIM + 1]
    o_ref[...] = o_t.T.reshape(o_ref.shape)


def _attn_a(q, k, v1t, key_norm, *, layer, batch, seq, tq, tks, unroll):
    n = q.shape[1]
    nq = seq // tq
    rows = Q_PER_KV * tq
    q_map = lambda b, g, i: (g, b * nq + i, 0)
    return pl.pallas_call(
        functools.partial(_attn_a_kernel, n_iter=seq // (tks * unroll), unroll=unroll, tks=tks),
        out_shape=jax.ShapeDtypeStruct((N_Q_HEADS, n, HEAD_DIM), F32),
        grid=(batch, N_KV_HEADS, nq),
        in_specs=[
            pl.BlockSpec((Q_PER_KV, tq, HEAD_DIM), q_map),
            pl.BlockSpec((None, seq, HEAD_DIM), lambda b, g, i: (g, b, 0)),
            pl.BlockSpec((None, VT_ROWS, seq), lambda b, g, i: (g, 0, b)),
            pl.BlockSpec((None, 1, HEAD_DIM), lambda b, g, i: (layer, 0, 0)),
        ],
        out_specs=pl.BlockSpec((Q_PER_KV, tq, HEAD_DIM), q_map),
        scratch_shapes=[pltpu.VMEM((1, rows), F32), pltpu.VMEM((VT_ROWS, rows), F32),
                        pltpu.VMEM((tks, rows), F32)],
        compiler_params=_cparams("arbitrary", "arbitrary", "arbitrary"),
        name="attn_axial",
    )(q, k, v1t, key_norm)


def _attn_b_kernel(q_ref, kp_ref, kc_ref, kn_ref, vp_ref, vc_ref, vn_ref, bias_ref, sink_ref, o_ref, *, nb):
    i = pl.program_id(2)
    rows = Q_PER_KV * BLOCK_Q
    q = q_ref[...].reshape(rows, HEAD_DIM)
    kcat = jnp.concatenate([kp_ref[...], kc_ref[...], kn_ref[...]], axis=0)
    vcat = jnp.concatenate([vp_ref[...], vc_ref[...], vn_ref[...]], axis=0)
    s = _dot_nt(q, kcat) * (HEAD_DIM ** -0.5)
    s = s.reshape(Q_PER_KV, BLOCK_Q, NEAR) + bias_ref[...]
    r = lax.broadcasted_iota(jnp.int32, (BLOCK_Q, NEAR), 0)
    c = lax.broadcasted_iota(jnp.int32, (BLOCK_Q, NEAR), 1)
    kpos = (i - 1) * BLOCK_Q + c
    allowed = (jnp.abs(c - BLOCK_Q - r) <= WINDOW) & (kpos >= 0) & (kpos < nb * BLOCK_Q)
    s = jnp.where(allowed[None], s, NEG)
    sink = sink_ref[...][:, :, :1]
    m = jnp.maximum(jnp.max(s, axis=-1, keepdims=True), sink)
    e = jnp.exp(s - m)
    den = jnp.sum(e, axis=-1, keepdims=True) + jnp.exp(sink - m)
    p = (e * (1.0 / den)).reshape(rows, NEAR).astype(BF16)
    o_ref[...] = _dot(p, vcat).reshape(o_ref.shape)


def _attn_b(q, k, v, bias, sink, *, layer, batch, seq):
    n = q.shape[1]
    nb = seq // BLOCK_Q
    blk = (None, BLOCK_Q, HEAD_DIM)
    prev_map = lambda b, g, i: (g, b * nb + jnp.maximum(i - 1, 0), 0)
    cur_map = lambda b, g, i: (g, b * nb + i, 0)
    next_map = lambda b, g, i: (g, b * nb + jnp.minimum(i + 1, nb - 1), 0)
    kv_specs = [pl.BlockSpec(blk, prev_map), pl.BlockSpec(blk, cur_map), pl.BlockSpec(blk, next_map)]
    return pl.pallas_call(
        functools.partial(_attn_b_kernel, nb=nb),
        out_shape=jax.ShapeDtypeStruct((N_Q_HEADS, n, HEAD_DIM), F32),
        grid=(batch, N_KV_HEADS, nb),
        in_specs=[pl.BlockSpec((Q_PER_KV, BLOCK_Q, HEAD_DIM), cur_map)] + kv_specs + kv_specs + [
            pl.BlockSpec((Q_PER_KV, BLOCK_Q, NEAR), lambda b, g, i: (g, 0, 0)),
            pl.BlockSpec((None, Q_PER_KV, 1, HEAD_DIM), lambda b, g, i: (layer, g, 0, 0)),
        ],
        out_specs=pl.BlockSpec((Q_PER_KV, BLOCK_Q, HEAD_DIM), cur_map),
        compiler_params=_cparams("arbitrary", "arbitrary", "arbitrary"),
        name="attn_window",
    )(q, k, k, k, v, v, v, bias, sink)


def _attn_c_kernel(q_ref, k_ref, v_ref, bias_ref, cfar_ref, kn_ref, lam_ref, gs_ref, o_ref,
                   m_ref, acct_ref, s_ref, *, n_iter, unroll, tks, nblk, out_scale):
    i = pl.program_id(2)
    rows = Q_PER_KV * BLOCK_Q
    blocks_per_sub = tks // BLOCK_Q
    blocks_per_iter = unroll * blocks_per_sub

    q = q_ref[...].reshape(rows, HEAD_DIM)
    lo = lax.broadcasted_iota(jnp.int32, q.shape, 1) < DIFF_DIM
    zero = jnp.zeros_like(q)
    q2 = jnp.concatenate([jnp.where(lo, q, zero), jnp.where(lo, zero, q)], axis=0)
    acct_ref[...] = jnp.zeros_like(acct_ref)
    s_ref[...] = _dot_nt(k_ref[0:tks, :], q2)
    refs = (q2, k_ref, v_ref, acct_ref, s_ref)

    c_left, c_right, c_max = cfar_ref[0], cfar_ref[1], cfar_ref[2]
    t_lo = jnp.maximum(i - 1, 0) // blocks_per_iter
    t_hi = jnp.minimum(i + 1, nblk - 1) // blocks_per_iter
    two = lambda x: jnp.concatenate([x, x], axis=1)
    shift = _score_bound(q2, kn_ref[:, :1]) + two(c_max)

    def add_bias(idx, s):
        blks = []
        for kb in range(blocks_per_sub):
            d = idx * blocks_per_sub + kb - i
            blks.append(bias_ref[jnp.clip(d + 2, 0, NEAR // BLOCK_Q + 1)])
        bias = jnp.concatenate(blks, axis=0)
        return jnp.concatenate([s[:, :rows] + bias, s[:, rows:] + bias], axis=1)

    def iteration(t, last):
        is_near = (t >= t_lo) & (t <= t_hi)

        @pl.when(jnp.logical_not(is_near))
        def _():
            side = jnp.where(t < t_lo, c_left, c_right)
            _bounded_iteration(t, last, *refs, unroll=unroll, tks=tks, shift=shift - two(side))

        @pl.when(is_near)
        def _():
            _bounded_iteration(t, last, *refs, unroll=unroll, tks=tks, shift=shift, bias_fn=add_bias)

    def body(t, carry):
        iteration(t, False)
        return carry

    lax.fori_loop(0, n_iter - 1, body, 0)
    iteration(n_iter - 1, True)

    @pl.when(jnp.logical_not(_denominators_ok(acct_ref)))
    def _():
        _exact_pass(q2, k_ref, v_ref, m_ref, acct_ref, n_sub=n_iter * unroll, tks=tks, bias_fn=add_bias)

    acct = acct_ref[...]
    o1_t = acct[:HEAD_DIM, :rows] / acct[HEAD_DIM:HEAD_DIM + 1, :rows]
    o2_t = acct[:HEAD_DIM, rows:] / acct[HEAD_DIM:HEAD_DIM + 1, rows:]
    o = (o1_t - lam_ref[:, :1] * o2_t).T
    ms = jnp.mean(o * o, axis=-1, keepdims=True)
    o = o * lax.rsqrt(ms + EPS) * gs_ref[...] * out_scale
    o_ref[...] = o.reshape(o_ref.shape)


def _attn_c(q, k, v1t, bias_t, cfar_t, key_norm, lam, g_subln, *, layer, batch, seq, tks, unroll):
    n = q.shape[1]
    nq = seq // BLOCK_Q
    rows = Q_PER_KV * BLOCK_Q
    lam_init = 0.8 - 0.6 * math.exp(-0.3 * layer)
    q_map = lambda b, g, i: (g, b * nq + i, 0)
    return pl.pallas_call(
        functools.partial(_attn_c_kernel, n_iter=seq // (tks * unroll), unroll=unroll, tks=tks,
                          nblk=nq, out_scale=1.0 - lam_init),
        out_shape=jax.ShapeDtypeStruct((N_Q_HEADS, n, HEAD_DIM), F32),
        grid=(batch, N_KV_HEADS, nq),
        in_specs=[
            pl.BlockSpec((Q_PER_KV, BLOCK_Q, HEAD_DIM), q_map),
            pl.BlockSpec((None, seq, HEAD_DIM), lambda b, g, i: (g, b, 0)),
            pl.BlockSpec((None, VT_ROWS, seq), lambda b, g, i: (g, 0, b)),
            pl.BlockSpec((None, NEAR // BLOCK_Q + 2, BLOCK_Q, rows), lambda b, g, i: (g, 0, 0, 0)),
            pl.BlockSpec((3, None, 1, rows), lambda b, g, i: (0, g, 0, 0)),
            pl.BlockSpec((None, 1, HEAD_DIM), lambda b, g, i: (layer, 0, 0)),
            pl.BlockSpec((None, 1, HEAD_DIM), lambda b, g, i: (layer, 0, 0)),
            pl.BlockSpec((None, 1, HEAD_DIM), lambda b, g, i: (layer, 0, 0)),
        ],
        out_specs=pl.BlockSpec((Q_PER_KV, BLOCK_Q, HEAD_DIM), q_map),
        scratch_shapes=[pltpu.VMEM((1, 2 * rows), F32), pltpu.VMEM((VT_ROWS, 2 * rows), F32),
                        pltpu.VMEM((tks, 2 * rows), F32)],
        compiler_params=_cparams("arbitrary", "arbitrary", "arbitrary"),
        name="attn_diff",
    )(q, k, v1t, bias_t, cfar_t, key_norm, lam, g_subln)


def _t5_bucket_np(rel):
    half = N_BUCKETS // 2
    max_exact = half // 2
    ret = np.where(rel > 0, half, 0)
    n = np.abs(rel)
    ratio = np.log(np.maximum(n, 1).astype(np.float32) / np.float32(max_exact)) / np.float32(
        math.log(MAX_DISTANCE / max_exact))
    large = max_exact + (ratio * np.float32(half - max_exact)).astype(np.int32)
    large = np.minimum(large, half - 1)
    return (ret + np.where(n < max_exact, n, large)).astype(np.int32)


def _near_bias(table):
    r = np.arange(BLOCK_Q)[:, None]
    c = np.arange(NEAR)[None, :]
    bucket = _t5_bucket_np(c - BLOCK_Q - r)
    onehot = (bucket.reshape(-1, 1) == np.arange(N_BUCKETS)[None, :]).astype(np.float32)
    rows = jnp.dot(jnp.asarray(onehot), table.astype(F32), precision=lax.Precision.HIGHEST)
    return rows.reshape(BLOCK_Q, NEAR, table.shape[1]).transpose(2, 0, 1)


def _rope_tables(seq):
    rows = seq // GRID_W
    row = jnp.repeat(jnp.arange(rows), GRID_W).astype(F32)
    col = jnp.tile(jnp.arange(GRID_W), rows).astype(F32)
    nfreq = HEAD_DIM // 4
    inv = ROPE_THETA ** (-jnp.arange(nfreq, dtype=F32) / nfreq)
    ang_r = row[:, None] * inv
    ang_c = col[:, None] * inv
    ang = jnp.concatenate([ang_r, ang_r, ang_c, ang_c], axis=-1)
    cos, sin = jnp.cos(ang), jnp.sin(ang)
    first = (np.arange(HEAD_DIM) % (HEAD_DIM // 2)) < HEAD_DIM // 4
    return cos, jnp.where(first, -sin, 0.0), jnp.where(first, 0.0, sin)


def _trunk(x, mod, p, *, batch, seq):
    t = _tiles(seq)
    tm = t["tm"]
    rope_tabs = _rope_tables(seq)
    for l in range(DEPTH):
        x = _ffn(x, mod, p["g_norm"], p["wgu"], p["wout"], layer=l, which=0, seq=seq, tm=tm)
        qa, ka, va, qb, kb, vb, qc, kc, vc = _qkv(x, mod, p["g_norm"], p["wqkv"], rope_tabs, p["gh"],
                                                  layer=l, seq=seq, tm=tm)
        oa = _attn_a(qa, ka, va, p["kn_a"], layer=l, batch=batch, seq=seq, tq=t["tq_a"], tks=t["tks"],
                     unroll=t["unroll"])
        ob = _attn_b(qb, kb, vb, p["bias_b"], p["sink"], layer=l, batch=batch, seq=seq)
        oc = _attn_c(qc, kc, vc, p["bias_c"], p["cfar_c"], p["kn_c"], p["lam"], p["g_subln"],
                     layer=l, batch=batch, seq=seq, tks=t["tks"], unroll=t["unroll"])
        x = _out_proj(x, mod, p["g_norm"], oa, ob, oc, p["wgate"], p["wo"], layer=l, seq=seq, tm=tm)
        x = _ffn(x, mod, p["g_norm"], p["wgu"], p["wout"], layer=l, which=1, seq=seq, tm=tm)
    return x


def _prepare(w_ff_in, w_ff_out, w_in, w_o, g_qa, g_ka, g_qb, g_kb, g_qc, g_kc, sink,
             lam_q1, lam_k1, lam_q2, lam_k2, g_subln, rel_bias):
    wg = w_ff_in[..., :D_FF].reshape(DEPTH, 2, D_MODEL, N_FF_CHUNKS, FF_CHUNK)
    wu = w_ff_in[..., D_FF:].reshape(DEPTH, 2, D_MODEL, N_FF_CHUNKS, FF_CHUNK)
    wgu = jnp.concatenate([wg, wu], axis=-1).transpose(0, 1, 3, 2, 4).astype(BF16)
    wout = w_ff_out.reshape(DEPTH, 2, N_FF_CHUNKS, FF_CHUNK, D_MODEL).astype(BF16)
    zeros = jnp.zeros_like(g_qa)
    gh = jnp.stack([g_qa, g_ka, g_qb, g_kb, jnp.tile(g_qc, (1, 2)), jnp.tile(g_kc, (1, 2)), zeros, zeros],
                   axis=1).astype(F32)
    table_b, table_c = rel_bias[:, :N_Q_HEADS], rel_bias[:, N_Q_HEADS:]
    half = N_BUCKETS // 2
    c_left, c_right = table_c[half - 1] * LOG2E, table_c[N_BUCKETS - 1] * LOG2E
    lam_init = jnp.asarray([0.8 - 0.6 * math.exp(-0.3 * l) for l in range(DEPTH)], F32)
    lam = (jnp.exp(jnp.sum(lam_q1.astype(F32) * lam_k1.astype(F32), axis=-1))
           - jnp.exp(jnp.sum(lam_q2.astype(F32) * lam_k2.astype(F32), axis=-1)) + lam_init)

    near_c = (_near_bias(table_c) * LOG2E).reshape(
        N_KV_HEADS, Q_PER_KV, BLOCK_Q, NEAR // BLOCK_Q, BLOCK_Q).transpose(0, 3, 4, 1, 2).reshape(
        N_KV_HEADS, NEAR // BLOCK_Q, BLOCK_Q, Q_PER_KV * BLOCK_Q)
    cfar_c = jnp.repeat(jnp.stack([c_left, c_right, jnp.max(table_c, axis=0) * LOG2E]).astype(F32),
                        BLOCK_Q, axis=1).reshape(3, N_KV_HEADS, 1, Q_PER_KV * BLOCK_Q)

    def key_norm(g, dim):
        bound = 1.01 * math.sqrt(dim) * jnp.max(jnp.abs(g.astype(F32)), axis=-1)
        return jnp.broadcast_to(bound[:, None, None], (DEPTH, 1, HEAD_DIM))

    return dict(
        wgu=wgu, wout=wout,
        wqkv=w_in[:, :, :QKV_W].astype(BF16), wgate=w_in[:, :, QKV_W:].astype(BF16), wo=w_o.astype(BF16),
        gh=gh,
        bias_b=_near_bias(table_b),
        sink=jnp.broadcast_to(sink.astype(F32)[:, :, None, None], (DEPTH, N_Q_HEADS, 1, HEAD_DIM)),
        bias_c=jnp.concatenate([cfar_c[0][:, None] + jnp.zeros((1, 1, BLOCK_Q, 1), F32), near_c,
                                cfar_c[1][:, None] + jnp.zeros((1, 1, BLOCK_Q, 1), F32)], axis=1),
        cfar_c=cfar_c,
        kn_a=key_norm(g_ka, HEAD_DIM), kn_c=key_norm(g_kc, DIFF_DIM),
        lam=jnp.broadcast_to(lam[:, None, None], (DEPTH, 1, HEAD_DIM)),
        g_subln=g_subln.astype(F32)[:, None, :],
    )


def kernel(x_prompt, x_sample, c_prompt, c_sample, w_ada, b_ada, g_norm, w_ff_in, w_ff_out, w_in, w_o,
           g_qa, g_ka, g_qb, g_kb, g_qc, g_kc, sink, lam_q1, lam_k1, lam_q2, lam_k2, g_subln, rel_bias):
    p = _prepare(w_ff_in, w_ff_out, w_in, w_o, g_qa, g_ka, g_qb, g_kb, g_qc, g_kc, sink,
                 lam_q1, lam_k1, lam_q2, lam_k2, g_subln, rel_bias)
    p["g_norm"] = g_norm.astype(F32)
    outs = []
    n_cond = 0
    conds = [c_prompt, c_sample]
    c_all = jnp.concatenate(conds + [jnp.zeros((ADA_ROWS - sum(c.shape[0] for c in conds), D_MODEL), F32)])
    mod_all = _ada(c_all, w_ada, b_ada)
    for x, c in ((x_prompt, c_prompt), (x_sample, c_sample)):
        batch, seq, _ = x.shape
        mod = mod_all[:, n_cond:n_cond + batch].reshape(DEPTH, batch, 9, D_MODEL)
        n_cond += batch
        y = _trunk(x.reshape(batch * seq, D_MODEL), mod, p, batch=batch, seq=seq)
        outs.append(y.reshape(batch, seq, D_MODEL))
    return tuple(outs)
```

```python
import functools
import math

import numpy as np
import jax
import jax.numpy as jnp
from jax import lax
from jax.experimental import pallas as pl
from jax.experimental.pallas import tpu as pltpu

F32 = jnp.float32
BF16 = jnp.bfloat16

D_MODEL = 1024
DEPTH = 4
N_Q_HEADS = 8
N_KV_HEADS = 2
Q_PER_KV = N_Q_HEADS // N_KV_HEADS
HEAD_DIM = 128
DIFF_DIM = 64
D_FF = 2816
BLOCK_Q = 128
WINDOW = 128
GRID_W = 64
N_BUCKETS = 32
MAX_DISTANCE = 128
ROPE_THETA = 10000.0
EPS = 1e-6
NEG = -1e30
LOG2E = 1.4426950408889634

Q_W = N_Q_HEADS * HEAD_DIM
KV_W = N_KV_HEADS * HEAD_DIM
BRANCH_IN = Q_W + 2 * KV_W
QKV_W = 3 * BRANCH_IN
FF_CHUNK = 256
N_FF_CHUNKS = D_FF // FF_CHUNK
ADA_ROWS = 16
ADA_TN = 1536
NEAR = 3 * BLOCK_Q
VT_ROWS = HEAD_DIM + 16

VMEM_LIMIT = 56 * 1024 * 1024


def _tiles(seq):
    tks = min(512, seq)
    return dict(
        tm=min(512, seq),
        tq_a=BLOCK_Q,
        tks=tks,
        qb=min(4, seq // BLOCK_Q),
        unroll=min(8, seq // tks),
    )


def _cparams(*sem):
    return pltpu.CompilerParams(dimension_semantics=sem, vmem_limit_bytes=VMEM_LIMIT)


def _dot(a, b):
    return jnp.dot(a, b, preferred_element_type=F32)


def _dot_nt(a, b):
    return lax.dot_general(a, b, (((1,), (1,)), ((), ())), preferred_element_type=F32)


def _ada_kernel(c_ref, w_ref, b_ref, o_ref):
    c = c_ref[...]
    a = (c * jax.nn.sigmoid(c)).astype(BF16)
    o_ref[0] = _dot(a, w_ref[0].astype(BF16)) + b_ref[0]


def _ada(c_all, w_ada, b_ada):
    n_out = w_ada.shape[-1]
    return pl.pallas_call(
        _ada_kernel,
        out_shape=jax.ShapeDtypeStruct((DEPTH, ADA_ROWS, n_out), F32),
        grid=(DEPTH, n_out // ADA_TN),
        in_specs=[
            pl.BlockSpec((ADA_ROWS, D_MODEL), lambda l, j: (0, 0)),
            pl.BlockSpec((1, D_MODEL, ADA_TN), lambda l, j: (l, 0, j)),
            pl.BlockSpec((1, 1, ADA_TN), lambda l, j: (l, 0, j)),
        ],
        out_specs=pl.BlockSpec((1, ADA_ROWS, ADA_TN), lambda l, j: (l, 0, j)),
        compiler_params=_cparams("arbitrary", "arbitrary"),
        name="ada",
    )(c_all, w_ada, b_ada.reshape(DEPTH, 1, n_out))


def _modulate(x, mod_ref, g, jj):
    ms = jnp.mean(x * x, axis=-1, keepdims=True)
    y = x * lax.rsqrt(ms + EPS) * g
    return y * (1.0 + mod_ref[0, 3 * jj + 1:3 * jj + 2, :]) + mod_ref[0, 3 * jj:3 * jj + 1, :]


def _ffn_kernel(x_ref, mod_ref, g_ref, wgu_ref, wout_ref, o_ref, nb_ref, acc_ref, *, jj):
    x = x_ref[...]
    nb_ref[...] = _modulate(x, mod_ref, g_ref[jj:jj + 1, :], jj).astype(BF16)
    acc_ref[...] = jnp.zeros_like(acc_ref)

    def body(c, carry):
        h = _dot(nb_ref[...], wgu_ref[c])
        hg = h[:, :FF_CHUNK]
        a = (hg * jax.nn.sigmoid(hg)) * h[:, FF_CHUNK:]
        acc_ref[...] += _dot(a.astype(BF16), wout_ref[c])
        return carry

    lax.fori_loop(0, N_FF_CHUNKS, body, 0, unroll=True)
    o_ref[...] = x + (0.5 * mod_ref[0, 3 * jj + 2:3 * jj + 3, :]) * acc_ref[...]


def _ffn(x, mod, g_norm, wgu, wout, *, layer, which, seq, tm):
    n = x.shape[0]
    jj = 2 * which
    const = dict(pipeline_mode=pl.Buffered(1))
    return pl.pallas_call(
        functools.partial(_ffn_kernel, jj=jj),
        out_shape=jax.ShapeDtypeStruct((n, D_MODEL), F32),
        grid=(n // tm,),
        in_specs=[
            pl.BlockSpec((tm, D_MODEL), lambda i: (i, 0)),
            pl.BlockSpec((None, 1, 9, D_MODEL), lambda i: (layer, (i * tm) // seq, 0, 0)),
            pl.BlockSpec((None, 3, D_MODEL), lambda i: (layer, 0, 0)),
            pl.BlockSpec((None, None, N_FF_CHUNKS, D_MODEL, 2 * FF_CHUNK),
                         lambda i: (layer, which, 0, 0, 0), **const),
            pl.BlockSpec((None, None, N_FF_CHUNKS, FF_CHUNK, D_MODEL),
                         lambda i: (layer, which, 0, 0, 0), **const),
        ],
        out_specs=pl.BlockSpec((tm, D_MODEL), lambda i: (i, 0)),
        scratch_shapes=[pltpu.VMEM((tm, D_MODEL), BF16), pltpu.VMEM((tm, D_MODEL), F32)],
        input_output_aliases={0: 0},
        compiler_params=_cparams("arbitrary"),
        name=f"ffn{which}",
    )(x, mod, g_norm, wgu, wout)


def _head_norm(r, g):
    ms = jnp.mean(r * r, axis=-1, keepdims=True)
    return r * lax.rsqrt(ms + EPS) * g


def _half_norm(r, g2):
    sq = r * r
    lo = lax.broadcasted_iota(jnp.int32, r.shape, 1) < DIFF_DIM
    s_lo = jnp.sum(jnp.where(lo, sq, 0.0), axis=-1, keepdims=True)
    s_hi = jnp.sum(jnp.where(lo, 0.0, sq), axis=-1, keepdims=True)
    ms = jnp.where(lo, s_lo, s_hi) * (1.0 / DIFF_DIM)
    return r * lax.rsqrt(ms + EPS) * g2


def _qkv_kernel(x_ref, mod_ref, g_ref, w_ref, cos_ref, sa_ref, sb_ref, gh_ref,
                qa_ref, ka_ref, va_ref, qb_ref, kb_ref, vb_ref, qc_ref, kc_ref, vc_ref, nb_ref):
    nb_ref[...] = _modulate(x_ref[...], mod_ref, g_ref[1:2, :], 1).astype(BF16)
    cos, sa, sb = cos_ref[...], sa_ref[...], sb_ref[...]

    def rope(r):
        return (r * cos + pltpu.roll(r, HEAD_DIM - 32, 1) * sa + pltpu.roll(r, 32, 1) * sb)

    def proj(col, width):
        return _dot(nb_ref[...], w_ref[:, col:col + width])

    def heads(r, n_heads, fn, out_ref):
        for h in range(n_heads):
            out_ref[h] = fn(r[:, h * HEAD_DIM:(h + 1) * HEAD_DIM]).astype(BF16)

    def values_transposed(r, out_ref):
        for h in range(N_KV_HEADS):
            out_ref[h, :HEAD_DIM, :] = r[:, h * HEAD_DIM:(h + 1) * HEAD_DIM].T.astype(BF16)
            out_ref[h, HEAD_DIM:, :] = jnp.ones((VT_ROWS - HEAD_DIM, r.shape[0]), BF16)

    qa_scale = HEAD_DIM ** -0.5 * LOG2E
    qc_scale = DIFF_DIM ** -0.5 * LOG2E
    g_qa, g_ka, g_qb, g_kb = (gh_ref[i:i + 1, :] for i in range(4))
    g_qc, g_kc = gh_ref[4:5, :], gh_ref[5:6, :]
    col = 0
    heads(proj(col, Q_W), N_Q_HEADS, lambda r: rope(_head_norm(r, g_qa)) * qa_scale, qa_ref)
    col += Q_W
    heads(proj(col, KV_W), N_KV_HEADS, lambda r: rope(_head_norm(r, g_ka)), ka_ref)
    col += KV_W
    values_transposed(proj(col, KV_W),va_ref)
    col += KV_W
    heads(proj(col, Q_W), N_Q_HEADS, lambda r: _head_norm(r, g_qb) * qa_scale, qb_ref)
    col += Q_W
    heads(proj(col, KV_W), N_KV_HEADS, lambda r: _head_norm(r, g_kb), kb_ref)
    col += KV_W
    values_transposed(proj(col, KV_W), vb_ref)
    col += KV_W
    heads(proj(col, Q_W), N_Q_HEADS, lambda r: _half_norm(r, g_qc) * qc_scale, qc_ref)
    col += Q_W
    heads(proj(col, KV_W), N_KV_HEADS, lambda r: _half_norm(r, g_kc), kc_ref)
    col += KV_W
    values_transposed(proj(col, KV_W),vc_ref)


def _qkv(x, mod, g_norm, wqkv, rope_tabs, gh, *, layer, seq, tm):
    n = x.shape[0]
    tiles_per_seq = seq // tm
    q_shape = jax.ShapeDtypeStruct((N_Q_HEADS, n, HEAD_DIM), BF16)
    kv_shape = jax.ShapeDtypeStruct((N_KV_HEADS, n, HEAD_DIM), BF16)
    q_spec = pl.BlockSpec((N_Q_HEADS, tm, HEAD_DIM), lambda i: (0, i, 0))
    kv_spec = pl.BlockSpec((N_KV_HEADS, tm, HEAD_DIM), lambda i: (0, i, 0))
    tab_spec = pl.BlockSpec((tm, HEAD_DIM), lambda i: (i % tiles_per_seq, 0))
    v1_shape = jax.ShapeDtypeStruct((N_KV_HEADS, VT_ROWS, n), BF16)
    v1_spec = pl.BlockSpec((N_KV_HEADS, VT_ROWS, tm), lambda i: (0, 0, i))
    return pl.pallas_call(
        _qkv_kernel,
        out_shape=[q_shape, kv_shape, v1_shape] * 3,
        grid=(n // tm,),
        in_specs=[
            pl.BlockSpec((tm, D_MODEL), lambda i: (i, 0)),
            pl.BlockSpec((None, 1, 9, D_MODEL), lambda i: (layer, (i * tm) // seq, 0, 0)),
            pl.BlockSpec((None, 3, D_MODEL), lambda i: (layer, 0, 0)),
            pl.BlockSpec((None, D_MODEL, QKV_W), lambda i: (layer, 0, 0), pipeline_mode=pl.Buffered(1)),
            tab_spec, tab_spec, tab_spec,
            pl.BlockSpec((None, 8, HEAD_DIM), lambda i: (layer, 0, 0)),
        ],
        out_specs=[q_spec, kv_spec, v1_spec] * 3,
        scratch_shapes=[pltpu.VMEM((tm, D_MODEL), BF16)],
        compiler_params=_cparams("arbitrary"),
        name="qkv",
    )(x, mod, g_norm, wqkv, *rope_tabs, gh)


def _out_kernel(x_ref, mod_ref, g_ref, oa_ref, ob_ref, oc_ref, wg_ref, wo_ref, o_ref, mg_ref):
    x = x_ref[...]
    nb = _modulate(x, mod_ref, g_ref[1:2, :], 1).astype(BF16)
    for br, br_ref in enumerate((oa_ref, ob_ref, oc_ref)):
        gate = jax.nn.sigmoid(_dot(nb, wg_ref[:, br * D_MODEL:(br + 1) * D_MODEL]))
        for h in range(N_Q_HEADS):
            lanes = slice(h * HEAD_DIM, (h + 1) * HEAD_DIM)
            term = gate[:, lanes] * br_ref[h]
            if br == 0:
                mg_ref[:, lanes] = term
            else:
                mg_ref[:, lanes] += term
    o_ref[...] = x + mod_ref[0, 5:6, :] * _dot(mg_ref[...].astype(BF16), wo_ref[...])


def _out_proj(x, mod, g_norm, oa, ob, oc, wgate, wo, *, layer, seq, tm):
    n = x.shape[0]
    o_spec = pl.BlockSpec((N_Q_HEADS, tm, HEAD_DIM), lambda i: (0, i, 0))
    return pl.pallas_call(
        _out_kernel,
        out_shape=jax.ShapeDtypeStruct((n, D_MODEL), F32),
        grid=(n // tm,),
        in_specs=[
            pl.BlockSpec((tm, D_MODEL), lambda i: (i, 0)),
            pl.BlockSpec((None, 1, 9, D_MODEL), lambda i: (layer, (i * tm) // seq, 0, 0)),
            pl.BlockSpec((None, 3, D_MODEL), lambda i: (layer, 0, 0)),
            o_spec, o_spec, o_spec,
            pl.BlockSpec((None, D_MODEL, 3 * D_MODEL), lambda i: (layer, 0, 0), pipeline_mode=pl.Buffered(1)),
            pl.BlockSpec((None, D_MODEL, D_MODEL), lambda i: (layer, 0, 0), pipeline_mode=pl.Buffered(1)),
        ],
        out_specs=pl.BlockSpec((tm, D_MODEL), lambda i: (i, 0)),
        scratch_shapes=[pltpu.VMEM((tm, D_MODEL), F32)],
        input_output_aliases={0: 0},
        compiler_params=_cparams("arbitrary"),
        name="out_proj",
    )(x, mod, g_norm, oa, ob, oc, wgate, wo)


MIN_DENOM = 2.0 ** -64


def _key_slice(idx, tks):
    return pl.ds(pl.multiple_of(idx * tks, tks), tks)


def _score_bound(q, key_norm):
    qf = q.astype(F32)
    qn2 = _dot_nt(jnp.ones((8, q.shape[1]), F32), qf * qf)[:1]
    return jnp.sqrt(qn2) * key_norm


def _bounded_iteration(t, last, q, k_ref, v_ref, acct_ref, s_ref, *, unroll, tks, shift, bias_fn=None):
    acct = acct_ref[...]
    s = s_ref[...]
    for u in range(unroll):
        idx = t * unroll + u
        s_next = None if (last and u == unroll - 1) else _dot_nt(k_ref[_key_slice(idx + 1, tks), :], q)
        if bias_fn is not None:
            s = bias_fn(idx, s)
        acct = acct + _dot(v_ref[:, _key_slice(idx, tks)], jnp.exp2(s - shift).astype(BF16))
        s = s_next
    acct_ref[...] = acct
    if not last:
        s_ref[...] = s


def _exact_pass(q, k_ref, v_ref, m_ref, acct_ref, *, n_sub, tks, bias_fn=None):
    m_ref[...] = jnp.full_like(m_ref, -jnp.inf)
    acct_ref[...] = jnp.zeros_like(acct_ref)

    def body(idx, carry):
        s = _dot_nt(k_ref[_key_slice(idx, tks), :], q)
        if bias_fn is not None:
            s = bias_fn(idx, s)
        m = m_ref[...]
        m_new = jnp.maximum(m, jnp.max(s, axis=0, keepdims=True))
        p = jnp.exp2(s - m_new).astype(BF16)
        acct_ref[...] = jnp.exp2(m - m_new) * acct_ref[...] + _dot(v_ref[:, _key_slice(idx, tks)], p)
        m_ref[...] = m_new
        return carry

    lax.fori_loop(0, n_sub, body, 0)


def _denominators_ok(acct_ref):
    return jnp.min(acct_ref[HEAD_DIM:HEAD_DIM + 1, :]) >= MIN_DENOM


def _attn_a_kernel(q_ref, k_ref, v_ref, kn_ref, o_ref, m_ref, acct_ref, s_ref, *, n_iter, unroll, tks):
    rows = acct_ref.shape[1]
    q = q_ref[...].reshape(rows, HEAD_DIM)
    shift = _score_bound(q, kn_ref[:, :1])
    acct_ref[...] = jnp.zeros_like(acct_ref)
    s_ref[...] = _dot_nt(k_ref[0:tks, :], q)
    refs = (q, k_ref, v_ref, acct_ref, s_ref)

    def body(t, carry):
        _bounded_iteration(t, False, *refs, unroll=unroll, tks=tks, shift=shift)
        return carry

    lax.fori_loop(0, n_iter - 1, body, 0)
    _bounded_iteration(n_iter - 1, True, *refs, unroll=unroll, tks=tks, shift=shift)

    @pl.when(jnp.logical_not(_denominators_ok(acct_ref)))
    def _():
        _exact_pass(q, k_ref, v_ref, m_ref, acct_ref, n_sub=n_iter * unroll, tks=tks)

    acct = acct_ref[...]
    o_t = acct[:HEAD_DIM] / acct[HEAD_DIM:HEAD_DIM + 1]
    o_ref[...] = o_t.T.reshape(o_ref.shape)


def _attn_a(q, k, v1t, key_norm, *, layer, batch, seq, tq, tks, unroll):
    n = q.shape[1]
    nq = seq // tq
    rows = Q_PER_KV * tq
    q_map = lambda b, g, i: (g, b * nq + i, 0)
    return pl.pallas_call(
        functools.partial(_attn_a_kernel, n_iter=seq // (tks * unroll), unroll=unroll, tks=tks),
        out_shape=jax.ShapeDtypeStruct((N_Q_HEADS, n, HEAD_DIM), F32),
        grid=(batch, N_KV_HEADS, nq),
        in_specs=[
            pl.BlockSpec((Q_PER_KV, tq, HEAD_DIM), q_map),
            pl.BlockSpec((None, seq, HEAD_DIM), lambda b, g, i: (g, b, 0)),
            pl.BlockSpec((None, VT_ROWS, seq), lambda b, g, i: (g, 0, b)),
            pl.BlockSpec((None, 1, HEAD_DIM), lambda b, g, i: (layer, 0, 0)),
        ],
        out_specs=pl.BlockSpec((Q_PER_KV, tq, HEAD_DIM), q_map),
        scratch_shapes=[pltpu.VMEM((1, rows), F32), pltpu.VMEM((VT_ROWS, rows), F32),
                        pltpu.VMEM((tks, rows), F32)],
        compiler_params=_cparams("arbitrary", "arbitrary", "arbitrary"),
        name="attn_axial",
    )(q, k, v1t, key_norm)


def _attn_b_kernel(q_ref, kp_ref, kc_ref, kn_ref, vp_ref, vc_ref, vn_ref, bias_ref, sink_ref, o_ref,
                   *, nb, qb):
    i = pl.program_id(2)
    rows = Q_PER_KV * BLOCK_Q
    kwin = jnp.concatenate([kp_ref[...], kc_ref[...], kn_ref[...]], axis=0)
    vwin = jnp.concatenate([vp_ref[...], vc_ref[...], vn_ref[...]], axis=1)
    bias = bias_ref[...]
    sink = sink_ref[...]
    for u in range(qb):
        blk = i * qb + u
        q = q_ref[:, u * BLOCK_Q:(u + 1) * BLOCK_Q, :].reshape(rows, HEAD_DIM)
        s = _dot_nt(kwin[u * BLOCK_Q:u * BLOCK_Q + NEAR], q) + bias
        s = jnp.concatenate([jnp.where(blk > 0, s[:BLOCK_Q], NEG), s[BLOCK_Q:2 * BLOCK_Q],
                             jnp.where(blk < nb - 1, s[2 * BLOCK_Q:], NEG)], axis=0)
        m = jnp.maximum(jnp.max(s, axis=0, keepdims=True), sink)
        p = jnp.exp2(s - m).astype(BF16)
        acct = _dot(vwin[:, u * BLOCK_Q:u * BLOCK_Q + NEAR], p)
        den = acct[HEAD_DIM:HEAD_DIM + 1] + jnp.exp2(sink - m)
        o_t = acct[:HEAD_DIM] / den
        o_ref[:, u * BLOCK_Q:(u + 1) * BLOCK_Q, :] = o_t.T.reshape(Q_PER_KV, BLOCK_Q, HEAD_DIM)


def _attn_b(q, k, v1t, bias_t, sink_t, *, layer, batch, seq, qb):
    n = q.shape[1]
    nb = seq // BLOCK_Q
    steps = nb // qb
    rows = Q_PER_KV * BLOCK_Q
    cur_map = lambda b, g, i: (g, b * steps + i, 0)
    prev_map = lambda b, g, i: (g, b * nb + jnp.maximum(i * qb - 1, 0), 0)
    next_map = lambda b, g, i: (g, b * nb + jnp.minimum(i * qb + qb, nb - 1), 0)
    t = lambda index_map: (lambda b, g, i: (index_map(b, g, i)[0], 0, index_map(b, g, i)[1]))
    k_specs = [pl.BlockSpec((None, BLOCK_Q, HEAD_DIM), prev_map),
               pl.BlockSpec((None, qb * BLOCK_Q, HEAD_DIM), cur_map),
               pl.BlockSpec((None, BLOCK_Q, HEAD_DIM), next_map)]
    v_specs = [pl.BlockSpec((None, VT_ROWS, BLOCK_Q), t(prev_map)),
               pl.BlockSpec((None, VT_ROWS, qb * BLOCK_Q), t(cur_map)),
               pl.BlockSpec((None, VT_ROWS, BLOCK_Q), t(next_map))]
    return pl.pallas_call(
        functools.partial(_attn_b_kernel, nb=nb, qb=qb),
        out_shape=jax.ShapeDtypeStruct((N_Q_HEADS, n, HEAD_DIM), F32),
        grid=(batch, N_KV_HEADS, steps),
        in_specs=[pl.BlockSpec((Q_PER_KV, qb * BLOCK_Q, HEAD_DIM), cur_map)] + k_specs + v_specs + [
            pl.BlockSpec((None, NEAR, rows), lambda b, g, i: (g, 0, 0)),
            pl.BlockSpec((None, None, 1, rows), lambda b, g, i: (layer, g, 0, 0)),
        ],
        out_specs=pl.BlockSpec((Q_PER_KV, qb * BLOCK_Q, HEAD_DIM), cur_map),
        compiler_params=_cparams("arbitrary", "arbitrary", "arbitrary"),
        name="attn_window",
    )(q, k, k, k, v1t, v1t, v1t, bias_t, sink_t)


def _attn_c_kernel(q_ref, k_ref, v_ref, bias_ref, cfar_ref, kn_ref, lam_ref, gs_ref, o_ref,
                   m_ref, acct_ref, s_ref, *, n_iter, unroll, tks, nblk, out_scale):
    i = pl.program_id(2)
    rows = Q_PER_KV * BLOCK_Q
    blocks_per_sub = tks // BLOCK_Q
    blocks_per_iter = unroll * blocks_per_sub

    q = q_ref[...].reshape(rows, HEAD_DIM)
    lo = lax.broadcasted_iota(jnp.int32, q.shape, 1) < DIFF_DIM
    zero = jnp.zeros_like(q)
    q2 = jnp.concatenate([jnp.where(lo, q, zero), jnp.where(lo, zero, q)], axis=0)
    acct_ref[...] = jnp.zeros_like(acct_ref)
    s_ref[...] = _dot_nt(k_ref[0:tks, :], q2)
    refs = (q2, k_ref, v_ref, acct_ref, s_ref)

    c_left, c_right, c_max = cfar_ref[0], cfar_ref[1], cfar_ref[2]
    t_lo = jnp.maximum(i - 1, 0) // blocks_per_iter
    t_hi = jnp.minimum(i + 1, nblk - 1) // blocks_per_iter
    two = lambda x: jnp.concatenate([x, x], axis=1)
    shift = _score_bound(q2, kn_ref[:, :1]) + two(c_max)

    def add_bias(idx, s):
        blks = []
        for kb in range(blocks_per_sub):
            d = idx * blocks_per_sub + kb - i
            blks.append(bias_ref[jnp.clip(d + 2, 0, NEAR // BLOCK_Q + 1)])
        bias = jnp.concatenate(blks, axis=0)
        return jnp.concatenate([s[:, :rows] + bias, s[:, rows:] + bias], axis=1)

    def iteration(t, last):
        is_near = (t >= t_lo) & (t <= t_hi)

        @pl.when(jnp.logical_not(is_near))
        def _():
            side = jnp.where(t < t_lo, c_left, c_right)
            _bounded_iteration(t, last, *refs, unroll=unroll, tks=tks, shift=shift - two(side))

        @pl.when(is_near)
        def _():
            _bounded_iteration(t, last, *refs, unroll=unroll, tks=tks, shift=shift, bias_fn=add_bias)

    def body(t, carry):
        iteration(t, False)
        return carry

    lax.fori_loop(0, n_iter - 1, body, 0)
    iteration(n_iter - 1, True)

    @pl.when(jnp.logical_not(_denominators_ok(acct_ref)))
    def _():
        _exact_pass(q2, k_ref, v_ref, m_ref, acct_ref, n_sub=n_iter * unroll, tks=tks, bias_fn=add_bias)

    acct = acct_ref[...]
    o1_t = acct[:HEAD_DIM, :rows] / acct[HEAD_DIM:HEAD_DIM + 1, :rows]
    o2_t = acct[:HEAD_DIM, rows:] / acct[HEAD_DIM:HEAD_DIM + 1, rows:]
    o = (o1_t - lam_ref[:, :1] * o2_t).T
    ms = jnp.mean(o * o, axis=-1, keepdims=True)
    o = o * lax.rsqrt(ms + EPS) * gs_ref[...] * out_scale
    o_ref[...] = o.reshape(o_ref.shape)


def _attn_c(q, k, v1t, bias_t, cfar_t, key_norm, lam, g_subln, *, layer, batch, seq, tks, unroll):
    n = q.shape[1]
    nq = seq // BLOCK_Q
    rows = Q_PER_KV * BLOCK_Q
    lam_init = 0.8 - 0.6 * math.exp(-0.3 * layer)
    q_map = lambda b, g, i: (g, b * nq + i, 0)
    return pl.pallas_call(
        functools.partial(_attn_c_kernel, n_iter=seq // (tks * unroll), unroll=unroll, tks=tks,
                          nblk=nq, out_scale=1.0 - lam_init),
        out_shape=jax.ShapeDtypeStruct((N_Q_HEADS, n, HEAD_DIM), F32),
        grid=(batch, N_KV_HEADS, nq),
        in_specs=[
            pl.BlockSpec((Q_PER_KV, BLOCK_Q, HEAD_DIM), q_map),
            pl.BlockSpec((None, seq, HEAD_DIM), lambda b, g, i: (g, b, 0)),
            pl.BlockSpec((None, VT_ROWS, seq), lambda b, g, i: (g, 0, b)),
            pl.BlockSpec((None, NEAR // BLOCK_Q + 2, BLOCK_Q, rows), lambda b, g, i: (g, 0, 0, 0)),
            pl.BlockSpec((3, None, 1, rows), lambda b, g, i: (0, g, 0, 0)),
            pl.BlockSpec((None, 1, HEAD_DIM), lambda b, g, i: (layer, 0, 0)),
            pl.BlockSpec((None, 1, HEAD_DIM), lambda b, g, i: (layer, 0, 0)),
            pl.BlockSpec((None, 1, HEAD_DIM), lambda b, g, i: (layer, 0, 0)),
        ],
        out_specs=pl.BlockSpec((Q_PER_KV, BLOCK_Q, HEAD_DIM), q_map),
        scratch_shapes=[pltpu.VMEM((1, 2 * rows), F32), pltpu.VMEM((VT_ROWS, 2 * rows), F32),
                        pltpu.VMEM((tks, 2 * rows), F32)],
        compiler_params=_cparams("arbitrary", "arbitrary", "arbitrary"),
        name="attn_diff",
    )(q, k, v1t, bias_t, cfar_t, key_norm, lam, g_subln)


def _t5_bucket_np(rel):
    half = N_BUCKETS // 2
    max_exact = half // 2
    ret = np.where(rel > 0, half, 0)
    n = np.abs(rel)
    ratio = np.log(np.maximum(n, 1).astype(np.float32) / np.float32(max_exact)) / np.float32(
        math.log(MAX_DISTANCE / max_exact))
    large = max_exact + (ratio * np.float32(half - max_exact)).astype(np.int32)
    large = np.minimum(large, half - 1)
    return (ret + np.where(n < max_exact, n, large)).astype(np.int32)


def _near_bias(table):
    r = np.arange(BLOCK_Q)[:, None]
    c = np.arange(NEAR)[None, :]
    bucket = _t5_bucket_np(c - BLOCK_Q - r)
    onehot = (bucket.reshape(-1, 1) == np.arange(N_BUCKETS)[None, :]).astype(np.float32)
    rows = jnp.dot(jnp.asarray(onehot), table.astype(F32), precision=lax.Precision.HIGHEST)
    return rows.reshape(BLOCK_Q, NEAR, table.shape[1]).transpose(2, 0, 1)


def _rope_tables(seq):
    rows = seq // GRID_W
    row = jnp.repeat(jnp.arange(rows), GRID_W).astype(F32)
    col = jnp.tile(jnp.arange(GRID_W), rows).astype(F32)
    nfreq = HEAD_DIM // 4
    inv = ROPE_THETA ** (-jnp.arange(nfreq, dtype=F32) / nfreq)
    ang_r = row[:, None] * inv
    ang_c = col[:, None] * inv
    ang = jnp.concatenate([ang_r, ang_r, ang_c, ang_c], axis=-1)
    cos, sin = jnp.cos(ang), jnp.sin(ang)
    first = (np.arange(HEAD_DIM) % (HEAD_DIM // 2)) < HEAD_DIM // 4
    return cos, jnp.where(first, -sin, 0.0), jnp.where(first, 0.0, sin)


def _trunk(x, mod, p, *, batch, seq):
    t = _tiles(seq)
    tm = t["tm"]
    rope_tabs = _rope_tables(seq)
    for l in range(DEPTH):
        x = _ffn(x, mod, p["g_norm"], p["wgu"], p["wout"], layer=l, which=0, seq=seq, tm=tm)
        qa, ka, va, qb, kb, vb, qc, kc, vc = _qkv(x, mod, p["g_norm"], p["wqkv"], rope_tabs, p["gh"],
                                                  layer=l, seq=seq, tm=tm)
        oa = _attn_a(qa, ka, va, p["kn_a"], layer=l, batch=batch, seq=seq, tq=t["tq_a"], tks=t["tks"],
                     unroll=t["unroll"])
        ob = _attn_b(qb, kb, vb, p["bias_b"], p["sink"], layer=l, batch=batch, seq=seq, qb=t["qb"])
        oc = _attn_c(qc, kc, vc, p["bias_c"], p["cfar_c"], p["kn_c"], p["lam"], p["g_subln"],
                     layer=l, batch=batch, seq=seq, tks=t["tks"], unroll=t["unroll"])
        x = _out_proj(x, mod, p["g_norm"], oa, ob, oc, p["wgate"], p["wo"], layer=l, seq=seq, tm=tm)
        x = _ffn(x, mod, p["g_norm"], p["wgu"], p["wout"], layer=l, which=1, seq=seq, tm=tm)
    return x


def _prepare(w_ff_in, w_ff_out, w_in, w_o, g_qa, g_ka, g_qb, g_kb, g_qc, g_kc, sink,
             lam_q1, lam_k1, lam_q2, lam_k2, g_subln, rel_bias):
    wg = w_ff_in[..., :D_FF].reshape(DEPTH, 2, D_MODEL, N_FF_CHUNKS, FF_CHUNK)
    wu = w_ff_in[..., D_FF:].reshape(DEPTH, 2, D_MODEL, N_FF_CHUNKS, FF_CHUNK)
    wgu = jnp.concatenate([wg, wu], axis=-1).transpose(0, 1, 3, 2, 4).astype(BF16)
    wout = w_ff_out.reshape(DEPTH, 2, N_FF_CHUNKS, FF_CHUNK, D_MODEL).astype(BF16)
    zeros = jnp.zeros_like(g_qa)
    gh = jnp.stack([g_qa, g_ka, g_qb, g_kb, jnp.tile(g_qc, (1, 2)), jnp.tile(g_kc, (1, 2)), zeros, zeros],
                   axis=1).astype(F32)
    table_b, table_c = rel_bias[:, :N_Q_HEADS], rel_bias[:, N_Q_HEADS:]
    half = N_BUCKETS // 2
    c_left, c_right = table_c[half - 1] * LOG2E, table_c[N_BUCKETS - 1] * LOG2E
    lam_init = jnp.asarray([0.8 - 0.6 * math.exp(-0.3 * l) for l in range(DEPTH)], F32)
    lam = (jnp.exp(jnp.sum(lam_q1.astype(F32) * lam_k1.astype(F32), axis=-1))
           - jnp.exp(jnp.sum(lam_q2.astype(F32) * lam_k2.astype(F32), axis=-1)) + lam_init)

    band = np.abs(np.arange(NEAR)[None, :] - BLOCK_Q - np.arange(BLOCK_Q)[:, None]) <= WINDOW
    near_c = (_near_bias(table_c) * LOG2E).reshape(
        N_KV_HEADS, Q_PER_KV, BLOCK_Q, NEAR // BLOCK_Q, BLOCK_Q).transpose(0, 3, 4, 1, 2).reshape(
        N_KV_HEADS, NEAR // BLOCK_Q, BLOCK_Q, Q_PER_KV * BLOCK_Q)
    cfar_c = jnp.repeat(jnp.stack([c_left, c_right, jnp.max(table_c, axis=0) * LOG2E]).astype(F32),
                        BLOCK_Q, axis=1).reshape(3, N_KV_HEADS, 1, Q_PER_KV * BLOCK_Q)

    def key_norm(g, dim):
        bound = 1.01 * math.sqrt(dim) * jnp.max(jnp.abs(g.astype(F32)), axis=-1)
        return jnp.broadcast_to(bound[:, None, None], (DEPTH, 1, HEAD_DIM))

    return dict(
        wgu=wgu, wout=wout,
        wqkv=w_in[:, :, :QKV_W].astype(BF16), wgate=w_in[:, :, QKV_W:].astype(BF16), wo=w_o.astype(BF16),
        gh=gh,
        bias_b=jnp.where(band, _near_bias(table_b) * LOG2E, NEG).reshape(
            N_KV_HEADS, Q_PER_KV, BLOCK_Q, NEAR).transpose(0, 3, 1, 2).reshape(
            N_KV_HEADS, NEAR, Q_PER_KV * BLOCK_Q),
        sink=jnp.repeat(sink.astype(F32) * LOG2E, BLOCK_Q, axis=1).reshape(
            DEPTH, N_KV_HEADS, 1, Q_PER_KV * BLOCK_Q),
        bias_c=jnp.concatenate([cfar_c[0][:, None] + jnp.zeros((1, 1, BLOCK_Q, 1), F32), near_c,
                                cfar_c[1][:, None] + jnp.zeros((1, 1, BLOCK_Q, 1), F32)], axis=1),
        cfar_c=cfar_c,
        kn_a=key_norm(g_ka, HEAD_DIM), kn_c=key_norm(g_kc, DIFF_DIM),
        lam=jnp.broadcast_to(lam[:, None, None], (DEPTH, 1, HEAD_DIM)),
        g_subln=g_subln.astype(F32)[:, None, :],
    )


def kernel(x_prompt, x_sample, c_prompt, c_sample, w_ada, b_ada, g_norm, w_ff_in, w_ff_out, w_in, w_o,
           g_qa, g_ka, g_qb, g_kb, g_qc, g_kc, sink, lam_q1, lam_k1, lam_q2, lam_k2, g_subln, rel_bias):
    p = _prepare(w_ff_in, w_ff_out, w_in, w_o, g_qa, g_ka, g_qb, g_kb, g_qc, g_kc, sink,
                 lam_q1, lam_k1, lam_q2, lam_k2, g_subln, rel_bias)
    p["g_norm"] = g_norm.astype(F32)
    outs = []
    n_cond = 0
    conds = [c_prompt, c_sample]
    c_all = jnp.concatenate(conds + [jnp.zeros((ADA_ROWS - sum(c.shape[0] for c in conds), D_MODEL), F32)])
    mod_all = _ada(c_all, w_ada, b_ada)
    for x, c in ((x_prompt, c_prompt), (x_sample, c_sample)):
        batch, seq, _ = x.shape
        mod = mod_all[:, n_cond:n_cond + batch].reshape(DEPTH, batch, 9, D_MODEL)
        n_cond += batch
        y = _trunk(x.reshape(batch * seq, D_MODEL), mod, p, batch=batch, seq=seq)
        outs.append(y.reshape(batch, seq, D_MODEL))
    return tuple(outs)
```

```python
import functools
import math

import numpy as np
import jax
import jax.numpy as jnp
from jax import lax
from jax.experimental import pallas as pl
from jax.experimental.pallas import tpu as pltpu

F32 = jnp.float32
BF16 = jnp.bfloat16

D_MODEL = 1024
DEPTH = 4
N_Q_HEADS = 8
N_KV_HEADS = 2
Q_PER_KV = N_Q_HEADS // N_KV_HEADS
HEAD_DIM = 128
DIFF_DIM = 64
D_FF = 2816
BLOCK_Q = 128
WINDOW = 128
GRID_W = 64
N_BUCKETS = 32
MAX_DISTANCE = 128
ROPE_THETA = 10000.0
EPS = 1e-6
NEG = -1e30
LOG2E = 1.4426950408889634

Q_W = N_Q_HEADS * HEAD_DIM
KV_W = N_KV_HEADS * HEAD_DIM
BRANCH_IN = Q_W + 2 * KV_W
QKV_W = 3 * BRANCH_IN
FF_CHUNK = 256
N_FF_CHUNKS = D_FF // FF_CHUNK
ADA_ROWS = 16
ADA_TN = 1536
NEAR = 3 * BLOCK_Q
VT_ROWS = HEAD_DIM + 16

VMEM_LIMIT = 56 * 1024 * 1024


def _tiles(seq):
    tks = min(512, seq)
    return dict(
        tm=min(512, seq),
        tq_a=BLOCK_Q,
        tks=tks,
        qb=min(8, seq // BLOCK_Q),
        unroll=min(8, seq // tks),
    )


def _cparams(*sem):
    return pltpu.CompilerParams(dimension_semantics=sem, vmem_limit_bytes=VMEM_LIMIT)


def _dot(a, b):
    return jnp.dot(a, b, preferred_element_type=F32)


def _dot_nt(a, b):
    return lax.dot_general(a, b, (((1,), (1,)), ((), ())), preferred_element_type=F32)


def _ada_kernel(c_ref, w_ref, b_ref, o_ref):
    c = c_ref[...]
    a = (c * jax.nn.sigmoid(c)).astype(BF16)
    o_ref[0] = _dot(a, w_ref[0].astype(BF16)) + b_ref[0]


def _ada(c_all, w_ada, b_ada):
    n_out = w_ada.shape[-1]
    return pl.pallas_call(
        _ada_kernel,
        out_shape=jax.ShapeDtypeStruct((DEPTH, ADA_ROWS, n_out), F32),
        grid=(DEPTH, n_out // ADA_TN),
        in_specs=[
            pl.BlockSpec((ADA_ROWS, D_MODEL), lambda l, j: (0, 0)),
            pl.BlockSpec((1, D_MODEL, ADA_TN), lambda l, j: (l, 0, j)),
            pl.BlockSpec((1, 1, ADA_TN), lambda l, j: (l, 0, j)),
        ],
        out_specs=pl.BlockSpec((1, ADA_ROWS, ADA_TN), lambda l, j: (l, 0, j)),
        compiler_params=_cparams("arbitrary", "arbitrary"),
        name="ada",
    )(c_all, w_ada, b_ada.reshape(DEPTH, 1, n_out))


def _modulate(x, mod_ref, g, jj):
    ms = jnp.mean(x * x, axis=-1, keepdims=True)
    y = x * lax.rsqrt(ms + EPS) * g
    return y * (1.0 + mod_ref[0, 3 * jj + 1:3 * jj + 2, :]) + mod_ref[0, 3 * jj:3 * jj + 1, :]


def _ffn_kernel(x_ref, mod_ref, g_ref, wgu_ref, wout_ref, o_ref, nb_ref, acc_ref, *, jj):
    x = x_ref[...]
    nb_ref[...] = _modulate(x, mod_ref, g_ref[jj:jj + 1, :], jj).astype(BF16)
    acc_ref[...] = jnp.zeros_like(acc_ref)

    def body(c, carry):
        h = _dot(nb_ref[...], wgu_ref[c])
        hg = h[:, :FF_CHUNK]
        a = (hg * jax.nn.sigmoid(hg)) * h[:, FF_CHUNK:]
        acc_ref[...] += _dot(a.astype(BF16), wout_ref[c])
        return carry

    lax.fori_loop(0, N_FF_CHUNKS, body, 0, unroll=True)
    o_ref[...] = x + (0.5 * mod_ref[0, 3 * jj + 2:3 * jj + 3, :]) * acc_ref[...]


def _ffn(x, mod, g_norm, wgu, wout, *, layer, which, seq, tm):
    n = x.shape[0]
    jj = 2 * which
    const = dict(pipeline_mode=pl.Buffered(1))
    return pl.pallas_call(
        functools.partial(_ffn_kernel, jj=jj),
        out_shape=jax.ShapeDtypeStruct((n, D_MODEL), F32),
        grid=(n // tm,),
        in_specs=[
            pl.BlockSpec((tm, D_MODEL), lambda i: (i, 0)),
            pl.BlockSpec((None, 1, 9, D_MODEL), lambda i: (layer, (i * tm) // seq, 0, 0)),
            pl.BlockSpec((None, 3, D_MODEL), lambda i: (layer, 0, 0)),
            pl.BlockSpec((None, None, N_FF_CHUNKS, D_MODEL, 2 * FF_CHUNK),
                         lambda i: (layer, which, 0, 0, 0), **const),
            pl.BlockSpec((None, None, N_FF_CHUNKS, FF_CHUNK, D_MODEL),
                         lambda i: (layer, which, 0, 0, 0), **const),
        ],
        out_specs=pl.BlockSpec((tm, D_MODEL), lambda i: (i, 0)),
        scratch_shapes=[pltpu.VMEM((tm, D_MODEL), BF16), pltpu.VMEM((tm, D_MODEL), F32)],
        input_output_aliases={0: 0},
        compiler_params=_cparams("arbitrary"),
        name=f"ffn{which}",
    )(x, mod, g_norm, wgu, wout)


def _head_norm(r, g):
    ms = jnp.mean(r * r, axis=-1, keepdims=True)
    return r * lax.rsqrt(ms + EPS) * g


def _half_norm(r, g2):
    sq = r * r
    lo = lax.broadcasted_iota(jnp.int32, r.shape, 1) < DIFF_DIM
    s_lo = jnp.sum(jnp.where(lo, sq, 0.0), axis=-1, keepdims=True)
    s_hi = jnp.sum(jnp.where(lo, 0.0, sq), axis=-1, keepdims=True)
    ms = jnp.where(lo, s_lo, s_hi) * (1.0 / DIFF_DIM)
    return r * lax.rsqrt(ms + EPS) * g2


def _qkv_kernel(x_ref, mod_ref, g_ref, w_ref, cos_ref, sa_ref, sb_ref, gh_ref,
                qa_ref, ka_ref, va_ref, qb_ref, kb_ref, vb_ref, qc_ref, kc_ref, vc_ref, nb_ref):
    nb_ref[...] = _modulate(x_ref[...], mod_ref, g_ref[1:2, :], 1).astype(BF16)
    cos, sa, sb = cos_ref[...], sa_ref[...], sb_ref[...]

    def rope(r):
        return (r * cos + pltpu.roll(r, HEAD_DIM - 32, 1) * sa + pltpu.roll(r, 32, 1) * sb)

    def proj(col, width):
        return _dot(nb_ref[...], w_ref[:, col:col + width])

    def heads(r, n_heads, fn, out_ref):
        for h in range(n_heads):
            out_ref[h] = fn(r[:, h * HEAD_DIM:(h + 1) * HEAD_DIM]).astype(BF16)

    def values_transposed(r, out_ref):
        for h in range(N_KV_HEADS):
            out_ref[h, :HEAD_DIM, :] = r[:, h * HEAD_DIM:(h + 1) * HEAD_DIM].T.astype(BF16)
            out_ref[h, HEAD_DIM:, :] = jnp.ones((VT_ROWS - HEAD_DIM, r.shape[0]), BF16)

    qa_scale = HEAD_DIM ** -0.5 * LOG2E
    qc_scale = DIFF_DIM ** -0.5 * LOG2E
    g_qa, g_ka, g_qb, g_kb = (gh_ref[i:i + 1, :] for i in range(4))
    g_qc, g_kc = gh_ref[4:5, :], gh_ref[5:6, :]
    col = 0
    heads(proj(col, Q_W), N_Q_HEADS, lambda r: rope(_head_norm(r, g_qa)) * qa_scale, qa_ref)
    col += Q_W
    heads(proj(col, KV_W), N_KV_HEADS, lambda r: rope(_head_norm(r, g_ka)), ka_ref)
    col += KV_W
    values_transposed(proj(col, KV_W),va_ref)
    col += KV_W
    heads(proj(col, Q_W), N_Q_HEADS, lambda r: _head_norm(r, g_qb) * qa_scale, qb_ref)
    col += Q_W
    heads(proj(col, KV_W), N_KV_HEADS, lambda r: _head_norm(r, g_kb), kb_ref)
    col += KV_W
    values_transposed(proj(col, KV_W), vb_ref)
    col += KV_W
    heads(proj(col, Q_W), N_Q_HEADS, lambda r: _half_norm(r, g_qc) * qc_scale, qc_ref)
    col += Q_W
    heads(proj(col, KV_W), N_KV_HEADS, lambda r: _half_norm(r, g_kc), kc_ref)
    col += KV_W
    values_transposed(proj(col, KV_W),vc_ref)


def _qkv(x, mod, g_norm, wqkv, rope_tabs, gh, *, layer, seq, tm):
    n = x.shape[0]
    tiles_per_seq = seq // tm
    q_shape = jax.ShapeDtypeStruct((N_Q_HEADS, n, HEAD_DIM), BF16)
    kv_shape = jax.ShapeDtypeStruct((N_KV_HEADS, n, HEAD_DIM), BF16)
    q_spec = pl.BlockSpec((N_Q_HEADS, tm, HEAD_DIM), lambda i: (0, i, 0))
    kv_spec = pl.BlockSpec((N_KV_HEADS, tm, HEAD_DIM), lambda i: (0, i, 0))
    tab_spec = pl.BlockSpec((tm, HEAD_DIM), lambda i: (i % tiles_per_seq, 0))
    v1_shape = jax.ShapeDtypeStruct((N_KV_HEADS, VT_ROWS, n), BF16)
    v1_spec = pl.BlockSpec((N_KV_HEADS, VT_ROWS, tm), lambda i: (0, 0, i))
    return pl.pallas_call(
        _qkv_kernel,
        out_shape=[q_shape, kv_shape, v1_shape] * 3,
        grid=(n // tm,),
        in_specs=[
            pl.BlockSpec((tm, D_MODEL), lambda i: (i, 0)),
            pl.BlockSpec((None, 1, 9, D_MODEL), lambda i: (layer, (i * tm) // seq, 0, 0)),
            pl.BlockSpec((None, 3, D_MODEL), lambda i: (layer, 0, 0)),
            pl.BlockSpec((None, D_MODEL, QKV_W), lambda i: (layer, 0, 0), pipeline_mode=pl.Buffered(1)),
            tab_spec, tab_spec, tab_spec,
            pl.BlockSpec((None, 8, HEAD_DIM), lambda i: (layer, 0, 0)),
        ],
        out_specs=[q_spec, kv_spec, v1_spec] * 3,
        scratch_shapes=[pltpu.VMEM((tm, D_MODEL), BF16)],
        compiler_params=_cparams("arbitrary"),
        name="qkv",
    )(x, mod, g_norm, wqkv, *rope_tabs, gh)


def _out_kernel(x_ref, mod_ref, g_ref, oa_ref, ob_ref, oc_ref, wg_ref, wo_ref, o_ref, mg_ref):
    x = x_ref[...]
    nb = _modulate(x, mod_ref, g_ref[1:2, :], 1).astype(BF16)
    for br, br_ref in enumerate((oa_ref, ob_ref, oc_ref)):
        gate = jax.nn.sigmoid(_dot(nb, wg_ref[:, br * D_MODEL:(br + 1) * D_MODEL]))
        for h in range(N_Q_HEADS):
            lanes = slice(h * HEAD_DIM, (h + 1) * HEAD_DIM)
            term = gate[:, lanes] * br_ref[h]
            if br == 0:
                mg_ref[:, lanes] = term
            else:
                mg_ref[:, lanes] += term
    o_ref[...] = x + mod_ref[0, 5:6, :] * _dot(mg_ref[...].astype(BF16), wo_ref[...])


def _out_proj(x, mod, g_norm, oa, ob, oc, wgate, wo, *, layer, seq, tm):
    n = x.shape[0]
    o_spec = pl.BlockSpec((N_Q_HEADS, tm, HEAD_DIM), lambda i: (0, i, 0))
    return pl.pallas_call(
        _out_kernel,
        out_shape=jax.ShapeDtypeStruct((n, D_MODEL), F32),
        grid=(n // tm,),
        in_specs=[
            pl.BlockSpec((tm, D_MODEL), lambda i: (i, 0)),
            pl.BlockSpec((None, 1, 9, D_MODEL), lambda i: (layer, (i * tm) // seq, 0, 0)),
            pl.BlockSpec((None, 3, D_MODEL), lambda i: (layer, 0, 0)),
            o_spec, o_spec, o_spec,
            pl.BlockSpec((None, D_MODEL, 3 * D_MODEL), lambda i: (layer, 0, 0), pipeline_mode=pl.Buffered(1)),
            pl.BlockSpec((None, D_MODEL, D_MODEL), lambda i: (layer, 0, 0), pipeline_mode=pl.Buffered(1)),
        ],
        out_specs=pl.BlockSpec((tm, D_MODEL), lambda i: (i, 0)),
        scratch_shapes=[pltpu.VMEM((tm, D_MODEL), F32)],
        input_output_aliases={0: 0},
        compiler_params=_cparams("arbitrary"),
        name="out_proj",
    )(x, mod, g_norm, oa, ob, oc, wgate, wo)


MIN_DENOM = 2.0 ** -64


def _key_slice(idx, tks):
    return pl.ds(pl.multiple_of(idx * tks, tks), tks)


def _score_bound(q, key_norm):
    qf = q.astype(F32)
    qn2 = _dot_nt(jnp.ones((8, q.shape[1]), F32), qf * qf)[:1]
    return jnp.sqrt(qn2) * key_norm


def _bounded_iteration(t, q, q_next, k_ref, v_ref, acct_ref, s_ref, *, unroll, tks, n_sub, shift,
                       bias_fn=None):
    acct = acct_ref[...]
    s = s_ref[...]
    for u in range(unroll):
        idx = t * unroll + u
        wraps = idx + 1 == n_sub
        s_next = _dot_nt(k_ref[_key_slice(jnp.where(wraps, 0, idx + 1), tks), :], jnp.where(wraps, q_next, q))
        if bias_fn is not None:
            s = bias_fn(idx, s)
        acct = acct + _dot(v_ref[:, _key_slice(idx, tks)], jnp.exp2(s - shift).astype(BF16))
        s = s_next
    acct_ref[...] = acct
    s_ref[...] = s


def _exact_pass(q, k_ref, v_ref, m_ref, acct_ref, *, n_sub, tks, bias_fn=None):
    m_ref[...] = jnp.full_like(m_ref, -jnp.inf)
    acct_ref[...] = jnp.zeros_like(acct_ref)

    def body(idx, carry):
        s = _dot_nt(k_ref[_key_slice(idx, tks), :], q)
        if bias_fn is not None:
            s = bias_fn(idx, s)
        m = m_ref[...]
        m_new = jnp.maximum(m, jnp.max(s, axis=0, keepdims=True))
        p = jnp.exp2(s - m_new).astype(BF16)
        acct_ref[...] = jnp.exp2(m - m_new) * acct_ref[...] + _dot(v_ref[:, _key_slice(idx, tks)], p)
        m_ref[...] = m_new
        return carry

    lax.fori_loop(0, n_sub, body, 0)


def _denominators_ok(acct_ref):
    return jnp.min(acct_ref[HEAD_DIM:HEAD_DIM + 1, :]) >= MIN_DENOM


def _attn_a_kernel(q_ref, qn_ref, k_ref, v_ref, kn_ref, o_ref, m_ref, acct_ref, s_ref, *, n_iter, unroll, tks):
    rows = acct_ref.shape[1]
    q = q_ref[...].reshape(rows, HEAD_DIM)
    q_next = qn_ref[...].reshape(rows, HEAD_DIM)
    shift = _score_bound(q, kn_ref[:, :1])
    acct_ref[...] = jnp.zeros_like(acct_ref)

    @pl.when(pl.program_id(2) == 0)
    def _():
        s_ref[...] = _dot_nt(k_ref[0:tks, :], q)

    def body(t, carry):
        _bounded_iteration(t, q, q_next, k_ref, v_ref, acct_ref, s_ref, unroll=unroll, tks=tks,
                           n_sub=n_iter * unroll, shift=shift)
        return carry

    lax.fori_loop(0, n_iter, body, 0)

    @pl.when(jnp.logical_not(_denominators_ok(acct_ref)))
    def _():
        _exact_pass(q, k_ref, v_ref, m_ref, acct_ref, n_sub=n_iter * unroll, tks=tks)

    acct = acct_ref[...]
    o_t = acct[:HEAD_DIM] / acct[HEAD_DIM:HEAD_DIM + 1]
    o_ref[...] = o_t.T.reshape(o_ref.shape)


def _attn_a(q, k, v1t, key_norm, *, layer, batch, seq, tq, tks, unroll):
    n = q.shape[1]
    nq = seq // tq
    rows = Q_PER_KV * tq
    q_map = lambda b, g, i: (g, b * nq + i, 0)
    q_next_map = lambda b, g, i: (g, b * nq + jnp.minimum(i + 1, nq - 1), 0)
    return pl.pallas_call(
        functools.partial(_attn_a_kernel, n_iter=seq // (tks * unroll), unroll=unroll, tks=tks),
        out_shape=jax.ShapeDtypeStruct((N_Q_HEADS, n, HEAD_DIM), F32),
        grid=(batch, N_KV_HEADS, nq),
        in_specs=[
            pl.BlockSpec((Q_PER_KV, tq, HEAD_DIM), q_map),
            pl.BlockSpec((Q_PER_KV, tq, HEAD_DIM), q_next_map),
            pl.BlockSpec((None, seq, HEAD_DIM), lambda b, g, i: (g, b, 0)),
            pl.BlockSpec((None, VT_ROWS, seq), lambda b, g, i: (g, 0, b)),
            pl.BlockSpec((None, 1, HEAD_DIM), lambda b, g, i: (layer, 0, 0)),
        ],
        out_specs=pl.BlockSpec((Q_PER_KV, tq, HEAD_DIM), q_map),
        scratch_shapes=[pltpu.VMEM((1, rows), F32), pltpu.VMEM((VT_ROWS, rows), F32),
                        pltpu.VMEM((tks, rows), F32)],
        compiler_params=_cparams("arbitrary", "arbitrary", "arbitrary"),
        name="attn_axial",
    )(q, q, k, v1t, key_norm)


def _attn_b_kernel(q_ref, kp_ref, kc_ref, kn_ref, vp_ref, vc_ref, vn_ref, bias_ref, sink_ref, o_ref,
                   *, nb, qb):
    i = pl.program_id(2)
    rows = Q_PER_KV * BLOCK_Q
    kwin = jnp.concatenate([kp_ref[...], kc_ref[...], kn_ref[...]], axis=0)
    vwin = jnp.concatenate([vp_ref[...], vc_ref[...], vn_ref[...]], axis=1)
    bias = bias_ref[...]
    sink = sink_ref[...]
    for u in range(qb):
        blk = i * qb + u
        q = q_ref[:, u * BLOCK_Q:(u + 1) * BLOCK_Q, :].reshape(rows, HEAD_DIM)
        s = _dot_nt(kwin[u * BLOCK_Q:u * BLOCK_Q + NEAR], q) + bias
        s = jnp.concatenate([jnp.where(blk > 0, s[:BLOCK_Q], NEG), s[BLOCK_Q:2 * BLOCK_Q],
                             jnp.where(blk < nb - 1, s[2 * BLOCK_Q:], NEG)], axis=0)
        m = jnp.maximum(jnp.max(s, axis=0, keepdims=True), sink)
        p = jnp.exp2(s - m).astype(BF16)
        acct = _dot(vwin[:, u * BLOCK_Q:u * BLOCK_Q + NEAR], p)
        den = acct[HEAD_DIM:HEAD_DIM + 1] + jnp.exp2(sink - m)
        o_t = acct[:HEAD_DIM] / den
        o_ref[:, u * BLOCK_Q:(u + 1) * BLOCK_Q, :] = o_t.T.reshape(Q_PER_KV, BLOCK_Q, HEAD_DIM)


def _attn_b(q, k, v1t, bias_t, sink_t, *, layer, batch, seq, qb):
    n = q.shape[1]
    nb = seq // BLOCK_Q
    steps = nb // qb
    rows = Q_PER_KV * BLOCK_Q
    cur_map = lambda b, g, i: (g, b * steps + i, 0)
    prev_map = lambda b, g, i: (g, b * nb + jnp.maximum(i * qb - 1, 0), 0)
    next_map = lambda b, g, i: (g, b * nb + jnp.minimum(i * qb + qb, nb - 1), 0)
    t = lambda index_map: (lambda b, g, i: (index_map(b, g, i)[0], 0, index_map(b, g, i)[1]))
    k_specs = [pl.BlockSpec((None, BLOCK_Q, HEAD_DIM), prev_map),
               pl.BlockSpec((None, qb * BLOCK_Q, HEAD_DIM), cur_map),
               pl.BlockSpec((None, BLOCK_Q, HEAD_DIM), next_map)]
    v_specs = [pl.BlockSpec((None, VT_ROWS, BLOCK_Q), t(prev_map)),
               pl.BlockSpec((None, VT_ROWS, qb * BLOCK_Q), t(cur_map)),
               pl.BlockSpec((None, VT_ROWS, BLOCK_Q), t(next_map))]
    return pl.pallas_call(
        functools.partial(_attn_b_kernel, nb=nb, qb=qb),
        out_shape=jax.ShapeDtypeStruct((N_Q_HEADS, n, HEAD_DIM), F32),
        grid=(batch, N_KV_HEADS, steps),
        in_specs=[pl.BlockSpec((Q_PER_KV, qb * BLOCK_Q, HEAD_DIM), cur_map)] + k_specs + v_specs + [
            pl.BlockSpec((None, NEAR, rows), lambda b, g, i: (g, 0, 0)),
            pl.BlockSpec((None, None, 1, rows), lambda b, g, i: (layer, g, 0, 0)),
        ],
        out_specs=pl.BlockSpec((Q_PER_KV, qb * BLOCK_Q, HEAD_DIM), cur_map),
        compiler_params=_cparams("arbitrary", "arbitrary", "arbitrary"),
        name="attn_window",
    )(q, k, k, k, v1t, v1t, v1t, bias_t, sink_t)


def _attn_c_kernel(q_ref, qn_ref, k_ref, v_ref, bias_ref, cfar_ref, kn_ref, lam_ref, gs_ref, o_ref,
                   m_ref, acct_ref, s_ref, *, n_iter, unroll, tks, nblk, out_scale):
    i = pl.program_id(2)
    rows = Q_PER_KV * BLOCK_Q
    blocks_per_sub = tks // BLOCK_Q
    blocks_per_iter = unroll * blocks_per_sub

    def stack_maps(ref):
        q = ref[...].reshape(rows, HEAD_DIM)
        lo = lax.broadcasted_iota(jnp.int32, q.shape, 1) < DIFF_DIM
        zero = jnp.zeros_like(q)
        return jnp.concatenate([jnp.where(lo, q, zero), jnp.where(lo, zero, q)], axis=0)

    q2, q2_next = stack_maps(q_ref), stack_maps(qn_ref)
    acct_ref[...] = jnp.zeros_like(acct_ref)

    @pl.when(i == 0)
    def _():
        s_ref[...] = _dot_nt(k_ref[0:tks, :], q2)

    run = functools.partial(_bounded_iteration, q=q2, q_next=q2_next, k_ref=k_ref, v_ref=v_ref,
                            acct_ref=acct_ref, s_ref=s_ref, unroll=unroll, tks=tks, n_sub=n_iter * unroll)

    c_left, c_right, c_max = cfar_ref[0], cfar_ref[1], cfar_ref[2]
    t_lo = jnp.maximum(i - 1, 0) // blocks_per_iter
    t_hi = jnp.minimum(i + 1, nblk - 1) // blocks_per_iter
    two = lambda x: jnp.concatenate([x, x], axis=1)
    shift = _score_bound(q2, kn_ref[:, :1]) + two(c_max)

    def add_bias(idx, s):
        blks = []
        for kb in range(blocks_per_sub):
            d = idx * blocks_per_sub + kb - i
            blks.append(bias_ref[jnp.clip(d + 2, 0, NEAR // BLOCK_Q + 1)])
        bias = jnp.concatenate(blks, axis=0)
        return jnp.concatenate([s[:, :rows] + bias, s[:, rows:] + bias], axis=1)

    def body(t, carry):
        is_near = (t >= t_lo) & (t <= t_hi)

        @pl.when(jnp.logical_not(is_near))
        def _():
            side = jnp.where(t < t_lo, c_left, c_right)
            run(t, shift=shift - two(side))

        @pl.when(is_near)
        def _():
            run(t, shift=shift, bias_fn=add_bias)

        return carry

    lax.fori_loop(0, n_iter, body, 0)

    @pl.when(jnp.logical_not(_denominators_ok(acct_ref)))
    def _():
        _exact_pass(q2, k_ref, v_ref, m_ref, acct_ref, n_sub=n_iter * unroll, tks=tks, bias_fn=add_bias)

    acct = acct_ref[...]
    o1_t = acct[:HEAD_DIM, :rows] / acct[HEAD_DIM:HEAD_DIM + 1, :rows]
    o2_t = acct[:HEAD_DIM, rows:] / acct[HEAD_DIM:HEAD_DIM + 1, rows:]
    o = (o1_t - lam_ref[:, :1] * o2_t).T
    ms = jnp.mean(o * o, axis=-1, keepdims=True)
    o = o * lax.rsqrt(ms + EPS) * gs_ref[...] * out_scale
    o_ref[...] = o.reshape(o_ref.shape)


def _attn_c(q, k, v1t, bias_t, cfar_t, key_norm, lam, g_subln, *, layer, batch, seq, tks, unroll):
    n = q.shape[1]
    nq = seq // BLOCK_Q
    rows = Q_PER_KV * BLOCK_Q
    lam_init = 0.8 - 0.6 * math.exp(-0.3 * layer)
    q_map = lambda b, g, i: (g, b * nq + i, 0)
    q_next_map = lambda b, g, i: (g, b * nq + jnp.minimum(i + 1, nq - 1), 0)
    return pl.pallas_call(
        functools.partial(_attn_c_kernel, n_iter=seq // (tks * unroll), unroll=unroll, tks=tks,
                          nblk=nq, out_scale=1.0 - lam_init),
        out_shape=jax.ShapeDtypeStruct((N_Q_HEADS, n, HEAD_DIM), F32),
        grid=(batch, N_KV_HEADS, nq),
        in_specs=[
            pl.BlockSpec((Q_PER_KV, BLOCK_Q, HEAD_DIM), q_map),
            pl.BlockSpec((Q_PER_KV, BLOCK_Q, HEAD_DIM), q_next_map),
            pl.BlockSpec((None, seq, HEAD_DIM), lambda b, g, i: (g, b, 0)),
            pl.BlockSpec((None, VT_ROWS, seq), lambda b, g, i: (g, 0, b)),
            pl.BlockSpec((None, NEAR // BLOCK_Q + 2, BLOCK_Q, rows), lambda b, g, i: (g, 0, 0, 0)),
            pl.BlockSpec((3, None, 1, rows), lambda b, g, i: (0, g, 0, 0)),
            pl.BlockSpec((None, 1, HEAD_DIM), lambda b, g, i: (layer, 0, 0)),
            pl.BlockSpec((None, 1, HEAD_DIM), lambda b, g, i: (layer, 0, 0)),
            pl.BlockSpec((None, 1, HEAD_DIM), lambda b, g, i: (layer, 0, 0)),
        ],
        out_specs=pl.BlockSpec((Q_PER_KV, BLOCK_Q, HEAD_DIM), q_map),
        scratch_shapes=[pltpu.VMEM((1, 2 * rows), F32), pltpu.VMEM((VT_ROWS, 2 * rows), F32),
                        pltpu.VMEM((tks, 2 * rows), F32)],
        compiler_params=_cparams("arbitrary", "arbitrary", "arbitrary"),
        name="attn_diff",
    )(q, q, k, v1t, bias_t, cfar_t, key_norm, lam, g_subln)


def _t5_bucket_np(rel):
    half = N_BUCKETS // 2
    max_exact = half // 2
    ret = np.where(rel > 0, half, 0)
    n = np.abs(rel)
    ratio = np.log(np.maximum(n, 1).astype(np.float32) / np.float32(max_exact)) / np.float32(
        math.log(MAX_DISTANCE / max_exact))
    large = max_exact + (ratio * np.float32(half - max_exact)).astype(np.int32)
    large = np.minimum(large, half - 1)
    return (ret + np.where(n < max_exact, n, large)).astype(np.int32)


def _near_bias(table):
    r = np.arange(BLOCK_Q)[:, None]
    c = np.arange(NEAR)[None, :]
    bucket = _t5_bucket_np(c - BLOCK_Q - r)
    onehot = (bucket.reshape(-1, 1) == np.arange(N_BUCKETS)[None, :]).astype(np.float32)
    rows = jnp.dot(jnp.asarray(onehot), table.astype(F32), precision=lax.Precision.HIGHEST)
    return rows.reshape(BLOCK_Q, NEAR, table.shape[1]).transpose(2, 0, 1)


def _rope_tables(seq):
    rows = seq // GRID_W
    row = jnp.repeat(jnp.arange(rows), GRID_W).astype(F32)
    col = jnp.tile(jnp.arange(GRID_W), rows).astype(F32)
    nfreq = HEAD_DIM // 4
    inv = ROPE_THETA ** (-jnp.arange(nfreq, dtype=F32) / nfreq)
    ang_r = row[:, None] * inv
    ang_c = col[:, None] * inv
    ang = jnp.concatenate([ang_r, ang_r, ang_c, ang_c], axis=-1)
    cos, sin = jnp.cos(ang), jnp.sin(ang)
    first = (np.arange(HEAD_DIM) % (HEAD_DIM // 2)) < HEAD_DIM // 4
    return cos, jnp.where(first, -sin, 0.0), jnp.where(first, 0.0, sin)


def _trunk(x, mod, p, *, batch, seq):
    t = _tiles(seq)
    tm = t["tm"]
    rope_tabs = _rope_tables(seq)
    for l in range(DEPTH):
        x = _ffn(x, mod, p["g_norm"], p["wgu"], p["wout"], layer=l, which=0, seq=seq, tm=tm)
        qa, ka, va, qb, kb, vb, qc, kc, vc = _qkv(x, mod, p["g_norm"], p["wqkv"], rope_tabs, p["gh"],
                                                  layer=l, seq=seq, tm=tm)
        oa = _attn_a(qa, ka, va, p["kn_a"], layer=l, batch=batch, seq=seq, tq=t["tq_a"], tks=t["tks"],
                     unroll=t["unroll"])
        ob = _attn_b(qb, kb, vb, p["bias_b"], p["sink"], layer=l, batch=batch, seq=seq, qb=t["qb"])
        oc = _attn_c(qc, kc, vc, p["bias_c"], p["cfar_c"], p["kn_c"], p["lam"], p["g_subln"],
                     layer=l, batch=batch, seq=seq, tks=t["tks"], unroll=t["unroll"])
        x = _out_proj(x, mod, p["g_norm"], oa, ob, oc, p["wgate"], p["wo"], layer=l, seq=seq, tm=tm)
        x = _ffn(x, mod, p["g_norm"], p["wgu"], p["wout"], layer=l, which=1, seq=seq, tm=tm)
    return x


def _prepare(w_ff_in, w_ff_out, w_in, w_o, g_qa, g_ka, g_qb, g_kb, g_qc, g_kc, sink,
             lam_q1, lam_k1, lam_q2, lam_k2, g_subln, rel_bias):
    wg = w_ff_in[..., :D_FF].reshape(DEPTH, 2, D_MODEL, N_FF_CHUNKS, FF_CHUNK)
    wu = w_ff_in[..., D_FF:].reshape(DEPTH, 2, D_MODEL, N_FF_CHUNKS, FF_CHUNK)
    wgu = jnp.concatenate([wg, wu], axis=-1).transpose(0, 1, 3, 2, 4).astype(BF16)
    wout = w_ff_out.reshape(DEPTH, 2, N_FF_CHUNKS, FF_CHUNK, D_MODEL).astype(BF16)
    zeros = jnp.zeros_like(g_qa)
    gh = jnp.stack([g_qa, g_ka, g_qb, g_kb, jnp.tile(g_qc, (1, 2)), jnp.tile(g_kc, (1, 2)), zeros, zeros],
                   axis=1).astype(F32)
    table_b, table_c = rel_bias[:, :N_Q_HEADS], rel_bias[:, N_Q_HEADS:]
    half = N_BUCKETS // 2
    c_left, c_right = table_c[half - 1] * LOG2E, table_c[N_BUCKETS - 1] * LOG2E
    lam_init = jnp.asarray([0.8 - 0.6 * math.exp(-0.3 * l) for l in range(DEPTH)], F32)
    lam = (jnp.exp(jnp.sum(lam_q1.astype(F32) * lam_k1.astype(F32), axis=-1))
           - jnp.exp(jnp.sum(lam_q2.astype(F32) * lam_k2.astype(F32), axis=-1)) + lam_init)

    band = np.abs(np.arange(NEAR)[None, :] - BLOCK_Q - np.arange(BLOCK_Q)[:, None]) <= WINDOW
    near_c = (_near_bias(table_c) * LOG2E).reshape(
        N_KV_HEADS, Q_PER_KV, BLOCK_Q, NEAR // BLOCK_Q, BLOCK_Q).transpose(0, 3, 4, 1, 2).reshape(
        N_KV_HEADS, NEAR // BLOCK_Q, BLOCK_Q, Q_PER_KV * BLOCK_Q)
    cfar_c = jnp.repeat(jnp.stack([c_left, c_right, jnp.max(table_c, axis=0) * LOG2E]).astype(F32),
                        BLOCK_Q, axis=1).reshape(3, N_KV_HEADS, 1, Q_PER_KV * BLOCK_Q)

    def key_norm(g, dim):
        bound = 1.01 * math.sqrt(dim) * jnp.max(jnp.abs(g.astype(F32)), axis=-1)
        return jnp.broadcast_to(bound[:, None, None], (DEPTH, 1, HEAD_DIM))

    return dict(
        wgu=wgu, wout=wout,
        wqkv=w_in[:, :, :QKV_W].astype(BF16), wgate=w_in[:, :, QKV_W:].astype(BF16), wo=w_o.astype(BF16),
        gh=gh,
        bias_b=jnp.where(band, _near_bias(table_b) * LOG2E, NEG).reshape(
            N_KV_HEADS, Q_PER_KV, BLOCK_Q, NEAR).transpose(0, 3, 1, 2).reshape(
            N_KV_HEADS, NEAR, Q_PER_KV * BLOCK_Q),
        sink=jnp.repeat(sink.astype(F32) * LOG2E, BLOCK_Q, axis=1).reshape(
            DEPTH, N_KV_HEADS, 1, Q_PER_KV * BLOCK_Q),
        bias_c=jnp.concatenate([cfar_c[0][:, None] + jnp.zeros((1, 1, BLOCK_Q, 1), F32), near_c,
                                cfar_c[1][:, None] + jnp.zeros((1, 1, BLOCK_Q, 1), F32)], axis=1),
        cfar_c=cfar_c,
        kn_a=key_norm(g_ka, HEAD_DIM), kn_c=key_norm(g_kc, DIFF_DIM),
        lam=jnp.broadcast_to(lam[:, None, None], (DEPTH, 1, HEAD_DIM)),
        g_subln=g_subln.astype(F32)[:, None, :],
    )


def kernel(x_prompt, x_sample, c_prompt, c_sample, w_ada, b_ada, g_norm, w_ff_in, w_ff_out, w_in, w_o,
           g_qa, g_ka, g_qb, g_kb, g_qc, g_kc, sink, lam_q1, lam_k1, lam_q2, lam_k2, g_subln, rel_bias):
    p = _prepare(w_ff_in, w_ff_out, w_in, w_o, g_qa, g_ka, g_qb, g_kb, g_qc, g_kc, sink,
                 lam_q1, lam_k1, lam_q2, lam_k2, g_subln, rel_bias)
    p["g_norm"] = g_norm.astype(F32)
    outs = []
    n_cond = 0
    conds = [c_prompt, c_sample]
    c_all = jnp.concatenate(conds + [jnp.zeros((ADA_ROWS - sum(c.shape[0] for c in conds), D_MODEL), F32)])
    mod_all = _ada(c_all, w_ada, b_ada)
    for x, c in ((x_prompt, c_prompt), (x_sample, c_sample)):
        batch, seq, _ = x.shape
        mod = mod_all[:, n_cond:n_cond + batch].reshape(DEPTH, batch, 9, D_MODEL)
        n_cond += batch
        y = _trunk(x.reshape(batch * seq, D_MODEL), mod, p, batch=batch, seq=seq)
        outs.append(y.reshape(batch, seq, D_MODEL))
    return tuple(outs)
```

```python
import functools
import math

import numpy as np
import jax
import jax.numpy as jnp
from jax import lax
from jax.experimental import pallas as pl
from jax.experimental.pallas import tpu as pltpu

F32 = jnp.float32
BF16 = jnp.bfloat16
ATTN_OUT = jnp.bfloat16

D_MODEL = 1024
DEPTH = 4
N_Q_HEADS = 8
N_KV_HEADS = 2
Q_PER_KV = N_Q_HEADS // N_KV_HEADS
HEAD_DIM = 128
DIFF_DIM = 64
D_FF = 2816
BLOCK_Q = 128
WINDOW = 128
GRID_W = 64
N_BUCKETS = 32
MAX_DISTANCE = 128
ROPE_THETA = 10000.0
EPS = 1e-6
NEG = -1e30
LOG2E = 1.4426950408889634

Q_W = N_Q_HEADS * HEAD_DIM
KV_W = N_KV_HEADS * HEAD_DIM
BRANCH_IN = Q_W + 2 * KV_W
QKV_W = 3 * BRANCH_IN
FF_CHUNK = 256
N_FF_CHUNKS = D_FF // FF_CHUNK
ADA_ROWS = 16
ADA_TN = 1536
NEAR = 3 * BLOCK_Q
VT_ROWS = HEAD_DIM + 16

VMEM_LIMIT = 56 * 1024 * 1024


def _tiles(seq):
    tks = min(512, seq)
    return dict(
        tm=min(512, seq),
        tq_a=2 * BLOCK_Q,
        tks=tks,
        qb=min(8, seq // BLOCK_Q),
        unroll=min(8, seq // tks),
    )


def _cparams(*sem):
    return pltpu.CompilerParams(dimension_semantics=sem, vmem_limit_bytes=VMEM_LIMIT)


def _dot(a, b):
    return jnp.dot(a, b, preferred_element_type=F32)


def _dot_nt(a, b):
    return lax.dot_general(a, b, (((1,), (1,)), ((), ())), preferred_element_type=F32)


def _ada_kernel(c_ref, w_ref, b_ref, o_ref):
    c = c_ref[...]
    a = (c * jax.nn.sigmoid(c)).astype(BF16)
    o_ref[0] = _dot(a, w_ref[0].astype(BF16)) + b_ref[0]


def _ada(c_all, w_ada, b_ada):
    n_out = w_ada.shape[-1]
    return pl.pallas_call(
        _ada_kernel,
        out_shape=jax.ShapeDtypeStruct((DEPTH, ADA_ROWS, n_out), F32),
        grid=(DEPTH, n_out // ADA_TN),
        in_specs=[
            pl.BlockSpec((ADA_ROWS, D_MODEL), lambda l, j: (0, 0)),
            pl.BlockSpec((1, D_MODEL, ADA_TN), lambda l, j: (l, 0, j)),
            pl.BlockSpec((1, 1, ADA_TN), lambda l, j: (l, 0, j)),
        ],
        out_specs=pl.BlockSpec((1, ADA_ROWS, ADA_TN), lambda l, j: (l, 0, j)),
        compiler_params=_cparams("arbitrary", "arbitrary"),
        name="ada",
    )(c_all, w_ada, b_ada.reshape(DEPTH, 1, n_out))


def _modulate(x, mod_ref, g, jj):
    ms = jnp.mean(x * x, axis=-1, keepdims=True)
    y = x * lax.rsqrt(ms + EPS) * g
    return y * (1.0 + mod_ref[0, 3 * jj + 1:3 * jj + 2, :]) + mod_ref[0, 3 * jj:3 * jj + 1, :]


def _ffn_kernel(x_ref, mod_ref, g_ref, wgu_ref, wout_ref, o_ref, nb_ref, acc_ref, *, jj):
    x = x_ref[...]
    nb_ref[...] = _modulate(x, mod_ref, g_ref[jj:jj + 1, :], jj).astype(BF16)
    acc_ref[...] = jnp.zeros_like(acc_ref)

    def body(c, carry):
        h = _dot(nb_ref[...], wgu_ref[c])
        hg = h[:, :FF_CHUNK]
        a = (hg * jax.nn.sigmoid(hg)) * h[:, FF_CHUNK:]
        acc_ref[...] += _dot(a.astype(BF16), wout_ref[c])
        return carry

    lax.fori_loop(0, N_FF_CHUNKS, body, 0, unroll=True)
    o_ref[...] = x + (0.5 * mod_ref[0, 3 * jj + 2:3 * jj + 3, :]) * acc_ref[...]


def _ffn(x, mod, g_norm, wgu, wout, *, layer, which, seq, tm):
    n = x.shape[0]
    jj = 2 * which
    const = dict(pipeline_mode=pl.Buffered(1))
    return pl.pallas_call(
        functools.partial(_ffn_kernel, jj=jj),
        out_shape=jax.ShapeDtypeStruct((n, D_MODEL), F32),
        grid=(n // tm,),
        in_specs=[
            pl.BlockSpec((tm, D_MODEL), lambda i: (i, 0)),
            pl.BlockSpec((None, 1, 9, D_MODEL), lambda i: (layer, (i * tm) // seq, 0, 0)),
            pl.BlockSpec((None, 3, D_MODEL), lambda i: (layer, 0, 0)),
            pl.BlockSpec((None, None, N_FF_CHUNKS, D_MODEL, 2 * FF_CHUNK),
                         lambda i: (layer, which, 0, 0, 0), **const),
            pl.BlockSpec((None, None, N_FF_CHUNKS, FF_CHUNK, D_MODEL),
                         lambda i: (layer, which, 0, 0, 0), **const),
        ],
        out_specs=pl.BlockSpec((tm, D_MODEL), lambda i: (i, 0)),
        scratch_shapes=[pltpu.VMEM((tm, D_MODEL), BF16), pltpu.VMEM((tm, D_MODEL), F32)],
        input_output_aliases={0: 0},
        compiler_params=_cparams("arbitrary"),
        name=f"ffn{which}",
    )(x, mod, g_norm, wgu, wout)


def _head_norm(r, g):
    ms = jnp.mean(r * r, axis=-1, keepdims=True)
    return r * lax.rsqrt(ms + EPS) * g


def _half_norm(r, g2):
    sq = r * r
    lo = lax.broadcasted_iota(jnp.int32, r.shape, 1) < DIFF_DIM
    s_lo = jnp.sum(jnp.where(lo, sq, 0.0), axis=-1, keepdims=True)
    s_hi = jnp.sum(jnp.where(lo, 0.0, sq), axis=-1, keepdims=True)
    ms = jnp.where(lo, s_lo, s_hi) * (1.0 / DIFF_DIM)
    return r * lax.rsqrt(ms + EPS) * g2


def _qkv_kernel(x_ref, mod_ref, g_ref, w_ref, cos_ref, sa_ref, sb_ref, gh_ref,
                qa_ref, ka_ref, va_ref, qb_ref, kb_ref, vb_ref, qc_ref, kc_ref, vc_ref, nb_ref):
    nb_ref[...] = _modulate(x_ref[...], mod_ref, g_ref[1:2, :], 1).astype(BF16)
    cos, sa, sb = cos_ref[...], sa_ref[...], sb_ref[...]

    def rope(r):
        return (r * cos + pltpu.roll(r, HEAD_DIM - 32, 1) * sa + pltpu.roll(r, 32, 1) * sb)

    def proj(col, width):
        return _dot(nb_ref[...], w_ref[:, col:col + width])

    def heads(r, n_heads, fn, out_ref):
        for h in range(n_heads):
            out_ref[h] = fn(r[:, h * HEAD_DIM:(h + 1) * HEAD_DIM]).astype(BF16)

    def values_transposed(r, out_ref):
        for h in range(N_KV_HEADS):
            out_ref[h, :HEAD_DIM, :] = r[:, h * HEAD_DIM:(h + 1) * HEAD_DIM].T.astype(BF16)
            out_ref[h, HEAD_DIM:, :] = jnp.ones((VT_ROWS - HEAD_DIM, r.shape[0]), BF16)

    qa_scale = HEAD_DIM ** -0.5 * LOG2E
    qc_scale = DIFF_DIM ** -0.5 * LOG2E
    g_qa, g_ka, g_qb, g_kb = (gh_ref[i:i + 1, :] for i in range(4))
    g_qc, g_kc = gh_ref[4:5, :], gh_ref[5:6, :]
    col = 0
    heads(proj(col, Q_W), N_Q_HEADS, lambda r: rope(_head_norm(r, g_qa)) * qa_scale, qa_ref)
    col += Q_W
    heads(proj(col, KV_W), N_KV_HEADS, lambda r: rope(_head_norm(r, g_ka)), ka_ref)
    col += KV_W
    values_transposed(proj(col, KV_W),va_ref)
    col += KV_W
    heads(proj(col, Q_W), N_Q_HEADS, lambda r: _head_norm(r, g_qb) * qa_scale, qb_ref)
    col += Q_W
    heads(proj(col, KV_W), N_KV_HEADS, lambda r: _head_norm(r, g_kb), kb_ref)
    col += KV_W
    values_transposed(proj(col, KV_W), vb_ref)
    col += KV_W
    heads(proj(col, Q_W), N_Q_HEADS, lambda r: _half_norm(r, g_qc) * qc_scale, qc_ref)
    col += Q_W
    heads(proj(col, KV_W), N_KV_HEADS, lambda r: _half_norm(r, g_kc), kc_ref)
    col += KV_W
    values_transposed(proj(col, KV_W),vc_ref)


def _qkv(x, mod, g_norm, wqkv, rope_tabs, gh, *, layer, seq, tm):
    n = x.shape[0]
    tiles_per_seq = seq // tm
    q_shape = jax.ShapeDtypeStruct((N_Q_HEADS, n, HEAD_DIM), BF16)
    kv_shape = jax.ShapeDtypeStruct((N_KV_HEADS, n, HEAD_DIM), BF16)
    q_spec = pl.BlockSpec((N_Q_HEADS, tm, HEAD_DIM), lambda i: (0, i, 0))
    kv_spec = pl.BlockSpec((N_KV_HEADS, tm, HEAD_DIM), lambda i: (0, i, 0))
    tab_spec = pl.BlockSpec((tm, HEAD_DIM), lambda i: (i % tiles_per_seq, 0))
    v1_shape = jax.ShapeDtypeStruct((N_KV_HEADS, VT_ROWS, n), BF16)
    v1_spec = pl.BlockSpec((N_KV_HEADS, VT_ROWS, tm), lambda i: (0, 0, i))
    return pl.pallas_call(
        _qkv_kernel,
        out_shape=[q_shape, kv_shape, v1_shape] * 3,
        grid=(n // tm,),
        in_specs=[
            pl.BlockSpec((tm, D_MODEL), lambda i: (i, 0)),
            pl.BlockSpec((None, 1, 9, D_MODEL), lambda i: (layer, (i * tm) // seq, 0, 0)),
            pl.BlockSpec((None, 3, D_MODEL), lambda i: (layer, 0, 0)),
            pl.BlockSpec((None, D_MODEL, QKV_W), lambda i: (layer, 0, 0), pipeline_mode=pl.Buffered(1)),
            tab_spec, tab_spec, tab_spec,
            pl.BlockSpec((None, 8, HEAD_DIM), lambda i: (layer, 0, 0)),
        ],
        out_specs=[q_spec, kv_spec, v1_spec] * 3,
        scratch_shapes=[pltpu.VMEM((tm, D_MODEL), BF16)],
        compiler_params=_cparams("arbitrary"),
        name="qkv",
    )(x, mod, g_norm, wqkv, *rope_tabs, gh)


def _out_kernel(x_ref, mod_ref, g_ref, oa_ref, ob_ref, oc_ref, wg_ref, wo_ref, o_ref, mg_ref):
    x = x_ref[...]
    nb = _modulate(x, mod_ref, g_ref[1:2, :], 1).astype(BF16)
    for br, br_ref in enumerate((oa_ref, ob_ref, oc_ref)):
        gate = jax.nn.sigmoid(_dot(nb, wg_ref[:, br * D_MODEL:(br + 1) * D_MODEL]))
        for h in range(N_Q_HEADS):
            lanes = slice(h * HEAD_DIM, (h + 1) * HEAD_DIM)
            term = gate[:, lanes] * br_ref[h].astype(F32)
            if br == 0:
                mg_ref[:, lanes] = term
            else:
                mg_ref[:, lanes] += term
    o_ref[...] = x + mod_ref[0, 5:6, :] * _dot(mg_ref[...].astype(BF16), wo_ref[...])


def _out_proj(x, mod, g_norm, oa, ob, oc, wgate, wo, *, layer, seq, tm):
    n = x.shape[0]
    o_spec = pl.BlockSpec((N_Q_HEADS, tm, HEAD_DIM), lambda i: (0, i, 0))
    return pl.pallas_call(
        _out_kernel,
        out_shape=jax.ShapeDtypeStruct((n, D_MODEL), F32),
        grid=(n // tm,),
        in_specs=[
            pl.BlockSpec((tm, D_MODEL), lambda i: (i, 0)),
            pl.BlockSpec((None, 1, 9, D_MODEL), lambda i: (layer, (i * tm) // seq, 0, 0)),
            pl.BlockSpec((None, 3, D_MODEL), lambda i: (layer, 0, 0)),
            o_spec, o_spec, o_spec,
            pl.BlockSpec((None, D_MODEL, 3 * D_MODEL), lambda i: (layer, 0, 0), pipeline_mode=pl.Buffered(1)),
            pl.BlockSpec((None, D_MODEL, D_MODEL), lambda i: (layer, 0, 0), pipeline_mode=pl.Buffered(1)),
        ],
        out_specs=pl.BlockSpec((tm, D_MODEL), lambda i: (i, 0)),
        scratch_shapes=[pltpu.VMEM((tm, D_MODEL), F32)],
        input_output_aliases={0: 0},
        compiler_params=_cparams("arbitrary"),
        name="out_proj",
    )(x, mod, g_norm, oa, ob, oc, wgate, wo)


MIN_DENOM = 2.0 ** -64


def _key_slice(idx, tks):
    return pl.ds(pl.multiple_of(idx * tks, tks), tks)


def _score_bound(q, key_norm):
    qf = q.astype(F32)
    qn2 = _dot_nt(jnp.ones((8, q.shape[1]), F32), qf * qf)[:1]
    return jnp.sqrt(qn2) * key_norm


def _bounded_iteration(t, last, q, k_ref, v_ref, acct_ref, s_ref, *, unroll, tks, shift, bias_fn=None):
    acct = acct_ref[...]
    s = s_ref[...]
    for u in range(unroll):
        idx = t * unroll + u
        s_next = None if (last and u == unroll - 1) else _dot_nt(k_ref[_key_slice(idx + 1, tks), :], q)
        if bias_fn is not None:
            s = bias_fn(idx, s)
        acct = acct + _dot(v_ref[:, _key_slice(idx, tks)], jnp.exp2(s - shift).astype(BF16))
        s = s_next
    acct_ref[...] = acct
    if not last:
        s_ref[...] = s


def _exact_pass(q, k_ref, v_ref, m_ref, acct_ref, *, n_sub, tks, bias_fn=None):
    m_ref[...] = jnp.full_like(m_ref, -jnp.inf)
    acct_ref[...] = jnp.zeros_like(acct_ref)

    def body(idx, carry):
        s = _dot_nt(k_ref[_key_slice(idx, tks), :], q)
        if bias_fn is not None:
            s = bias_fn(idx, s)
        m = m_ref[...]
        m_new = jnp.maximum(m, jnp.max(s, axis=0, keepdims=True))
        p = jnp.exp2(s - m_new).astype(BF16)
        acct_ref[...] = jnp.exp2(m - m_new) * acct_ref[...] + _dot(v_ref[:, _key_slice(idx, tks)], p)
        m_ref[...] = m_new
        return carry

    lax.fori_loop(0, n_sub, body, 0)


def _denominators_ok(acct_ref):
    return jnp.min(acct_ref[HEAD_DIM:HEAD_DIM + 1, :]) >= MIN_DENOM


def _attn_a_kernel(q_ref, k_ref, v_ref, kn_ref, o_ref, m_ref, acct_ref, s_ref, *, n_iter, unroll, tks):
    rows = acct_ref.shape[1]
    q = q_ref[...].reshape(rows, HEAD_DIM)
    shift = _score_bound(q, kn_ref[:, :1])
    acct_ref[...] = jnp.zeros_like(acct_ref)
    s_ref[...] = _dot_nt(k_ref[0:tks, :], q)
    refs = (q, k_ref, v_ref, acct_ref, s_ref)

    def body(t, carry):
        _bounded_iteration(t, False, *refs, unroll=unroll, tks=tks, shift=shift)
        return carry

    lax.fori_loop(0, n_iter - 1, body, 0)
    _bounded_iteration(n_iter - 1, True, *refs, unroll=unroll, tks=tks, shift=shift)

    @pl.when(jnp.logical_not(_denominators_ok(acct_ref)))
    def _():
        _exact_pass(q, k_ref, v_ref, m_ref, acct_ref, n_sub=n_iter * unroll, tks=tks)

    acct = acct_ref[...]
    o_t = acct[:HEAD_DIM] / acct[HEAD_DIM:HEAD_DIM + 1]
    o_ref[...] = o_t.T.astype(o_ref.dtype).reshape(o_ref.shape)


def _attn_a(q, k, v1t, key_norm, *, layer, batch, seq, tq, tks, unroll):
    n = q.shape[1]
    nq = seq // tq
    rows = Q_PER_KV * tq
    q_map = lambda b, g, i: (g, b * nq + i, 0)
    return pl.pallas_call(
        functools.partial(_attn_a_kernel, n_iter=seq // (tks * unroll), unroll=unroll, tks=tks),
        out_shape=jax.ShapeDtypeStruct((N_Q_HEADS, n, HEAD_DIM), ATTN_OUT),
        grid=(batch, N_KV_HEADS, nq),
        in_specs=[
            pl.BlockSpec((Q_PER_KV, tq, HEAD_DIM), q_map),
            pl.BlockSpec((None, seq, HEAD_DIM), lambda b, g, i: (g, b, 0)),
            pl.BlockSpec((None, VT_ROWS, seq), lambda b, g, i: (g, 0, b)),
            pl.BlockSpec((None, 1, HEAD_DIM), lambda b, g, i: (layer, 0, 0)),
        ],
        out_specs=pl.BlockSpec((Q_PER_KV, tq, HEAD_DIM), q_map),
        scratch_shapes=[pltpu.VMEM((1, rows), F32), pltpu.VMEM((VT_ROWS, rows), F32),
                        pltpu.VMEM((tks, rows), F32)],
        compiler_params=_cparams("arbitrary", "arbitrary", "arbitrary"),
        name="attn_axial",
    )(q, k, v1t, key_norm)


def _attn_b_kernel(q_ref, kp_ref, kc_ref, kn_ref, vp_ref, vc_ref, vn_ref, bias_ref, sink_ref, o_ref,
                   *, nb, qb):
    i = pl.program_id(2)
    rows = Q_PER_KV * BLOCK_Q
    kwin = jnp.concatenate([kp_ref[...], kc_ref[...], kn_ref[...]], axis=0)
    vwin = jnp.concatenate([vp_ref[...], vc_ref[...], vn_ref[...]], axis=1)
    bias = bias_ref[...]
    sink = sink_ref[...]
    for u in range(qb):
        blk = i * qb + u
        q = q_ref[:, u * BLOCK_Q:(u + 1) * BLOCK_Q, :].reshape(rows, HEAD_DIM)
        s = _dot_nt(kwin[u * BLOCK_Q:u * BLOCK_Q + NEAR], q) + bias
        s = jnp.concatenate([jnp.where(blk > 0, s[:BLOCK_Q], NEG), s[BLOCK_Q:2 * BLOCK_Q],
                             jnp.where(blk < nb - 1, s[2 * BLOCK_Q:], NEG)], axis=0)
        m = jnp.maximum(jnp.max(s, axis=0, keepdims=True), sink)
        p = jnp.exp2(s - m).astype(BF16)
        acct = _dot(vwin[:, u * BLOCK_Q:u * BLOCK_Q + NEAR], p)
        den = acct[HEAD_DIM:HEAD_DIM + 1] + jnp.exp2(sink - m)
        o_t = acct[:HEAD_DIM] / den
        o_ref[:, u * BLOCK_Q:(u + 1) * BLOCK_Q, :] = o_t.T.astype(o_ref.dtype).reshape(
            Q_PER_KV, BLOCK_Q, HEAD_DIM)


def _attn_b(q, k, v1t, bias_t, sink_t, *, layer, batch, seq, qb):
    n = q.shape[1]
    nb = seq // BLOCK_Q
    steps = nb // qb
    rows = Q_PER_KV * BLOCK_Q
    cur_map = lambda b, g, i: (g, b * steps + i, 0)
    prev_map = lambda b, g, i: (g, b * nb + jnp.maximum(i * qb - 1, 0), 0)
    next_map = lambda b, g, i: (g, b * nb + jnp.minimum(i * qb + qb, nb - 1), 0)
    t = lambda index_map: (lambda b, g, i: (index_map(b, g, i)[0], 0, index_map(b, g, i)[1]))
    k_specs = [pl.BlockSpec((None, BLOCK_Q, HEAD_DIM), prev_map),
               pl.BlockSpec((None, qb * BLOCK_Q, HEAD_DIM), cur_map),
               pl.BlockSpec((None, BLOCK_Q, HEAD_DIM), next_map)]
    v_specs = [pl.BlockSpec((None, VT_ROWS, BLOCK_Q), t(prev_map)),
               pl.BlockSpec((None, VT_ROWS, qb * BLOCK_Q), t(cur_map)),
               pl.BlockSpec((None, VT_ROWS, BLOCK_Q), t(next_map))]
    return pl.pallas_call(
        functools.partial(_attn_b_kernel, nb=nb, qb=qb),
        out_shape=jax.ShapeDtypeStruct((N_Q_HEADS, n, HEAD_DIM), ATTN_OUT),
        grid=(batch, N_KV_HEADS, steps),
        in_specs=[pl.BlockSpec((Q_PER_KV, qb * BLOCK_Q, HEAD_DIM), cur_map)] + k_specs + v_specs + [
            pl.BlockSpec((None, NEAR, rows), lambda b, g, i: (g, 0, 0)),
            pl.BlockSpec((None, None, 1, rows), lambda b, g, i: (layer, g, 0, 0)),
        ],
        out_specs=pl.BlockSpec((Q_PER_KV, qb * BLOCK_Q, HEAD_DIM), cur_map),
        compiler_params=_cparams("arbitrary", "arbitrary", "arbitrary"),
        name="attn_window",
    )(q, k, k, k, v1t, v1t, v1t, bias_t, sink_t)


def _attn_c_kernel(q_ref, k_ref, v_ref, bias_ref, cfar_ref, kn_ref, lam_ref, gs_ref, o_ref,
                   m_ref, acct_ref, s_ref, *, n_iter, unroll, tks, nblk, out_scale):
    i = pl.program_id(2)
    rows = Q_PER_KV * BLOCK_Q
    blocks_per_sub = tks // BLOCK_Q
    blocks_per_iter = unroll * blocks_per_sub

    q = q_ref[...].reshape(rows, HEAD_DIM)
    lo = lax.broadcasted_iota(jnp.int32, q.shape, 1) < DIFF_DIM
    zero = jnp.zeros_like(q)
    q2 = jnp.concatenate([jnp.where(lo, q, zero), jnp.where(lo, zero, q)], axis=0)
    acct_ref[...] = jnp.zeros_like(acct_ref)
    s_ref[...] = _dot_nt(k_ref[0:tks, :], q2)
    refs = (q2, k_ref, v_ref, acct_ref, s_ref)

    c_left, c_right, c_max = cfar_ref[0], cfar_ref[1], cfar_ref[2]
    t_lo = jnp.maximum(i - 1, 0) // blocks_per_iter
    t_hi = jnp.minimum(i + 1, nblk - 1) // blocks_per_iter
    two = lambda x: jnp.concatenate([x, x], axis=1)
    shift = _score_bound(q2, kn_ref[:, :1]) + two(c_max)

    def add_bias(idx, s):
        blks = []
        for kb in range(blocks_per_sub):
            d = idx * blocks_per_sub + kb - i
            blks.append(bias_ref[jnp.clip(d + 2, 0, NEAR // BLOCK_Q + 1)])
        bias = jnp.concatenate(blks, axis=0)
        return jnp.concatenate([s[:, :rows] + bias, s[:, rows:] + bias], axis=1)

    def iteration(t, last):
        is_near = (t >= t_lo) & (t <= t_hi)

        @pl.when(jnp.logical_not(is_near))
        def _():
            side = jnp.where(t < t_lo, c_left, c_right)
            _bounded_iteration(t, last, *refs, unroll=unroll, tks=tks, shift=shift - two(side))

        @pl.when(is_near)
        def _():
            _bounded_iteration(t, last, *refs, unroll=unroll, tks=tks, shift=shift, bias_fn=add_bias)

    def body(t, carry):
        iteration(t, False)
        return carry

    lax.fori_loop(0, n_iter - 1, body, 0)
    iteration(n_iter - 1, True)

    @pl.when(jnp.logical_not(_denominators_ok(acct_ref)))
    def _():
        _exact_pass(q2, k_ref, v_ref, m_ref, acct_ref, n_sub=n_iter * unroll, tks=tks, bias_fn=add_bias)

    acct = acct_ref[...]
    o1_t = acct[:HEAD_DIM, :rows] / acct[HEAD_DIM:HEAD_DIM + 1, :rows]
    o2_t = acct[:HEAD_DIM, rows:] / acct[HEAD_DIM:HEAD_DIM + 1, rows:]
    o = (o1_t - lam_ref[:, :1] * o2_t).T
    ms = jnp.mean(o * o, axis=-1, keepdims=True)
    o = o * lax.rsqrt(ms + EPS) * gs_ref[...] * out_scale
    o_ref[...] = o.astype(o_ref.dtype).reshape(o_ref.shape)


def _attn_c(q, k, v1t, bias_t, cfar_t, key_norm, lam, g_subln, *, layer, batch, seq, tks, unroll):
    n = q.shape[1]
    nq = seq // BLOCK_Q
    rows = Q_PER_KV * BLOCK_Q
    lam_init = 0.8 - 0.6 * math.exp(-0.3 * layer)
    q_map = lambda b, g, i: (g, b * nq + i, 0)
    return pl.pallas_call(
        functools.partial(_attn_c_kernel, n_iter=seq // (tks * unroll), unroll=unroll, tks=tks,
                          nblk=nq, out_scale=1.0 - lam_init),
        out_shape=jax.ShapeDtypeStruct((N_Q_HEADS, n, HEAD_DIM), ATTN_OUT),
        grid=(batch, N_KV_HEADS, nq),
        in_specs=[
            pl.BlockSpec((Q_PER_KV, BLOCK_Q, HEAD_DIM), q_map),
            pl.BlockSpec((None, seq, HEAD_DIM), lambda b, g, i: (g, b, 0)),
            pl.BlockSpec((None, VT_ROWS, seq), lambda b, g, i: (g, 0, b)),
            pl.BlockSpec((None, NEAR // BLOCK_Q + 2, BLOCK_Q, rows), lambda b, g, i: (g, 0, 0, 0)),
            pl.BlockSpec((3, None, 1, rows), lambda b, g, i: (0, g, 0, 0)),
            pl.BlockSpec((None, 1, HEAD_DIM), lambda b, g, i: (layer, 0, 0)),
            pl.BlockSpec((None, 1, HEAD_DIM), lambda b, g, i: (layer, 0, 0)),
            pl.BlockSpec((None, 1, HEAD_DIM), lambda b, g, i: (layer, 0, 0)),
        ],
        out_specs=pl.BlockSpec((Q_PER_KV, BLOCK_Q, HEAD_DIM), q_map),
        scratch_shapes=[pltpu.VMEM((1, 2 * rows), F32), pltpu.VMEM((VT_ROWS, 2 * rows), F32),
                        pltpu.VMEM((tks, 2 * rows), F32)],
        compiler_params=_cparams("arbitrary", "arbitrary", "arbitrary"),
        name="attn_diff",
    )(q, k, v1t, bias_t, cfar_t, key_norm, lam, g_subln)


def _t5_bucket_np(rel):
    half = N_BUCKETS // 2
    max_exact = half // 2
    ret = np.where(rel > 0, half, 0)
    n = np.abs(rel)
    ratio = np.log(np.maximum(n, 1).astype(np.float32) / np.float32(max_exact)) / np.float32(
        math.log(MAX_DISTANCE / max_exact))
    large = max_exact + (ratio * np.float32(half - max_exact)).astype(np.int32)
    large = np.minimum(large, half - 1)
    return (ret + np.where(n < max_exact, n, large)).astype(np.int32)


def _near_bias(table):
    r = np.arange(BLOCK_Q)[:, None]
    c = np.arange(NEAR)[None, :]
    bucket = _t5_bucket_np(c - BLOCK_Q - r)
    onehot = (bucket.reshape(-1, 1) == np.arange(N_BUCKETS)[None, :]).astype(np.float32)
    rows = jnp.dot(jnp.asarray(onehot), table.astype(F32), precision=lax.Precision.HIGHEST)
    return rows.reshape(BLOCK_Q, NEAR, table.shape[1]).transpose(2, 0, 1)


def _rope_tables(seq):
    rows = seq // GRID_W
    row = jnp.repeat(jnp.arange(rows), GRID_W).astype(F32)
    col = jnp.tile(jnp.arange(GRID_W), rows).astype(F32)
    nfreq = HEAD_DIM // 4
    inv = ROPE_THETA ** (-jnp.arange(nfreq, dtype=F32) / nfreq)
    ang_r = row[:, None] * inv
    ang_c = col[:, None] * inv
    ang = jnp.concatenate([ang_r, ang_r, ang_c, ang_c], axis=-1)
    cos, sin = jnp.cos(ang), jnp.sin(ang)
    first = (np.arange(HEAD_DIM) % (HEAD_DIM // 2)) < HEAD_DIM // 4
    return cos, jnp.where(first, -sin, 0.0), jnp.where(first, 0.0, sin)


def _trunk(x, mod, p, *, batch, seq):
    t = _tiles(seq)
    tm = t["tm"]
    rope_tabs = _rope_tables(seq)
    for l in range(DEPTH):
        x = _ffn(x, mod, p["g_norm"], p["wgu"], p["wout"], layer=l, which=0, seq=seq, tm=tm)
        qa, ka, va, qb, kb, vb, qc, kc, vc = _qkv(x, mod, p["g_norm"], p["wqkv"], rope_tabs, p["gh"],
                                                  layer=l, seq=seq, tm=tm)
        oa = _attn_a(qa, ka, va, p["kn_a"], layer=l, batch=batch, seq=seq, tq=t["tq_a"], tks=t["tks"],
                     unroll=t["unroll"])
        ob = _attn_b(qb, kb, vb, p["bias_b"], p["sink"], layer=l, batch=batch, seq=seq, qb=t["qb"])
        oc = _attn_c(qc, kc, vc, p["bias_c"], p["cfar_c"], p["kn_c"], p["lam"], p["g_subln"],
                     layer=l, batch=batch, seq=seq, tks=t["tks"], unroll=t["unroll"])
        x = _out_proj(x, mod, p["g_norm"], oa, ob, oc, p["wgate"], p["wo"], layer=l, seq=seq, tm=tm)
        x = _ffn(x, mod, p["g_norm"], p["wgu"], p["wout"], layer=l, which=1, seq=seq, tm=tm)
    return x


def _prepare(w_ff_in, w_ff_out, w_in, w_o, g_qa, g_ka, g_qb, g_kb, g_qc, g_kc, sink,
             lam_q1, lam_k1, lam_q2, lam_k2, g_subln, rel_bias):
    wg = w_ff_in[..., :D_FF].reshape(DEPTH, 2, D_MODEL, N_FF_CHUNKS, FF_CHUNK)
    wu = w_ff_in[..., D_FF:].reshape(DEPTH, 2, D_MODEL, N_FF_CHUNKS, FF_CHUNK)
    wgu = jnp.concatenate([wg, wu], axis=-1).transpose(0, 1, 3, 2, 4).astype(BF16)
    wout = w_ff_out.reshape(DEPTH, 2, N_FF_CHUNKS, FF_CHUNK, D_MODEL).astype(BF16)
    zeros = jnp.zeros_like(g_qa)
    gh = jnp.stack([g_qa, g_ka, g_qb, g_kb, jnp.tile(g_qc, (1, 2)), jnp.tile(g_kc, (1, 2)), zeros, zeros],
                   axis=1).astype(F32)
    table_b, table_c = rel_bias[:, :N_Q_HEADS], rel_bias[:, N_Q_HEADS:]
    half = N_BUCKETS // 2
    c_left, c_right = table_c[half - 1] * LOG2E, table_c[N_BUCKETS - 1] * LOG2E
    lam_init = jnp.asarray([0.8 - 0.6 * math.exp(-0.3 * l) for l in range(DEPTH)], F32)
    lam = (jnp.exp(jnp.sum(lam_q1.astype(F32) * lam_k1.astype(F32), axis=-1))
           - jnp.exp(jnp.sum(lam_q2.astype(F32) * lam_k2.astype(F32), axis=-1)) + lam_init)

    band = np.abs(np.arange(NEAR)[None, :] - BLOCK_Q - np.arange(BLOCK_Q)[:, None]) <= WINDOW
    near_c = (_near_bias(table_c) * LOG2E).reshape(
        N_KV_HEADS, Q_PER_KV, BLOCK_Q, NEAR // BLOCK_Q, BLOCK_Q).transpose(0, 3, 4, 1, 2).reshape(
        N_KV_HEADS, NEAR // BLOCK_Q, BLOCK_Q, Q_PER_KV * BLOCK_Q)
    cfar_c = jnp.repeat(jnp.stack([c_left, c_right, jnp.max(table_c, axis=0) * LOG2E]).astype(F32),
                        BLOCK_Q, axis=1).reshape(3, N_KV_HEADS, 1, Q_PER_KV * BLOCK_Q)

    def key_norm(g, dim):
        bound = 1.01 * math.sqrt(dim) * jnp.max(jnp.abs(g.astype(F32)), axis=-1)
        return jnp.broadcast_to(bound[:, None, None], (DEPTH, 1, HEAD_DIM))

    return dict(
        wgu=wgu, wout=wout,
        wqkv=w_in[:, :, :QKV_W].astype(BF16), wgate=w_in[:, :, QKV_W:].astype(BF16), wo=w_o.astype(BF16),
        gh=gh,
        bias_b=jnp.where(band, _near_bias(table_b) * LOG2E, NEG).reshape(
            N_KV_HEADS, Q_PER_KV, BLOCK_Q, NEAR).transpose(0, 3, 1, 2).reshape(
            N_KV_HEADS, NEAR, Q_PER_KV * BLOCK_Q),
        sink=jnp.repeat(sink.astype(F32) * LOG2E, BLOCK_Q, axis=1).reshape(
            DEPTH, N_KV_HEADS, 1, Q_PER_KV * BLOCK_Q),
        bias_c=jnp.concatenate([cfar_c[0][:, None] + jnp.zeros((1, 1, BLOCK_Q, 1), F32), near_c,
                                cfar_c[1][:, None] + jnp.zeros((1, 1, BLOCK_Q, 1), F32)], axis=1),
        cfar_c=cfar_c,
        kn_a=key_norm(g_ka, HEAD_DIM), kn_c=key_norm(g_kc, DIFF_DIM),
        lam=jnp.broadcast_to(lam[:, None, None], (DEPTH, 1, HEAD_DIM)),
        g_subln=g_subln.astype(F32)[:, None, :],
    )


def kernel(x_prompt, x_sample, c_prompt, c_sample, w_ada, b_ada, g_norm, w_ff_in, w_ff_out, w_in, w_o,
           g_qa, g_ka, g_qb, g_kb, g_qc, g_kc, sink, lam_q1, lam_k1, lam_q2, lam_k2, g_subln, rel_bias):
    p = _prepare(w_ff_in, w_ff_out, w_in, w_o, g_qa, g_ka, g_qb, g_kb, g_qc, g_kc, sink,
                 lam_q1, lam_k1, lam_q2, lam_k2, g_subln, rel_bias)
    p["g_norm"] = g_norm.astype(F32)
    outs = []
    n_cond = 0
    conds = [c_prompt, c_sample]
    c_all = jnp.concatenate(conds + [jnp.zeros((ADA_ROWS - sum(c.shape[0] for c in conds), D_MODEL), F32)])
    mod_all = _ada(c_all, w_ada, b_ada)
    for x, c in ((x_prompt, c_prompt), (x_sample, c_sample)):
        batch, seq, _ = x.shape
        mod = mod_all[:, n_cond:n_cond + batch].reshape(DEPTH, batch, 9, D_MODEL)
        n_cond += batch
        y = _trunk(x.reshape(batch * seq, D_MODEL), mod, p, batch=batch, seq=seq)
        outs.append(y.reshape(batch, seq, D_MODEL))
    return tuple(outs)
```

```python
import functools
import math

import numpy as np
import jax
import jax.numpy as jnp
from jax import lax
from jax.experimental import pallas as pl
from jax.experimental.pallas import tpu as pltpu

F32 = jnp.float32
BF16 = jnp.bfloat16
ATTN_OUT = jnp.bfloat16

D_MODEL = 1024
DEPTH = 4
N_Q_HEADS = 8
N_KV_HEADS = 2
Q_PER_KV = N_Q_HEADS // N_KV_HEADS
HEAD_DIM = 128
DIFF_DIM = 64
D_FF = 2816
BLOCK_Q = 128
WINDOW = 128
GRID_W = 64
N_BUCKETS = 32
MAX_DISTANCE = 128
ROPE_THETA = 10000.0
EPS = 1e-6
NEG = -1e30
LOG2E = 1.4426950408889634

Q_W = N_Q_HEADS * HEAD_DIM
KV_W = N_KV_HEADS * HEAD_DIM
BRANCH_IN = Q_W + 2 * KV_W
QKV_W = 3 * BRANCH_IN
FF_CHUNK = 256
N_FF_CHUNKS = D_FF // FF_CHUNK
ADA_ROWS = 16
ADA_TN = 1536
NEAR = 3 * BLOCK_Q
VT_ROWS = HEAD_DIM + 16

VMEM_LIMIT = 56 * 1024 * 1024


def _tiles(seq):
    tks = min(512, seq)
    single_region = seq <= 8 * tks
    return dict(
        tm=min(512, seq),
        tq_a=(2 if single_region else 1) * BLOCK_Q,
        tks=tks,
        qb=min(8, seq // BLOCK_Q),
        unroll=min(8, seq // tks),
    )


def _cparams(*sem):
    return pltpu.CompilerParams(dimension_semantics=sem, vmem_limit_bytes=VMEM_LIMIT)


def _dot(a, b):
    return jnp.dot(a, b, preferred_element_type=F32)


def _dot_nt(a, b):
    return lax.dot_general(a, b, (((1,), (1,)), ((), ())), preferred_element_type=F32)


def _ada_kernel(c_ref, w_ref, b_ref, o_ref):
    c = c_ref[...]
    a = (c * jax.nn.sigmoid(c)).astype(BF16)
    o_ref[0] = _dot(a, w_ref[0].astype(BF16)) + b_ref[0]


def _ada(c_all, w_ada, b_ada):
    n_out = w_ada.shape[-1]
    return pl.pallas_call(
        _ada_kernel,
        out_shape=jax.ShapeDtypeStruct((DEPTH, ADA_ROWS, n_out), F32),
        grid=(DEPTH, n_out // ADA_TN),
        in_specs=[
            pl.BlockSpec((ADA_ROWS, D_MODEL), lambda l, j: (0, 0)),
            pl.BlockSpec((1, D_MODEL, ADA_TN), lambda l, j: (l, 0, j)),
            pl.BlockSpec((1, 1, ADA_TN), lambda l, j: (l, 0, j)),
        ],
        out_specs=pl.BlockSpec((1, ADA_ROWS, ADA_TN), lambda l, j: (l, 0, j)),
        compiler_params=_cparams("arbitrary", "arbitrary"),
        name="ada",
    )(c_all, w_ada, b_ada.reshape(DEPTH, 1, n_out))


def _modulate(x, mod_ref, g, jj):
    ms = jnp.mean(x * x, axis=-1, keepdims=True)
    y = x * lax.rsqrt(ms + EPS) * g
    return y * (1.0 + mod_ref[0, 3 * jj + 1:3 * jj + 2, :]) + mod_ref[0, 3 * jj:3 * jj + 1, :]


def _ffn_kernel(x_ref, mod_ref, g_ref, wgu_ref, wout_ref, o_ref, nb_ref, acc_ref, *, jj):
    x = x_ref[...]
    nb_ref[...] = _modulate(x, mod_ref, g_ref[jj:jj + 1, :], jj).astype(BF16)
    acc_ref[...] = jnp.zeros_like(acc_ref)

    def body(c, carry):
        h = _dot(nb_ref[...], wgu_ref[c])
        hg = h[:, :FF_CHUNK]
        a = (hg * jax.nn.sigmoid(hg)) * h[:, FF_CHUNK:]
        acc_ref[...] += _dot(a.astype(BF16), wout_ref[c])
        return carry

    lax.fori_loop(0, N_FF_CHUNKS, body, 0, unroll=True)
    o_ref[...] = x + (0.5 * mod_ref[0, 3 * jj + 2:3 * jj + 3, :]) * acc_ref[...]


def _ffn(x, mod, g_norm, wgu, wout, *, layer, which, seq, tm):
    n = x.shape[0]
    jj = 2 * which
    const = dict(pipeline_mode=pl.Buffered(1))
    return pl.pallas_call(
        functools.partial(_ffn_kernel, jj=jj),
        out_shape=jax.ShapeDtypeStruct((n, D_MODEL), F32),
        grid=(n // tm,),
        in_specs=[
            pl.BlockSpec((tm, D_MODEL), lambda i: (i, 0)),
            pl.BlockSpec((None, 1, 9, D_MODEL), lambda i: (layer, (i * tm) // seq, 0, 0)),
            pl.BlockSpec((None, 3, D_MODEL), lambda i: (layer, 0, 0)),
            pl.BlockSpec((None, None, N_FF_CHUNKS, D_MODEL, 2 * FF_CHUNK),
                         lambda i: (layer, which, 0, 0, 0), **const),
            pl.BlockSpec((None, None, N_FF_CHUNKS, FF_CHUNK, D_MODEL),
                         lambda i: (layer, which, 0, 0, 0), **const),
        ],
        out_specs=pl.BlockSpec((tm, D_MODEL), lambda i: (i, 0)),
        scratch_shapes=[pltpu.VMEM((tm, D_MODEL), BF16), pltpu.VMEM((tm, D_MODEL), F32)],
        input_output_aliases={} if (layer == 0 and which == 0) else {0: 0},
        compiler_params=_cparams("arbitrary"),
        name=f"ffn{which}",
    )(x, mod, g_norm, wgu, wout)


def _head_norm(r, g):
    ms = jnp.mean(r * r, axis=-1, keepdims=True)
    return r * lax.rsqrt(ms + EPS) * g


def _half_norm(r, g2):
    sq = r * r
    lo = lax.broadcasted_iota(jnp.int32, r.shape, 1) < DIFF_DIM
    s_lo = jnp.sum(jnp.where(lo, sq, 0.0), axis=-1, keepdims=True)
    s_hi = jnp.sum(jnp.where(lo, 0.0, sq), axis=-1, keepdims=True)
    ms = jnp.where(lo, s_lo, s_hi) * (1.0 / DIFF_DIM)
    return r * lax.rsqrt(ms + EPS) * g2


def _qkv_kernel(x_ref, mod_ref, g_ref, w_ref, cos_ref, sa_ref, sb_ref, gh_ref,
                qa_ref, ka_ref, va_ref, qb_ref, kb_ref, vb_ref, qc_ref, kc_ref, vc_ref, nb_ref):
    nb_ref[...] = _modulate(x_ref[...], mod_ref, g_ref[1:2, :], 1).astype(BF16)
    cos, sa, sb = cos_ref[...], sa_ref[...], sb_ref[...]

    def rope(r):
        return (r * cos + pltpu.roll(r, HEAD_DIM - 32, 1) * sa + pltpu.roll(r, 32, 1) * sb)

    def proj(col, width):
        return _dot(nb_ref[...], w_ref[:, col:col + width])

    def heads(r, n_heads, fn, out_ref):
        for h in range(n_heads):
            out_ref[h] = fn(r[:, h * HEAD_DIM:(h + 1) * HEAD_DIM]).astype(BF16)

    def values_transposed(r, out_ref):
        for h in range(N_KV_HEADS):
            out_ref[h, :HEAD_DIM, :] = r[:, h * HEAD_DIM:(h + 1) * HEAD_DIM].T.astype(BF16)
            out_ref[h, HEAD_DIM:, :] = jnp.ones((VT_ROWS - HEAD_DIM, r.shape[0]), BF16)

    qa_scale = HEAD_DIM ** -0.5 * LOG2E
    qc_scale = DIFF_DIM ** -0.5 * LOG2E
    g_qa, g_ka, g_qb, g_kb = (gh_ref[i:i + 1, :] for i in range(4))
    g_qc, g_kc = gh_ref[4:5, :], gh_ref[5:6, :]
    col = 0
    heads(proj(col, Q_W), N_Q_HEADS, lambda r: rope(_head_norm(r, g_qa)) * qa_scale, qa_ref)
    col += Q_W
    heads(proj(col, KV_W), N_KV_HEADS, lambda r: rope(_head_norm(r, g_ka)), ka_ref)
    col += KV_W
    values_transposed(proj(col, KV_W),va_ref)
    col += KV_W
    heads(proj(col, Q_W), N_Q_HEADS, lambda r: _head_norm(r, g_qb) * qa_scale, qb_ref)
    col += Q_W
    heads(proj(col, KV_W), N_KV_HEADS, lambda r: _head_norm(r, g_kb), kb_ref)
    col += KV_W
    values_transposed(proj(col, KV_W), vb_ref)
    col += KV_W
    heads(proj(col, Q_W), N_Q_HEADS, lambda r: _half_norm(r, g_qc) * qc_scale, qc_ref)
    col += Q_W
    heads(proj(col, KV_W), N_KV_HEADS, lambda r: _half_norm(r, g_kc), kc_ref)
    col += KV_W
    values_transposed(proj(col, KV_W),vc_ref)


def _qkv(x, mod, g_norm, wqkv, rope_tabs, gh, *, layer, seq, tm):
    n = x.shape[0]
    tiles_per_seq = seq // tm
    q_shape = jax.ShapeDtypeStruct((N_Q_HEADS, n, HEAD_DIM), BF16)
    kv_shape = jax.ShapeDtypeStruct((N_KV_HEADS, n, HEAD_DIM), BF16)
    q_spec = pl.BlockSpec((N_Q_HEADS, tm, HEAD_DIM), lambda i: (0, i, 0))
    kv_spec = pl.BlockSpec((N_KV_HEADS, tm, HEAD_DIM), lambda i: (0, i, 0))
    tab_spec = pl.BlockSpec((tm, HEAD_DIM), lambda i: (i % tiles_per_seq, 0))
    v1_shape = jax.ShapeDtypeStruct((N_KV_HEADS, VT_ROWS, n), BF16)
    v1_spec = pl.BlockSpec((N_KV_HEADS, VT_ROWS, tm), lambda i: (0, 0, i))
    return pl.pallas_call(
        _qkv_kernel,
        out_shape=[q_shape, kv_shape, v1_shape] * 3,
        grid=(n // tm,),
        in_specs=[
            pl.BlockSpec((tm, D_MODEL), lambda i: (i, 0)),
            pl.BlockSpec((None, 1, 9, D_MODEL), lambda i: (layer, (i * tm) // seq, 0, 0)),
            pl.BlockSpec((None, 3, D_MODEL), lambda i: (layer, 0, 0)),
            pl.BlockSpec((None, D_MODEL, QKV_W), lambda i: (layer, 0, 0), pipeline_mode=pl.Buffered(1)),
            tab_spec, tab_spec, tab_spec,
            pl.BlockSpec((None, 8, HEAD_DIM), lambda i: (layer, 0, 0)),
        ],
        out_specs=[q_spec, kv_spec, v1_spec] * 3,
        scratch_shapes=[pltpu.VMEM((tm, D_MODEL), BF16)],
        compiler_params=_cparams("arbitrary"),
        name="qkv",
    )(x, mod, g_norm, wqkv, *rope_tabs, gh)


def _out_kernel(x_ref, mod_ref, g_ref, oa_ref, ob_ref, oc_ref, wg_ref, wo_ref, o_ref, mg_ref):
    x = x_ref[...]
    nb = _modulate(x, mod_ref, g_ref[1:2, :], 1).astype(BF16)
    for br, br_ref in enumerate((oa_ref, ob_ref, oc_ref)):
        gate = jax.nn.sigmoid(_dot(nb, wg_ref[:, br * D_MODEL:(br + 1) * D_MODEL]))
        for h in range(N_Q_HEADS):
            lanes = slice(h * HEAD_DIM, (h + 1) * HEAD_DIM)
            term = gate[:, lanes] * br_ref[h].astype(F32)
            if br == 0:
                mg_ref[:, lanes] = term
            else:
                mg_ref[:, lanes] += term
    o_ref[...] = x + mod_ref[0, 5:6, :] * _dot(mg_ref[...].astype(BF16), wo_ref[...])


def _out_proj(x, mod, g_norm, oa, ob, oc, wgate, wo, *, layer, seq, tm):
    n = x.shape[0]
    o_spec = pl.BlockSpec((N_Q_HEADS, tm, HEAD_DIM), lambda i: (0, i, 0))
    return pl.pallas_call(
        _out_kernel,
        out_shape=jax.ShapeDtypeStruct((n, D_MODEL), F32),
        grid=(n // tm,),
        in_specs=[
            pl.BlockSpec((tm, D_MODEL), lambda i: (i, 0)),
            pl.BlockSpec((None, 1, 9, D_MODEL), lambda i: (layer, (i * tm) // seq, 0, 0)),
            pl.BlockSpec((None, 3, D_MODEL), lambda i: (layer, 0, 0)),
            o_spec, o_spec, o_spec,
            pl.BlockSpec((None, D_MODEL, 3 * D_MODEL), lambda i: (layer, 0, 0), pipeline_mode=pl.Buffered(1)),
            pl.BlockSpec((None, D_MODEL, D_MODEL), lambda i: (layer, 0, 0), pipeline_mode=pl.Buffered(1)),
        ],
        out_specs=pl.BlockSpec((tm, D_MODEL), lambda i: (i, 0)),
        scratch_shapes=[pltpu.VMEM((tm, D_MODEL), F32)],
        input_output_aliases={0: 0},
        compiler_params=_cparams("arbitrary"),
        name="out_proj",
    )(x, mod, g_norm, oa, ob, oc, wgate, wo)


MIN_DENOM = 2.0 ** -64


def _key_slice(idx, tks):
    return pl.ds(pl.multiple_of(idx * tks, tks), tks)


def _score_bound(q, key_norm):
    qf = q.astype(F32)
    qn2 = _dot_nt(jnp.ones((8, q.shape[1]), F32), qf * qf)[:1]
    return jnp.sqrt(qn2) * key_norm


def _bounded_iteration(t, last, q, k_ref, v_ref, acct_ref, s_ref, *, unroll, tks, shift, bias_fn=None):
    acct = acct_ref[...]
    s = s_ref[...]
    for u in range(unroll):
        idx = t * unroll + u
        s_next = None if (last and u == unroll - 1) else _dot_nt(k_ref[_key_slice(idx + 1, tks), :], q)
        if bias_fn is not None:
            s = bias_fn(idx, s)
        acct = acct + _dot(v_ref[:, _key_slice(idx, tks)], jnp.exp2(s - shift).astype(BF16))
        s = s_next
    acct_ref[...] = acct
    if not last:
        s_ref[...] = s


def _exact_pass(q, k_ref, v_ref, m_ref, acct_ref, *, n_sub, tks, bias_fn=None):
    m_ref[...] = jnp.full_like(m_ref, -jnp.inf)
    acct_ref[...] = jnp.zeros_like(acct_ref)

    def body(idx, carry):
        s = _dot_nt(k_ref[_key_slice(idx, tks), :], q)
        if bias_fn is not None:
            s = bias_fn(idx, s)
        m = m_ref[...]
        m_new = jnp.maximum(m, jnp.max(s, axis=0, keepdims=True))
        p = jnp.exp2(s - m_new).astype(BF16)
        acct_ref[...] = jnp.exp2(m - m_new) * acct_ref[...] + _dot(v_ref[:, _key_slice(idx, tks)], p)
        m_ref[...] = m_new
        return carry

    lax.fori_loop(0, n_sub, body, 0)


def _denominators_ok(acct_ref):
    return jnp.min(acct_ref[HEAD_DIM:HEAD_DIM + 1, :]) >= MIN_DENOM


def _attn_a_kernel(q_ref, k_ref, v_ref, kn_ref, o_ref, m_ref, acct_ref, s_ref, *, n_iter, unroll, tks):
    rows = acct_ref.shape[1]
    q = q_ref[...].reshape(rows, HEAD_DIM)
    shift = _score_bound(q, kn_ref[:, :1])
    acct_ref[...] = jnp.zeros_like(acct_ref)
    s_ref[...] = _dot_nt(k_ref[0:tks, :], q)
    refs = (q, k_ref, v_ref, acct_ref, s_ref)

    def body(t, carry):
        _bounded_iteration(t, False, *refs, unroll=unroll, tks=tks, shift=shift)
        return carry

    lax.fori_loop(0, n_iter - 1, body, 0)
    _bounded_iteration(n_iter - 1, True, *refs, unroll=unroll, tks=tks, shift=shift)

    @pl.when(jnp.logical_not(_denominators_ok(acct_ref)))
    def _():
        _exact_pass(q, k_ref, v_ref, m_ref, acct_ref, n_sub=n_iter * unroll, tks=tks)

    acct = acct_ref[...]
    o_t = acct[:HEAD_DIM] / acct[HEAD_DIM:HEAD_DIM + 1]
    o_ref[...] = o_t.T.astype(o_ref.dtype).reshape(o_ref.shape)


def _attn_a(q, k, v1t, key_norm, *, layer, batch, seq, tq, tks, unroll):
    n = q.shape[1]
    nq = seq // tq
    rows = Q_PER_KV * tq
    q_map = lambda b, g, i: (g, b * nq + i, 0)
    return pl.pallas_call(
        functools.partial(_attn_a_kernel, n_iter=seq // (tks * unroll), unroll=unroll, tks=tks),
        out_shape=jax.ShapeDtypeStruct((N_Q_HEADS, n, HEAD_DIM), ATTN_OUT),
        grid=(batch, N_KV_HEADS, nq),
        in_specs=[
            pl.BlockSpec((Q_PER_KV, tq, HEAD_DIM), q_map),
            pl.BlockSpec((None, seq, HEAD_DIM), lambda b, g, i: (g, b, 0)),
            pl.BlockSpec((None, VT_ROWS, seq), lambda b, g, i: (g, 0, b)),
            pl.BlockSpec((None, 1, HEAD_DIM), lambda b, g, i: (layer, 0, 0)),
        ],
        out_specs=pl.BlockSpec((Q_PER_KV, tq, HEAD_DIM), q_map),
        scratch_shapes=[pltpu.VMEM((1, rows), F32), pltpu.VMEM((VT_ROWS, rows), F32),
                        pltpu.VMEM((tks, rows), F32)],
        compiler_params=_cparams("arbitrary", "arbitrary", "arbitrary"),
        name="attn_axial",
    )(q, k, v1t, key_norm)


def _attn_b_kernel(q_ref, kp_ref, kc_ref, kn_ref, vp_ref, vc_ref, vn_ref, bias_ref, sink_ref, o_ref,
                   *, nb, qb):
    i = pl.program_id(2)
    rows = Q_PER_KV * BLOCK_Q
    kwin = jnp.concatenate([kp_ref[...], kc_ref[...], kn_ref[...]], axis=0)
    vwin = jnp.concatenate([vp_ref[...], vc_ref[...], vn_ref[...]], axis=1)
    bias = bias_ref[...]
    sink = sink_ref[...]
    for u in range(qb):
        blk = i * qb + u
        q = q_ref[:, u * BLOCK_Q:(u + 1) * BLOCK_Q, :].reshape(rows, HEAD_DIM)
        s = _dot_nt(kwin[u * BLOCK_Q:u * BLOCK_Q + NEAR], q) + bias
        s = jnp.concatenate([jnp.where(blk > 0, s[:BLOCK_Q], NEG), s[BLOCK_Q:2 * BLOCK_Q],
                             jnp.where(blk < nb - 1, s[2 * BLOCK_Q:], NEG)], axis=0)
        m = jnp.maximum(jnp.max(s, axis=0, keepdims=True), sink)
        p = jnp.exp2(s - m).astype(BF16)
        acct = _dot(vwin[:, u * BLOCK_Q:u * BLOCK_Q + NEAR], p)
        den = acct[HEAD_DIM:HEAD_DIM + 1] + jnp.exp2(sink - m)
        o_t = acct[:HEAD_DIM] / den
        o_ref[:, u * BLOCK_Q:(u + 1) * BLOCK_Q, :] = o_t.T.astype(o_ref.dtype).reshape(
            Q_PER_KV, BLOCK_Q, HEAD_DIM)


def _attn_b(q, k, v1t, bias_t, sink_t, *, layer, batch, seq, qb):
    n = q.shape[1]
    nb = seq // BLOCK_Q
    steps = nb // qb
    rows = Q_PER_KV * BLOCK_Q
    cur_map = lambda b, g, i: (g, b * steps + i, 0)
    prev_map = lambda b, g, i: (g, b * nb + jnp.maximum(i * qb - 1, 0), 0)
    next_map = lambda b, g, i: (g, b * nb + jnp.minimum(i * qb + qb, nb - 1), 0)
    t = lambda index_map: (lambda b, g, i: (index_map(b, g, i)[0], 0, index_map(b, g, i)[1]))
    k_specs = [pl.BlockSpec((None, BLOCK_Q, HEAD_DIM), prev_map),
               pl.BlockSpec((None, qb * BLOCK_Q, HEAD_DIM), cur_map),
               pl.BlockSpec((None, BLOCK_Q, HEAD_DIM), next_map)]
    v_specs = [pl.BlockSpec((None, VT_ROWS, BLOCK_Q), t(prev_map)),
               pl.BlockSpec((None, VT_ROWS, qb * BLOCK_Q), t(cur_map)),
               pl.BlockSpec((None, VT_ROWS, BLOCK_Q), t(next_map))]
    return pl.pallas_call(
        functools.partial(_attn_b_kernel, nb=nb, qb=qb),
        out_shape=jax.ShapeDtypeStruct((N_Q_HEADS, n, HEAD_DIM), ATTN_OUT),
        grid=(batch, N_KV_HEADS, steps),
        in_specs=[pl.BlockSpec((Q_PER_KV, qb * BLOCK_Q, HEAD_DIM), cur_map)] + k_specs + v_specs + [
            pl.BlockSpec((None, NEAR, rows), lambda b, g, i: (g, 0, 0)),
            pl.BlockSpec((None, None, 1, rows), lambda b, g, i: (layer, g, 0, 0)),
        ],
        out_specs=pl.BlockSpec((Q_PER_KV, qb * BLOCK_Q, HEAD_DIM), cur_map),
        compiler_params=_cparams("arbitrary", "arbitrary", "arbitrary"),
        name="attn_window",
    )(q, k, k, k, v1t, v1t, v1t, bias_t, sink_t)


def _attn_c_kernel(q_ref, k_ref, v_ref, bias_ref, cfar_ref, kn_ref, lam_ref, gs_ref, o_ref,
                   m_ref, acct_ref, s_ref, *, n_iter, unroll, tks, nblk, out_scale):
    i = pl.program_id(2)
    rows = Q_PER_KV * BLOCK_Q
    blocks_per_sub = tks // BLOCK_Q
    blocks_per_iter = unroll * blocks_per_sub

    q = q_ref[...].reshape(rows, HEAD_DIM)
    lo = lax.broadcasted_iota(jnp.int32, q.shape, 1) < DIFF_DIM
    zero = jnp.zeros_like(q)
    q2 = jnp.concatenate([jnp.where(lo, q, zero), jnp.where(lo, zero, q)], axis=0)
    acct_ref[...] = jnp.zeros_like(acct_ref)
    s_ref[...] = _dot_nt(k_ref[0:tks, :], q2)
    refs = (q2, k_ref, v_ref, acct_ref, s_ref)

    c_left, c_right, c_max = cfar_ref[0], cfar_ref[1], cfar_ref[2]
    t_lo = jnp.maximum(i - 1, 0) // blocks_per_iter
    t_hi = jnp.minimum(i + 1, nblk - 1) // blocks_per_iter
    two = lambda x: jnp.concatenate([x, x], axis=1)
    shift = _score_bound(q2, kn_ref[:, :1]) + two(c_max)

    def add_bias(idx, s):
        blks = []
        for kb in range(blocks_per_sub):
            d = idx * blocks_per_sub + kb - i
            blks.append(bias_ref[jnp.clip(d + 2, 0, NEAR // BLOCK_Q + 1)])
        bias = jnp.concatenate(blks, axis=0)
        return jnp.concatenate([s[:, :rows] + bias, s[:, rows:] + bias], axis=1)

    def iteration(t, last):
        is_near = (t >= t_lo) & (t <= t_hi)

        @pl.when(jnp.logical_not(is_near))
        def _():
            side = jnp.where(t < t_lo, c_left, c_right)
            _bounded_iteration(t, last, *refs, unroll=unroll, tks=tks, shift=shift - two(side))

        @pl.when(is_near)
        def _():
            _bounded_iteration(t, last, *refs, unroll=unroll, tks=tks, shift=shift, bias_fn=add_bias)

    def body(t, carry):
        iteration(t, False)
        return carry

    lax.fori_loop(0, n_iter - 1, body, 0)
    iteration(n_iter - 1, True)

    @pl.when(jnp.logical_not(_denominators_ok(acct_ref)))
    def _():
        _exact_pass(q2, k_ref, v_ref, m_ref, acct_ref, n_sub=n_iter * unroll, tks=tks, bias_fn=add_bias)

    acct = acct_ref[...]
    o1_t = acct[:HEAD_DIM, :rows] / acct[HEAD_DIM:HEAD_DIM + 1, :rows]
    o2_t = acct[:HEAD_DIM, rows:] / acct[HEAD_DIM:HEAD_DIM + 1, rows:]
    o_t = o1_t - lam_ref[:, :1] * o2_t
    ms = jnp.mean(o_t * o_t, axis=0, keepdims=True)
    o = (o_t * lax.rsqrt(ms + EPS)).T * gs_ref[...] * out_scale
    o_ref[...] = o.astype(o_ref.dtype).reshape(o_ref.shape)


def _attn_c(q, k, v1t, bias_t, cfar_t, key_norm, lam, g_subln, *, layer, batch, seq, tks, unroll):
    n = q.shape[1]
    nq = seq // BLOCK_Q
    rows = Q_PER_KV * BLOCK_Q
    lam_init = 0.8 - 0.6 * math.exp(-0.3 * layer)
    q_map = lambda b, g, i: (g, b * nq + i, 0)
    return pl.pallas_call(
        functools.partial(_attn_c_kernel, n_iter=seq // (tks * unroll), unroll=unroll, tks=tks,
                          nblk=nq, out_scale=1.0 - lam_init),
        out_shape=jax.ShapeDtypeStruct((N_Q_HEADS, n, HEAD_DIM), ATTN_OUT),
        grid=(batch, N_KV_HEADS, nq),
        in_specs=[
            pl.BlockSpec((Q_PER_KV, BLOCK_Q, HEAD_DIM), q_map),
            pl.BlockSpec((None, seq, HEAD_DIM), lambda b, g, i: (g, b, 0)),
            pl.BlockSpec((None, VT_ROWS, seq), lambda b, g, i: (g, 0, b)),
            pl.BlockSpec((None, NEAR // BLOCK_Q + 2, BLOCK_Q, rows), lambda b, g, i: (g, 0, 0, 0)),
            pl.BlockSpec((3, None, 1, rows), lambda b, g, i: (0, g, 0, 0)),
            pl.BlockSpec((None, 1, HEAD_DIM), lambda b, g, i: (layer, 0, 0)),
            pl.BlockSpec((None, 1, HEAD_DIM), lambda b, g, i: (layer, 0, 0)),
            pl.BlockSpec((None, 1, HEAD_DIM), lambda b, g, i: (layer, 0, 0)),
        ],
        out_specs=pl.BlockSpec((Q_PER_KV, BLOCK_Q, HEAD_DIM), q_map),
        scratch_shapes=[pltpu.VMEM((1, 2 * rows), F32), pltpu.VMEM((VT_ROWS, 2 * rows), F32),
                        pltpu.VMEM((tks, 2 * rows), F32)],
        compiler_params=_cparams("arbitrary", "arbitrary", "arbitrary"),
        name="attn_diff",
    )(q, k, v1t, bias_t, cfar_t, key_norm, lam, g_subln)


def _t5_bucket_np(rel):
    half = N_BUCKETS // 2
    max_exact = half // 2
    ret = np.where(rel > 0, half, 0)
    n = np.abs(rel)
    ratio = np.log(np.maximum(n, 1).astype(np.float32) / np.float32(max_exact)) / np.float32(
        math.log(MAX_DISTANCE / max_exact))
    large = max_exact + (ratio * np.float32(half - max_exact)).astype(np.int32)
    large = np.minimum(large, half - 1)
    return (ret + np.where(n < max_exact, n, large)).astype(np.int32)


def _near_bias(table):
    r = np.arange(BLOCK_Q)[:, None]
    c = np.arange(NEAR)[None, :]
    bucket = _t5_bucket_np(c - BLOCK_Q - r)
    onehot = (bucket.reshape(-1, 1) == np.arange(N_BUCKETS)[None, :]).astype(np.float32)
    rows = jnp.dot(jnp.asarray(onehot), table.astype(F32), precision=lax.Precision.HIGHEST)
    return rows.reshape(BLOCK_Q, NEAR, table.shape[1]).transpose(2, 0, 1)


def _rope_tables(seq):
    rows = seq // GRID_W
    nfreq = HEAD_DIM // 4
    inv = ROPE_THETA ** (-jnp.arange(nfreq, dtype=F32) / nfreq)
    ang_r = jnp.arange(rows).astype(F32)[:, None] * inv
    ang_c = jnp.arange(GRID_W).astype(F32)[:, None] * inv
    by_row = lambda t: jnp.broadcast_to(t[:, None, :], (rows, GRID_W, nfreq)).reshape(seq, nfreq)
    by_col = lambda t: jnp.broadcast_to(t[None, :, :], (rows, GRID_W, nfreq)).reshape(seq, nfreq)
    table = lambda fn: jnp.concatenate([by_row(fn(ang_r))] * 2 + [by_col(fn(ang_c))] * 2, axis=-1)
    cos, sin = table(jnp.cos), table(jnp.sin)
    first = (np.arange(HEAD_DIM) % (HEAD_DIM // 2)) < HEAD_DIM // 4
    return cos, jnp.where(first, -sin, 0.0), jnp.where(first, 0.0, sin)


def _trunk(x, mod, p, *, batch, seq):
    t = _tiles(seq)
    tm = t["tm"]
    rope_tabs = _rope_tables(seq)
    for l in range(DEPTH):
        x = _ffn(x, mod, p["g_norm"], p["wgu"], p["wout"], layer=l, which=0, seq=seq, tm=tm)
        qa, ka, va, qb, kb, vb, qc, kc, vc = _qkv(x, mod, p["g_norm"], p["wqkv"], rope_tabs, p["gh"],
                                                  layer=l, seq=seq, tm=tm)
        oa = _attn_a(qa, ka, va, p["kn_a"], layer=l, batch=batch, seq=seq, tq=t["tq_a"], tks=t["tks"],
                     unroll=t["unroll"])
        ob = _attn_b(qb, kb, vb, p["bias_b"], p["sink"], layer=l, batch=batch, seq=seq, qb=t["qb"])
        oc = _attn_c(qc, kc, vc, p["bias_c"], p["cfar_c"], p["kn_c"], p["lam"], p["g_subln"],
                     layer=l, batch=batch, seq=seq, tks=t["tks"], unroll=t["unroll"])
        x = _out_proj(x, mod, p["g_norm"], oa, ob, oc, p["wgate"], p["wo"], layer=l, seq=seq, tm=tm)
        x = _ffn(x, mod, p["g_norm"], p["wgu"], p["wout"], layer=l, which=1, seq=seq, tm=tm)
    return x


def _prepare(w_ff_in, w_ff_out, w_in, w_o, g_qa, g_ka, g_qb, g_kb, g_qc, g_kc, sink,
             lam_q1, lam_k1, lam_q2, lam_k2, g_subln, rel_bias):
    wg = w_ff_in[..., :D_FF].reshape(DEPTH, 2, D_MODEL, N_FF_CHUNKS, FF_CHUNK)
    wu = w_ff_in[..., D_FF:].reshape(DEPTH, 2, D_MODEL, N_FF_CHUNKS, FF_CHUNK)
    wgu = jnp.concatenate([wg, wu], axis=-1).transpose(0, 1, 3, 2, 4).astype(BF16)
    wout = w_ff_out.reshape(DEPTH, 2, N_FF_CHUNKS, FF_CHUNK, D_MODEL).astype(BF16)
    zeros = jnp.zeros_like(g_qa)
    gh = jnp.stack([g_qa, g_ka, g_qb, g_kb, jnp.tile(g_qc, (1, 2)), jnp.tile(g_kc, (1, 2)), zeros, zeros],
                   axis=1).astype(F32)
    table_b, table_c = rel_bias[:, :N_Q_HEADS], rel_bias[:, N_Q_HEADS:]
    half = N_BUCKETS // 2
    c_left, c_right = table_c[half - 1] * LOG2E, table_c[N_BUCKETS - 1] * LOG2E
    lam_init = jnp.asarray([0.8 - 0.6 * math.exp(-0.3 * l) for l in range(DEPTH)], F32)
    lam = (jnp.exp(jnp.sum(lam_q1.astype(F32) * lam_k1.astype(F32), axis=-1))
           - jnp.exp(jnp.sum(lam_q2.astype(F32) * lam_k2.astype(F32), axis=-1)) + lam_init)

    band = np.abs(np.arange(NEAR)[None, :] - BLOCK_Q - np.arange(BLOCK_Q)[:, None]) <= WINDOW
    near_c = (_near_bias(table_c) * LOG2E).reshape(
        N_KV_HEADS, Q_PER_KV, BLOCK_Q, NEAR // BLOCK_Q, BLOCK_Q).transpose(0, 3, 4, 1, 2).reshape(
        N_KV_HEADS, NEAR // BLOCK_Q, BLOCK_Q, Q_PER_KV * BLOCK_Q)
    cfar_c = jnp.repeat(jnp.stack([c_left, c_right, jnp.max(table_c, axis=0) * LOG2E]).astype(F32),
                        BLOCK_Q, axis=1).reshape(3, N_KV_HEADS, 1, Q_PER_KV * BLOCK_Q)

    def key_norm(g, dim):
        bound = 1.01 * math.sqrt(dim) * jnp.max(jnp.abs(g.astype(F32)), axis=-1)
        return jnp.broadcast_to(bound[:, None, None], (DEPTH, 1, HEAD_DIM))

    return dict(
        wgu=wgu, wout=wout,
        wqkv=w_in[:, :, :QKV_W].astype(BF16), wgate=w_in[:, :, QKV_W:].astype(BF16), wo=w_o.astype(BF16),
        gh=gh,
        bias_b=jnp.where(band, _near_bias(table_b) * LOG2E, NEG).reshape(
            N_KV_HEADS, Q_PER_KV, BLOCK_Q, NEAR).transpose(0, 3, 1, 2).reshape(
            N_KV_HEADS, NEAR, Q_PER_KV * BLOCK_Q),
        sink=jnp.repeat(sink.astype(F32) * LOG2E, BLOCK_Q, axis=1).reshape(
            DEPTH, N_KV_HEADS, 1, Q_PER_KV * BLOCK_Q),
        bias_c=jnp.concatenate([cfar_c[0][:, None] + jnp.zeros((1, 1, BLOCK_Q, 1), F32), near_c,
                                cfar_c[1][:, None] + jnp.zeros((1, 1, BLOCK_Q, 1), F32)], axis=1),
        cfar_c=cfar_c,
        kn_a=key_norm(g_ka, HEAD_DIM), kn_c=key_norm(g_kc, DIFF_DIM),
        lam=jnp.broadcast_to(lam[:, None, None], (DEPTH, 1, HEAD_DIM)),
        g_subln=g_subln.astype(F32)[:, None, :],
    )


def kernel(x_prompt, x_sample, c_prompt, c_sample, w_ada, b_ada, g_norm, w_ff_in, w_ff_out, w_in, w_o,
           g_qa, g_ka, g_qb, g_kb, g_qc, g_kc, sink, lam_q1, lam_k1, lam_q2, lam_k2, g_subln, rel_bias):
    p = _prepare(w_ff_in, w_ff_out, w_in, w_o, g_qa, g_ka, g_qb, g_kb, g_qc, g_kc, sink,
                 lam_q1, lam_k1, lam_q2, lam_k2, g_subln, rel_bias)
    p["g_norm"] = g_norm.astype(F32)
    outs = []
    n_cond = 0
    conds = [c_prompt, c_sample]
    c_all = jnp.concatenate(conds + [jnp.zeros((ADA_ROWS - sum(c.shape[0] for c in conds), D_MODEL), F32)])
    mod_all = _ada(c_all, w_ada, b_ada)
    for x, c in ((x_prompt, c_prompt), (x_sample, c_sample)):
        batch, seq, _ = x.shape
        mod = mod_all[:, n_cond:n_cond + batch].reshape(DEPTH, batch, 9, D_MODEL)
        n_cond += batch
        y = _trunk(x.reshape(batch * seq, D_MODEL), mod, p, batch=batch, seq=seq)
        outs.append(y.reshape(batch, seq, D_MODEL))
    return tuple(outs)
```

```python
import functools
import math

import numpy as np
import jax
import jax.numpy as jnp
from jax import lax
from jax.experimental import pallas as pl
from jax.experimental.pallas import tpu as pltpu

F32 = jnp.float32
BF16 = jnp.bfloat16
ATTN_OUT = jnp.bfloat16

D_MODEL = 1024
DEPTH = 4
N_Q_HEADS = 8
N_KV_HEADS = 2
Q_PER_KV = N_Q_HEADS // N_KV_HEADS
HEAD_DIM = 128
DIFF_DIM = 64
D_FF = 2816
BLOCK_Q = 128
WINDOW = 128
GRID_W = 64
N_BUCKETS = 32
MAX_DISTANCE = 128
ROPE_THETA = 10000.0
EPS = 1e-6
NEG = -1e30
LOG2E = 1.4426950408889634

Q_W = N_Q_HEADS * HEAD_DIM
KV_W = N_KV_HEADS * HEAD_DIM
BRANCH_IN = Q_W + 2 * KV_W
QKV_W = 3 * BRANCH_IN
FF_CHUNK = 256
N_FF_CHUNKS = D_FF // FF_CHUNK
ADA_ROWS = 16
ADA_TN = 1536
NEAR = 3 * BLOCK_Q
VT_ROWS = HEAD_DIM + 16

VMEM_LIMIT = 56 * 1024 * 1024


def _tiles(seq):
    tks = min(512, seq)
    single_region = seq <= 8 * tks
    return dict(
        tm=min(512, seq),
        tq_a=(2 if single_region else 1) * BLOCK_Q,
        nqb_c=2 if single_region else 1,
        tks=tks,
        qb=min(8, seq // BLOCK_Q),
        unroll=min(8, seq // tks),
    )


def _cparams(*sem):
    return pltpu.CompilerParams(dimension_semantics=sem, vmem_limit_bytes=VMEM_LIMIT)


def _dot(a, b):
    return jnp.dot(a, b, preferred_element_type=F32)


def _dot_nt(a, b):
    return lax.dot_general(a, b, (((1,), (1,)), ((), ())), preferred_element_type=F32)


def _ada_kernel(c_ref, w_ref, b_ref, o_ref):
    c = c_ref[...]
    a = (c * jax.nn.sigmoid(c)).astype(BF16)
    o_ref[0] = _dot(a, w_ref[0].astype(BF16)) + b_ref[0]


def _ada(c_all, w_ada, b_ada):
    n_out = w_ada.shape[-1]
    return pl.pallas_call(
        _ada_kernel,
        out_shape=jax.ShapeDtypeStruct((DEPTH, ADA_ROWS, n_out), F32),
        grid=(DEPTH, n_out // ADA_TN),
        in_specs=[
            pl.BlockSpec((ADA_ROWS, D_MODEL), lambda l, j: (0, 0)),
            pl.BlockSpec((1, D_MODEL, ADA_TN), lambda l, j: (l, 0, j)),
            pl.BlockSpec((1, 1, ADA_TN), lambda l, j: (l, 0, j)),
        ],
        out_specs=pl.BlockSpec((1, ADA_ROWS, ADA_TN), lambda l, j: (l, 0, j)),
        compiler_params=_cparams("arbitrary", "arbitrary"),
        name="ada",
    )(c_all, w_ada, b_ada.reshape(DEPTH, 1, n_out))


def _modulate(x, mod_ref, g, jj):
    ms = jnp.mean(x * x, axis=-1, keepdims=True)
    y = x * lax.rsqrt(ms + EPS) * g
    return y * (1.0 + mod_ref[0, 3 * jj + 1:3 * jj + 2, :]) + mod_ref[0, 3 * jj:3 * jj + 1, :]


def _ffn_kernel(x_ref, mod_ref, g_ref, wgu_ref, wout_ref, o_ref, nb_ref, acc_ref, *, jj):
    x = x_ref[...]
    nb_ref[...] = _modulate(x, mod_ref, g_ref[jj:jj + 1, :], jj).astype(BF16)
    acc_ref[...] = jnp.zeros_like(acc_ref)

    def body(c, carry):
        h = _dot(nb_ref[...], wgu_ref[c])
        hg = h[:, :FF_CHUNK]
        a = (hg * jax.nn.sigmoid(hg)) * h[:, FF_CHUNK:]
        acc_ref[...] += _dot(a.astype(BF16), wout_ref[c])
        return carry

    lax.fori_loop(0, N_FF_CHUNKS, body, 0, unroll=True)
    o_ref[...] = x + (0.5 * mod_ref[0, 3 * jj + 2:3 * jj + 3, :]) * acc_ref[...]


def _ffn(x, mod, g_norm, wgu, wout, *, layer, which, seq, tm):
    n = x.shape[0]
    jj = 2 * which
    const = dict(pipeline_mode=pl.Buffered(1))
    return pl.pallas_call(
        functools.partial(_ffn_kernel, jj=jj),
        out_shape=jax.ShapeDtypeStruct((n, D_MODEL), F32),
        grid=(n // tm,),
        in_specs=[
            pl.BlockSpec((tm, D_MODEL), lambda i: (i, 0)),
            pl.BlockSpec((None, 1, 9, D_MODEL), lambda i: (layer, (i * tm) // seq, 0, 0)),
            pl.BlockSpec((None, 3, D_MODEL), lambda i: (layer, 0, 0)),
            pl.BlockSpec((None, None, N_FF_CHUNKS, D_MODEL, 2 * FF_CHUNK),
                         lambda i: (layer, which, 0, 0, 0), **const),
            pl.BlockSpec((None, None, N_FF_CHUNKS, FF_CHUNK, D_MODEL),
                         lambda i: (layer, which, 0, 0, 0), **const),
        ],
        out_specs=pl.BlockSpec((tm, D_MODEL), lambda i: (i, 0)),
        scratch_shapes=[pltpu.VMEM((tm, D_MODEL), BF16), pltpu.VMEM((tm, D_MODEL), F32)],
        input_output_aliases={} if (layer == 0 and which == 0) else {0: 0},
        compiler_params=_cparams("arbitrary"),
        name=f"ffn{which}",
    )(x, mod, g_norm, wgu, wout)


def _head_norm(r, g):
    ms = jnp.mean(r * r, axis=-1, keepdims=True)
    return r * lax.rsqrt(ms + EPS) * g


def _half_norm(r, g2):
    sq = r * r
    lo = lax.broadcasted_iota(jnp.int32, r.shape, 1) < DIFF_DIM
    s_lo = jnp.sum(jnp.where(lo, sq, 0.0), axis=-1, keepdims=True)
    s_hi = jnp.sum(jnp.where(lo, 0.0, sq), axis=-1, keepdims=True)
    ms = jnp.where(lo, s_lo, s_hi) * (1.0 / DIFF_DIM)
    return r * lax.rsqrt(ms + EPS) * g2


def _qkv_kernel(x_ref, mod_ref, g_ref, w_ref, cos_ref, sa_ref, sb_ref, gh_ref,
                qa_ref, ka_ref, va_ref, qb_ref, kb_ref, vb_ref, qc_ref, kc_ref, vc_ref, nb_ref):
    nb_ref[...] = _modulate(x_ref[...], mod_ref, g_ref[1:2, :], 1).astype(BF16)
    cos, sa, sb = cos_ref[...], sa_ref[...], sb_ref[...]

    def rope(r):
        return (r * cos + pltpu.roll(r, HEAD_DIM - 32, 1) * sa + pltpu.roll(r, 32, 1) * sb)

    def proj(col, width):
        return _dot(nb_ref[...], w_ref[:, col:col + width])

    def heads(r, n_heads, fn, out_ref):
        for h in range(n_heads):
            out_ref[h] = fn(r[:, h * HEAD_DIM:(h + 1) * HEAD_DIM]).astype(BF16)

    def values_transposed(r, out_ref):
        for h in range(N_KV_HEADS):
            out_ref[h, :HEAD_DIM, :] = r[:, h * HEAD_DIM:(h + 1) * HEAD_DIM].T.astype(BF16)
            out_ref[h, HEAD_DIM:, :] = jnp.ones((VT_ROWS - HEAD_DIM, r.shape[0]), BF16)

    qa_scale = HEAD_DIM ** -0.5 * LOG2E
    qc_scale = DIFF_DIM ** -0.5 * LOG2E
    g_qa, g_ka, g_qb, g_kb = (gh_ref[i:i + 1, :] for i in range(4))
    g_qc, g_kc = gh_ref[4:5, :], gh_ref[5:6, :]
    col = 0
    heads(proj(col, Q_W), N_Q_HEADS, lambda r: rope(_head_norm(r, g_qa)) * qa_scale, qa_ref)
    col += Q_W
    heads(proj(col, KV_W), N_KV_HEADS, lambda r: rope(_head_norm(r, g_ka)), ka_ref)
    col += KV_W
    values_transposed(proj(col, KV_W),va_ref)
    col += KV_W
    heads(proj(col, Q_W), N_Q_HEADS, lambda r: _head_norm(r, g_qb) * qa_scale, qb_ref)
    col += Q_W
    heads(proj(col, KV_W), N_KV_HEADS, lambda r: _head_norm(r, g_kb), kb_ref)
    col += KV_W
    values_transposed(proj(col, KV_W), vb_ref)
    col += KV_W
    heads(proj(col, Q_W), N_Q_HEADS, lambda r: _half_norm(r, g_qc) * qc_scale, qc_ref)
    col += Q_W
    heads(proj(col, KV_W), N_KV_HEADS, lambda r: _half_norm(r, g_kc), kc_ref)
    col += KV_W
    values_transposed(proj(col, KV_W),vc_ref)


def _qkv(x, mod, g_norm, wqkv, rope_tabs, gh, *, layer, seq, tm):
    n = x.shape[0]
    tiles_per_seq = seq // tm
    q_shape = jax.ShapeDtypeStruct((N_Q_HEADS, n, HEAD_DIM), BF16)
    kv_shape = jax.ShapeDtypeStruct((N_KV_HEADS, n, HEAD_DIM), BF16)
    q_spec = pl.BlockSpec((N_Q_HEADS, tm, HEAD_DIM), lambda i: (0, i, 0))
    kv_spec = pl.BlockSpec((N_KV_HEADS, tm, HEAD_DIM), lambda i: (0, i, 0))
    tab_spec = pl.BlockSpec((tm, HEAD_DIM), lambda i: (i % tiles_per_seq, 0))
    v1_shape = jax.ShapeDtypeStruct((N_KV_HEADS, VT_ROWS, n), BF16)
    v1_spec = pl.BlockSpec((N_KV_HEADS, VT_ROWS, tm), lambda i: (0, 0, i))
    return pl.pallas_call(
        _qkv_kernel,
        out_shape=[q_shape, kv_shape, v1_shape] * 3,
        grid=(n // tm,),
        in_specs=[
            pl.BlockSpec((tm, D_MODEL), lambda i: (i, 0)),
            pl.BlockSpec((None, 1, 9, D_MODEL), lambda i: (layer, (i * tm) // seq, 0, 0)),
            pl.BlockSpec((None, 3, D_MODEL), lambda i: (layer, 0, 0)),
            pl.BlockSpec((None, D_MODEL, QKV_W), lambda i: (layer, 0, 0), pipeline_mode=pl.Buffered(1)),
            tab_spec, tab_spec, tab_spec,
            pl.BlockSpec((None, 8, HEAD_DIM), lambda i: (layer, 0, 0)),
        ],
        out_specs=[q_spec, kv_spec, v1_spec] * 3,
        scratch_shapes=[pltpu.VMEM((tm, D_MODEL), BF16)],
        compiler_params=_cparams("arbitrary"),
        name="qkv",
    )(x, mod, g_norm, wqkv, *rope_tabs, gh)


def _out_kernel(x_ref, mod_ref, g_ref, oa_ref, ob_ref, oc_ref, wg_ref, wo_ref, o_ref, mg_ref):
    x = x_ref[...]
    nb = _modulate(x, mod_ref, g_ref[1:2, :], 1).astype(BF16)
    for br, br_ref in enumerate((oa_ref, ob_ref, oc_ref)):
        gate = jax.nn.sigmoid(_dot(nb, wg_ref[:, br * D_MODEL:(br + 1) * D_MODEL]))
        for h in range(N_Q_HEADS):
            lanes = slice(h * HEAD_DIM, (h + 1) * HEAD_DIM)
            term = gate[:, lanes] * br_ref[h].astype(F32)
            if br == 0:
                mg_ref[:, lanes] = term
            else:
                mg_ref[:, lanes] += term
    o_ref[...] = x + mod_ref[0, 5:6, :] * _dot(mg_ref[...].astype(BF16), wo_ref[...])


def _out_proj(x, mod, g_norm, oa, ob, oc, wgate, wo, *, layer, seq, tm):
    n = x.shape[0]
    o_spec = pl.BlockSpec((N_Q_HEADS, tm, HEAD_DIM), lambda i: (0, i, 0))
    return pl.pallas_call(
        _out_kernel,
        out_shape=jax.ShapeDtypeStruct((n, D_MODEL), F32),
        grid=(n // tm,),
        in_specs=[
            pl.BlockSpec((tm, D_MODEL), lambda i: (i, 0)),
            pl.BlockSpec((None, 1, 9, D_MODEL), lambda i: (layer, (i * tm) // seq, 0, 0)),
            pl.BlockSpec((None, 3, D_MODEL), lambda i: (layer, 0, 0)),
            o_spec, o_spec, o_spec,
            pl.BlockSpec((None, D_MODEL, 3 * D_MODEL), lambda i: (layer, 0, 0), pipeline_mode=pl.Buffered(1)),
            pl.BlockSpec((None, D_MODEL, D_MODEL), lambda i: (layer, 0, 0), pipeline_mode=pl.Buffered(1)),
        ],
        out_specs=pl.BlockSpec((tm, D_MODEL), lambda i: (i, 0)),
        scratch_shapes=[pltpu.VMEM((tm, D_MODEL), F32)],
        input_output_aliases={0: 0},
        compiler_params=_cparams("arbitrary"),
        name="out_proj",
    )(x, mod, g_norm, oa, ob, oc, wgate, wo)


MIN_DENOM = 2.0 ** -64


def _key_slice(idx, tks):
    return pl.ds(pl.multiple_of(idx * tks, tks), tks)


def _score_bound(q, key_norm):
    qf = q.astype(F32)
    qn2 = _dot_nt(jnp.ones((8, q.shape[1]), F32), qf * qf)[:1]
    return jnp.sqrt(qn2) * key_norm


def _bounded_iteration(t, last, q, k_ref, v_ref, acct_ref, s_ref, *, unroll, tks, shift, bias_fn=None):
    acct = acct_ref[...]
    s = s_ref[...]
    for u in range(unroll):
        idx = t * unroll + u
        s_next = None if (last and u == unroll - 1) else _dot_nt(k_ref[_key_slice(idx + 1, tks), :], q)
        if bias_fn is not None:
            s = bias_fn(idx, s)
        acct = acct + _dot(v_ref[:, _key_slice(idx, tks)], jnp.exp2(s - shift).astype(BF16))
        s = s_next
    acct_ref[...] = acct
    if not last:
        s_ref[...] = s


def _exact_pass(q, k_ref, v_ref, m_ref, acct_ref, *, n_sub, tks, bias_fn=None):
    m_ref[...] = jnp.full_like(m_ref, -jnp.inf)
    acct_ref[...] = jnp.zeros_like(acct_ref)

    def body(idx, carry):
        s = _dot_nt(k_ref[_key_slice(idx, tks), :], q)
        if bias_fn is not None:
            s = bias_fn(idx, s)
        m = m_ref[...]
        m_new = jnp.maximum(m, jnp.max(s, axis=0, keepdims=True))
        p = jnp.exp2(s - m_new).astype(BF16)
        acct_ref[...] = jnp.exp2(m - m_new) * acct_ref[...] + _dot(v_ref[:, _key_slice(idx, tks)], p)
        m_ref[...] = m_new
        return carry

    lax.fori_loop(0, n_sub, body, 0)


def _denominators_ok(acct_ref):
    return jnp.min(acct_ref[HEAD_DIM:HEAD_DIM + 1, :]) >= MIN_DENOM


def _attn_a_kernel(q_ref, k_ref, v_ref, kn_ref, o_ref, m_ref, acct_ref, s_ref, *, n_iter, unroll, tks):
    rows = acct_ref.shape[1]
    q = q_ref[...].reshape(rows, HEAD_DIM)
    shift = _score_bound(q, kn_ref[:, :1])
    acct_ref[...] = jnp.zeros_like(acct_ref)
    s_ref[...] = _dot_nt(k_ref[0:tks, :], q)
    refs = (q, k_ref, v_ref, acct_ref, s_ref)

    def body(t, carry):
        _bounded_iteration(t, False, *refs, unroll=unroll, tks=tks, shift=shift)
        return carry

    lax.fori_loop(0, n_iter - 1, body, 0)
    _bounded_iteration(n_iter - 1, True, *refs, unroll=unroll, tks=tks, shift=shift)

    @pl.when(jnp.logical_not(_denominators_ok(acct_ref)))
    def _():
        _exact_pass(q, k_ref, v_ref, m_ref, acct_ref, n_sub=n_iter * unroll, tks=tks)

    acct = acct_ref[...]
    o_t = acct[:HEAD_DIM] / acct[HEAD_DIM:HEAD_DIM + 1]
    o_ref[...] = o_t.T.astype(o_ref.dtype).reshape(o_ref.shape)


def _attn_a(q, k, v1t, key_norm, *, layer, batch, seq, tq, tks, unroll):
    n = q.shape[1]
    nq = seq // tq
    rows = Q_PER_KV * tq
    q_map = lambda b, g, i: (g, b * nq + i, 0)
    return pl.pallas_call(
        functools.partial(_attn_a_kernel, n_iter=seq // (tks * unroll), unroll=unroll, tks=tks),
        out_shape=jax.ShapeDtypeStruct((N_Q_HEADS, n, HEAD_DIM), ATTN_OUT),
        grid=(batch, N_KV_HEADS, nq),
        in_specs=[
            pl.BlockSpec((Q_PER_KV, tq, HEAD_DIM), q_map),
            pl.BlockSpec((None, seq, HEAD_DIM), lambda b, g, i: (g, b, 0)),
            pl.BlockSpec((None, VT_ROWS, seq), lambda b, g, i: (g, 0, b)),
            pl.BlockSpec((None, 1, HEAD_DIM), lambda b, g, i: (layer, 0, 0)),
        ],
        out_specs=pl.BlockSpec((Q_PER_KV, tq, HEAD_DIM), q_map),
        scratch_shapes=[pltpu.VMEM((1, rows), F32), pltpu.VMEM((VT_ROWS, rows), F32),
                        pltpu.VMEM((tks, rows), F32)],
        compiler_params=_cparams("arbitrary", "arbitrary", "arbitrary"),
        name="attn_axial",
    )(q, k, v1t, key_norm)


def _attn_b_kernel(q_ref, kp_ref, kc_ref, kn_ref, vp_ref, vc_ref, vn_ref, bias_ref, sink_ref, o_ref,
                   *, nb, qb):
    i = pl.program_id(2)
    rows = Q_PER_KV * BLOCK_Q
    kwin = jnp.concatenate([kp_ref[...], kc_ref[...], kn_ref[...]], axis=0)
    vwin = jnp.concatenate([vp_ref[...], vc_ref[...], vn_ref[...]], axis=1)
    bias = bias_ref[...]
    sink = sink_ref[...]
    for u in range(qb):
        blk = i * qb + u
        q = q_ref[:, u * BLOCK_Q:(u + 1) * BLOCK_Q, :].reshape(rows, HEAD_DIM)
        s = _dot_nt(kwin[u * BLOCK_Q:u * BLOCK_Q + NEAR], q) + bias
        s = jnp.concatenate([jnp.where(blk > 0, s[:BLOCK_Q], NEG), s[BLOCK_Q:2 * BLOCK_Q],
                             jnp.where(blk < nb - 1, s[2 * BLOCK_Q:], NEG)], axis=0)
        m = jnp.maximum(jnp.max(s, axis=0, keepdims=True), sink)
        p = jnp.exp2(s - m).astype(BF16)
        acct = _dot(vwin[:, u * BLOCK_Q:u * BLOCK_Q + NEAR], p)
        den = acct[HEAD_DIM:HEAD_DIM + 1] + jnp.exp2(sink - m)
        o_t = acct[:HEAD_DIM] / den
        o_ref[:, u * BLOCK_Q:(u + 1) * BLOCK_Q, :] = o_t.T.astype(o_ref.dtype).reshape(
            Q_PER_KV, BLOCK_Q, HEAD_DIM)


def _attn_b(q, k, v1t, bias_t, sink_t, *, layer, batch, seq, qb):
    n = q.shape[1]
    nb = seq // BLOCK_Q
    steps = nb // qb
    rows = Q_PER_KV * BLOCK_Q
    cur_map = lambda b, g, i: (g, b * steps + i, 0)
    prev_map = lambda b, g, i: (g, b * nb + jnp.maximum(i * qb - 1, 0), 0)
    next_map = lambda b, g, i: (g, b * nb + jnp.minimum(i * qb + qb, nb - 1), 0)
    t = lambda index_map: (lambda b, g, i: (index_map(b, g, i)[0], 0, index_map(b, g, i)[1]))
    k_specs = [pl.BlockSpec((None, BLOCK_Q, HEAD_DIM), prev_map),
               pl.BlockSpec((None, qb * BLOCK_Q, HEAD_DIM), cur_map),
               pl.BlockSpec((None, BLOCK_Q, HEAD_DIM), next_map)]
    v_specs = [pl.BlockSpec((None, VT_ROWS, BLOCK_Q), t(prev_map)),
               pl.BlockSpec((None, VT_ROWS, qb * BLOCK_Q), t(cur_map)),
               pl.BlockSpec((None, VT_ROWS, BLOCK_Q), t(next_map))]
    return pl.pallas_call(
        functools.partial(_attn_b_kernel, nb=nb, qb=qb),
        out_shape=jax.ShapeDtypeStruct((N_Q_HEADS, n, HEAD_DIM), ATTN_OUT),
        grid=(batch, N_KV_HEADS, steps),
        in_specs=[pl.BlockSpec((Q_PER_KV, qb * BLOCK_Q, HEAD_DIM), cur_map)] + k_specs + v_specs + [
            pl.BlockSpec((None, NEAR, rows), lambda b, g, i: (g, 0, 0)),
            pl.BlockSpec((None, None, 1, rows), lambda b, g, i: (layer, g, 0, 0)),
        ],
        out_specs=pl.BlockSpec((Q_PER_KV, qb * BLOCK_Q, HEAD_DIM), cur_map),
        compiler_params=_cparams("arbitrary", "arbitrary", "arbitrary"),
        name="attn_window",
    )(q, k, k, k, v1t, v1t, v1t, bias_t, sink_t)


def _attn_c_kernel(q_ref, k_ref, v_ref, bias_ref, cfar_ref, kn_ref, lam_ref, gs_ref, o_ref,
                   m_ref, acct_ref, s_ref, *, n_iter, unroll, tks, nblk, nqb, out_scale):
    first_blk = pl.program_id(2) * nqb
    rows = Q_PER_KV * nqb * BLOCK_Q
    blocks_per_sub = tks // BLOCK_Q
    blocks_per_iter = unroll * blocks_per_sub

    q = q_ref[...].reshape(rows, HEAD_DIM)
    lo = lax.broadcasted_iota(jnp.int32, q.shape, 1) < DIFF_DIM
    zero = jnp.zeros_like(q)
    q2 = jnp.concatenate([jnp.where(lo, q, zero), jnp.where(lo, zero, q)], axis=0)
    acct_ref[...] = jnp.zeros_like(acct_ref)
    s_ref[...] = _dot_nt(k_ref[0:tks, :], q2)
    refs = (q2, k_ref, v_ref, acct_ref, s_ref)

    c_left, c_right, c_max = cfar_ref[0], cfar_ref[1], cfar_ref[2]
    t_lo = jnp.maximum(first_blk - 1, 0) // blocks_per_iter
    t_hi = jnp.minimum(first_blk + nqb, nblk - 1) // blocks_per_iter
    two = lambda x: jnp.concatenate([x, x], axis=1)
    shift = _score_bound(q2, kn_ref[:, :1]) + two(c_max)

    def add_bias(idx, s):
        blks = []
        for kb in range(blocks_per_sub):
            key_blk = idx * blocks_per_sub + kb
            tiles = [bias_ref[jnp.clip(key_blk - (first_blk + a) + 2, 0, NEAR // BLOCK_Q + 1)]
                     for a in range(nqb)]
            if nqb == 1:
                blks.append(tiles[0])
            else:
                blks.append(jnp.concatenate([tiles[a][:, h * BLOCK_Q:(h + 1) * BLOCK_Q]
                                             for h in range(Q_PER_KV) for a in range(nqb)], axis=1))
        bias = jnp.concatenate(blks, axis=0)
        return jnp.concatenate([s[:, :rows] + bias, s[:, rows:] + bias], axis=1)

    def iteration(t, last):
        is_near = (t >= t_lo) & (t <= t_hi)

        @pl.when(jnp.logical_not(is_near))
        def _():
            side = jnp.where(t < t_lo, c_left, c_right)
            _bounded_iteration(t, last, *refs, unroll=unroll, tks=tks, shift=shift - two(side))

        @pl.when(is_near)
        def _():
            _bounded_iteration(t, last, *refs, unroll=unroll, tks=tks, shift=shift, bias_fn=add_bias)

    def body(t, carry):
        iteration(t, False)
        return carry

    lax.fori_loop(0, n_iter - 1, body, 0)
    iteration(n_iter - 1, True)

    @pl.when(jnp.logical_not(_denominators_ok(acct_ref)))
    def _():
        _exact_pass(q2, k_ref, v_ref, m_ref, acct_ref, n_sub=n_iter * unroll, tks=tks, bias_fn=add_bias)

    acct = acct_ref[...]
    o1_t = acct[:HEAD_DIM, :rows] / acct[HEAD_DIM:HEAD_DIM + 1, :rows]
    o2_t = acct[:HEAD_DIM, rows:] / acct[HEAD_DIM:HEAD_DIM + 1, rows:]
    o_t = o1_t - lam_ref[:, :1] * o2_t
    ms = jnp.mean(o_t * o_t, axis=0, keepdims=True)
    o = (o_t * lax.rsqrt(ms + EPS)).T * gs_ref[...] * out_scale
    o_ref[...] = o.astype(o_ref.dtype).reshape(o_ref.shape)


def _attn_c(q, k, v1t, bias_t, cfar, key_norm, lam, g_subln, *, layer, batch, seq, tks, unroll, nqb):
    n = q.shape[1]
    nq = seq // (nqb * BLOCK_Q)
    rows = Q_PER_KV * nqb * BLOCK_Q
    lam_init = 0.8 - 0.6 * math.exp(-0.3 * layer)
    q_map = lambda b, g, i: (g, b * nq + i, 0)
    cfar_t = jnp.repeat(cfar, nqb * BLOCK_Q, axis=1).reshape(3, N_KV_HEADS, 1, rows)
    return pl.pallas_call(
        functools.partial(_attn_c_kernel, n_iter=seq // (tks * unroll), unroll=unroll, tks=tks,
                          nblk=seq // BLOCK_Q, nqb=nqb, out_scale=1.0 - lam_init),
        out_shape=jax.ShapeDtypeStruct((N_Q_HEADS, n, HEAD_DIM), ATTN_OUT),
        grid=(batch, N_KV_HEADS, nq),
        in_specs=[
            pl.BlockSpec((Q_PER_KV, nqb * BLOCK_Q, HEAD_DIM), q_map),
            pl.BlockSpec((None, seq, HEAD_DIM), lambda b, g, i: (g, b, 0)),
            pl.BlockSpec((None, VT_ROWS, seq), lambda b, g, i: (g, 0, b)),
            pl.BlockSpec((None, NEAR // BLOCK_Q + 2, BLOCK_Q, Q_PER_KV * BLOCK_Q), lambda b, g, i: (g, 0, 0, 0)),
            pl.BlockSpec((3, None, 1, rows), lambda b, g, i: (0, g, 0, 0)),
            pl.BlockSpec((None, 1, HEAD_DIM), lambda b, g, i: (layer, 0, 0)),
            pl.BlockSpec((None, 1, HEAD_DIM), lambda b, g, i: (layer, 0, 0)),
            pl.BlockSpec((None, 1, HEAD_DIM), lambda b, g, i: (layer, 0, 0)),
        ],
        out_specs=pl.BlockSpec((Q_PER_KV, nqb * BLOCK_Q, HEAD_DIM), q_map),
        scratch_shapes=[pltpu.VMEM((1, 2 * rows), F32), pltpu.VMEM((VT_ROWS, 2 * rows), F32),
                        pltpu.VMEM((tks, 2 * rows), F32)],
        compiler_params=_cparams("arbitrary", "arbitrary", "arbitrary"),
        name="attn_diff",
    )(q, k, v1t, bias_t, cfar_t, key_norm, lam, g_subln)


def _t5_bucket_np(rel):
    half = N_BUCKETS // 2
    max_exact = half // 2
    ret = np.where(rel > 0, half, 0)
    n = np.abs(rel)
    ratio = np.log(np.maximum(n, 1).astype(np.float32) / np.float32(max_exact)) / np.float32(
        math.log(MAX_DISTANCE / max_exact))
    large = max_exact + (ratio * np.float32(half - max_exact)).astype(np.int32)
    large = np.minimum(large, half - 1)
    return (ret + np.where(n < max_exact, n, large)).astype(np.int32)


def _near_bias(table):
    r = np.arange(BLOCK_Q)[:, None]
    c = np.arange(NEAR)[None, :]
    bucket = _t5_bucket_np(c - BLOCK_Q - r)
    onehot = (bucket.reshape(-1, 1) == np.arange(N_BUCKETS)[None, :]).astype(np.float32)
    rows = jnp.dot(jnp.asarray(onehot), table.astype(F32), precision=lax.Precision.HIGHEST)
    return rows.reshape(BLOCK_Q, NEAR, table.shape[1]).transpose(2, 0, 1)


def _rope_tables(seq):
    rows = seq // GRID_W
    nfreq = HEAD_DIM // 4
    inv = ROPE_THETA ** (-jnp.arange(nfreq, dtype=F32) / nfreq)
    ang_r = jnp.arange(rows).astype(F32)[:, None] * inv
    ang_c = jnp.arange(GRID_W).astype(F32)[:, None] * inv
    by_row = lambda t: jnp.broadcast_to(t[:, None, :], (rows, GRID_W, nfreq)).reshape(seq, nfreq)
    by_col = lambda t: jnp.broadcast_to(t[None, :, :], (rows, GRID_W, nfreq)).reshape(seq, nfreq)
    table = lambda fn: jnp.concatenate([by_row(fn(ang_r))] * 2 + [by_col(fn(ang_c))] * 2, axis=-1)
    cos, sin = table(jnp.cos), table(jnp.sin)
    first = (np.arange(HEAD_DIM) % (HEAD_DIM // 2)) < HEAD_DIM // 4
    return cos, jnp.where(first, -sin, 0.0), jnp.where(first, 0.0, sin)


def _trunk(x, mod, p, *, batch, seq):
    t = _tiles(seq)
    tm = t["tm"]
    rope_tabs = _rope_tables(seq)
    for l in range(DEPTH):
        x = _ffn(x, mod, p["g_norm"], p["wgu"], p["wout"], layer=l, which=0, seq=seq, tm=tm)
        qa, ka, va, qb, kb, vb, qc, kc, vc = _qkv(x, mod, p["g_norm"], p["wqkv"], rope_tabs, p["gh"],
                                                  layer=l, seq=seq, tm=tm)
        oa = _attn_a(qa, ka, va, p["kn_a"], layer=l, batch=batch, seq=seq, tq=t["tq_a"], tks=t["tks"],
                     unroll=t["unroll"])
        ob = _attn_b(qb, kb, vb, p["bias_b"], p["sink"], layer=l, batch=batch, seq=seq, qb=t["qb"])
        oc = _attn_c(qc, kc, vc, p["bias_c"], p["cfar_c"], p["kn_c"], p["lam"], p["g_subln"],
                     layer=l, batch=batch, seq=seq, tks=t["tks"], unroll=t["unroll"], nqb=t["nqb_c"])
        x = _out_proj(x, mod, p["g_norm"], oa, ob, oc, p["wgate"], p["wo"], layer=l, seq=seq, tm=tm)
        x = _ffn(x, mod, p["g_norm"], p["wgu"], p["wout"], layer=l, which=1, seq=seq, tm=tm)
    return x


def _prepare(w_ff_in, w_ff_out, w_in, w_o, g_qa, g_ka, g_qb, g_kb, g_qc, g_kc, sink,
             lam_q1, lam_k1, lam_q2, lam_k2, g_subln, rel_bias):
    wg = w_ff_in[..., :D_FF].reshape(DEPTH, 2, D_MODEL, N_FF_CHUNKS, FF_CHUNK)
    wu = w_ff_in[..., D_FF:].reshape(DEPTH, 2, D_MODEL, N_FF_CHUNKS, FF_CHUNK)
    wgu = jnp.concatenate([wg, wu], axis=-1).transpose(0, 1, 3, 2, 4).astype(BF16)
    wout = w_ff_out.reshape(DEPTH, 2, N_FF_CHUNKS, FF_CHUNK, D_MODEL).astype(BF16)
    zeros = jnp.zeros_like(g_qa)
    gh = jnp.stack([g_qa, g_ka, g_qb, g_kb, jnp.tile(g_qc, (1, 2)), jnp.tile(g_kc, (1, 2)), zeros, zeros],
                   axis=1).astype(F32)
    table_b, table_c = rel_bias[:, :N_Q_HEADS], rel_bias[:, N_Q_HEADS:]
    half = N_BUCKETS // 2
    c_left, c_right = table_c[half - 1] * LOG2E, table_c[N_BUCKETS - 1] * LOG2E
    lam_init = jnp.asarray([0.8 - 0.6 * math.exp(-0.3 * l) for l in range(DEPTH)], F32)
    lam = (jnp.exp(jnp.sum(lam_q1.astype(F32) * lam_k1.astype(F32), axis=-1))
           - jnp.exp(jnp.sum(lam_q2.astype(F32) * lam_k2.astype(F32), axis=-1)) + lam_init)

    band = np.abs(np.arange(NEAR)[None, :] - BLOCK_Q - np.arange(BLOCK_Q)[:, None]) <= WINDOW
    near_c = (_near_bias(table_c) * LOG2E).reshape(
        N_KV_HEADS, Q_PER_KV, BLOCK_Q, NEAR // BLOCK_Q, BLOCK_Q).transpose(0, 3, 4, 1, 2).reshape(
        N_KV_HEADS, NEAR // BLOCK_Q, BLOCK_Q, Q_PER_KV * BLOCK_Q)
    cfar_heads = jnp.stack([c_left, c_right, jnp.max(table_c, axis=0) * LOG2E]).astype(F32)
    cfar_c = jnp.repeat(cfar_heads, BLOCK_Q, axis=1).reshape(3, N_KV_HEADS, 1, Q_PER_KV * BLOCK_Q)

    def key_norm(g, dim):
        bound = 1.01 * math.sqrt(dim) * jnp.max(jnp.abs(g.astype(F32)), axis=-1)
        return jnp.broadcast_to(bound[:, None, None], (DEPTH, 1, HEAD_DIM))

    return dict(
        wgu=wgu, wout=wout,
        wqkv=w_in[:, :, :QKV_W].astype(BF16), wgate=w_in[:, :, QKV_W:].astype(BF16), wo=w_o.astype(BF16),
        gh=gh,
        bias_b=jnp.where(band, _near_bias(table_b) * LOG2E, NEG).reshape(
            N_KV_HEADS, Q_PER_KV, BLOCK_Q, NEAR).transpose(0, 3, 1, 2).reshape(
            N_KV_HEADS, NEAR, Q_PER_KV * BLOCK_Q),
        sink=jnp.repeat(sink.astype(F32) * LOG2E, BLOCK_Q, axis=1).reshape(
            DEPTH, N_KV_HEADS, 1, Q_PER_KV * BLOCK_Q),
        bias_c=jnp.concatenate([cfar_c[0][:, None] + jnp.zeros((1, 1, BLOCK_Q, 1), F32), near_c,
                                cfar_c[1][:, None] + jnp.zeros((1, 1, BLOCK_Q, 1), F32)], axis=1),
        cfar_c=cfar_heads,
        kn_a=key_norm(g_ka, HEAD_DIM), kn_c=key_norm(g_kc, DIFF_DIM),
        lam=jnp.broadcast_to(lam[:, None, None], (DEPTH, 1, HEAD_DIM)),
        g_subln=g_subln.astype(F32)[:, None, :],
    )


def kernel(x_prompt, x_sample, c_prompt, c_sample, w_ada, b_ada, g_norm, w_ff_in, w_ff_out, w_in, w_o,
           g_qa, g_ka, g_qb, g_kb, g_qc, g_kc, sink, lam_q1, lam_k1, lam_q2, lam_k2, g_subln, rel_bias):
    p = _prepare(w_ff_in, w_ff_out, w_in, w_o, g_qa, g_ka, g_qb, g_kb, g_qc, g_kc, sink,
                 lam_q1, lam_k1, lam_q2, lam_k2, g_subln, rel_bias)
    p["g_norm"] = g_norm.astype(F32)
    outs = []
    n_cond = 0
    conds = [c_prompt, c_sample]
    c_all = jnp.concatenate(conds + [jnp.zeros((ADA_ROWS - sum(c.shape[0] for c in conds), D_MODEL), F32)])
    mod_all = _ada(c_all, w_ada, b_ada)
    for x, c in ((x_prompt, c_prompt), (x_sample, c_sample)):
        batch, seq, _ = x.shape
        mod = mod_all[:, n_cond:n_cond + batch].reshape(DEPTH, batch, 9, D_MODEL)
        n_cond += batch
        y = _trunk(x.reshape(batch * seq, D_MODEL), mod, p, batch=batch, seq=seq)
        outs.append(y.reshape(batch, seq, D_MODEL))
    return tuple(outs)
```

```python
import functools
import math

import numpy as np
import jax
import jax.numpy as jnp
from jax import lax
from jax.experimental import pallas as pl
from jax.experimental.pallas import tpu as pltpu

F32 = jnp.float32
BF16 = jnp.bfloat16
ATTN_OUT = jnp.bfloat16

D_MODEL = 1024
DEPTH = 4
N_Q_HEADS = 8
N_KV_HEADS = 2
Q_PER_KV = N_Q_HEADS // N_KV_HEADS
HEAD_DIM = 128
DIFF_DIM = 64
D_FF = 2816
BLOCK_Q = 128
WINDOW = 128
GRID_W = 64
N_BUCKETS = 32
MAX_DISTANCE = 128
ROPE_THETA = 10000.0
EPS = 1e-6
NEG = -1e30
LOG2E = 1.4426950408889634

Q_W = N_Q_HEADS * HEAD_DIM
KV_W = N_KV_HEADS * HEAD_DIM
BRANCH_IN = Q_W + 2 * KV_W
QKV_W = 3 * BRANCH_IN
FF_CHUNK = 256
N_FF_CHUNKS = D_FF // FF_CHUNK
ADA_ROWS = 16
ADA_TN = 1536
NEAR = 3 * BLOCK_Q
VT_ROWS = HEAD_DIM + 16

VMEM_LIMIT = 56 * 1024 * 1024


def _tiles(seq):
    tks = min(512, seq)
    single_region = seq <= 8 * tks
    return dict(
        tm=min(512, seq),
        tq_a=(2 if single_region else 1) * BLOCK_Q,
        nqb_c=2 if single_region else 1,
        tks=tks,
        qb=min(8, seq // BLOCK_Q),
        unroll=min(8, seq // tks),
        tks_c=tks // 2,
        unroll_c=min(16, 2 * seq // tks),
    )


def _cparams(*sem):
    return pltpu.CompilerParams(dimension_semantics=sem, vmem_limit_bytes=VMEM_LIMIT)


def _dot(a, b):
    return jnp.dot(a, b, preferred_element_type=F32)


def _dot_nt(a, b):
    return lax.dot_general(a, b, (((1,), (1,)), ((), ())), preferred_element_type=F32)


def _ada_kernel(c_ref, w_ref, b_ref, o_ref):
    c = c_ref[...]
    a = (c * jax.nn.sigmoid(c)).astype(BF16)
    o_ref[0] = _dot(a, w_ref[0].astype(BF16)) + b_ref[0]


def _ada(c_all, w_ada, b_ada):
    n_out = w_ada.shape[-1]
    return pl.pallas_call(
        _ada_kernel,
        out_shape=jax.ShapeDtypeStruct((DEPTH, ADA_ROWS, n_out), F32),
        grid=(DEPTH, n_out // ADA_TN),
        in_specs=[
            pl.BlockSpec((ADA_ROWS, D_MODEL), lambda l, j: (0, 0)),
            pl.BlockSpec((1, D_MODEL, ADA_TN), lambda l, j: (l, 0, j)),
            pl.BlockSpec((1, 1, ADA_TN), lambda l, j: (l, 0, j)),
        ],
        out_specs=pl.BlockSpec((1, ADA_ROWS, ADA_TN), lambda l, j: (l, 0, j)),
        compiler_params=_cparams("arbitrary", "arbitrary"),
        name="ada",
    )(c_all, w_ada, b_ada.reshape(DEPTH, 1, n_out))


def _modulate(x, mod_ref, g, jj):
    ms = jnp.mean(x * x, axis=-1, keepdims=True)
    y = x * lax.rsqrt(ms + EPS) * g
    return y * (1.0 + mod_ref[0, 3 * jj + 1:3 * jj + 2, :]) + mod_ref[0, 3 * jj:3 * jj + 1, :]


def _ffn_kernel(x_ref, mod_ref, g_ref, wgu_ref, wout_ref, o_ref, nb_ref, acc_ref, *, jj):
    x = x_ref[...]
    nb_ref[...] = _modulate(x, mod_ref, g_ref[jj:jj + 1, :], jj).astype(BF16)
    acc_ref[...] = jnp.zeros_like(acc_ref)

    def body(c, carry):
        h = _dot(nb_ref[...], wgu_ref[c])
        hg = h[:, :FF_CHUNK]
        a = (hg * jax.nn.sigmoid(hg)) * h[:, FF_CHUNK:]
        acc_ref[...] += _dot(a.astype(BF16), wout_ref[c])
        return carry

    lax.fori_loop(0, N_FF_CHUNKS, body, 0, unroll=True)
    o_ref[...] = x + (0.5 * mod_ref[0, 3 * jj + 2:3 * jj + 3, :]) * acc_ref[...]


def _ffn(x, mod, g_norm, wgu, wout, *, layer, which, seq, tm):
    n = x.shape[0]
    jj = 2 * which
    const = dict(pipeline_mode=pl.Buffered(1))
    return pl.pallas_call(
        functools.partial(_ffn_kernel, jj=jj),
        out_shape=jax.ShapeDtypeStruct((n, D_MODEL), F32),
        grid=(n // tm,),
        in_specs=[
            pl.BlockSpec((tm, D_MODEL), lambda i: (i, 0)),
            pl.BlockSpec((None, 1, 9, D_MODEL), lambda i: (layer, (i * tm) // seq, 0, 0)),
            pl.BlockSpec((None, 3, D_MODEL), lambda i: (layer, 0, 0)),
            pl.BlockSpec((None, None, N_FF_CHUNKS, D_MODEL, 2 * FF_CHUNK),
                         lambda i: (layer, which, 0, 0, 0), **const),
            pl.BlockSpec((None, None, N_FF_CHUNKS, FF_CHUNK, D_MODEL),
                         lambda i: (layer, which, 0, 0, 0), **const),
        ],
        out_specs=pl.BlockSpec((tm, D_MODEL), lambda i: (i, 0)),
        scratch_shapes=[pltpu.VMEM((tm, D_MODEL), BF16), pltpu.VMEM((tm, D_MODEL), F32)],
        input_output_aliases={} if (layer == 0 and which == 0) else {0: 0},
        compiler_params=_cparams("arbitrary"),
        name=f"ffn{which}",
    )(x, mod, g_norm, wgu, wout)


def _head_norm(r, g):
    ms = jnp.mean(r * r, axis=-1, keepdims=True)
    return r * lax.rsqrt(ms + EPS) * g


def _half_norm(r, g2):
    sq = r * r
    lo = lax.broadcasted_iota(jnp.int32, r.shape, 1) < DIFF_DIM
    s_lo = jnp.sum(jnp.where(lo, sq, 0.0), axis=-1, keepdims=True)
    s_hi = jnp.sum(jnp.where(lo, 0.0, sq), axis=-1, keepdims=True)
    ms = jnp.where(lo, s_lo, s_hi) * (1.0 / DIFF_DIM)
    return r * lax.rsqrt(ms + EPS) * g2


def _qkv_kernel(x_ref, mod_ref, g_ref, w_ref, cos_ref, sa_ref, sb_ref, gh_ref,
                qa_ref, ka_ref, va_ref, qb_ref, kb_ref, vb_ref, qc_ref, kc_ref, vc_ref, nb_ref):
    nb_ref[...] = _modulate(x_ref[...], mod_ref, g_ref[1:2, :], 1).astype(BF16)
    cos, sa, sb = cos_ref[...], sa_ref[...], sb_ref[...]

    def rope(r):
        return (r * cos + pltpu.roll(r, HEAD_DIM - 32, 1) * sa + pltpu.roll(r, 32, 1) * sb)

    def proj(col, width):
        return _dot(nb_ref[...], w_ref[:, col:col + width])

    def heads(r, n_heads, fn, out_ref):
        for h in range(n_heads):
            out_ref[h] = fn(r[:, h * HEAD_DIM:(h + 1) * HEAD_DIM]).astype(BF16)

    def values_transposed(r, out_ref):
        for h in range(N_KV_HEADS):
            out_ref[h, :HEAD_DIM, :] = r[:, h * HEAD_DIM:(h + 1) * HEAD_DIM].T.astype(BF16)
            out_ref[h, HEAD_DIM:, :] = jnp.ones((VT_ROWS - HEAD_DIM, r.shape[0]), BF16)

    qa_scale = HEAD_DIM ** -0.5 * LOG2E
    qc_scale = DIFF_DIM ** -0.5 * LOG2E
    g_qa, g_ka, g_qb, g_kb = (gh_ref[i:i + 1, :] for i in range(4))
    g_qc, g_kc = gh_ref[4:5, :], gh_ref[5:6, :]
    col = 0
    heads(proj(col, Q_W), N_Q_HEADS, lambda r: rope(_head_norm(r, g_qa)) * qa_scale, qa_ref)
    col += Q_W
    heads(proj(col, KV_W), N_KV_HEADS, lambda r: rope(_head_norm(r, g_ka)), ka_ref)
    col += KV_W
    values_transposed(proj(col, KV_W),va_ref)
    col += KV_W
    heads(proj(col, Q_W), N_Q_HEADS, lambda r: _head_norm(r, g_qb) * qa_scale, qb_ref)
    col += Q_W
    heads(proj(col, KV_W), N_KV_HEADS, lambda r: _head_norm(r, g_kb), kb_ref)
    col += KV_W
    values_transposed(proj(col, KV_W), vb_ref)
    col += KV_W
    heads(proj(col, Q_W), N_Q_HEADS, lambda r: _half_norm(r, g_qc) * qc_scale, qc_ref)
    col += Q_W
    heads(proj(col, KV_W), N_KV_HEADS, lambda r: _half_norm(r, g_kc), kc_ref)
    col += KV_W
    values_transposed(proj(col, KV_W),vc_ref)


def _qkv(x, mod, g_norm, wqkv, rope_tabs, gh, *, layer, seq, tm):
    n = x.shape[0]
    tiles_per_seq = seq // tm
    q_shape = jax.ShapeDtypeStruct((N_Q_HEADS, n, HEAD_DIM), BF16)
    kv_shape = jax.ShapeDtypeStruct((N_KV_HEADS, n, HEAD_DIM), BF16)
    q_spec = pl.BlockSpec((N_Q_HEADS, tm, HEAD_DIM), lambda i: (0, i, 0))
    kv_spec = pl.BlockSpec((N_KV_HEADS, tm, HEAD_DIM), lambda i: (0, i, 0))
    tab_spec = pl.BlockSpec((tm, HEAD_DIM), lambda i: (i % tiles_per_seq, 0))
    v1_shape = jax.ShapeDtypeStruct((N_KV_HEADS, VT_ROWS, n), BF16)
    v1_spec = pl.BlockSpec((N_KV_HEADS, VT_ROWS, tm), lambda i: (0, 0, i))
    return pl.pallas_call(
        _qkv_kernel,
        out_shape=[q_shape, kv_shape, v1_shape] * 3,
        grid=(n // tm,),
        in_specs=[
            pl.BlockSpec((tm, D_MODEL), lambda i: (i, 0)),
            pl.BlockSpec((None, 1, 9, D_MODEL), lambda i: (layer, (i * tm) // seq, 0, 0)),
            pl.BlockSpec((None, 3, D_MODEL), lambda i: (layer, 0, 0)),
            pl.BlockSpec((None, D_MODEL, QKV_W), lambda i: (layer, 0, 0), pipeline_mode=pl.Buffered(1)),
            tab_spec, tab_spec, tab_spec,
            pl.BlockSpec((None, 8, HEAD_DIM), lambda i: (layer, 0, 0)),
        ],
        out_specs=[q_spec, kv_spec, v1_spec] * 3,
        scratch_shapes=[pltpu.VMEM((tm, D_MODEL), BF16)],
        compiler_params=_cparams("arbitrary"),
        name="qkv",
    )(x, mod, g_norm, wqkv, *rope_tabs, gh)


def _out_kernel(x_ref, mod_ref, g_ref, oa_ref, ob_ref, oc_ref, wg_ref, wo_ref, o_ref, mg_ref):
    x = x_ref[...]
    nb = _modulate(x, mod_ref, g_ref[1:2, :], 1).astype(BF16)
    for br, br_ref in enumerate((oa_ref, ob_ref, oc_ref)):
        gate = jax.nn.sigmoid(_dot(nb, wg_ref[:, br * D_MODEL:(br + 1) * D_MODEL]))
        for h in range(N_Q_HEADS):
            lanes = slice(h * HEAD_DIM, (h + 1) * HEAD_DIM)
            term = gate[:, lanes] * br_ref[h].astype(F32)
            if br == 0:
                mg_ref[:, lanes] = term
            else:
                mg_ref[:, lanes] += term
    o_ref[...] = x + mod_ref[0, 5:6, :] * _dot(mg_ref[...].astype(BF16), wo_ref[...])


def _out_proj(x, mod, g_norm, oa, ob, oc, wgate, wo, *, layer, seq, tm):
    n = x.shape[0]
    o_spec = pl.BlockSpec((N_Q_HEADS, tm, HEAD_DIM), lambda i: (0, i, 0))
    return pl.pallas_call(
        _out_kernel,
        out_shape=jax.ShapeDtypeStruct((n, D_MODEL), F32),
        grid=(n // tm,),
        in_specs=[
            pl.BlockSpec((tm, D_MODEL), lambda i: (i, 0)),
            pl.BlockSpec((None, 1, 9, D_MODEL), lambda i: (layer, (i * tm) // seq, 0, 0)),
            pl.BlockSpec((None, 3, D_MODEL), lambda i: (layer, 0, 0)),
            o_spec, o_spec, o_spec,
            pl.BlockSpec((None, D_MODEL, 3 * D_MODEL), lambda i: (layer, 0, 0), pipeline_mode=pl.Buffered(1)),
            pl.BlockSpec((None, D_MODEL, D_MODEL), lambda i: (layer, 0, 0), pipeline_mode=pl.Buffered(1)),
        ],
        out_specs=pl.BlockSpec((tm, D_MODEL), lambda i: (i, 0)),
        scratch_shapes=[pltpu.VMEM((tm, D_MODEL), F32)],
        input_output_aliases={0: 0},
        compiler_params=_cparams("arbitrary"),
        name="out_proj",
    )(x, mod, g_norm, oa, ob, oc, wgate, wo)


MIN_DENOM = 2.0 ** -64


def _key_slice(idx, tks):
    return pl.ds(pl.multiple_of(idx * tks, tks), tks)


def _score_bound(q, key_norm):
    qf = q.astype(F32)
    qn2 = _dot_nt(jnp.ones((8, q.shape[1]), F32), qf * qf)[:1]
    return jnp.sqrt(qn2) * key_norm


def _bounded_iteration(t, last, q, k_ref, v_ref, acct_ref, s_ref, *, unroll, tks, shift, bias_fn=None):
    acct = acct_ref[...]
    s = s_ref[...]
    for u in range(unroll):
        idx = t * unroll + u
        s_next = None if (last and u == unroll - 1) else _dot_nt(k_ref[_key_slice(idx + 1, tks), :], q)
        if bias_fn is not None:
            s = bias_fn(idx, s)
        acct = acct + _dot(v_ref[:, _key_slice(idx, tks)], jnp.exp2(s - shift).astype(BF16))
        s = s_next
    acct_ref[...] = acct
    if not last:
        s_ref[...] = s


def _exact_pass(q, k_ref, v_ref, m_ref, acct_ref, *, n_sub, tks, bias_fn=None):
    m_ref[...] = jnp.full_like(m_ref, -jnp.inf)
    acct_ref[...] = jnp.zeros_like(acct_ref)

    def body(idx, carry):
        s = _dot_nt(k_ref[_key_slice(idx, tks), :], q)
        if bias_fn is not None:
            s = bias_fn(idx, s)
        m = m_ref[...]
        m_new = jnp.maximum(m, jnp.max(s, axis=0, keepdims=True))
        p = jnp.exp2(s - m_new).astype(BF16)
        acct_ref[...] = jnp.exp2(m - m_new) * acct_ref[...] + _dot(v_ref[:, _key_slice(idx, tks)], p)
        m_ref[...] = m_new
        return carry

    lax.fori_loop(0, n_sub, body, 0)


def _denominators_ok(acct_ref):
    return jnp.min(acct_ref[HEAD_DIM:HEAD_DIM + 1, :]) >= MIN_DENOM


def _attn_a_kernel(q_ref, k_ref, v_ref, kn_ref, o_ref, m_ref, acct_ref, s_ref, *, n_iter, unroll, tks):
    rows = acct_ref.shape[1]
    q = q_ref[...].reshape(rows, HEAD_DIM)
    shift = _score_bound(q, kn_ref[:, :1])
    acct_ref[...] = jnp.zeros_like(acct_ref)
    s_ref[...] = _dot_nt(k_ref[0:tks, :], q)
    refs = (q, k_ref, v_ref, acct_ref, s_ref)

    def body(t, carry):
        _bounded_iteration(t, False, *refs, unroll=unroll, tks=tks, shift=shift)
        return carry

    lax.fori_loop(0, n_iter - 1, body, 0)
    _bounded_iteration(n_iter - 1, True, *refs, unroll=unroll, tks=tks, shift=shift)

    @pl.when(jnp.logical_not(_denominators_ok(acct_ref)))
    def _():
        _exact_pass(q, k_ref, v_ref, m_ref, acct_ref, n_sub=n_iter * unroll, tks=tks)

    acct = acct_ref[...]
    o_t = acct[:HEAD_DIM] / acct[HEAD_DIM:HEAD_DIM + 1]
    o_ref[...] = o_t.T.astype(o_ref.dtype).reshape(o_ref.shape)


def _attn_a(q, k, v1t, key_norm, *, layer, batch, seq, tq, tks, unroll):
    n = q.shape[1]
    nq = seq // tq
    rows = Q_PER_KV * tq
    q_map = lambda b, g, i: (g, b * nq + i, 0)
    return pl.pallas_call(
        functools.partial(_attn_a_kernel, n_iter=seq // (tks * unroll), unroll=unroll, tks=tks),
        out_shape=jax.ShapeDtypeStruct((N_Q_HEADS, n, HEAD_DIM), ATTN_OUT),
        grid=(batch, N_KV_HEADS, nq),
        in_specs=[
            pl.BlockSpec((Q_PER_KV, tq, HEAD_DIM), q_map),
            pl.BlockSpec((None, seq, HEAD_DIM), lambda b, g, i: (g, b, 0)),
            pl.BlockSpec((None, VT_ROWS, seq), lambda b, g, i: (g, 0, b)),
            pl.BlockSpec((None, 1, HEAD_DIM), lambda b, g, i: (layer, 0, 0)),
        ],
        out_specs=pl.BlockSpec((Q_PER_KV, tq, HEAD_DIM), q_map),
        scratch_shapes=[pltpu.VMEM((1, rows), F32), pltpu.VMEM((VT_ROWS, rows), F32),
                        pltpu.VMEM((tks, rows), F32)],
        compiler_params=_cparams("arbitrary", "arbitrary", "arbitrary"),
        name="attn_axial",
    )(q, k, v1t, key_norm)


def _attn_b_kernel(q_ref, kp_ref, kc_ref, kn_ref, vp_ref, vc_ref, vn_ref, bias_ref, sink_ref, o_ref,
                   *, nb, qb):
    i = pl.program_id(2)
    rows = Q_PER_KV * BLOCK_Q
    kwin = jnp.concatenate([kp_ref[...], kc_ref[...], kn_ref[...]], axis=0)
    vwin = jnp.concatenate([vp_ref[...], vc_ref[...], vn_ref[...]], axis=1)
    bias = bias_ref[...]
    sink = sink_ref[...]
    for u in range(qb):
        blk = i * qb + u
        q = q_ref[:, u * BLOCK_Q:(u + 1) * BLOCK_Q, :].reshape(rows, HEAD_DIM)
        s = _dot_nt(kwin[u * BLOCK_Q:u * BLOCK_Q + NEAR], q) + bias
        s = jnp.concatenate([jnp.where(blk > 0, s[:BLOCK_Q], NEG), s[BLOCK_Q:2 * BLOCK_Q],
                             jnp.where(blk < nb - 1, s[2 * BLOCK_Q:], NEG)], axis=0)
        m = jnp.maximum(jnp.max(s, axis=0, keepdims=True), sink)
        p = jnp.exp2(s - m).astype(BF16)
        acct = _dot(vwin[:, u * BLOCK_Q:u * BLOCK_Q + NEAR], p)
        den = acct[HEAD_DIM:HEAD_DIM + 1] + jnp.exp2(sink - m)
        o_t = acct[:HEAD_DIM] / den
        o_ref[:, u * BLOCK_Q:(u + 1) * BLOCK_Q, :] = o_t.T.astype(o_ref.dtype).reshape(
            Q_PER_KV, BLOCK_Q, HEAD_DIM)


def _attn_b(q, k, v1t, bias_t, sink_t, *, layer, batch, seq, qb):
    n = q.shape[1]
    nb = seq // BLOCK_Q
    steps = nb // qb
    rows = Q_PER_KV * BLOCK_Q
    cur_map = lambda b, g, i: (g, b * steps + i, 0)
    prev_map = lambda b, g, i: (g, b * nb + jnp.maximum(i * qb - 1, 0), 0)
    next_map = lambda b, g, i: (g, b * nb + jnp.minimum(i * qb + qb, nb - 1), 0)
    t = lambda index_map: (lambda b, g, i: (index_map(b, g, i)[0], 0, index_map(b, g, i)[1]))
    k_specs = [pl.BlockSpec((None, BLOCK_Q, HEAD_DIM), prev_map),
               pl.BlockSpec((None, qb * BLOCK_Q, HEAD_DIM), cur_map),
               pl.BlockSpec((None, BLOCK_Q, HEAD_DIM), next_map)]
    v_specs = [pl.BlockSpec((None, VT_ROWS, BLOCK_Q), t(prev_map)),
               pl.BlockSpec((None, VT_ROWS, qb * BLOCK_Q), t(cur_map)),
               pl.BlockSpec((None, VT_ROWS, BLOCK_Q), t(next_map))]
    return pl.pallas_call(
        functools.partial(_attn_b_kernel, nb=nb, qb=qb),
        out_shape=jax.ShapeDtypeStruct((N_Q_HEADS, n, HEAD_DIM), ATTN_OUT),
        grid=(batch, N_KV_HEADS, steps),
        in_specs=[pl.BlockSpec((Q_PER_KV, qb * BLOCK_Q, HEAD_DIM), cur_map)] + k_specs + v_specs + [
            pl.BlockSpec((None, NEAR, rows), lambda b, g, i: (g, 0, 0)),
            pl.BlockSpec((None, None, 1, rows), lambda b, g, i: (layer, g, 0, 0)),
        ],
        out_specs=pl.BlockSpec((Q_PER_KV, qb * BLOCK_Q, HEAD_DIM), cur_map),
        compiler_params=_cparams("arbitrary", "arbitrary", "arbitrary"),
        name="attn_window",
    )(q, k, k, k, v1t, v1t, v1t, bias_t, sink_t)


def _attn_c_kernel(q_ref, k_ref, v_ref, bias_ref, cfar_ref, kn_ref, lam_ref, gs_ref, o_ref,
                   m_ref, acct_ref, s_ref, *, n_iter, unroll, tks, nblk, nqb, out_scale):
    first_blk = pl.program_id(2) * nqb
    rows = Q_PER_KV * nqb * BLOCK_Q
    blocks_per_sub = tks // BLOCK_Q
    blocks_per_iter = unroll * blocks_per_sub

    q = q_ref[...].reshape(rows, HEAD_DIM)
    lo = lax.broadcasted_iota(jnp.int32, q.shape, 1) < DIFF_DIM
    zero = jnp.zeros_like(q)
    q2 = jnp.concatenate([jnp.where(lo, q, zero), jnp.where(lo, zero, q)], axis=0)
    acct_ref[...] = jnp.zeros_like(acct_ref)
    s_ref[...] = _dot_nt(k_ref[0:tks, :], q2)
    refs = (q2, k_ref, v_ref, acct_ref, s_ref)

    c_left, c_right, c_max = cfar_ref[0], cfar_ref[1], cfar_ref[2]
    t_lo = jnp.maximum(first_blk - 1, 0) // blocks_per_iter
    t_hi = jnp.minimum(first_blk + nqb, nblk - 1) // blocks_per_iter
    two = lambda x: jnp.concatenate([x, x], axis=1)
    shift = _score_bound(q2, kn_ref[:, :1]) + two(c_max)

    def add_bias(idx, s):
        blks = []
        for kb in range(blocks_per_sub):
            key_blk = idx * blocks_per_sub + kb
            tiles = [bias_ref[jnp.clip(key_blk - (first_blk + a) + 2, 0, NEAR // BLOCK_Q + 1)]
                     for a in range(nqb)]
            if nqb == 1:
                blks.append(tiles[0])
            else:
                blks.append(jnp.concatenate([tiles[a][:, h * BLOCK_Q:(h + 1) * BLOCK_Q]
                                             for h in range(Q_PER_KV) for a in range(nqb)], axis=1))
        bias = jnp.concatenate(blks, axis=0)
        return jnp.concatenate([s[:, :rows] + bias, s[:, rows:] + bias], axis=1)

    def iteration(t, last):
        is_near = (t >= t_lo) & (t <= t_hi)

        @pl.when(jnp.logical_not(is_near))
        def _():
            side = jnp.where(t < t_lo, c_left, c_right)
            _bounded_iteration(t, last, *refs, unroll=unroll, tks=tks, shift=shift - two(side))

        @pl.when(is_near)
        def _():
            _bounded_iteration(t, last, *refs, unroll=unroll, tks=tks, shift=shift, bias_fn=add_bias)

    def body(t, carry):
        iteration(t, False)
        return carry

    lax.fori_loop(0, n_iter - 1, body, 0)
    iteration(n_iter - 1, True)

    @pl.when(jnp.logical_not(_denominators_ok(acct_ref)))
    def _():
        _exact_pass(q2, k_ref, v_ref, m_ref, acct_ref, n_sub=n_iter * unroll, tks=tks, bias_fn=add_bias)

    acct = acct_ref[...]
    o1_t = acct[:HEAD_DIM, :rows] / acct[HEAD_DIM:HEAD_DIM + 1, :rows]
    o2_t = acct[:HEAD_DIM, rows:] / acct[HEAD_DIM:HEAD_DIM + 1, rows:]
    o_t = o1_t - lam_ref[:, :1] * o2_t
    ms = jnp.mean(o_t * o_t, axis=0, keepdims=True)
    o = (o_t * lax.rsqrt(ms + EPS)).T * gs_ref[...] * out_scale
    o_ref[...] = o.astype(o_ref.dtype).reshape(o_ref.shape)


def _attn_c(q, k, v1t, bias_t, cfar, key_norm, lam, g_subln, *, layer, batch, seq, tks, unroll, nqb):
    n = q.shape[1]
    nq = seq // (nqb * BLOCK_Q)
    rows = Q_PER_KV * nqb * BLOCK_Q
    lam_init = 0.8 - 0.6 * math.exp(-0.3 * layer)
    q_map = lambda b, g, i: (g, b * nq + i, 0)
    cfar_t = jnp.repeat(cfar, nqb * BLOCK_Q, axis=1).reshape(3, N_KV_HEADS, 1, rows)
    return pl.pallas_call(
        functools.partial(_attn_c_kernel, n_iter=seq // (tks * unroll), unroll=unroll, tks=tks,
                          nblk=seq // BLOCK_Q, nqb=nqb, out_scale=1.0 - lam_init),
        out_shape=jax.ShapeDtypeStruct((N_Q_HEADS, n, HEAD_DIM), ATTN_OUT),
        grid=(batch, N_KV_HEADS, nq),
        in_specs=[
            pl.BlockSpec((Q_PER_KV, nqb * BLOCK_Q, HEAD_DIM), q_map),
            pl.BlockSpec((None, seq, HEAD_DIM), lambda b, g, i: (g, b, 0)),
            pl.BlockSpec((None, VT_ROWS, seq), lambda b, g, i: (g, 0, b)),
            pl.BlockSpec((None, NEAR // BLOCK_Q + 2, BLOCK_Q, Q_PER_KV * BLOCK_Q), lambda b, g, i: (g, 0, 0, 0)),
            pl.BlockSpec((3, None, 1, rows), lambda b, g, i: (0, g, 0, 0)),
            pl.BlockSpec((None, 1, HEAD_DIM), lambda b, g, i: (layer, 0, 0)),
            pl.BlockSpec((None, 1, HEAD_DIM), lambda b, g, i: (layer, 0, 0)),
            pl.BlockSpec((None, 1, HEAD_DIM), lambda b, g, i: (layer, 0, 0)),
        ],
        out_specs=pl.BlockSpec((Q_PER_KV, nqb * BLOCK_Q, HEAD_DIM), q_map),
        scratch_shapes=[pltpu.VMEM((1, 2 * rows), F32), pltpu.VMEM((VT_ROWS, 2 * rows), F32),
                        pltpu.VMEM((tks, 2 * rows), F32)],
        compiler_params=_cparams("arbitrary", "arbitrary", "arbitrary"),
        name="attn_diff",
    )(q, k, v1t, bias_t, cfar_t, key_norm, lam, g_subln)


def _t5_bucket_np(rel):
    half = N_BUCKETS // 2
    max_exact = half // 2
    ret = np.where(rel > 0, half, 0)
    n = np.abs(rel)
    ratio = np.log(np.maximum(n, 1).astype(np.float32) / np.float32(max_exact)) / np.float32(
        math.log(MAX_DISTANCE / max_exact))
    large = max_exact + (ratio * np.float32(half - max_exact)).astype(np.int32)
    large = np.minimum(large, half - 1)
    return (ret + np.where(n < max_exact, n, large)).astype(np.int32)


def _near_bias(table):
    r = np.arange(BLOCK_Q)[:, None]
    c = np.arange(NEAR)[None, :]
    bucket = _t5_bucket_np(c - BLOCK_Q - r)
    onehot = (bucket.reshape(-1, 1) == np.arange(N_BUCKETS)[None, :]).astype(np.float32)
    rows = jnp.dot(jnp.asarray(onehot), table.astype(F32), precision=lax.Precision.HIGHEST)
    return rows.reshape(BLOCK_Q, NEAR, table.shape[1]).transpose(2, 0, 1)


def _rope_tables(seq):
    rows = seq // GRID_W
    nfreq = HEAD_DIM // 4
    inv = ROPE_THETA ** (-jnp.arange(nfreq, dtype=F32) / nfreq)
    ang_r = jnp.arange(rows).astype(F32)[:, None] * inv
    ang_c = jnp.arange(GRID_W).astype(F32)[:, None] * inv
    by_row = lambda t: jnp.broadcast_to(t[:, None, :], (rows, GRID_W, nfreq)).reshape(seq, nfreq)
    by_col = lambda t: jnp.broadcast_to(t[None, :, :], (rows, GRID_W, nfreq)).reshape(seq, nfreq)
    table = lambda fn: jnp.concatenate([by_row(fn(ang_r))] * 2 + [by_col(fn(ang_c))] * 2, axis=-1)
    cos, sin = table(jnp.cos), table(jnp.sin)
    first = (np.arange(HEAD_DIM) % (HEAD_DIM // 2)) < HEAD_DIM // 4
    return cos, jnp.where(first, -sin, 0.0), jnp.where(first, 0.0, sin)


def _trunk(x, mod, p, *, batch, seq):
    t = _tiles(seq)
    tm = t["tm"]
    rope_tabs = _rope_tables(seq)
    for l in range(DEPTH):
        x = _ffn(x, mod, p["g_norm"], p["wgu"], p["wout"], layer=l, which=0, seq=seq, tm=tm)
        qa, ka, va, qb, kb, vb, qc, kc, vc = _qkv(x, mod, p["g_norm"], p["wqkv"], rope_tabs, p["gh"],
                                                  layer=l, seq=seq, tm=tm)
        oa = _attn_a(qa, ka, va, p["kn_a"], layer=l, batch=batch, seq=seq, tq=t["tq_a"], tks=t["tks"],
                     unroll=t["unroll"])
        ob = _attn_b(qb, kb, vb, p["bias_b"], p["sink"], layer=l, batch=batch, seq=seq, qb=t["qb"])
        oc = _attn_c(qc, kc, vc, p["bias_c"], p["cfar_c"], p["kn_c"], p["lam"], p["g_subln"],
                     layer=l, batch=batch, seq=seq, tks=t["tks_c"], unroll=t["unroll_c"], nqb=t["nqb_c"])
        x = _out_proj(x, mod, p["g_norm"], oa, ob, oc, p["wgate"], p["wo"], layer=l, seq=seq, tm=tm)
        x = _ffn(x, mod, p["g_norm"], p["wgu"], p["wout"], layer=l, which=1, seq=seq, tm=tm)
    return x


def _prepare(w_ff_in, w_ff_out, w_in, w_o, g_qa, g_ka, g_qb, g_kb, g_qc, g_kc, sink,
             lam_q1, lam_k1, lam_q2, lam_k2, g_subln, rel_bias):
    wg = w_ff_in[..., :D_FF].reshape(DEPTH, 2, D_MODEL, N_FF_CHUNKS, FF_CHUNK)
    wu = w_ff_in[..., D_FF:].reshape(DEPTH, 2, D_MODEL, N_FF_CHUNKS, FF_CHUNK)
    wgu = jnp.concatenate([wg, wu], axis=-1).transpose(0, 1, 3, 2, 4).astype(BF16)
    wout = w_ff_out.reshape(DEPTH, 2, N_FF_CHUNKS, FF_CHUNK, D_MODEL).astype(BF16)
    zeros = jnp.zeros_like(g_qa)
    gh = jnp.stack([g_qa, g_ka, g_qb, g_kb, jnp.tile(g_qc, (1, 2)), jnp.tile(g_kc, (1, 2)), zeros, zeros],
                   axis=1).astype(F32)
    table_b, table_c = rel_bias[:, :N_Q_HEADS], rel_bias[:, N_Q_HEADS:]
    half = N_BUCKETS // 2
    c_left, c_right = table_c[half - 1] * LOG2E, table_c[N_BUCKETS - 1] * LOG2E
    lam_init = jnp.asarray([0.8 - 0.6 * math.exp(-0.3 * l) for l in range(DEPTH)], F32)
    lam = (jnp.exp(jnp.sum(lam_q1.astype(F32) * lam_k1.astype(F32), axis=-1))
           - jnp.exp(jnp.sum(lam_q2.astype(F32) * lam_k2.astype(F32), axis=-1)) + lam_init)

    band = np.abs(np.arange(NEAR)[None, :] - BLOCK_Q - np.arange(BLOCK_Q)[:, None]) <= WINDOW
    near_c = (_near_bias(table_c) * LOG2E).reshape(
        N_KV_HEADS, Q_PER_KV, BLOCK_Q, NEAR // BLOCK_Q, BLOCK_Q).transpose(0, 3, 4, 1, 2).reshape(
        N_KV_HEADS, NEAR // BLOCK_Q, BLOCK_Q, Q_PER_KV * BLOCK_Q)
    cfar_heads = jnp.stack([c_left, c_right, jnp.max(table_c, axis=0) * LOG2E]).astype(F32)
    cfar_c = jnp.repeat(cfar_heads, BLOCK_Q, axis=1).reshape(3, N_KV_HEADS, 1, Q_PER_KV * BLOCK_Q)

    def key_norm(g, dim):
        bound = 1.01 * math.sqrt(dim) * jnp.max(jnp.abs(g.astype(F32)), axis=-1)
        return jnp.broadcast_to(bound[:, None, None], (DEPTH, 1, HEAD_DIM))

    return dict(
        wgu=wgu, wout=wout,
        wqkv=w_in[:, :, :QKV_W].astype(BF16), wgate=w_in[:, :, QKV_W:].astype(BF16), wo=w_o.astype(BF16),
        gh=gh,
        bias_b=jnp.where(band, _near_bias(table_b) * LOG2E, NEG).reshape(
            N_KV_HEADS, Q_PER_KV, BLOCK_Q, NEAR).transpose(0, 3, 1, 2).reshape(
            N_KV_HEADS, NEAR, Q_PER_KV * BLOCK_Q),
        sink=jnp.repeat(sink.astype(F32) * LOG2E, BLOCK_Q, axis=1).reshape(
            DEPTH, N_KV_HEADS, 1, Q_PER_KV * BLOCK_Q),
        bias_c=jnp.concatenate([cfar_c[0][:, None] + jnp.zeros((1, 1, BLOCK_Q, 1), F32), near_c,
                                cfar_c[1][:, None] + jnp.zeros((1, 1, BLOCK_Q, 1), F32)], axis=1),
        cfar_c=cfar_heads,
        kn_a=key_norm(g_ka, HEAD_DIM), kn_c=key_norm(g_kc, DIFF_DIM),
        lam=jnp.broadcast_to(lam[:, None, None], (DEPTH, 1, HEAD_DIM)),
        g_subln=g_subln.astype(F32)[:, None, :],
    )


def kernel(x_prompt, x_sample, c_prompt, c_sample, w_ada, b_ada, g_norm, w_ff_in, w_ff_out, w_in, w_o,
           g_qa, g_ka, g_qb, g_kb, g_qc, g_kc, sink, lam_q1, lam_k1, lam_q2, lam_k2, g_subln, rel_bias):
    p = _prepare(w_ff_in, w_ff_out, w_in, w_o, g_qa, g_ka, g_qb, g_kb, g_qc, g_kc, sink,
                 lam_q1, lam_k1, lam_q2, lam_k2, g_subln, rel_bias)
    p["g_norm"] = g_norm.astype(F32)
    outs = []
    n_cond = 0
    conds = [c_prompt, c_sample]
    c_all = jnp.concatenate(conds + [jnp.zeros((ADA_ROWS - sum(c.shape[0] for c in conds), D_MODEL), F32)])
    mod_all = _ada(c_all, w_ada, b_ada)
    for x, c in ((x_prompt, c_prompt), (x_sample, c_sample)):
        batch, seq, _ = x.shape
        mod = mod_all[:, n_cond:n_cond + batch].reshape(DEPTH, batch, 9, D_MODEL)
        n_cond += batch
        y = _trunk(x.reshape(batch * seq, D_MODEL), mod, p, batch=batch, seq=seq)
        outs.append(y.reshape(batch, seq, D_MODEL))
    return tuple(outs)
```

```python
import functools
import math

import numpy as np
import jax
import jax.numpy as jnp
from jax import lax
from jax.experimental import pallas as pl
from jax.experimental.pallas import tpu as pltpu

F32 = jnp.float32
BF16 = jnp.bfloat16
ATTN_OUT = jnp.bfloat16

D_MODEL = 1024
DEPTH = 4
N_Q_HEADS = 8
N_KV_HEADS = 2
Q_PER_KV = N_Q_HEADS // N_KV_HEADS
HEAD_DIM = 128
DIFF_DIM = 64
D_FF = 2816
BLOCK_Q = 128
WINDOW = 128
GRID_W = 64
N_BUCKETS = 32
MAX_DISTANCE = 128
ROPE_THETA = 10000.0
EPS = 1e-6
NEG = -1e30
LOG2E = 1.4426950408889634

Q_W = N_Q_HEADS * HEAD_DIM
KV_W = N_KV_HEADS * HEAD_DIM
BRANCH_IN = Q_W + 2 * KV_W
QKV_W = 3 * BRANCH_IN
FF_CHUNK = 256
N_FF_CHUNKS = D_FF // FF_CHUNK
ADA_ROWS = 16
ADA_TN = 1536
NEAR = 3 * BLOCK_Q
VT_ROWS = HEAD_DIM + 16

VMEM_LIMIT = 56 * 1024 * 1024


def _tiles(seq):
    tks = min(256, seq)
    unroll = min(16, seq // tks)
    single_region = seq <= unroll * tks
    return dict(
        tm=min(512, seq),
        tq_a=2 * BLOCK_Q,
        nqb_c=2 if single_region else 1,
        tks=tks,
        qb=min(8, seq // BLOCK_Q),
        unroll=unroll,
        tks_c=tks,
        unroll_c=unroll,
    )


def _cparams(*sem):
    return pltpu.CompilerParams(dimension_semantics=sem, vmem_limit_bytes=VMEM_LIMIT)


def _dot(a, b):
    return jnp.dot(a, b, preferred_element_type=F32)


def _dot_nt(a, b):
    return lax.dot_general(a, b, (((1,), (1,)), ((), ())), preferred_element_type=F32)


def _ada_kernel(c_ref, w_ref, b_ref, o_ref):
    c = c_ref[...]
    a = (c * jax.nn.sigmoid(c)).astype(BF16)
    o_ref[0] = _dot(a, w_ref[0].astype(BF16)) + b_ref[0]


def _ada(c_all, w_ada, b_ada):
    n_out = w_ada.shape[-1]
    return pl.pallas_call(
        _ada_kernel,
        out_shape=jax.ShapeDtypeStruct((DEPTH, ADA_ROWS, n_out), F32),
        grid=(DEPTH, n_out // ADA_TN),
        in_specs=[
            pl.BlockSpec((ADA_ROWS, D_MODEL), lambda l, j: (0, 0)),
            pl.BlockSpec((1, D_MODEL, ADA_TN), lambda l, j: (l, 0, j)),
            pl.BlockSpec((1, 1, ADA_TN), lambda l, j: (l, 0, j)),
        ],
        out_specs=pl.BlockSpec((1, ADA_ROWS, ADA_TN), lambda l, j: (l, 0, j)),
        compiler_params=_cparams("arbitrary", "arbitrary"),
        name="ada",
    )(c_all, w_ada, b_ada.reshape(DEPTH, 1, n_out))


def _modulate(x, mod_ref, g, jj):
    ms = jnp.mean(x * x, axis=-1, keepdims=True)
    y = x * lax.rsqrt(ms + EPS) * g
    return y * (1.0 + mod_ref[0, 3 * jj + 1:3 * jj + 2, :]) + mod_ref[0, 3 * jj:3 * jj + 1, :]


def _ffn_kernel(x_ref, mod_ref, g_ref, wgu_ref, wout_ref, o_ref, nb_ref, acc_ref, *, jj):
    x = x_ref[...]
    nb_ref[...] = _modulate(x, mod_ref, g_ref[jj:jj + 1, :], jj).astype(BF16)
    acc_ref[...] = jnp.zeros_like(acc_ref)

    def body(c, carry):
        h = _dot(nb_ref[...], wgu_ref[c])
        hg = h[:, :FF_CHUNK]
        a = (hg * jax.nn.sigmoid(hg)) * h[:, FF_CHUNK:]
        acc_ref[...] += _dot(a.astype(BF16), wout_ref[c])
        return carry

    lax.fori_loop(0, N_FF_CHUNKS, body, 0, unroll=True)
    o_ref[...] = x + (0.5 * mod_ref[0, 3 * jj + 2:3 * jj + 3, :]) * acc_ref[...]


def _ffn(x, mod, g_norm, wgu, wout, *, layer, which, seq, tm):
    n = x.shape[0]
    jj = 2 * which
    const = dict(pipeline_mode=pl.Buffered(1))
    return pl.pallas_call(
        functools.partial(_ffn_kernel, jj=jj),
        out_shape=jax.ShapeDtypeStruct((n, D_MODEL), F32),
        grid=(n // tm,),
        in_specs=[
            pl.BlockSpec((tm, D_MODEL), lambda i: (i, 0)),
            pl.BlockSpec((None, 1, 9, D_MODEL), lambda i: (layer, (i * tm) // seq, 0, 0)),
            pl.BlockSpec((None, 3, D_MODEL), lambda i: (layer, 0, 0)),
            pl.BlockSpec((None, None, N_FF_CHUNKS, D_MODEL, 2 * FF_CHUNK),
                         lambda i: (layer, which, 0, 0, 0), **const),
            pl.BlockSpec((None, None, N_FF_CHUNKS, FF_CHUNK, D_MODEL),
                         lambda i: (layer, which, 0, 0, 0), **const),
        ],
        out_specs=pl.BlockSpec((tm, D_MODEL), lambda i: (i, 0)),
        scratch_shapes=[pltpu.VMEM((tm, D_MODEL), BF16), pltpu.VMEM((tm, D_MODEL), F32)],
        input_output_aliases={} if (layer == 0 and which == 0) else {0: 0},
        compiler_params=_cparams("arbitrary"),
        name=f"ffn{which}",
    )(x, mod, g_norm, wgu, wout)


def _head_norm(r, g):
    ms = jnp.mean(r * r, axis=-1, keepdims=True)
    return r * lax.rsqrt(ms + EPS) * g


def _half_norm(r, g2):
    sq = r * r
    lo = lax.broadcasted_iota(jnp.int32, r.shape, 1) < DIFF_DIM
    s_lo = jnp.sum(jnp.where(lo, sq, 0.0), axis=-1, keepdims=True)
    s_hi = jnp.sum(jnp.where(lo, 0.0, sq), axis=-1, keepdims=True)
    ms = jnp.where(lo, s_lo, s_hi) * (1.0 / DIFF_DIM)
    return r * lax.rsqrt(ms + EPS) * g2


def _qkv_kernel(x_ref, mod_ref, g_ref, w_ref, cos_ref, sa_ref, sb_ref, gh_ref,
                qa_ref, ka_ref, va_ref, qb_ref, kb_ref, vb_ref, qc_ref, kc_ref, vc_ref, nb_ref):
    nb_ref[...] = _modulate(x_ref[...], mod_ref, g_ref[1:2, :], 1).astype(BF16)
    cos, sa, sb = cos_ref[...], sa_ref[...], sb_ref[...]

    def rope(r):
        return (r * cos + pltpu.roll(r, HEAD_DIM - 32, 1) * sa + pltpu.roll(r, 32, 1) * sb)

    def proj(col, width):
        return _dot(nb_ref[...], w_ref[:, col:col + width])

    def heads(r, n_heads, fn, out_ref):
        for h in range(n_heads):
            out_ref[h] = fn(r[:, h * HEAD_DIM:(h + 1) * HEAD_DIM]).astype(BF16)

    def values_transposed(r, out_ref):
        for h in range(N_KV_HEADS):
            out_ref[h, :HEAD_DIM, :] = r[:, h * HEAD_DIM:(h + 1) * HEAD_DIM].T.astype(BF16)
            out_ref[h, HEAD_DIM:, :] = jnp.ones((VT_ROWS - HEAD_DIM, r.shape[0]), BF16)

    qa_scale = HEAD_DIM ** -0.5 * LOG2E
    qc_scale = DIFF_DIM ** -0.5 * LOG2E
    g_qa, g_ka, g_qb, g_kb = (gh_ref[i:i + 1, :] for i in range(4))
    g_qc, g_kc = gh_ref[4:5, :], gh_ref[5:6, :]
    col = 0
    heads(proj(col, Q_W), N_Q_HEADS, lambda r: rope(_head_norm(r, g_qa)) * qa_scale, qa_ref)
    col += Q_W
    heads(proj(col, KV_W), N_KV_HEADS, lambda r: rope(_head_norm(r, g_ka)), ka_ref)
    col += KV_W
    values_transposed(proj(col, KV_W),va_ref)
    col += KV_W
    heads(proj(col, Q_W), N_Q_HEADS, lambda r: _head_norm(r, g_qb) * qa_scale, qb_ref)
    col += Q_W
    heads(proj(col, KV_W), N_KV_HEADS, lambda r: _head_norm(r, g_kb), kb_ref)
    col += KV_W
    values_transposed(proj(col, KV_W), vb_ref)
    col += KV_W
    heads(proj(col, Q_W), N_Q_HEADS, lambda r: _half_norm(r, g_qc) * qc_scale, qc_ref)
    col += Q_W
    heads(proj(col, KV_W), N_KV_HEADS, lambda r: _half_norm(r, g_kc), kc_ref)
    col += KV_W
    values_transposed(proj(col, KV_W),vc_ref)


def _qkv(x, mod, g_norm, wqkv, rope_tabs, gh, *, layer, seq, tm):
    n = x.shape[0]
    tiles_per_seq = seq // tm
    q_shape = jax.ShapeDtypeStruct((N_Q_HEADS, n, HEAD_DIM), BF16)
    kv_shape = jax.ShapeDtypeStruct((N_KV_HEADS, n, HEAD_DIM), BF16)
    q_spec = pl.BlockSpec((N_Q_HEADS, tm, HEAD_DIM), lambda i: (0, i, 0))
    kv_spec = pl.BlockSpec((N_KV_HEADS, tm, HEAD_DIM), lambda i: (0, i, 0))
    tab_spec = pl.BlockSpec((tm, HEAD_DIM), lambda i: (i % tiles_per_seq, 0))
    v1_shape = jax.ShapeDtypeStruct((N_KV_HEADS, VT_ROWS, n), BF16)
    v1_spec = pl.BlockSpec((N_KV_HEADS, VT_ROWS, tm), lambda i: (0, 0, i))
    return pl.pallas_call(
        _qkv_kernel,
        out_shape=[q_shape, kv_shape, v1_shape] * 3,
        grid=(n // tm,),
        in_specs=[
            pl.BlockSpec((tm, D_MODEL), lambda i: (i, 0)),
            pl.BlockSpec((None, 1, 9, D_MODEL), lambda i: (layer, (i * tm) // seq, 0, 0)),
            pl.BlockSpec((None, 3, D_MODEL), lambda i: (layer, 0, 0)),
            pl.BlockSpec((None, D_MODEL, QKV_W), lambda i: (layer, 0, 0), pipeline_mode=pl.Buffered(1)),
            tab_spec, tab_spec, tab_spec,
            pl.BlockSpec((None, 8, HEAD_DIM), lambda i: (layer, 0, 0)),
        ],
        out_specs=[q_spec, kv_spec, v1_spec] * 3,
        scratch_shapes=[pltpu.VMEM((tm, D_MODEL), BF16)],
        compiler_params=_cparams("arbitrary"),
        name="qkv",
    )(x, mod, g_norm, wqkv, *rope_tabs, gh)


def _out_kernel(x_ref, mod_ref, g_ref, oa_ref, ob_ref, oc_ref, wg_ref, wo_ref, o_ref, mg_ref):
    x = x_ref[...]
    nb = _modulate(x, mod_ref, g_ref[1:2, :], 1).astype(BF16)
    for br, br_ref in enumerate((oa_ref, ob_ref, oc_ref)):
        gate = jax.nn.sigmoid(_dot(nb, wg_ref[:, br * D_MODEL:(br + 1) * D_MODEL]))
        for h in range(N_Q_HEADS):
            lanes = slice(h * HEAD_DIM, (h + 1) * HEAD_DIM)
            term = gate[:, lanes] * br_ref[h].astype(F32)
            if br == 0:
                mg_ref[:, lanes] = term
            else:
                mg_ref[:, lanes] += term
    o_ref[...] = x + mod_ref[0, 5:6, :] * _dot(mg_ref[...].astype(BF16), wo_ref[...])


def _out_proj(x, mod, g_norm, oa, ob, oc, wgate, wo, *, layer, seq, tm):
    n = x.shape[0]
    o_spec = pl.BlockSpec((N_Q_HEADS, tm, HEAD_DIM), lambda i: (0, i, 0))
    return pl.pallas_call(
        _out_kernel,
        out_shape=jax.ShapeDtypeStruct((n, D_MODEL), F32),
        grid=(n // tm,),
        in_specs=[
            pl.BlockSpec((tm, D_MODEL), lambda i: (i, 0)),
            pl.BlockSpec((None, 1, 9, D_MODEL), lambda i: (layer, (i * tm) // seq, 0, 0)),
            pl.BlockSpec((None, 3, D_MODEL), lambda i: (layer, 0, 0)),
            o_spec, o_spec, o_spec,
            pl.BlockSpec((None, D_MODEL, 3 * D_MODEL), lambda i: (layer, 0, 0), pipeline_mode=pl.Buffered(1)),
            pl.BlockSpec((None, D_MODEL, D_MODEL), lambda i: (layer, 0, 0), pipeline_mode=pl.Buffered(1)),
        ],
        out_specs=pl.BlockSpec((tm, D_MODEL), lambda i: (i, 0)),
        scratch_shapes=[pltpu.VMEM((tm, D_MODEL), F32)],
        input_output_aliases={0: 0},
        compiler_params=_cparams("arbitrary"),
        name="out_proj",
    )(x, mod, g_norm, oa, ob, oc, wgate, wo)


MIN_DENOM = 2.0 ** -64


def _key_slice(idx, tks):
    return pl.ds(pl.multiple_of(idx * tks, tks), tks)


def _score_bound(q, key_norm):
    qf = q.astype(F32)
    qn2 = _dot_nt(jnp.ones((8, q.shape[1]), F32), qf * qf)[:1]
    return jnp.sqrt(qn2) * key_norm


def _bounded_iteration(t, last, q, k_ref, v_ref, acct_ref, s_ref, *, unroll, tks, shift, bias_fn=None):
    acct = acct_ref[...]
    s = s_ref[...]
    for u in range(unroll):
        idx = t * unroll + u
        s_next = None if (last and u == unroll - 1) else _dot_nt(k_ref[_key_slice(idx + 1, tks), :], q)
        if bias_fn is not None:
            s = bias_fn(idx, s)
        acct = acct + _dot(v_ref[:, _key_slice(idx, tks)], jnp.exp2(s - shift).astype(BF16))
        s = s_next
    acct_ref[...] = acct
    if not last:
        s_ref[...] = s


def _exact_pass(q, k_ref, v_ref, m_ref, acct_ref, *, n_sub, tks, bias_fn=None):
    m_ref[...] = jnp.full_like(m_ref, -jnp.inf)
    acct_ref[...] = jnp.zeros_like(acct_ref)

    def body(idx, carry):
        s = _dot_nt(k_ref[_key_slice(idx, tks), :], q)
        if bias_fn is not None:
            s = bias_fn(idx, s)
        m = m_ref[...]
        m_new = jnp.maximum(m, jnp.max(s, axis=0, keepdims=True))
        p = jnp.exp2(s - m_new).astype(BF16)
        acct_ref[...] = jnp.exp2(m - m_new) * acct_ref[...] + _dot(v_ref[:, _key_slice(idx, tks)], p)
        m_ref[...] = m_new
        return carry

    lax.fori_loop(0, n_sub, body, 0)


def _denominators_ok(acct_ref):
    return jnp.min(acct_ref[HEAD_DIM:HEAD_DIM + 1, :]) >= MIN_DENOM


def _attn_a_kernel(q_ref, k_ref, v_ref, kn_ref, o_ref, m_ref, acct_ref, s_ref, *, n_iter, unroll, tks):
    rows = acct_ref.shape[1]
    q = q_ref[...].reshape(rows, HEAD_DIM)
    shift = _score_bound(q, kn_ref[:, :1])
    acct_ref[...] = jnp.zeros_like(acct_ref)
    s_ref[...] = _dot_nt(k_ref[0:tks, :], q)
    refs = (q, k_ref, v_ref, acct_ref, s_ref)

    def body(t, carry):
        _bounded_iteration(t, False, *refs, unroll=unroll, tks=tks, shift=shift)
        return carry

    lax.fori_loop(0, n_iter - 1, body, 0)
    _bounded_iteration(n_iter - 1, True, *refs, unroll=unroll, tks=tks, shift=shift)

    @pl.when(jnp.logical_not(_denominators_ok(acct_ref)))
    def _():
        _exact_pass(q, k_ref, v_ref, m_ref, acct_ref, n_sub=n_iter * unroll, tks=tks)

    acct = acct_ref[...]
    o_t = acct[:HEAD_DIM] / acct[HEAD_DIM:HEAD_DIM + 1]
    o_ref[...] = o_t.T.astype(o_ref.dtype).reshape(o_ref.shape)


def _attn_a(q, k, v1t, key_norm, *, layer, batch, seq, tq, tks, unroll):
    n = q.shape[1]
    nq = seq // tq
    rows = Q_PER_KV * tq
    q_map = lambda b, g, i: (g, b * nq + i, 0)
    return pl.pallas_call(
        functools.partial(_attn_a_kernel, n_iter=seq // (tks * unroll), unroll=unroll, tks=tks),
        out_shape=jax.ShapeDtypeStruct((N_Q_HEADS, n, HEAD_DIM), ATTN_OUT),
        grid=(batch, N_KV_HEADS, nq),
        in_specs=[
            pl.BlockSpec((Q_PER_KV, tq, HEAD_DIM), q_map),
            pl.BlockSpec((None, seq, HEAD_DIM), lambda b, g, i: (g, b, 0)),
            pl.BlockSpec((None, VT_ROWS, seq), lambda b, g, i: (g, 0, b)),
            pl.BlockSpec((None, 1, HEAD_DIM), lambda b, g, i: (layer, 0, 0)),
        ],
        out_specs=pl.BlockSpec((Q_PER_KV, tq, HEAD_DIM), q_map),
        scratch_shapes=[pltpu.VMEM((1, rows), F32), pltpu.VMEM((VT_ROWS, rows), F32),
                        pltpu.VMEM((tks, rows), F32)],
        compiler_params=_cparams("arbitrary", "arbitrary", "arbitrary"),
        name="attn_axial",
    )(q, k, v1t, key_norm)


def _attn_b_kernel(q_ref, kp_ref, kc_ref, kn_ref, vp_ref, vc_ref, vn_ref, bias_ref, sink_ref, o_ref,
                   *, nb, qb):
    i = pl.program_id(2)
    rows = Q_PER_KV * BLOCK_Q
    kwin = jnp.concatenate([kp_ref[...], kc_ref[...], kn_ref[...]], axis=0)
    vwin = jnp.concatenate([vp_ref[...], vc_ref[...], vn_ref[...]], axis=1)
    bias = bias_ref[...]
    sink = sink_ref[...]
    for u in range(qb):
        blk = i * qb + u
        q = q_ref[:, u * BLOCK_Q:(u + 1) * BLOCK_Q, :].reshape(rows, HEAD_DIM)
        s = _dot_nt(kwin[u * BLOCK_Q:u * BLOCK_Q + NEAR], q) + bias
        s = jnp.concatenate([jnp.where(blk > 0, s[:BLOCK_Q], NEG), s[BLOCK_Q:2 * BLOCK_Q],
                             jnp.where(blk < nb - 1, s[2 * BLOCK_Q:], NEG)], axis=0)
        m = jnp.maximum(jnp.max(s, axis=0, keepdims=True), sink)
        p = jnp.exp2(s - m).astype(BF16)
        acct = _dot(vwin[:, u * BLOCK_Q:u * BLOCK_Q + NEAR], p)
        den = acct[HEAD_DIM:HEAD_DIM + 1] + jnp.exp2(sink - m)
        o_t = acct[:HEAD_DIM] / den
        o_ref[:, u * BLOCK_Q:(u + 1) * BLOCK_Q, :] = o_t.T.astype(o_ref.dtype).reshape(
            Q_PER_KV, BLOCK_Q, HEAD_DIM)


def _attn_b(q, k, v1t, bias_t, sink_t, *, layer, batch, seq, qb):
    n = q.shape[1]
    nb = seq // BLOCK_Q
    steps = nb // qb
    rows = Q_PER_KV * BLOCK_Q
    cur_map = lambda b, g, i: (g, b * steps + i, 0)
    prev_map = lambda b, g, i: (g, b * nb + jnp.maximum(i * qb - 1, 0), 0)
    next_map = lambda b, g, i: (g, b * nb + jnp.minimum(i * qb + qb, nb - 1), 0)
    t = lambda index_map: (lambda b, g, i: (index_map(b, g, i)[0], 0, index_map(b, g, i)[1]))
    k_specs = [pl.BlockSpec((None, BLOCK_Q, HEAD_DIM), prev_map),
               pl.BlockSpec((None, qb * BLOCK_Q, HEAD_DIM), cur_map),
               pl.BlockSpec((None, BLOCK_Q, HEAD_DIM), next_map)]
    v_specs = [pl.BlockSpec((None, VT_ROWS, BLOCK_Q), t(prev_map)),
               pl.BlockSpec((None, VT_ROWS, qb * BLOCK_Q), t(cur_map)),
               pl.BlockSpec((None, VT_ROWS, BLOCK_Q), t(next_map))]
    return pl.pallas_call(
        functools.partial(_attn_b_kernel, nb=nb, qb=qb),
        out_shape=jax.ShapeDtypeStruct((N_Q_HEADS, n, HEAD_DIM), ATTN_OUT),
        grid=(batch, N_KV_HEADS, steps),
        in_specs=[pl.BlockSpec((Q_PER_KV, qb * BLOCK_Q, HEAD_DIM), cur_map)] + k_specs + v_specs + [
            pl.BlockSpec((None, NEAR, rows), lambda b, g, i: (g, 0, 0)),
            pl.BlockSpec((None, None, 1, rows), lambda b, g, i: (layer, g, 0, 0)),
        ],
        out_specs=pl.BlockSpec((Q_PER_KV, qb * BLOCK_Q, HEAD_DIM), cur_map),
        compiler_params=_cparams("arbitrary", "arbitrary", "arbitrary"),
        name="attn_window",
    )(q, k, k, k, v1t, v1t, v1t, bias_t, sink_t)


def _attn_c_kernel(q_ref, k_ref, v_ref, bias_ref, cfar_ref, kn_ref, lam_ref, gs_ref, o_ref,
                   m_ref, acct_ref, s_ref, *, n_iter, unroll, tks, nblk, nqb, out_scale):
    first_blk = pl.program_id(2) * nqb
    rows = Q_PER_KV * nqb * BLOCK_Q
    blocks_per_sub = tks // BLOCK_Q
    blocks_per_iter = unroll * blocks_per_sub

    q = q_ref[...].reshape(rows, HEAD_DIM)
    lo = lax.broadcasted_iota(jnp.int32, q.shape, 1) < DIFF_DIM
    zero = jnp.zeros_like(q)
    q2 = jnp.concatenate([jnp.where(lo, q, zero), jnp.where(lo, zero, q)], axis=0)
    acct_ref[...] = jnp.zeros_like(acct_ref)
    s_ref[...] = _dot_nt(k_ref[0:tks, :], q2)
    refs = (q2, k_ref, v_ref, acct_ref, s_ref)

    c_left, c_right, c_max = cfar_ref[0], cfar_ref[1], cfar_ref[2]
    t_lo = jnp.maximum(first_blk - 1, 0) // blocks_per_iter
    t_hi = jnp.minimum(first_blk + nqb, nblk - 1) // blocks_per_iter
    two = lambda x: jnp.concatenate([x, x], axis=1)
    shift = _score_bound(q2, kn_ref[:, :1]) + two(c_max)

    def add_bias(idx, s):
        blks = []
        for kb in range(blocks_per_sub):
            key_blk = idx * blocks_per_sub + kb
            tiles = [bias_ref[jnp.clip(key_blk - (first_blk + a) + 2, 0, NEAR // BLOCK_Q + 1)]
                     for a in range(nqb)]
            if nqb == 1:
                blks.append(tiles[0])
            else:
                blks.append(jnp.concatenate([tiles[a][:, h * BLOCK_Q:(h + 1) * BLOCK_Q]
                                             for h in range(Q_PER_KV) for a in range(nqb)], axis=1))
        bias = jnp.concatenate(blks, axis=0)
        return jnp.concatenate([s[:, :rows] + bias, s[:, rows:] + bias], axis=1)

    def iteration(t, last):
        is_near = (t >= t_lo) & (t <= t_hi)

        @pl.when(jnp.logical_not(is_near))
        def _():
            side = jnp.where(t < t_lo, c_left, c_right)
            _bounded_iteration(t, last, *refs, unroll=unroll, tks=tks, shift=shift - two(side))

        @pl.when(is_near)
        def _():
            _bounded_iteration(t, last, *refs, unroll=unroll, tks=tks, shift=shift, bias_fn=add_bias)

    def body(t, carry):
        iteration(t, False)
        return carry

    lax.fori_loop(0, n_iter - 1, body, 0)
    iteration(n_iter - 1, True)

    @pl.when(jnp.logical_not(_denominators_ok(acct_ref)))
    def _():
        _exact_pass(q2, k_ref, v_ref, m_ref, acct_ref, n_sub=n_iter * unroll, tks=tks, bias_fn=add_bias)

    acct = acct_ref[...]
    o1_t = acct[:HEAD_DIM, :rows] / acct[HEAD_DIM:HEAD_DIM + 1, :rows]
    o2_t = acct[:HEAD_DIM, rows:] / acct[HEAD_DIM:HEAD_DIM + 1, rows:]
    o_t = o1_t - lam_ref[:, :1] * o2_t
    ms = jnp.mean(o_t * o_t, axis=0, keepdims=True)
    o = (o_t * lax.rsqrt(ms + EPS)).T * gs_ref[...] * out_scale
    o_ref[...] = o.astype(o_ref.dtype).reshape(o_ref.shape)


def _attn_c(q, k, v1t, bias_t, cfar, key_norm, lam, g_subln, *, layer, batch, seq, tks, unroll, nqb):
    n = q.shape[1]
    nq = seq // (nqb * BLOCK_Q)
    rows = Q_PER_KV * nqb * BLOCK_Q
    lam_init = 0.8 - 0.6 * math.exp(-0.3 * layer)
    q_map = lambda b, g, i: (g, b * nq + i, 0)
    cfar_t = jnp.repeat(cfar, nqb * BLOCK_Q, axis=1).reshape(3, N_KV_HEADS, 1, rows)
    return pl.pallas_call(
        functools.partial(_attn_c_kernel, n_iter=seq // (tks * unroll), unroll=unroll, tks=tks,
                          nblk=seq // BLOCK_Q, nqb=nqb, out_scale=1.0 - lam_init),
        out_shape=jax.ShapeDtypeStruct((N_Q_HEADS, n, HEAD_DIM), ATTN_OUT),
        grid=(batch, N_KV_HEADS, nq),
        in_specs=[
            pl.BlockSpec((Q_PER_KV, nqb * BLOCK_Q, HEAD_DIM), q_map),
            pl.BlockSpec((None, seq, HEAD_DIM), lambda b, g, i: (g, b, 0)),
            pl.BlockSpec((None, VT_ROWS, seq), lambda b, g, i: (g, 0, b)),
            pl.BlockSpec((None, NEAR // BLOCK_Q + 2, BLOCK_Q, Q_PER_KV * BLOCK_Q), lambda b, g, i: (g, 0, 0, 0)),
            pl.BlockSpec((3, None, 1, rows), lambda b, g, i: (0, g, 0, 0)),
            pl.BlockSpec((None, 1, HEAD_DIM), lambda b, g, i: (layer, 0, 0)),
            pl.BlockSpec((None, 1, HEAD_DIM), lambda b, g, i: (layer, 0, 0)),
            pl.BlockSpec((None, 1, HEAD_DIM), lambda b, g, i: (layer, 0, 0)),
        ],
        out_specs=pl.BlockSpec((Q_PER_KV, nqb * BLOCK_Q, HEAD_DIM), q_map),
        scratch_shapes=[pltpu.VMEM((1, 2 * rows), F32), pltpu.VMEM((VT_ROWS, 2 * rows), F32),
                        pltpu.VMEM((tks, 2 * rows), F32)],
        compiler_params=_cparams("arbitrary", "arbitrary", "arbitrary"),
        name="attn_diff",
    )(q, k, v1t, bias_t, cfar_t, key_norm, lam, g_subln)


def _t5_bucket_np(rel):
    half = N_BUCKETS // 2
    max_exact = half // 2
    ret = np.where(rel > 0, half, 0)
    n = np.abs(rel)
    ratio = np.log(np.maximum(n, 1).astype(np.float32) / np.float32(max_exact)) / np.float32(
        math.log(MAX_DISTANCE / max_exact))
    large = max_exact + (ratio * np.float32(half - max_exact)).astype(np.int32)
    large = np.minimum(large, half - 1)
    return (ret + np.where(n < max_exact, n, large)).astype(np.int32)


def _near_bias(table):
    r = np.arange(BLOCK_Q)[:, None]
    c = np.arange(NEAR)[None, :]
    bucket = _t5_bucket_np(c - BLOCK_Q - r)
    onehot = (bucket.reshape(-1, 1) == np.arange(N_BUCKETS)[None, :]).astype(np.float32)
    rows = jnp.dot(jnp.asarray(onehot), table.astype(F32), precision=lax.Precision.HIGHEST)
    return rows.reshape(BLOCK_Q, NEAR, table.shape[1]).transpose(2, 0, 1)


def _rope_tables(seq):
    rows = seq // GRID_W
    nfreq = HEAD_DIM // 4
    inv = ROPE_THETA ** (-jnp.arange(nfreq, dtype=F32) / nfreq)
    ang_r = jnp.arange(rows).astype(F32)[:, None] * inv
    ang_c = jnp.arange(GRID_W).astype(F32)[:, None] * inv
    by_row = lambda t: jnp.broadcast_to(t[:, None, :], (rows, GRID_W, nfreq)).reshape(seq, nfreq)
    by_col = lambda t: jnp.broadcast_to(t[None, :, :], (rows, GRID_W, nfreq)).reshape(seq, nfreq)
    table = lambda fn: jnp.concatenate([by_row(fn(ang_r))] * 2 + [by_col(fn(ang_c))] * 2, axis=-1)
    cos, sin = table(jnp.cos), table(jnp.sin)
    first = (np.arange(HEAD_DIM) % (HEAD_DIM // 2)) < HEAD_DIM // 4
    return cos, jnp.where(first, -sin, 0.0), jnp.where(first, 0.0, sin)


def _trunk(x, mod, p, *, batch, seq):
    t = _tiles(seq)
    tm = t["tm"]
    rope_tabs = _rope_tables(seq)
    for l in range(DEPTH):
        x = _ffn(x, mod, p["g_norm"], p["wgu"], p["wout"], layer=l, which=0, seq=seq, tm=tm)
        qa, ka, va, qb, kb, vb, qc, kc, vc = _qkv(x, mod, p["g_norm"], p["wqkv"], rope_tabs, p["gh"],
                                                  layer=l, seq=seq, tm=tm)
        oa = _attn_a(qa, ka, va, p["kn_a"], layer=l, batch=batch, seq=seq, tq=t["tq_a"], tks=t["tks"],
                     unroll=t["unroll"])
        ob = _attn_b(qb, kb, vb, p["bias_b"], p["sink"], layer=l, batch=batch, seq=seq, qb=t["qb"])
        oc = _attn_c(qc, kc, vc, p["bias_c"], p["cfar_c"], p["kn_c"], p["lam"], p["g_subln"],
                     layer=l, batch=batch, seq=seq, tks=t["tks_c"], unroll=t["unroll_c"], nqb=t["nqb_c"])
        x = _out_proj(x, mod, p["g_norm"], oa, ob, oc, p["wgate"], p["wo"], layer=l, seq=seq, tm=tm)
        x = _ffn(x, mod, p["g_norm"], p["wgu"], p["wout"], layer=l, which=1, seq=seq, tm=tm)
    return x


def _prepare(w_ff_in, w_ff_out, w_in, w_o, g_qa, g_ka, g_qb, g_kb, g_qc, g_kc, sink,
             lam_q1, lam_k1, lam_q2, lam_k2, g_subln, rel_bias):
    wg = w_ff_in[..., :D_FF].reshape(DEPTH, 2, D_MODEL, N_FF_CHUNKS, FF_CHUNK)
    wu = w_ff_in[..., D_FF:].reshape(DEPTH, 2, D_MODEL, N_FF_CHUNKS, FF_CHUNK)
    wgu = jnp.concatenate([wg, wu], axis=-1).transpose(0, 1, 3, 2, 4).astype(BF16)
    wout = w_ff_out.reshape(DEPTH, 2, N_FF_CHUNKS, FF_CHUNK, D_MODEL).astype(BF16)
    zeros = jnp.zeros_like(g_qa)
    gh = jnp.stack([g_qa, g_ka, g_qb, g_kb, jnp.tile(g_qc, (1, 2)), jnp.tile(g_kc, (1, 2)), zeros, zeros],
                   axis=1).astype(F32)
    table_b, table_c = rel_bias[:, :N_Q_HEADS], rel_bias[:, N_Q_HEADS:]
    half = N_BUCKETS // 2
    c_left, c_right = table_c[half - 1] * LOG2E, table_c[N_BUCKETS - 1] * LOG2E
    lam_init = jnp.asarray([0.8 - 0.6 * math.exp(-0.3 * l) for l in range(DEPTH)], F32)
    lam = (jnp.exp(jnp.sum(lam_q1.astype(F32) * lam_k1.astype(F32), axis=-1))
           - jnp.exp(jnp.sum(lam_q2.astype(F32) * lam_k2.astype(F32), axis=-1)) + lam_init)

    band = np.abs(np.arange(NEAR)[None, :] - BLOCK_Q - np.arange(BLOCK_Q)[:, None]) <= WINDOW
    near_c = (_near_bias(table_c) * LOG2E).reshape(
        N_KV_HEADS, Q_PER_KV, BLOCK_Q, NEAR // BLOCK_Q, BLOCK_Q).transpose(0, 3, 4, 1, 2).reshape(
        N_KV_HEADS, NEAR // BLOCK_Q, BLOCK_Q, Q_PER_KV * BLOCK_Q)
    cfar_heads = jnp.stack([c_left, c_right, jnp.max(table_c, axis=0) * LOG2E]).astype(F32)
    cfar_c = jnp.repeat(cfar_heads, BLOCK_Q, axis=1).reshape(3, N_KV_HEADS, 1, Q_PER_KV * BLOCK_Q)

    def key_norm(g, dim):
        bound = 1.01 * math.sqrt(dim) * jnp.max(jnp.abs(g.astype(F32)), axis=-1)
        return jnp.broadcast_to(bound[:, None, None], (DEPTH, 1, HEAD_DIM))

    return dict(
        wgu=wgu, wout=wout,
        wqkv=w_in[:, :, :QKV_W].astype(BF16), wgate=w_in[:, :, QKV_W:].astype(BF16), wo=w_o.astype(BF16),
        gh=gh,
        bias_b=jnp.where(band, _near_bias(table_b) * LOG2E, NEG).reshape(
            N_KV_HEADS, Q_PER_KV, BLOCK_Q, NEAR).transpose(0, 3, 1, 2).reshape(
            N_KV_HEADS, NEAR, Q_PER_KV * BLOCK_Q),
        sink=jnp.repeat(sink.astype(F32) * LOG2E, BLOCK_Q, axis=1).reshape(
            DEPTH, N_KV_HEADS, 1, Q_PER_KV * BLOCK_Q),
        bias_c=jnp.concatenate([cfar_c[0][:, None] + jnp.zeros((1, 1, BLOCK_Q, 1), F32), near_c,
                                cfar_c[1][:, None] + jnp.zeros((1, 1, BLOCK_Q, 1), F32)], axis=1),
        cfar_c=cfar_heads,
        kn_a=key_norm(g_ka, HEAD_DIM), kn_c=key_norm(g_kc, DIFF_DIM),
        lam=jnp.broadcast_to(lam[:, None, None], (DEPTH, 1, HEAD_DIM)),
        g_subln=g_subln.astype(F32)[:, None, :],
    )


def kernel(x_prompt, x_sample, c_prompt, c_sample, w_ada, b_ada, g_norm, w_ff_in, w_ff_out, w_in, w_o,
           g_qa, g_ka, g_qb, g_kb, g_qc, g_kc, sink, lam_q1, lam_k1, lam_q2, lam_k2, g_subln, rel_bias):
    p = _prepare(w_ff_in, w_ff_out, w_in, w_o, g_qa, g_ka, g_qb, g_kb, g_qc, g_kc, sink,
                 lam_q1, lam_k1, lam_q2, lam_k2, g_subln, rel_bias)
    p["g_norm"] = g_norm.astype(F32)
    outs = []
    n_cond = 0
    conds = [c_prompt, c_sample]
    c_all = jnp.concatenate(conds + [jnp.zeros((ADA_ROWS - sum(c.shape[0] for c in conds), D_MODEL), F32)])
    mod_all = _ada(c_all, w_ada, b_ada)
    for x, c in ((x_prompt, c_prompt), (x_sample, c_sample)):
        batch, seq, _ = x.shape
        mod = mod_all[:, n_cond:n_cond + batch].reshape(DEPTH, batch, 9, D_MODEL)
        n_cond += batch
        y = _trunk(x.reshape(batch * seq, D_MODEL), mod, p, batch=batch, seq=seq)
        outs.append(y.reshape(batch, seq, D_MODEL))
    return tuple(outs)
```

```python
import functools
import math

import numpy as np
import jax
import jax.numpy as jnp
from jax import lax
from jax.experimental import pallas as pl
from jax.experimental.pallas import tpu as pltpu

F32 = jnp.float32
BF16 = jnp.bfloat16
ATTN_OUT = jnp.bfloat16

D_MODEL = 1024
DEPTH = 4
N_Q_HEADS = 8
N_KV_HEADS = 2
Q_PER_KV = N_Q_HEADS // N_KV_HEADS
HEAD_DIM = 128
DIFF_DIM = 64
D_FF = 2816
BLOCK_Q = 128
WINDOW = 128
GRID_W = 64
N_BUCKETS = 32
MAX_DISTANCE = 128
ROPE_THETA = 10000.0
EPS = 1e-6
NEG = -1e30
LOG2E = 1.4426950408889634

Q_W = N_Q_HEADS * HEAD_DIM
KV_W = N_KV_HEADS * HEAD_DIM
BRANCH_IN = Q_W + 2 * KV_W
QKV_W = 3 * BRANCH_IN
FF_CHUNK = 256
N_FF_CHUNKS = D_FF // FF_CHUNK
ADA_ROWS = 16
ADA_TN = 1536
NEAR = 3 * BLOCK_Q
VT_ROWS = HEAD_DIM + 16

VMEM_LIMIT = 56 * 1024 * 1024


def _tiles(seq):
    tks = min(256, seq)
    unroll = min(16, seq // tks)
    single_region = seq <= unroll * tks
    return dict(
        tm=min(512, seq),
        tm_out=min(1024, seq),
        tq_a=2 * BLOCK_Q,
        nqb_c=2 if single_region else 1,
        tks=tks,
        qb=min(8, seq // BLOCK_Q),
        unroll=unroll,
        tks_c=tks,
        unroll_c=unroll,
    )


def _cparams(*sem):
    return pltpu.CompilerParams(dimension_semantics=sem, vmem_limit_bytes=VMEM_LIMIT)


def _dot(a, b):
    return jnp.dot(a, b, preferred_element_type=F32)


def _dot_nt(a, b):
    return lax.dot_general(a, b, (((1,), (1,)), ((), ())), preferred_element_type=F32)


def _ada_kernel(c_ref, w_ref, b_ref, o_ref):
    c = c_ref[...]
    a = (c * jax.nn.sigmoid(c)).astype(BF16)
    o_ref[0] = _dot(a, w_ref[0].astype(BF16)) + b_ref[0]


def _ada(c_all, w_ada, b_ada):
    n_out = w_ada.shape[-1]
    return pl.pallas_call(
        _ada_kernel,
        out_shape=jax.ShapeDtypeStruct((DEPTH, ADA_ROWS, n_out), F32),
        grid=(DEPTH, n_out // ADA_TN),
        in_specs=[
            pl.BlockSpec((ADA_ROWS, D_MODEL), lambda l, j: (0, 0)),
            pl.BlockSpec((1, D_MODEL, ADA_TN), lambda l, j: (l, 0, j)),
            pl.BlockSpec((1, 1, ADA_TN), lambda l, j: (l, 0, j)),
        ],
        out_specs=pl.BlockSpec((1, ADA_ROWS, ADA_TN), lambda l, j: (l, 0, j)),
        compiler_params=_cparams("arbitrary", "arbitrary"),
        name="ada",
    )(c_all, w_ada, b_ada.reshape(DEPTH, 1, n_out))


def _modulate(x, mod_ref, g, jj):
    ms = jnp.mean(x * x, axis=-1, keepdims=True)
    y = x * lax.rsqrt(ms + EPS) * g
    return y * (1.0 + mod_ref[0, 3 * jj + 1:3 * jj + 2, :]) + mod_ref[0, 3 * jj:3 * jj + 1, :]


def _ffn_kernel(x_ref, mod_ref, g_ref, wgu_ref, wout_ref, o_ref, nb_ref, acc_ref, *, jj):
    x = x_ref[...]
    nb_ref[...] = _modulate(x, mod_ref, g_ref[jj:jj + 1, :], jj).astype(BF16)
    acc_ref[...] = jnp.zeros_like(acc_ref)

    def body(c, carry):
        h = _dot(nb_ref[...], wgu_ref[c])
        hg = h[:, :FF_CHUNK]
        a = (hg * jax.nn.sigmoid(hg)) * h[:, FF_CHUNK:]
        acc_ref[...] += _dot(a.astype(BF16), wout_ref[c])
        return carry

    lax.fori_loop(0, N_FF_CHUNKS, body, 0, unroll=True)
    o_ref[...] = x + (0.5 * mod_ref[0, 3 * jj + 2:3 * jj + 3, :]) * acc_ref[...]


def _ffn(x, mod, g_norm, wgu, wout, *, layer, which, seq, tm):
    n = x.shape[0]
    jj = 2 * which
    const = dict(pipeline_mode=pl.Buffered(1))
    return pl.pallas_call(
        functools.partial(_ffn_kernel, jj=jj),
        out_shape=jax.ShapeDtypeStruct((n, D_MODEL), F32),
        grid=(n // tm,),
        in_specs=[
            pl.BlockSpec((tm, D_MODEL), lambda i: (i, 0)),
            pl.BlockSpec((None, 1, 9, D_MODEL), lambda i: (layer, (i * tm) // seq, 0, 0)),
            pl.BlockSpec((None, 3, D_MODEL), lambda i: (layer, 0, 0)),
            pl.BlockSpec((None, None, N_FF_CHUNKS, D_MODEL, 2 * FF_CHUNK),
                         lambda i: (layer, which, 0, 0, 0), **const),
            pl.BlockSpec((None, None, N_FF_CHUNKS, FF_CHUNK, D_MODEL),
                         lambda i: (layer, which, 0, 0, 0), **const),
        ],
        out_specs=pl.BlockSpec((tm, D_MODEL), lambda i: (i, 0)),
        scratch_shapes=[pltpu.VMEM((tm, D_MODEL), BF16), pltpu.VMEM((tm, D_MODEL), F32)],
        input_output_aliases={} if (layer == 0 and which == 0) else {0: 0},
        compiler_params=_cparams("arbitrary"),
        name=f"ffn{which}",
    )(x, mod, g_norm, wgu, wout)


def _head_norm(r, g):
    ms = jnp.mean(r * r, axis=-1, keepdims=True)
    return r * lax.rsqrt(ms + EPS) * g


def _half_norm(r, g2):
    sq = r * r
    lo = lax.broadcasted_iota(jnp.int32, r.shape, 1) < DIFF_DIM
    s_lo = jnp.sum(jnp.where(lo, sq, 0.0), axis=-1, keepdims=True)
    s_hi = jnp.sum(jnp.where(lo, 0.0, sq), axis=-1, keepdims=True)
    ms = jnp.where(lo, s_lo, s_hi) * (1.0 / DIFF_DIM)
    return r * lax.rsqrt(ms + EPS) * g2


def _qkv_kernel(x_ref, mod_ref, g_ref, w_ref, cos_ref, sa_ref, sb_ref, gh_ref,
                qa_ref, ka_ref, va_ref, qb_ref, kb_ref, vb_ref, qc_ref, kc_ref, vc_ref, nb_ref):
    nb_ref[...] = _modulate(x_ref[...], mod_ref, g_ref[1:2, :], 1).astype(BF16)
    cos, sa, sb = cos_ref[...], sa_ref[...], sb_ref[...]

    def rope(r):
        return (r * cos + pltpu.roll(r, HEAD_DIM - 32, 1) * sa + pltpu.roll(r, 32, 1) * sb)

    def proj(col, width):
        return _dot(nb_ref[...], w_ref[:, col:col + width])

    def heads(r, n_heads, fn, out_ref):
        for h in range(n_heads):
            out_ref[h] = fn(r[:, h * HEAD_DIM:(h + 1) * HEAD_DIM]).astype(BF16)

    def values_transposed(r, out_ref):
        for h in range(N_KV_HEADS):
            out_ref[h, :HEAD_DIM, :] = r[:, h * HEAD_DIM:(h + 1) * HEAD_DIM].T.astype(BF16)
            out_ref[h, HEAD_DIM:, :] = jnp.ones((VT_ROWS - HEAD_DIM, r.shape[0]), BF16)

    qa_scale = HEAD_DIM ** -0.5 * LOG2E
    qc_scale = DIFF_DIM ** -0.5 * LOG2E
    g_qa, g_ka, g_qb, g_kb = (gh_ref[i:i + 1, :] for i in range(4))
    g_qc, g_kc = gh_ref[4:5, :], gh_ref[5:6, :]
    col = 0
    heads(proj(col, Q_W), N_Q_HEADS, lambda r: rope(_head_norm(r, g_qa)) * qa_scale, qa_ref)
    col += Q_W
    heads(proj(col, KV_W), N_KV_HEADS, lambda r: rope(_head_norm(r, g_ka)), ka_ref)
    col += KV_W
    values_transposed(proj(col, KV_W),va_ref)
    col += KV_W
    heads(proj(col, Q_W), N_Q_HEADS, lambda r: _head_norm(r, g_qb) * qa_scale, qb_ref)
    col += Q_W
    heads(proj(col, KV_W), N_KV_HEADS, lambda r: _head_norm(r, g_kb), kb_ref)
    col += KV_W
    values_transposed(proj(col, KV_W), vb_ref)
    col += KV_W
    heads(proj(col, Q_W), N_Q_HEADS, lambda r: _half_norm(r, g_qc) * qc_scale, qc_ref)
    col += Q_W
    heads(proj(col, KV_W), N_KV_HEADS, lambda r: _half_norm(r, g_kc), kc_ref)
    col += KV_W
    values_transposed(proj(col, KV_W),vc_ref)


def _qkv(x, mod, g_norm, wqkv, rope_tabs, gh, *, layer, seq, tm):
    n = x.shape[0]
    tiles_per_seq = seq // tm
    q_shape = jax.ShapeDtypeStruct((N_Q_HEADS, n, HEAD_DIM), BF16)
    kv_shape = jax.ShapeDtypeStruct((N_KV_HEADS, n, HEAD_DIM), BF16)
    q_spec = pl.BlockSpec((N_Q_HEADS, tm, HEAD_DIM), lambda i: (0, i, 0))
    kv_spec = pl.BlockSpec((N_KV_HEADS, tm, HEAD_DIM), lambda i: (0, i, 0))
    tab_spec = pl.BlockSpec((tm, HEAD_DIM), lambda i: (i % tiles_per_seq, 0))
    v1_shape = jax.ShapeDtypeStruct((N_KV_HEADS, VT_ROWS, n), BF16)
    v1_spec = pl.BlockSpec((N_KV_HEADS, VT_ROWS, tm), lambda i: (0, 0, i))
    return pl.pallas_call(
        _qkv_kernel,
        out_shape=[q_shape, kv_shape, v1_shape] * 3,
        grid=(n // tm,),
        in_specs=[
            pl.BlockSpec((tm, D_MODEL), lambda i: (i, 0)),
            pl.BlockSpec((None, 1, 9, D_MODEL), lambda i: (layer, (i * tm) // seq, 0, 0)),
            pl.BlockSpec((None, 3, D_MODEL), lambda i: (layer, 0, 0)),
            pl.BlockSpec((None, D_MODEL, QKV_W), lambda i: (layer, 0, 0), pipeline_mode=pl.Buffered(1)),
            tab_spec, tab_spec, tab_spec,
            pl.BlockSpec((None, 8, HEAD_DIM), lambda i: (layer, 0, 0)),
        ],
        out_specs=[q_spec, kv_spec, v1_spec] * 3,
        scratch_shapes=[pltpu.VMEM((tm, D_MODEL), BF16)],
        compiler_params=_cparams("arbitrary"),
        name="qkv",
    )(x, mod, g_norm, wqkv, *rope_tabs, gh)


def _out_kernel(x_ref, mod_ref, g_ref, oa_ref, ob_ref, oc_ref, wg_ref, wo_ref, o_ref, mg_ref):
    x = x_ref[...]
    nb = _modulate(x, mod_ref, g_ref[1:2, :], 1).astype(BF16)
    for br, br_ref in enumerate((oa_ref, ob_ref, oc_ref)):
        gate = jax.nn.sigmoid(_dot(nb, wg_ref[:, br * D_MODEL:(br + 1) * D_MODEL]))
        for h in range(N_Q_HEADS):
            lanes = slice(h * HEAD_DIM, (h + 1) * HEAD_DIM)
            term = gate[:, lanes] * br_ref[h].astype(F32)
            if br == 0:
                mg_ref[:, lanes] = term
            else:
                mg_ref[:, lanes] += term
    o_ref[...] = x + mod_ref[0, 5:6, :] * _dot(mg_ref[...].astype(BF16), wo_ref[...])


def _out_proj(x, mod, g_norm, oa, ob, oc, wgate, wo, *, layer, seq, tm):
    n = x.shape[0]
    o_spec = pl.BlockSpec((N_Q_HEADS, tm, HEAD_DIM), lambda i: (0, i, 0))
    return pl.pallas_call(
        _out_kernel,
        out_shape=jax.ShapeDtypeStruct((n, D_MODEL), F32),
        grid=(n // tm,),
        in_specs=[
            pl.BlockSpec((tm, D_MODEL), lambda i: (i, 0)),
            pl.BlockSpec((None, 1, 9, D_MODEL), lambda i: (layer, (i * tm) // seq, 0, 0)),
            pl.BlockSpec((None, 3, D_MODEL), lambda i: (layer, 0, 0)),
            o_spec, o_spec, o_spec,
            pl.BlockSpec((None, D_MODEL, 3 * D_MODEL), lambda i: (layer, 0, 0), pipeline_mode=pl.Buffered(1)),
            pl.BlockSpec((None, D_MODEL, D_MODEL), lambda i: (layer, 0, 0), pipeline_mode=pl.Buffered(1)),
        ],
        out_specs=pl.BlockSpec((tm, D_MODEL), lambda i: (i, 0)),
        scratch_shapes=[pltpu.VMEM((tm, D_MODEL), F32)],
        input_output_aliases={0: 0},
        compiler_params=_cparams("arbitrary"),
        name="out_proj",
    )(x, mod, g_norm, oa, ob, oc, wgate, wo)


MIN_DENOM = 2.0 ** -64


def _key_slice(idx, tks):
    return pl.ds(pl.multiple_of(idx * tks, tks), tks)


def _score_bound(q, key_norm):
    qf = q.astype(F32)
    qn2 = _dot_nt(jnp.ones((8, q.shape[1]), F32), qf * qf)[:1]
    return jnp.sqrt(qn2) * key_norm


def _bounded_iteration(t, last, q, k_ref, v_ref, acct_ref, s_ref, *, unroll, tks, shift, bias_fn=None):
    acct = acct_ref[...]
    s = s_ref[...]
    for u in range(unroll):
        idx = t * unroll + u
        s_next = None if (last and u == unroll - 1) else _dot_nt(k_ref[_key_slice(idx + 1, tks), :], q)
        if bias_fn is not None:
            s = bias_fn(idx, s)
        acct = acct + _dot(v_ref[:, _key_slice(idx, tks)], jnp.exp2(s - shift).astype(BF16))
        s = s_next
    acct_ref[...] = acct
    if not last:
        s_ref[...] = s


def _exact_pass(q, k_ref, v_ref, m_ref, acct_ref, *, n_sub, tks, bias_fn=None):
    m_ref[...] = jnp.full_like(m_ref, -jnp.inf)
    acct_ref[...] = jnp.zeros_like(acct_ref)

    def body(idx, carry):
        s = _dot_nt(k_ref[_key_slice(idx, tks), :], q)
        if bias_fn is not None:
            s = bias_fn(idx, s)
        m = m_ref[...]
        m_new = jnp.maximum(m, jnp.max(s, axis=0, keepdims=True))
        p = jnp.exp2(s - m_new).astype(BF16)
        acct_ref[...] = jnp.exp2(m - m_new) * acct_ref[...] + _dot(v_ref[:, _key_slice(idx, tks)], p)
        m_ref[...] = m_new
        return carry

    lax.fori_loop(0, n_sub, body, 0)


def _denominators_ok(acct_ref):
    return jnp.min(acct_ref[HEAD_DIM:HEAD_DIM + 1, :]) >= MIN_DENOM


def _attn_a_kernel(q_ref, k_ref, v_ref, kn_ref, o_ref, m_ref, acct_ref, s_ref, *, n_iter, unroll, tks):
    rows = acct_ref.shape[1]
    q = q_ref[...].reshape(rows, HEAD_DIM)
    shift = _score_bound(q, kn_ref[:, :1])
    acct_ref[...] = jnp.zeros_like(acct_ref)
    s_ref[...] = _dot_nt(k_ref[0:tks, :], q)
    refs = (q, k_ref, v_ref, acct_ref, s_ref)

    def body(t, carry):
        _bounded_iteration(t, False, *refs, unroll=unroll, tks=tks, shift=shift)
        return carry

    lax.fori_loop(0, n_iter - 1, body, 0)
    _bounded_iteration(n_iter - 1, True, *refs, unroll=unroll, tks=tks, shift=shift)

    @pl.when(jnp.logical_not(_denominators_ok(acct_ref)))
    def _():
        _exact_pass(q, k_ref, v_ref, m_ref, acct_ref, n_sub=n_iter * unroll, tks=tks)

    acct = acct_ref[...]
    o_t = acct[:HEAD_DIM] / acct[HEAD_DIM:HEAD_DIM + 1]
    o_ref[...] = o_t.T.astype(o_ref.dtype).reshape(o_ref.shape)


def _attn_a(q, k, v1t, key_norm, *, layer, batch, seq, tq, tks, unroll):
    n = q.shape[1]
    nq = seq // tq
    rows = Q_PER_KV * tq
    q_map = lambda b, g, i: (g, b * nq + i, 0)
    return pl.pallas_call(
        functools.partial(_attn_a_kernel, n_iter=seq // (tks * unroll), unroll=unroll, tks=tks),
        out_shape=jax.ShapeDtypeStruct((N_Q_HEADS, n, HEAD_DIM), ATTN_OUT),
        grid=(batch, N_KV_HEADS, nq),
        in_specs=[
            pl.BlockSpec((Q_PER_KV, tq, HEAD_DIM), q_map),
            pl.BlockSpec((None, seq, HEAD_DIM), lambda b, g, i: (g, b, 0)),
            pl.BlockSpec((None, VT_ROWS, seq), lambda b, g, i: (g, 0, b)),
            pl.BlockSpec((None, 1, HEAD_DIM), lambda b, g, i: (layer, 0, 0)),
        ],
        out_specs=pl.BlockSpec((Q_PER_KV, tq, HEAD_DIM), q_map),
        scratch_shapes=[pltpu.VMEM((1, rows), F32), pltpu.VMEM((VT_ROWS, rows), F32),
                        pltpu.VMEM((tks, rows), F32)],
        compiler_params=_cparams("arbitrary", "arbitrary", "arbitrary"),
        name="attn_axial",
    )(q, k, v1t, key_norm)


def _attn_b_kernel(q_ref, kp_ref, kc_ref, kn_ref, vp_ref, vc_ref, vn_ref, bias_ref, sink_ref, o_ref,
                   *, nb, qb):
    i = pl.program_id(2)
    rows = Q_PER_KV * BLOCK_Q
    kwin = jnp.concatenate([kp_ref[...], kc_ref[...], kn_ref[...]], axis=0)
    vwin = jnp.concatenate([vp_ref[...], vc_ref[...], vn_ref[...]], axis=1)
    bias = bias_ref[...]
    sink = sink_ref[...]
    for u in range(qb):
        blk = i * qb + u
        q = q_ref[:, u * BLOCK_Q:(u + 1) * BLOCK_Q, :].reshape(rows, HEAD_DIM)
        s = _dot_nt(kwin[u * BLOCK_Q:u * BLOCK_Q + NEAR], q) + bias
        s = jnp.concatenate([jnp.where(blk > 0, s[:BLOCK_Q], NEG), s[BLOCK_Q:2 * BLOCK_Q],
                             jnp.where(blk < nb - 1, s[2 * BLOCK_Q:], NEG)], axis=0)
        m = jnp.maximum(jnp.max(s, axis=0, keepdims=True), sink)
        p = jnp.exp2(s - m).astype(BF16)
        acct = _dot(vwin[:, u * BLOCK_Q:u * BLOCK_Q + NEAR], p)
        den = acct[HEAD_DIM:HEAD_DIM + 1] + jnp.exp2(sink - m)
        o_t = acct[:HEAD_DIM] / den
        o_ref[:, u * BLOCK_Q:(u + 1) * BLOCK_Q, :] = o_t.T.astype(o_ref.dtype).reshape(
            Q_PER_KV, BLOCK_Q, HEAD_DIM)


def _attn_b(q, k, v1t, bias_t, sink_t, *, layer, batch, seq, qb):
    n = q.shape[1]
    nb = seq // BLOCK_Q
    steps = nb // qb
    rows = Q_PER_KV * BLOCK_Q
    cur_map = lambda b, g, i: (g, b * steps + i, 0)
    prev_map = lambda b, g, i: (g, b * nb + jnp.maximum(i * qb - 1, 0), 0)
    next_map = lambda b, g, i: (g, b * nb + jnp.minimum(i * qb + qb, nb - 1), 0)
    t = lambda index_map: (lambda b, g, i: (index_map(b, g, i)[0], 0, index_map(b, g, i)[1]))
    k_specs = [pl.BlockSpec((None, BLOCK_Q, HEAD_DIM), prev_map),
               pl.BlockSpec((None, qb * BLOCK_Q, HEAD_DIM), cur_map),
               pl.BlockSpec((None, BLOCK_Q, HEAD_DIM), next_map)]
    v_specs = [pl.BlockSpec((None, VT_ROWS, BLOCK_Q), t(prev_map)),
               pl.BlockSpec((None, VT_ROWS, qb * BLOCK_Q), t(cur_map)),
               pl.BlockSpec((None, VT_ROWS, BLOCK_Q), t(next_map))]
    return pl.pallas_call(
        functools.partial(_attn_b_kernel, nb=nb, qb=qb),
        out_shape=jax.ShapeDtypeStruct((N_Q_HEADS, n, HEAD_DIM), ATTN_OUT),
        grid=(batch, N_KV_HEADS, steps),
        in_specs=[pl.BlockSpec((Q_PER_KV, qb * BLOCK_Q, HEAD_DIM), cur_map)] + k_specs + v_specs + [
            pl.BlockSpec((None, NEAR, rows), lambda b, g, i: (g, 0, 0)),
            pl.BlockSpec((None, None, 1, rows), lambda b, g, i: (layer, g, 0, 0)),
        ],
        out_specs=pl.BlockSpec((Q_PER_KV, qb * BLOCK_Q, HEAD_DIM), cur_map),
        compiler_params=_cparams("arbitrary", "arbitrary", "arbitrary"),
        name="attn_window",
    )(q, k, k, k, v1t, v1t, v1t, bias_t, sink_t)


def _attn_c_kernel(q_ref, k_ref, v_ref, bias_ref, cfar_ref, kn_ref, lam_ref, gs_ref, o_ref,
                   m_ref, acct_ref, s_ref, *, n_iter, unroll, tks, nblk, nqb, out_scale):
    first_blk = pl.program_id(2) * nqb
    rows = Q_PER_KV * nqb * BLOCK_Q
    blocks_per_sub = tks // BLOCK_Q
    blocks_per_iter = unroll * blocks_per_sub

    q = q_ref[...].reshape(rows, HEAD_DIM)
    lo = lax.broadcasted_iota(jnp.int32, q.shape, 1) < DIFF_DIM
    zero = jnp.zeros_like(q)
    q2 = jnp.concatenate([jnp.where(lo, q, zero), jnp.where(lo, zero, q)], axis=0)
    acct_ref[...] = jnp.zeros_like(acct_ref)
    s_ref[...] = _dot_nt(k_ref[0:tks, :], q2)
    refs = (q2, k_ref, v_ref, acct_ref, s_ref)

    c_left, c_right, c_max = cfar_ref[0], cfar_ref[1], cfar_ref[2]
    t_lo = jnp.maximum(first_blk - 1, 0) // blocks_per_iter
    t_hi = jnp.minimum(first_blk + nqb, nblk - 1) // blocks_per_iter
    two = lambda x: jnp.concatenate([x, x], axis=1)
    shift = _score_bound(q2, kn_ref[:, :1]) + two(c_max)

    def add_bias(idx, s):
        blks = []
        for kb in range(blocks_per_sub):
            key_blk = idx * blocks_per_sub + kb
            tiles = [bias_ref[jnp.clip(key_blk - (first_blk + a) + 2, 0, NEAR // BLOCK_Q + 1)]
                     for a in range(nqb)]
            if nqb == 1:
                blks.append(tiles[0])
            else:
                blks.append(jnp.concatenate([tiles[a][:, h * BLOCK_Q:(h + 1) * BLOCK_Q]
                                             for h in range(Q_PER_KV) for a in range(nqb)], axis=1))
        bias = jnp.concatenate(blks, axis=0)
        return jnp.concatenate([s[:, :rows] + bias, s[:, rows:] + bias], axis=1)

    def iteration(t, last):
        is_near = (t >= t_lo) & (t <= t_hi)

        @pl.when(jnp.logical_not(is_near))
        def _():
            side = jnp.where(t < t_lo, c_left, c_right)
            _bounded_iteration(t, last, *refs, unroll=unroll, tks=tks, shift=shift - two(side))

        @pl.when(is_near)
        def _():
            _bounded_iteration(t, last, *refs, unroll=unroll, tks=tks, shift=shift, bias_fn=add_bias)

    def body(t, carry):
        iteration(t, False)
        return carry

    lax.fori_loop(0, n_iter - 1, body, 0)
    iteration(n_iter - 1, True)

    @pl.when(jnp.logical_not(_denominators_ok(acct_ref)))
    def _():
        _exact_pass(q2, k_ref, v_ref, m_ref, acct_ref, n_sub=n_iter * unroll, tks=tks, bias_fn=add_bias)

    acct = acct_ref[...]
    o1_t = acct[:HEAD_DIM, :rows] / acct[HEAD_DIM:HEAD_DIM + 1, :rows]
    o2_t = acct[:HEAD_DIM, rows:] / acct[HEAD_DIM:HEAD_DIM + 1, rows:]
    o_t = o1_t - lam_ref[:, :1] * o2_t
    ms = jnp.mean(o_t * o_t, axis=0, keepdims=True)
    o = (o_t * lax.rsqrt(ms + EPS)).T * gs_ref[...] * out_scale
    o_ref[...] = o.astype(o_ref.dtype).reshape(o_ref.shape)


def _attn_c(q, k, v1t, bias_t, cfar, key_norm, lam, g_subln, *, layer, batch, seq, tks, unroll, nqb):
    n = q.shape[1]
    nq = seq // (nqb * BLOCK_Q)
    rows = Q_PER_KV * nqb * BLOCK_Q
    lam_init = 0.8 - 0.6 * math.exp(-0.3 * layer)
    q_map = lambda b, g, i: (g, b * nq + i, 0)
    cfar_t = jnp.repeat(cfar, nqb * BLOCK_Q, axis=1).reshape(3, N_KV_HEADS, 1, rows)
    return pl.pallas_call(
        functools.partial(_attn_c_kernel, n_iter=seq // (tks * unroll), unroll=unroll, tks=tks,
                          nblk=seq // BLOCK_Q, nqb=nqb, out_scale=1.0 - lam_init),
        out_shape=jax.ShapeDtypeStruct((N_Q_HEADS, n, HEAD_DIM), ATTN_OUT),
        grid=(batch, N_KV_HEADS, nq),
        in_specs=[
            pl.BlockSpec((Q_PER_KV, nqb * BLOCK_Q, HEAD_DIM), q_map),
            pl.BlockSpec((None, seq, HEAD_DIM), lambda b, g, i: (g, b, 0)),
            pl.BlockSpec((None, VT_ROWS, seq), lambda b, g, i: (g, 0, b)),
            pl.BlockSpec((None, NEAR // BLOCK_Q + 2, BLOCK_Q, Q_PER_KV * BLOCK_Q), lambda b, g, i: (g, 0, 0, 0)),
            pl.BlockSpec((3, None, 1, rows), lambda b, g, i: (0, g, 0, 0)),
            pl.BlockSpec((None, 1, HEAD_DIM), lambda b, g, i: (layer, 0, 0)),
            pl.BlockSpec((None, 1, HEAD_DIM), lambda b, g, i: (layer, 0, 0)),
            pl.BlockSpec((None, 1, HEAD_DIM), lambda b, g, i: (layer, 0, 0)),
        ],
        out_specs=pl.BlockSpec((Q_PER_KV, nqb * BLOCK_Q, HEAD_DIM), q_map),
        scratch_shapes=[pltpu.VMEM((1, 2 * rows), F32), pltpu.VMEM((VT_ROWS, 2 * rows), F32),
                        pltpu.VMEM((tks, 2 * rows), F32)],
        compiler_params=_cparams("arbitrary", "arbitrary", "arbitrary"),
        name="attn_diff",
    )(q, k, v1t, bias_t, cfar_t, key_norm, lam, g_subln)


def _t5_bucket_np(rel):
    half = N_BUCKETS // 2
    max_exact = half // 2
    ret = np.where(rel > 0, half, 0)
    n = np.abs(rel)
    ratio = np.log(np.maximum(n, 1).astype(np.float32) / np.float32(max_exact)) / np.float32(
        math.log(MAX_DISTANCE / max_exact))
    large = max_exact + (ratio * np.float32(half - max_exact)).astype(np.int32)
    large = np.minimum(large, half - 1)
    return (ret + np.where(n < max_exact, n, large)).astype(np.int32)


def _near_bias(table):
    r = np.arange(BLOCK_Q)[:, None]
    c = np.arange(NEAR)[None, :]
    bucket = _t5_bucket_np(c - BLOCK_Q - r)
    onehot = (bucket.reshape(-1, 1) == np.arange(N_BUCKETS)[None, :]).astype(np.float32)
    rows = jnp.dot(jnp.asarray(onehot), table.astype(F32), precision=lax.Precision.HIGHEST)
    return rows.reshape(BLOCK_Q, NEAR, table.shape[1]).transpose(2, 0, 1)


def _rope_tables(seq):
    rows = seq // GRID_W
    nfreq = HEAD_DIM // 4
    inv = ROPE_THETA ** (-jnp.arange(nfreq, dtype=F32) / nfreq)
    ang_r = jnp.arange(rows).astype(F32)[:, None] * inv
    ang_c = jnp.arange(GRID_W).astype(F32)[:, None] * inv
    by_row = lambda t: jnp.broadcast_to(t[:, None, :], (rows, GRID_W, nfreq)).reshape(seq, nfreq)
    by_col = lambda t: jnp.broadcast_to(t[None, :, :], (rows, GRID_W, nfreq)).reshape(seq, nfreq)
    table = lambda fn: jnp.concatenate([by_row(fn(ang_r))] * 2 + [by_col(fn(ang_c))] * 2, axis=-1)
    cos, sin = table(jnp.cos), table(jnp.sin)
    first = (np.arange(HEAD_DIM) % (HEAD_DIM // 2)) < HEAD_DIM // 4
    return cos, jnp.where(first, -sin, 0.0), jnp.where(first, 0.0, sin)


def _trunk(x, mod, p, *, batch, seq):
    t = _tiles(seq)
    tm = t["tm"]
    rope_tabs = _rope_tables(seq)
    for l in range(DEPTH):
        x = _ffn(x, mod, p["g_norm"], p["wgu"], p["wout"], layer=l, which=0, seq=seq, tm=tm)
        qa, ka, va, qb, kb, vb, qc, kc, vc = _qkv(x, mod, p["g_norm"], p["wqkv"], rope_tabs, p["gh"],
                                                  layer=l, seq=seq, tm=tm)
        oa = _attn_a(qa, ka, va, p["kn_a"], layer=l, batch=batch, seq=seq, tq=t["tq_a"], tks=t["tks"],
                     unroll=t["unroll"])
        ob = _attn_b(qb, kb, vb, p["bias_b"], p["sink"], layer=l, batch=batch, seq=seq, qb=t["qb"])
        oc = _attn_c(qc, kc, vc, p["bias_c"], p["cfar_c"], p["kn_c"], p["lam"], p["g_subln"],
                     layer=l, batch=batch, seq=seq, tks=t["tks_c"], unroll=t["unroll_c"], nqb=t["nqb_c"])
        x = _out_proj(x, mod, p["g_norm"], oa, ob, oc, p["wgate"], p["wo"], layer=l, seq=seq,
                      tm=t["tm_out"])
        x = _ffn(x, mod, p["g_norm"], p["wgu"], p["wout"], layer=l, which=1, seq=seq, tm=tm)
    return x


def _prepare(w_ff_in, w_ff_out, w_in, w_o, g_qa, g_ka, g_qb, g_kb, g_qc, g_kc, sink,
             lam_q1, lam_k1, lam_q2, lam_k2, g_subln, rel_bias):
    wg = w_ff_in[..., :D_FF].reshape(DEPTH, 2, D_MODEL, N_FF_CHUNKS, FF_CHUNK)
    wu = w_ff_in[..., D_FF:].reshape(DEPTH, 2, D_MODEL, N_FF_CHUNKS, FF_CHUNK)
    wgu = jnp.concatenate([wg, wu], axis=-1).transpose(0, 1, 3, 2, 4).astype(BF16)
    wout = w_ff_out.reshape(DEPTH, 2, N_FF_CHUNKS, FF_CHUNK, D_MODEL).astype(BF16)
    zeros = jnp.zeros_like(g_qa)
    gh = jnp.stack([g_qa, g_ka, g_qb, g_kb, jnp.tile(g_qc, (1, 2)), jnp.tile(g_kc, (1, 2)), zeros, zeros],
                   axis=1).astype(F32)
    table_b, table_c = rel_bias[:, :N_Q_HEADS], rel_bias[:, N_Q_HEADS:]
    half = N_BUCKETS // 2
    c_left, c_right = table_c[half - 1] * LOG2E, table_c[N_BUCKETS - 1] * LOG2E
    lam_init = jnp.asarray([0.8 - 0.6 * math.exp(-0.3 * l) for l in range(DEPTH)], F32)
    lam = (jnp.exp(jnp.sum(lam_q1.astype(F32) * lam_k1.astype(F32), axis=-1))
           - jnp.exp(jnp.sum(lam_q2.astype(F32) * lam_k2.astype(F32), axis=-1)) + lam_init)

    band = np.abs(np.arange(NEAR)[None, :] - BLOCK_Q - np.arange(BLOCK_Q)[:, None]) <= WINDOW
    near_c = (_near_bias(table_c) * LOG2E).reshape(
        N_KV_HEADS, Q_PER_KV, BLOCK_Q, NEAR // BLOCK_Q, BLOCK_Q).transpose(0, 3, 4, 1, 2).reshape(
        N_KV_HEADS, NEAR // BLOCK_Q, BLOCK_Q, Q_PER_KV * BLOCK_Q)
    cfar_heads = jnp.stack([c_left, c_right, jnp.max(table_c, axis=0) * LOG2E]).astype(F32)
    cfar_c = jnp.repeat(cfar_heads, BLOCK_Q, axis=1).reshape(3, N_KV_HEADS, 1, Q_PER_KV * BLOCK_Q)

    def key_norm(g, dim):
        bound = 1.01 * math.sqrt(dim) * jnp.max(jnp.abs(g.astype(F32)), axis=-1)
        return jnp.broadcast_to(bound[:, None, None], (DEPTH, 1, HEAD_DIM))

    return dict(
        wgu=wgu, wout=wout,
        wqkv=w_in[:, :, :QKV_W].astype(BF16), wgate=w_in[:, :, QKV_W:].astype(BF16), wo=w_o.astype(BF16),
        gh=gh,
        bias_b=jnp.where(band, _near_bias(table_b) * LOG2E, NEG).reshape(
            N_KV_HEADS, Q_PER_KV, BLOCK_Q, NEAR).transpose(0, 3, 1, 2).reshape(
            N_KV_HEADS, NEAR, Q_PER_KV * BLOCK_Q),
        sink=jnp.repeat(sink.astype(F32) * LOG2E, BLOCK_Q, axis=1).reshape(
            DEPTH, N_KV_HEADS, 1, Q_PER_KV * BLOCK_Q),
        bias_c=jnp.concatenate([cfar_c[0][:, None] + jnp.zeros((1, 1, BLOCK_Q, 1), F32), near_c,
                                cfar_c[1][:, None] + jnp.zeros((1, 1, BLOCK_Q, 1), F32)], axis=1),
        cfar_c=cfar_heads,
        kn_a=key_norm(g_ka, HEAD_DIM), kn_c=key_norm(g_kc, DIFF_DIM),
        lam=jnp.broadcast_to(lam[:, None, None], (DEPTH, 1, HEAD_DIM)),
        g_subln=g_subln.astype(F32)[:, None, :],
    )


def kernel(x_prompt, x_sample, c_prompt, c_sample, w_ada, b_ada, g_norm, w_ff_in, w_ff_out, w_in, w_o,
           g_qa, g_ka, g_qb, g_kb, g_qc, g_kc, sink, lam_q1, lam_k1, lam_q2, lam_k2, g_subln, rel_bias):
    p = _prepare(w_ff_in, w_ff_out, w_in, w_o, g_qa, g_ka, g_qb, g_kb, g_qc, g_kc, sink,
                 lam_q1, lam_k1, lam_q2, lam_k2, g_subln, rel_bias)
    p["g_norm"] = g_norm.astype(F32)
    outs = []
    n_cond = 0
    conds = [c_prompt, c_sample]
    c_all = jnp.concatenate(conds + [jnp.zeros((ADA_ROWS - sum(c.shape[0] for c in conds), D_MODEL), F32)])
    mod_all = _ada(c_all, w_ada, b_ada)
    for x, c in ((x_prompt, c_prompt), (x_sample, c_sample)):
        batch, seq, _ = x.shape
        mod = mod_all[:, n_cond:n_cond + batch].reshape(DEPTH, batch, 9, D_MODEL)
        n_cond += batch
        y = _trunk(x.reshape(batch * seq, D_MODEL), mod, p, batch=batch, seq=seq)
        outs.append(y.reshape(batch, seq, D_MODEL))
    return tuple(outs)
```

```python
import functools
import math

import numpy as np
import jax
import jax.numpy as jnp
from jax import lax
from jax.experimental import pallas as pl
from jax.experimental.pallas import tpu as pltpu

F32 = jnp.float32
BF16 = jnp.bfloat16
ATTN_OUT = jnp.bfloat16

D_MODEL = 1024
DEPTH = 4
N_Q_HEADS = 8
N_KV_HEADS = 2
Q_PER_KV = N_Q_HEADS // N_KV_HEADS
HEAD_DIM = 128
DIFF_DIM = 64
D_FF = 2816
BLOCK_Q = 128
WINDOW = 128
GRID_W = 64
N_BUCKETS = 32
MAX_DISTANCE = 128
ROPE_THETA = 10000.0
EPS = 1e-6
NEG = -1e30
LOG2E = 1.4426950408889634

Q_W = N_Q_HEADS * HEAD_DIM
KV_W = N_KV_HEADS * HEAD_DIM
BRANCH_IN = Q_W + 2 * KV_W
QKV_W = 3 * BRANCH_IN
MXU_TILE = 256
FF_CHUNK = MXU_TILE
N_FF_CHUNKS = D_FF // FF_CHUNK
ADA_ROWS = 16
ADA_TN = 1536
NEAR = 3 * BLOCK_Q
VT_ROWS = HEAD_DIM + 16

VMEM_LIMIT = 56 * 1024 * 1024


def _tiles(seq):
    tks = min(MXU_TILE, seq)
    return dict(
        tm=min(512, seq),
        tm_out=min(1024, seq),
        tq_a=2 * BLOCK_Q,
        nqb_c=1,
        tks=tks,
        qb=min(16, seq // BLOCK_Q),
        unroll=min(16, seq // tks),
    )


def _cparams(*sem):
    return pltpu.CompilerParams(dimension_semantics=sem, vmem_limit_bytes=VMEM_LIMIT)


def _dot(a, b):
    return jnp.dot(a, b, preferred_element_type=F32)


def _dot_nt(a, b):
    return lax.dot_general(a, b, (((1,), (1,)), ((), ())), preferred_element_type=F32)


def _ada_kernel(c_ref, w_ref, b_ref, o_ref):
    c = c_ref[...]
    a = (c * jax.nn.sigmoid(c)).astype(BF16)
    o_ref[0] = _dot(a, w_ref[0].astype(BF16)) + b_ref[0]


def _ada(c_all, w_ada, b_ada):
    n_out = w_ada.shape[-1]
    return pl.pallas_call(
        _ada_kernel,
        out_shape=jax.ShapeDtypeStruct((DEPTH, ADA_ROWS, n_out), F32),
        grid=(DEPTH, n_out // ADA_TN),
        in_specs=[
            pl.BlockSpec((ADA_ROWS, D_MODEL), lambda l, j: (0, 0)),
            pl.BlockSpec((1, D_MODEL, ADA_TN), lambda l, j: (l, 0, j)),
            pl.BlockSpec((1, 1, ADA_TN), lambda l, j: (l, 0, j)),
        ],
        out_specs=pl.BlockSpec((1, ADA_ROWS, ADA_TN), lambda l, j: (l, 0, j)),
        compiler_params=_cparams("arbitrary", "arbitrary"),
        name="ada",
    )(c_all, w_ada, b_ada.reshape(DEPTH, 1, n_out))


def _modulate(x, mod_ref, g, jj):
    ms = jnp.mean(x * x, axis=-1, keepdims=True)
    y = x * lax.rsqrt(ms + EPS) * g
    return y * (1.0 + mod_ref[0, 3 * jj + 1:3 * jj + 2, :]) + mod_ref[0, 3 * jj:3 * jj + 1, :]


def _ffn_kernel(x_ref, mod_ref, g_ref, wgu_ref, wout_ref, o_ref, nb_ref, acc_ref, *, jj):
    x = x_ref[...]
    nb_ref[...] = _modulate(x, mod_ref, g_ref[jj:jj + 1, :], jj).astype(BF16)
    acc_ref[...] = jnp.zeros_like(acc_ref)

    def body(c, carry):
        h = _dot(nb_ref[...], wgu_ref[c])
        hg = h[:, :FF_CHUNK]
        a = (hg * jax.nn.sigmoid(hg)) * h[:, FF_CHUNK:]
        acc_ref[...] += _dot(a.astype(BF16), wout_ref[c])
        return carry

    lax.fori_loop(0, N_FF_CHUNKS, body, 0, unroll=True)
    o_ref[...] = x + (0.5 * mod_ref[0, 3 * jj + 2:3 * jj + 3, :]) * acc_ref[...]


def _ffn(x, mod, g_norm, wgu, wout, *, layer, which, seq, tm):
    n = x.shape[0]
    jj = 2 * which
    const = dict(pipeline_mode=pl.Buffered(1))
    return pl.pallas_call(
        functools.partial(_ffn_kernel, jj=jj),
        out_shape=jax.ShapeDtypeStruct((n, D_MODEL), F32),
        grid=(n // tm,),
        in_specs=[
            pl.BlockSpec((tm, D_MODEL), lambda i: (i, 0)),
            pl.BlockSpec((None, 1, 9, D_MODEL), lambda i: (layer, (i * tm) // seq, 0, 0)),
            pl.BlockSpec((None, 3, D_MODEL), lambda i: (layer, 0, 0)),
            pl.BlockSpec((None, None, N_FF_CHUNKS, D_MODEL, 2 * FF_CHUNK),
                         lambda i: (layer, which, 0, 0, 0), **const),
            pl.BlockSpec((None, None, N_FF_CHUNKS, FF_CHUNK, D_MODEL),
                         lambda i: (layer, which, 0, 0, 0), **const),
        ],
        out_specs=pl.BlockSpec((tm, D_MODEL), lambda i: (i, 0)),
        scratch_shapes=[pltpu.VMEM((tm, D_MODEL), BF16), pltpu.VMEM((tm, D_MODEL), F32)],
        input_output_aliases={} if (layer == 0 and which == 0) else {0: 0},
        compiler_params=_cparams("arbitrary"),
        name=f"ffn{which}",
    )(x, mod, g_norm, wgu, wout)


def _head_norm(r, g):
    ms = jnp.mean(r * r, axis=-1, keepdims=True)
    return r * lax.rsqrt(ms + EPS) * g


def _half_norm(r, g2):
    sq = r * r
    lo = lax.broadcasted_iota(jnp.int32, r.shape, 1) < DIFF_DIM
    s_lo = jnp.sum(jnp.where(lo, sq, 0.0), axis=-1, keepdims=True)
    s_hi = jnp.sum(jnp.where(lo, 0.0, sq), axis=-1, keepdims=True)
    ms = jnp.where(lo, s_lo, s_hi) * (1.0 / DIFF_DIM)
    return r * lax.rsqrt(ms + EPS) * g2


def _qkv_kernel(x_ref, mod_ref, g_ref, w_ref, cos_ref, sa_ref, sb_ref, gh_ref,
                qa_ref, ka_ref, va_ref, qb_ref, kb_ref, vb_ref, qc_ref, kc_ref, vc_ref, nb_ref):
    nb_ref[...] = _modulate(x_ref[...], mod_ref, g_ref[1:2, :], 1).astype(BF16)
    cos, sa, sb = cos_ref[...], sa_ref[...], sb_ref[...]

    def rope(r):
        return (r * cos + pltpu.roll(r, HEAD_DIM - 32, 1) * sa + pltpu.roll(r, 32, 1) * sb)

    def proj(col, width):
        return _dot(nb_ref[...], w_ref[:, col:col + width])

    def heads(r, n_heads, fn, out_ref):
        for h in range(n_heads):
            out_ref[h] = fn(r[:, h * HEAD_DIM:(h + 1) * HEAD_DIM]).astype(BF16)

    def values_transposed(r, out_ref):
        for h in range(N_KV_HEADS):
            out_ref[h, :HEAD_DIM, :] = r[:, h * HEAD_DIM:(h + 1) * HEAD_DIM].T.astype(BF16)
            out_ref[h, HEAD_DIM:, :] = jnp.ones((VT_ROWS - HEAD_DIM, r.shape[0]), BF16)

    qa_scale = HEAD_DIM ** -0.5 * LOG2E
    qc_scale = DIFF_DIM ** -0.5 * LOG2E
    g_qa, g_ka, g_qb, g_kb = (gh_ref[i:i + 1, :] for i in range(4))
    g_qc, g_kc = gh_ref[4:5, :], gh_ref[5:6, :]
    col = 0
    heads(proj(col, Q_W), N_Q_HEADS, lambda r: rope(_head_norm(r, g_qa)) * qa_scale, qa_ref)
    col += Q_W
    heads(proj(col, KV_W), N_KV_HEADS, lambda r: rope(_head_norm(r, g_ka)), ka_ref)
    col += KV_W
    values_transposed(proj(col, KV_W),va_ref)
    col += KV_W
    heads(proj(col, Q_W), N_Q_HEADS, lambda r: _head_norm(r, g_qb) * qa_scale, qb_ref)
    col += Q_W
    heads(proj(col, KV_W), N_KV_HEADS, lambda r: _head_norm(r, g_kb), kb_ref)
    col += KV_W
    values_transposed(proj(col, KV_W), vb_ref)
    col += KV_W
    heads(proj(col, Q_W), N_Q_HEADS, lambda r: _half_norm(r, g_qc) * qc_scale, qc_ref)
    col += Q_W
    heads(proj(col, KV_W), N_KV_HEADS, lambda r: _half_norm(r, g_kc), kc_ref)
    col += KV_W
    values_transposed(proj(col, KV_W),vc_ref)


def _qkv(x, mod, g_norm, wqkv, rope_tabs, gh, *, layer, seq, tm):
    n = x.shape[0]
    tiles_per_seq = seq // tm
    q_shape = jax.ShapeDtypeStruct((N_Q_HEADS, n, HEAD_DIM), BF16)
    kv_shape = jax.ShapeDtypeStruct((N_KV_HEADS, n, HEAD_DIM), BF16)
    q_spec = pl.BlockSpec((N_Q_HEADS, tm, HEAD_DIM), lambda i: (0, i, 0))
    kv_spec = pl.BlockSpec((N_KV_HEADS, tm, HEAD_DIM), lambda i: (0, i, 0))
    tab_spec = pl.BlockSpec((tm, HEAD_DIM), lambda i: (i % tiles_per_seq, 0))
    v1_shape = jax.ShapeDtypeStruct((N_KV_HEADS, VT_ROWS, n), BF16)
    v1_spec = pl.BlockSpec((N_KV_HEADS, VT_ROWS, tm), lambda i: (0, 0, i))
    return pl.pallas_call(
        _qkv_kernel,
        out_shape=[q_shape, kv_shape, v1_shape] * 3,
        grid=(n // tm,),
        in_specs=[
            pl.BlockSpec((tm, D_MODEL), lambda i: (i, 0)),
            pl.BlockSpec((None, 1, 9, D_MODEL), lambda i: (layer, (i * tm) // seq, 0, 0)),
            pl.BlockSpec((None, 3, D_MODEL), lambda i: (layer, 0, 0)),
            pl.BlockSpec((None, D_MODEL, QKV_W), lambda i: (layer, 0, 0), pipeline_mode=pl.Buffered(1)),
            tab_spec, tab_spec, tab_spec,
            pl.BlockSpec((None, 8, HEAD_DIM), lambda i: (layer, 0, 0)),
        ],
        out_specs=[q_spec, kv_spec, v1_spec] * 3,
        scratch_shapes=[pltpu.VMEM((tm, D_MODEL), BF16)],
        compiler_params=_cparams("arbitrary"),
        name="qkv",
    )(x, mod, g_norm, wqkv, *rope_tabs, gh)


def _out_kernel(x_ref, mod_ref, g_ref, oa_ref, ob_ref, oc_ref, wg_ref, wo_ref, o_ref, mg_ref):
    x = x_ref[...]
    nb = _modulate(x, mod_ref, g_ref[1:2, :], 1).astype(BF16)
    for br, br_ref in enumerate((oa_ref, ob_ref, oc_ref)):
        gate = jax.nn.sigmoid(_dot(nb, wg_ref[:, br * D_MODEL:(br + 1) * D_MODEL]))
        for h in range(N_Q_HEADS):
            lanes = slice(h * HEAD_DIM, (h + 1) * HEAD_DIM)
            term = gate[:, lanes] * br_ref[h].astype(F32)
            if br == 0:
                mg_ref[:, lanes] = term
            else:
                mg_ref[:, lanes] += term
    o_ref[...] = x + mod_ref[0, 5:6, :] * _dot(mg_ref[...].astype(BF16), wo_ref[...])


def _out_proj(x, mod, g_norm, oa, ob, oc, wgate, wo, *, layer, seq, tm):
    n = x.shape[0]
    o_spec = pl.BlockSpec((N_Q_HEADS, tm, HEAD_DIM), lambda i: (0, i, 0))
    return pl.pallas_call(
        _out_kernel,
        out_shape=jax.ShapeDtypeStruct((n, D_MODEL), F32),
        grid=(n // tm,),
        in_specs=[
            pl.BlockSpec((tm, D_MODEL), lambda i: (i, 0)),
            pl.BlockSpec((None, 1, 9, D_MODEL), lambda i: (layer, (i * tm) // seq, 0, 0)),
            pl.BlockSpec((None, 3, D_MODEL), lambda i: (layer, 0, 0)),
            o_spec, o_spec, o_spec,
            pl.BlockSpec((None, D_MODEL, 3 * D_MODEL), lambda i: (layer, 0, 0), pipeline_mode=pl.Buffered(1)),
            pl.BlockSpec((None, D_MODEL, D_MODEL), lambda i: (layer, 0, 0), pipeline_mode=pl.Buffered(1)),
        ],
        out_specs=pl.BlockSpec((tm, D_MODEL), lambda i: (i, 0)),
        scratch_shapes=[pltpu.VMEM((tm, D_MODEL), F32)],
        input_output_aliases={0: 0},
        compiler_params=_cparams("arbitrary"),
        name="out_proj",
    )(x, mod, g_norm, oa, ob, oc, wgate, wo)


MIN_DENOM = 2.0 ** -64


def _key_slice(idx, tks):
    return pl.ds(pl.multiple_of(idx * tks, tks), tks)


def _score_bound(q, key_norm):
    qf = q.astype(F32)
    qn2 = _dot_nt(jnp.ones((8, q.shape[1]), F32), qf * qf)[:1]
    return jnp.sqrt(qn2) * key_norm


def _bounded_iteration(t, last, q, k_ref, v_ref, acct_ref, s_ref, *, unroll, tks, shift, bias_fn=None):
    acct = acct_ref[...]
    s = s_ref[...]
    for u in range(unroll):
        idx = t * unroll + u
        s_next = None if (last and u == unroll - 1) else _dot_nt(k_ref[_key_slice(idx + 1, tks), :], q)
        if bias_fn is not None:
            s = bias_fn(idx, s)
        acct = acct + _dot(v_ref[:, _key_slice(idx, tks)], jnp.exp2(s - shift).astype(BF16))
        s = s_next
    acct_ref[...] = acct
    if not last:
        s_ref[...] = s


def _exact_pass(q, k_ref, v_ref, m_ref, acct_ref, *, n_sub, tks, bias_fn=None):
    m_ref[...] = jnp.full_like(m_ref, -jnp.inf)
    acct_ref[...] = jnp.zeros_like(acct_ref)

    def body(idx, carry):
        s = _dot_nt(k_ref[_key_slice(idx, tks), :], q)
        if bias_fn is not None:
            s = bias_fn(idx, s)
        m = m_ref[...]
        m_new = jnp.maximum(m, jnp.max(s, axis=0, keepdims=True))
        p = jnp.exp2(s - m_new).astype(BF16)
        acct_ref[...] = jnp.exp2(m - m_new) * acct_ref[...] + _dot(v_ref[:, _key_slice(idx, tks)], p)
        m_ref[...] = m_new
        return carry

    lax.fori_loop(0, n_sub, body, 0)


def _denominators_ok(acct_ref):
    return jnp.min(acct_ref[HEAD_DIM:HEAD_DIM + 1, :]) >= MIN_DENOM


def _attn_a_kernel(q_ref, k_ref, v_ref, kn_ref, o_ref, m_ref, acct_ref, s_ref, *, n_iter, unroll, tks):
    rows = acct_ref.shape[1]
    q = q_ref[...].reshape(rows, HEAD_DIM)
    shift = _score_bound(q, kn_ref[:, :1])
    acct_ref[...] = jnp.zeros_like(acct_ref)
    s_ref[...] = _dot_nt(k_ref[0:tks, :], q)
    refs = (q, k_ref, v_ref, acct_ref, s_ref)

    def body(t, carry):
        _bounded_iteration(t, False, *refs, unroll=unroll, tks=tks, shift=shift)
        return carry

    lax.fori_loop(0, n_iter - 1, body, 0)
    _bounded_iteration(n_iter - 1, True, *refs, unroll=unroll, tks=tks, shift=shift)

    @pl.when(jnp.logical_not(_denominators_ok(acct_ref)))
    def _():
        _exact_pass(q, k_ref, v_ref, m_ref, acct_ref, n_sub=n_iter * unroll, tks=tks)

    acct = acct_ref[...]
    o_t = acct[:HEAD_DIM] / acct[HEAD_DIM:HEAD_DIM + 1]
    o_ref[...] = o_t.T.astype(o_ref.dtype).reshape(o_ref.shape)


def _attn_a(q, k, v1t, key_norm, *, layer, batch, seq, tq, tks, unroll):
    n = q.shape[1]
    nq = seq // tq
    rows = Q_PER_KV * tq
    q_map = lambda b, g, i: (g, b * nq + i, 0)
    return pl.pallas_call(
        functools.partial(_attn_a_kernel, n_iter=seq // (tks * unroll), unroll=unroll, tks=tks),
        out_shape=jax.ShapeDtypeStruct((N_Q_HEADS, n, HEAD_DIM), ATTN_OUT),
        grid=(batch, N_KV_HEADS, nq),
        in_specs=[
            pl.BlockSpec((Q_PER_KV, tq, HEAD_DIM), q_map),
            pl.BlockSpec((None, seq, HEAD_DIM), lambda b, g, i: (g, b, 0)),
            pl.BlockSpec((None, VT_ROWS, seq), lambda b, g, i: (g, 0, b)),
            pl.BlockSpec((None, 1, HEAD_DIM), lambda b, g, i: (layer, 0, 0)),
        ],
        out_specs=pl.BlockSpec((Q_PER_KV, tq, HEAD_DIM), q_map),
        scratch_shapes=[pltpu.VMEM((1, rows), F32), pltpu.VMEM((VT_ROWS, rows), F32),
                        pltpu.VMEM((tks, rows), F32)],
        compiler_params=_cparams("arbitrary", "arbitrary", "arbitrary"),
        name="attn_axial",
    )(q, k, v1t, key_norm)


def _attn_b_kernel(q_ref, kp_ref, kc_ref, kn_ref, vp_ref, vc_ref, vn_ref, bias_ref, sink_ref, o_ref,
                   *, nb, qb):
    i = pl.program_id(2)
    rows = Q_PER_KV * BLOCK_Q
    kwin = jnp.concatenate([kp_ref[...], kc_ref[...], kn_ref[...]], axis=0)
    vwin = jnp.concatenate([vp_ref[...], vc_ref[...], vn_ref[...]], axis=1)
    bias = bias_ref[...]
    sink = sink_ref[...]
    for u in range(qb):
        blk = i * qb + u
        q = q_ref[:, u * BLOCK_Q:(u + 1) * BLOCK_Q, :].reshape(rows, HEAD_DIM)
        s = _dot_nt(kwin[u * BLOCK_Q:u * BLOCK_Q + NEAR], q) + bias
        s = jnp.concatenate([jnp.where(blk > 0, s[:BLOCK_Q], NEG), s[BLOCK_Q:2 * BLOCK_Q],
                             jnp.where(blk < nb - 1, s[2 * BLOCK_Q:], NEG)], axis=0)
        m = jnp.maximum(jnp.max(s, axis=0, keepdims=True), sink)
        p = jnp.exp2(s - m).astype(BF16)
        acct = _dot(vwin[:, u * BLOCK_Q:u * BLOCK_Q + NEAR], p)
        den = acct[HEAD_DIM:HEAD_DIM + 1] + jnp.exp2(sink - m)
        o_t = acct[:HEAD_DIM] / den
        o_ref[:, u * BLOCK_Q:(u + 1) * BLOCK_Q, :] = o_t.T.astype(o_ref.dtype).reshape(
            Q_PER_KV, BLOCK_Q, HEAD_DIM)


def _attn_b(q, k, v1t, bias_t, sink_t, *, layer, batch, seq, qb):
    n = q.shape[1]
    nb = seq // BLOCK_Q
    steps = nb // qb
    rows = Q_PER_KV * BLOCK_Q
    cur_map = lambda b, g, i: (g, b * steps + i, 0)
    prev_map = lambda b, g, i: (g, b * nb + jnp.maximum(i * qb - 1, 0), 0)
    next_map = lambda b, g, i: (g, b * nb + jnp.minimum(i * qb + qb, nb - 1), 0)
    t = lambda index_map: (lambda b, g, i: (index_map(b, g, i)[0], 0, index_map(b, g, i)[1]))
    k_specs = [pl.BlockSpec((None, BLOCK_Q, HEAD_DIM), prev_map),
               pl.BlockSpec((None, qb * BLOCK_Q, HEAD_DIM), cur_map),
               pl.BlockSpec((None, BLOCK_Q, HEAD_DIM), next_map)]
    v_specs = [pl.BlockSpec((None, VT_ROWS, BLOCK_Q), t(prev_map)),
               pl.BlockSpec((None, VT_ROWS, qb * BLOCK_Q), t(cur_map)),
               pl.BlockSpec((None, VT_ROWS, BLOCK_Q), t(next_map))]
    return pl.pallas_call(
        functools.partial(_attn_b_kernel, nb=nb, qb=qb),
        out_shape=jax.ShapeDtypeStruct((N_Q_HEADS, n, HEAD_DIM), ATTN_OUT),
        grid=(batch, N_KV_HEADS, steps),
        in_specs=[pl.BlockSpec((Q_PER_KV, qb * BLOCK_Q, HEAD_DIM), cur_map)] + k_specs + v_specs + [
            pl.BlockSpec((None, NEAR, rows), lambda b, g, i: (g, 0, 0)),
            pl.BlockSpec((None, None, 1, rows), lambda b, g, i: (layer, g, 0, 0)),
        ],
        out_specs=pl.BlockSpec((Q_PER_KV, qb * BLOCK_Q, HEAD_DIM), cur_map),
        compiler_params=_cparams("arbitrary", "arbitrary", "arbitrary"),
        name="attn_window",
    )(q, k, k, k, v1t, v1t, v1t, bias_t, sink_t)


def _attn_c_kernel(q_ref, k_ref, v_ref, bias_ref, cfar_ref, kn_ref, lam_ref, gs_ref, o_ref,
                   m_ref, acct_ref, s_ref, *, n_iter, unroll, tks, nblk, nqb, out_scale):
    first_blk = pl.program_id(2) * nqb
    rows = Q_PER_KV * nqb * BLOCK_Q
    blocks_per_sub = tks // BLOCK_Q
    blocks_per_iter = unroll * blocks_per_sub

    q = q_ref[...].reshape(rows, HEAD_DIM)
    lo = lax.broadcasted_iota(jnp.int32, q.shape, 1) < DIFF_DIM
    zero = jnp.zeros_like(q)
    q2 = jnp.concatenate([jnp.where(lo, q, zero), jnp.where(lo, zero, q)], axis=0)
    acct_ref[...] = jnp.zeros_like(acct_ref)
    s_ref[...] = _dot_nt(k_ref[0:tks, :], q2)
    refs = (q2, k_ref, v_ref, acct_ref, s_ref)

    c_left, c_right, c_max = cfar_ref[0], cfar_ref[1], cfar_ref[2]
    t_lo = jnp.maximum(first_blk - 1, 0) // blocks_per_iter
    t_hi = jnp.minimum(first_blk + nqb, nblk - 1) // blocks_per_iter
    two = lambda x: jnp.concatenate([x, x], axis=1)
    shift = _score_bound(q2, kn_ref[:, :1]) + two(c_max)

    def add_bias(idx, s):
        blks = []
        for kb in range(blocks_per_sub):
            key_blk = idx * blocks_per_sub + kb
            tiles = [bias_ref[jnp.clip(key_blk - (first_blk + a) + 2, 0, NEAR // BLOCK_Q + 1)]
                     for a in range(nqb)]
            if nqb == 1:
                blks.append(tiles[0])
            else:
                blks.append(jnp.concatenate([tiles[a][:, h * BLOCK_Q:(h + 1) * BLOCK_Q]
                                             for h in range(Q_PER_KV) for a in range(nqb)], axis=1))
        bias = jnp.concatenate(blks, axis=0)
        return jnp.concatenate([s[:, :rows] + bias, s[:, rows:] + bias], axis=1)

    def iteration(t, last):
        is_near = (t >= t_lo) & (t <= t_hi)

        @pl.when(jnp.logical_not(is_near))
        def _():
            side = jnp.where(t < t_lo, c_left, c_right)
            _bounded_iteration(t, last, *refs, unroll=unroll, tks=tks, shift=shift - two(side))

        @pl.when(is_near)
        def _():
            _bounded_iteration(t, last, *refs, unroll=unroll, tks=tks, shift=shift, bias_fn=add_bias)

    def body(t, carry):
        iteration(t, False)
        return carry

    lax.fori_loop(0, n_iter - 1, body, 0)
    iteration(n_iter - 1, True)

    @pl.when(jnp.logical_not(_denominators_ok(acct_ref)))
    def _():
        _exact_pass(q2, k_ref, v_ref, m_ref, acct_ref, n_sub=n_iter * unroll, tks=tks, bias_fn=add_bias)

    acct = acct_ref[...]
    o1_t = acct[:HEAD_DIM, :rows] / acct[HEAD_DIM:HEAD_DIM + 1, :rows]
    o2_t = acct[:HEAD_DIM, rows:] / acct[HEAD_DIM:HEAD_DIM + 1, rows:]
    o_t = o1_t - lam_ref[:, :1] * o2_t
    ms = jnp.mean(o_t * o_t, axis=0, keepdims=True)
    o = (o_t * lax.rsqrt(ms + EPS)).T * gs_ref[...] * out_scale
    o_ref[...] = o.astype(o_ref.dtype).reshape(o_ref.shape)


def _attn_c(q, k, v1t, bias_t, cfar, key_norm, lam, g_subln, *, layer, batch, seq, tks, unroll, nqb):
    n = q.shape[1]
    nq = seq // (nqb * BLOCK_Q)
    rows = Q_PER_KV * nqb * BLOCK_Q
    lam_init = 0.8 - 0.6 * math.exp(-0.3 * layer)
    q_map = lambda b, g, i: (g, b * nq + i, 0)
    cfar_t = jnp.repeat(cfar, nqb * BLOCK_Q, axis=1).reshape(3, N_KV_HEADS, 1, rows)
    return pl.pallas_call(
        functools.partial(_attn_c_kernel, n_iter=seq // (tks * unroll), unroll=unroll, tks=tks,
                          nblk=seq // BLOCK_Q, nqb=nqb, out_scale=1.0 - lam_init),
        out_shape=jax.ShapeDtypeStruct((N_Q_HEADS, n, HEAD_DIM), ATTN_OUT),
        grid=(batch, N_KV_HEADS, nq),
        in_specs=[
            pl.BlockSpec((Q_PER_KV, nqb * BLOCK_Q, HEAD_DIM), q_map),
            pl.BlockSpec((None, seq, HEAD_DIM), lambda b, g, i: (g, b, 0)),
            pl.BlockSpec((None, VT_ROWS, seq), lambda b, g, i: (g, 0, b)),
            pl.BlockSpec((None, NEAR // BLOCK_Q + 2, BLOCK_Q, Q_PER_KV * BLOCK_Q), lambda b, g, i: (g, 0, 0, 0)),
            pl.BlockSpec((3, None, 1, rows), lambda b, g, i: (0, g, 0, 0)),
            pl.BlockSpec((None, 1, HEAD_DIM), lambda b, g, i: (layer, 0, 0)),
            pl.BlockSpec((None, 1, HEAD_DIM), lambda b, g, i: (layer, 0, 0)),
            pl.BlockSpec((None, 1, HEAD_DIM), lambda b, g, i: (layer, 0, 0)),
        ],
        out_specs=pl.BlockSpec((Q_PER_KV, nqb * BLOCK_Q, HEAD_DIM), q_map),
        scratch_shapes=[pltpu.VMEM((1, 2 * rows), F32), pltpu.VMEM((VT_ROWS, 2 * rows), F32),
                        pltpu.VMEM((tks, 2 * rows), F32)],
        compiler_params=_cparams("arbitrary", "arbitrary", "arbitrary"),
        name="attn_diff",
    )(q, k, v1t, bias_t, cfar_t, key_norm, lam, g_subln)


def _t5_bucket_np(rel):
    half = N_BUCKETS // 2
    max_exact = half // 2
    ret = np.where(rel > 0, half, 0)
    n = np.abs(rel)
    ratio = np.log(np.maximum(n, 1).astype(np.float32) / np.float32(max_exact)) / np.float32(
        math.log(MAX_DISTANCE / max_exact))
    large = max_exact + (ratio * np.float32(half - max_exact)).astype(np.int32)
    large = np.minimum(large, half - 1)
    return (ret + np.where(n < max_exact, n, large)).astype(np.int32)


def _near_bias(table):
    r = np.arange(BLOCK_Q)[:, None]
    c = np.arange(NEAR)[None, :]
    bucket = _t5_bucket_np(c - BLOCK_Q - r)
    onehot = (bucket.reshape(-1, 1) == np.arange(N_BUCKETS)[None, :]).astype(np.float32)
    rows = jnp.dot(jnp.asarray(onehot), table.astype(F32), precision=lax.Precision.HIGHEST)
    return rows.reshape(BLOCK_Q, NEAR, table.shape[1]).transpose(2, 0, 1)


def _rope_tables(seq):
    rows = seq // GRID_W
    nfreq = HEAD_DIM // 4
    inv = ROPE_THETA ** (-jnp.arange(nfreq, dtype=F32) / nfreq)
    ang_r = jnp.arange(rows).astype(F32)[:, None] * inv
    ang_c = jnp.arange(GRID_W).astype(F32)[:, None] * inv
    by_row = lambda t: jnp.broadcast_to(t[:, None, :], (rows, GRID_W, nfreq)).reshape(seq, nfreq)
    by_col = lambda t: jnp.broadcast_to(t[None, :, :], (rows, GRID_W, nfreq)).reshape(seq, nfreq)
    table = lambda fn: jnp.concatenate([by_row(fn(ang_r))] * 2 + [by_col(fn(ang_c))] * 2, axis=-1)
    cos, sin = table(jnp.cos), table(jnp.sin)
    first = (np.arange(HEAD_DIM) % (HEAD_DIM // 2)) < HEAD_DIM // 4
    return cos, jnp.where(first, -sin, 0.0), jnp.where(first, 0.0, sin)


def _trunk(x, mod, p, *, batch, seq):
    t = _tiles(seq)
    tm = t["tm"]
    rope_tabs = _rope_tables(seq)
    for l in range(DEPTH):
        x = _ffn(x, mod, p["g_norm"], p["wgu"], p["wout"], layer=l, which=0, seq=seq, tm=tm)
        qa, ka, va, qb, kb, vb, qc, kc, vc = _qkv(x, mod, p["g_norm"], p["wqkv"], rope_tabs, p["gh"],
                                                  layer=l, seq=seq, tm=tm)
        oa = _attn_a(qa, ka, va, p["kn_a"], layer=l, batch=batch, seq=seq, tq=t["tq_a"], tks=t["tks"],
                     unroll=t["unroll"])
        ob = _attn_b(qb, kb, vb, p["bias_b"], p["sink"], layer=l, batch=batch, seq=seq, qb=t["qb"])
        oc = _attn_c(qc, kc, vc, p["bias_c"], p["cfar_c"], p["kn_c"], p["lam"], p["g_subln"],
                     layer=l, batch=batch, seq=seq, tks=t["tks"], unroll=t["unroll"], nqb=t["nqb_c"])
        x = _out_proj(x, mod, p["g_norm"], oa, ob, oc, p["wgate"], p["wo"], layer=l, seq=seq,
                      tm=t["tm_out"])
        x = _ffn(x, mod, p["g_norm"], p["wgu"], p["wout"], layer=l, which=1, seq=seq, tm=tm)
    return x


def _prepare(w_ff_in, w_ff_out, w_in, w_o, g_qa, g_ka, g_qb, g_kb, g_qc, g_kc, sink,
             lam_q1, lam_k1, lam_q2, lam_k2, g_subln, rel_bias):
    wg = w_ff_in[..., :D_FF].reshape(DEPTH, 2, D_MODEL, N_FF_CHUNKS, FF_CHUNK)
    wu = w_ff_in[..., D_FF:].reshape(DEPTH, 2, D_MODEL, N_FF_CHUNKS, FF_CHUNK)
    wgu = jnp.concatenate([wg, wu], axis=-1).transpose(0, 1, 3, 2, 4).astype(BF16)
    wout = w_ff_out.reshape(DEPTH, 2, N_FF_CHUNKS, FF_CHUNK, D_MODEL).astype(BF16)
    zeros = jnp.zeros_like(g_qa)
    gh = jnp.stack([g_qa, g_ka, g_qb, g_kb, jnp.tile(g_qc, (1, 2)), jnp.tile(g_kc, (1, 2)), zeros, zeros],
                   axis=1).astype(F32)
    table_b, table_c = rel_bias[:, :N_Q_HEADS], rel_bias[:, N_Q_HEADS:]
    half = N_BUCKETS // 2
    c_left, c_right = table_c[half - 1] * LOG2E, table_c[N_BUCKETS - 1] * LOG2E
    lam_init = jnp.asarray([0.8 - 0.6 * math.exp(-0.3 * l) for l in range(DEPTH)], F32)
    lam = (jnp.exp(jnp.sum(lam_q1.astype(F32) * lam_k1.astype(F32), axis=-1))
           - jnp.exp(jnp.sum(lam_q2.astype(F32) * lam_k2.astype(F32), axis=-1)) + lam_init)

    band = np.abs(np.arange(NEAR)[None, :] - BLOCK_Q - np.arange(BLOCK_Q)[:, None]) <= WINDOW
    near_c = (_near_bias(table_c) * LOG2E).reshape(
        N_KV_HEADS, Q_PER_KV, BLOCK_Q, NEAR // BLOCK_Q, BLOCK_Q).transpose(0, 3, 4, 1, 2).reshape(
        N_KV_HEADS, NEAR // BLOCK_Q, BLOCK_Q, Q_PER_KV * BLOCK_Q)
    cfar_heads = jnp.stack([c_left, c_right, jnp.max(table_c, axis=0) * LOG2E]).astype(F32)
    cfar_c = jnp.repeat(cfar_heads, BLOCK_Q, axis=1).reshape(3, N_KV_HEADS, 1, Q_PER_KV * BLOCK_Q)

    def key_norm(g, dim):
        bound = 1.01 * math.sqrt(dim) * jnp.max(jnp.abs(g.astype(F32)), axis=-1)
        return jnp.broadcast_to(bound[:, None, None], (DEPTH, 1, HEAD_DIM))

    return dict(
        wgu=wgu, wout=wout,
        wqkv=w_in[:, :, :QKV_W].astype(BF16), wgate=w_in[:, :, QKV_W:].astype(BF16), wo=w_o.astype(BF16),
        gh=gh,
        bias_b=jnp.where(band, _near_bias(table_b) * LOG2E, NEG).reshape(
            N_KV_HEADS, Q_PER_KV, BLOCK_Q, NEAR).transpose(0, 3, 1, 2).reshape(
            N_KV_HEADS, NEAR, Q_PER_KV * BLOCK_Q),
        sink=jnp.repeat(sink.astype(F32) * LOG2E, BLOCK_Q, axis=1).reshape(
            DEPTH, N_KV_HEADS, 1, Q_PER_KV * BLOCK_Q),
        bias_c=jnp.concatenate([cfar_c[0][:, None] + jnp.zeros((1, 1, BLOCK_Q, 1), F32), near_c,
                                cfar_c[1][:, None] + jnp.zeros((1, 1, BLOCK_Q, 1), F32)], axis=1),
        cfar_c=cfar_heads,
        kn_a=key_norm(g_ka, HEAD_DIM), kn_c=key_norm(g_kc, DIFF_DIM),
        lam=jnp.broadcast_to(lam[:, None, None], (DEPTH, 1, HEAD_DIM)),
        g_subln=g_subln.astype(F32)[:, None, :],
    )


def kernel(x_prompt, x_sample, c_prompt, c_sample, w_ada, b_ada, g_norm, w_ff_in, w_ff_out, w_in, w_o,
           g_qa, g_ka, g_qb, g_kb, g_qc, g_kc, sink, lam_q1, lam_k1, lam_q2, lam_k2, g_subln, rel_bias):
    p = _prepare(w_ff_in, w_ff_out, w_in, w_o, g_qa, g_ka, g_qb, g_kb, g_qc, g_kc, sink,
                 lam_q1, lam_k1, lam_q2, lam_k2, g_subln, rel_bias)
    p["g_norm"] = g_norm.astype(F32)
    outs = []
    n_cond = 0
    conds = [c_prompt, c_sample]
    c_all = jnp.concatenate(conds + [jnp.zeros((ADA_ROWS - sum(c.shape[0] for c in conds), D_MODEL), F32)])
    mod_all = _ada(c_all, w_ada, b_ada)
    for x, c in ((x_prompt, c_prompt), (x_sample, c_sample)):
        batch, seq, _ = x.shape
        mod = mod_all[:, n_cond:n_cond + batch].reshape(DEPTH, batch, 9, D_MODEL)
        n_cond += batch
        y = _trunk(x.reshape(batch * seq, D_MODEL), mod, p, batch=batch, seq=seq)
        outs.append(y.reshape(batch, seq, D_MODEL))
    return tuple(outs)
```

```python
import functools
import math

import numpy as np
import jax
import jax.numpy as jnp
from jax import lax
from jax.experimental import pallas as pl
from jax.experimental.pallas import tpu as pltpu

F32 = jnp.float32
BF16 = jnp.bfloat16
ATTN_OUT = jnp.bfloat16

D_MODEL = 1024
DEPTH = 4
N_Q_HEADS = 8
N_KV_HEADS = 2
Q_PER_KV = N_Q_HEADS // N_KV_HEADS
HEAD_DIM = 128
DIFF_DIM = 64
D_FF = 2816
BLOCK_Q = 128
WINDOW = 128
GRID_W = 64
N_BUCKETS = 32
MAX_DISTANCE = 128
ROPE_THETA = 10000.0
EPS = 1e-6
NEG = -1e30
LOG2E = 1.4426950408889634

Q_W = N_Q_HEADS * HEAD_DIM
KV_W = N_KV_HEADS * HEAD_DIM
BRANCH_IN = Q_W + 2 * KV_W
QKV_W = 3 * BRANCH_IN
MXU_TILE = 256
FF_CHUNK = MXU_TILE
N_FF_CHUNKS = D_FF // FF_CHUNK
ADA_ROWS = 16
ADA_TN = 1536
NEAR = 3 * BLOCK_Q
VT_ROWS = HEAD_DIM + 16

VMEM_LIMIT = 56 * 1024 * 1024


def _tiles(seq):
    tks = min(MXU_TILE, seq)
    return dict(
        tm=min(512, seq),
        tm_out=min(1024, seq),
        tq_a=2 * BLOCK_Q,
        nqb_c=1,
        tks=tks,
        unroll=min(16, seq // tks),
    )


def _cparams(*sem):
    return pltpu.CompilerParams(dimension_semantics=sem, vmem_limit_bytes=VMEM_LIMIT)


def _dot(a, b):
    return jnp.dot(a, b, preferred_element_type=F32)


def _dot_nt(a, b):
    return lax.dot_general(a, b, (((1,), (1,)), ((), ())), preferred_element_type=F32)


def _ada_kernel(c_ref, w_ref, b_ref, o_ref):
    c = c_ref[...]
    a = (c * jax.nn.sigmoid(c)).astype(BF16)
    o_ref[0] = _dot(a, w_ref[0].astype(BF16)) + b_ref[0]


def _ada(c_all, w_ada, b_ada):
    n_out = w_ada.shape[-1]
    return pl.pallas_call(
        _ada_kernel,
        out_shape=jax.ShapeDtypeStruct((DEPTH, ADA_ROWS, n_out), F32),
        grid=(DEPTH, n_out // ADA_TN),
        in_specs=[
            pl.BlockSpec((ADA_ROWS, D_MODEL), lambda l, j: (0, 0)),
            pl.BlockSpec((1, D_MODEL, ADA_TN), lambda l, j: (l, 0, j)),
            pl.BlockSpec((1, 1, ADA_TN), lambda l, j: (l, 0, j)),
        ],
        out_specs=pl.BlockSpec((1, ADA_ROWS, ADA_TN), lambda l, j: (l, 0, j)),
        compiler_params=_cparams("arbitrary", "arbitrary"),
        name="ada",
    )(c_all, w_ada, b_ada.reshape(DEPTH, 1, n_out))


def _modulate(x, mod_ref, g, jj):
    ms = jnp.mean(x * x, axis=-1, keepdims=True)
    y = x * lax.rsqrt(ms + EPS) * g
    return y * (1.0 + mod_ref[0, 3 * jj + 1:3 * jj + 2, :]) + mod_ref[0, 3 * jj:3 * jj + 1, :]


def _ffn_kernel(x_ref, mod_ref, g_ref, wgu_ref, wout_ref, o_ref, nb_ref, acc_ref, *, jj):
    x = x_ref[...]
    nb_ref[...] = _modulate(x, mod_ref, g_ref[jj:jj + 1, :], jj).astype(BF16)
    acc_ref[...] = jnp.zeros_like(acc_ref)

    def body(c, carry):
        h = _dot(nb_ref[...], wgu_ref[c])
        hg = h[:, :FF_CHUNK]
        a = (hg * jax.nn.sigmoid(hg)) * h[:, FF_CHUNK:]
        acc_ref[...] += _dot(a.astype(BF16), wout_ref[c])
        return carry

    lax.fori_loop(0, N_FF_CHUNKS, body, 0, unroll=True)
    o_ref[...] = x + (0.5 * mod_ref[0, 3 * jj + 2:3 * jj + 3, :]) * acc_ref[...]


def _ffn(x, mod, g_norm, wgu, wout, *, layer, which, seq, tm):
    n = x.shape[0]
    jj = 2 * which
    const = dict(pipeline_mode=pl.Buffered(1))
    return pl.pallas_call(
        functools.partial(_ffn_kernel, jj=jj),
        out_shape=jax.ShapeDtypeStruct((n, D_MODEL), F32),
        grid=(n // tm,),
        in_specs=[
            pl.BlockSpec((tm, D_MODEL), lambda i: (i, 0)),
            pl.BlockSpec((None, 1, 9, D_MODEL), lambda i: (layer, (i * tm) // seq, 0, 0)),
            pl.BlockSpec((None, 3, D_MODEL), lambda i: (layer, 0, 0)),
            pl.BlockSpec((None, None, N_FF_CHUNKS, D_MODEL, 2 * FF_CHUNK),
                         lambda i: (layer, which, 0, 0, 0), **const),
            pl.BlockSpec((None, None, N_FF_CHUNKS, FF_CHUNK, D_MODEL),
                         lambda i: (layer, which, 0, 0, 0), **const),
        ],
        out_specs=pl.BlockSpec((tm, D_MODEL), lambda i: (i, 0)),
        scratch_shapes=[pltpu.VMEM((tm, D_MODEL), BF16), pltpu.VMEM((tm, D_MODEL), F32)],
        input_output_aliases={} if (layer == 0 and which == 0) else {0: 0},
        compiler_params=_cparams("arbitrary"),
        name=f"ffn{which}",
    )(x, mod, g_norm, wgu, wout)


def _head_norm(r, g):
    ms = jnp.mean(r * r, axis=-1, keepdims=True)
    return r * lax.rsqrt(ms + EPS) * g


def _half_norm(r, g2):
    sq = r * r
    lo = lax.broadcasted_iota(jnp.int32, r.shape, 1) < DIFF_DIM
    s_lo = jnp.sum(jnp.where(lo, sq, 0.0), axis=-1, keepdims=True)
    s_hi = jnp.sum(jnp.where(lo, 0.0, sq), axis=-1, keepdims=True)
    ms = jnp.where(lo, s_lo, s_hi) * (1.0 / DIFF_DIM)
    return r * lax.rsqrt(ms + EPS) * g2


def _qkv_kernel(x_ref, mod_ref, g_ref, w_ref, cos_ref, sa_ref, sb_ref, gh_ref,
                qa_ref, ka_ref, va_ref, qb_ref, kb_ref, vb_ref, qc_ref, kc_ref, vc_ref, nb_ref):
    nb_ref[...] = _modulate(x_ref[...], mod_ref, g_ref[1:2, :], 1).astype(BF16)
    cos, sa, sb = cos_ref[...], sa_ref[...], sb_ref[...]

    def rope(r):
        return (r * cos + pltpu.roll(r, HEAD_DIM - 32, 1) * sa + pltpu.roll(r, 32, 1) * sb)

    def proj(col, width):
        return _dot(nb_ref[...], w_ref[:, col:col + width])

    def heads(r, n_heads, fn, out_ref):
        for h in range(n_heads):
            out_ref[h] = fn(r[:, h * HEAD_DIM:(h + 1) * HEAD_DIM]).astype(BF16)

    def values_transposed(r, out_ref):
        for h in range(N_KV_HEADS):
            out_ref[h, :HEAD_DIM, :] = r[:, h * HEAD_DIM:(h + 1) * HEAD_DIM].T.astype(BF16)
            out_ref[h, HEAD_DIM:, :] = jnp.ones((VT_ROWS - HEAD_DIM, r.shape[0]), BF16)

    qa_scale = HEAD_DIM ** -0.5 * LOG2E
    qc_scale = DIFF_DIM ** -0.5 * LOG2E
    g_qa, g_ka, g_qb, g_kb = (gh_ref[i:i + 1, :] for i in range(4))
    g_qc, g_kc = gh_ref[4:5, :], gh_ref[5:6, :]
    col = 0
    heads(proj(col, Q_W), N_Q_HEADS, lambda r: rope(_head_norm(r, g_qa)) * qa_scale, qa_ref)
    col += Q_W
    heads(proj(col, KV_W), N_KV_HEADS, lambda r: rope(_head_norm(r, g_ka)), ka_ref)
    col += KV_W
    values_transposed(proj(col, KV_W),va_ref)
    col += KV_W
    heads(proj(col, Q_W), N_Q_HEADS, lambda r: _head_norm(r, g_qb) * qa_scale, qb_ref)
    col += Q_W
    heads(proj(col, KV_W), N_KV_HEADS, lambda r: _head_norm(r, g_kb), kb_ref)
    col += KV_W
    values_transposed(proj(col, KV_W), vb_ref)
    col += KV_W
    heads(proj(col, Q_W), N_Q_HEADS, lambda r: _half_norm(r, g_qc) * qc_scale, qc_ref)
    col += Q_W
    heads(proj(col, KV_W), N_KV_HEADS, lambda r: _half_norm(r, g_kc), kc_ref)
    col += KV_W
    values_transposed(proj(col, KV_W),vc_ref)


def _qkv(x, mod, g_norm, wqkv, rope_tabs, gh, *, layer, seq, tm):
    n = x.shape[0]
    tiles_per_seq = seq // tm
    q_shape = jax.ShapeDtypeStruct((N_Q_HEADS, n, HEAD_DIM), BF16)
    kv_shape = jax.ShapeDtypeStruct((N_KV_HEADS, n, HEAD_DIM), BF16)
    q_spec = pl.BlockSpec((N_Q_HEADS, tm, HEAD_DIM), lambda i: (0, i, 0))
    kv_spec = pl.BlockSpec((N_KV_HEADS, tm, HEAD_DIM), lambda i: (0, i, 0))
    tab_spec = pl.BlockSpec((tm, HEAD_DIM), lambda i: (i % tiles_per_seq, 0))
    v1_shape = jax.ShapeDtypeStruct((N_KV_HEADS, VT_ROWS, n), BF16)
    v1_spec = pl.BlockSpec((N_KV_HEADS, VT_ROWS, tm), lambda i: (0, 0, i))
    return pl.pallas_call(
        _qkv_kernel,
        out_shape=[q_shape, kv_shape, v1_shape] * 3,
        grid=(n // tm,),
        in_specs=[
            pl.BlockSpec((tm, D_MODEL), lambda i: (i, 0)),
            pl.BlockSpec((None, 1, 9, D_MODEL), lambda i: (layer, (i * tm) // seq, 0, 0)),
            pl.BlockSpec((None, 3, D_MODEL), lambda i: (layer, 0, 0)),
            pl.BlockSpec((None, D_MODEL, QKV_W), lambda i: (layer, 0, 0), pipeline_mode=pl.Buffered(1)),
            tab_spec, tab_spec, tab_spec,
            pl.BlockSpec((None, 8, HEAD_DIM), lambda i: (layer, 0, 0)),
        ],
        out_specs=[q_spec, kv_spec, v1_spec] * 3,
        scratch_shapes=[pltpu.VMEM((tm, D_MODEL), BF16)],
        compiler_params=_cparams("arbitrary"),
        name="qkv",
    )(x, mod, g_norm, wqkv, *rope_tabs, gh)


def _out_kernel(x_ref, mod_ref, g_ref, oa_ref, ob_ref, oc_ref, wg_ref, wo_ref, o_ref, mg_ref):
    x = x_ref[...]
    nb = _modulate(x, mod_ref, g_ref[1:2, :], 1).astype(BF16)
    for br, br_ref in enumerate((oa_ref, ob_ref, oc_ref)):
        gate = jax.nn.sigmoid(_dot(nb, wg_ref[:, br * D_MODEL:(br + 1) * D_MODEL]))
        for h in range(N_Q_HEADS):
            lanes = slice(h * HEAD_DIM, (h + 1) * HEAD_DIM)
            term = gate[:, lanes] * br_ref[h].astype(F32)
            if br == 0:
                mg_ref[:, lanes] = term
            else:
                mg_ref[:, lanes] += term
    o_ref[...] = x + mod_ref[0, 5:6, :] * _dot(mg_ref[...].astype(BF16), wo_ref[...])


def _out_proj(x, mod, g_norm, oa, ob, oc, wgate, wo, *, layer, seq, tm):
    n = x.shape[0]
    o_spec = pl.BlockSpec((N_Q_HEADS, tm, HEAD_DIM), lambda i: (0, i, 0))
    return pl.pallas_call(
        _out_kernel,
        out_shape=jax.ShapeDtypeStruct((n, D_MODEL), F32),
        grid=(n // tm,),
        in_specs=[
            pl.BlockSpec((tm, D_MODEL), lambda i: (i, 0)),
            pl.BlockSpec((None, 1, 9, D_MODEL), lambda i: (layer, (i * tm) // seq, 0, 0)),
            pl.BlockSpec((None, 3, D_MODEL), lambda i: (layer, 0, 0)),
            o_spec, o_spec, o_spec,
            pl.BlockSpec((None, D_MODEL, 3 * D_MODEL), lambda i: (layer, 0, 0), pipeline_mode=pl.Buffered(1)),
            pl.BlockSpec((None, D_MODEL, D_MODEL), lambda i: (layer, 0, 0), pipeline_mode=pl.Buffered(1)),
        ],
        out_specs=pl.BlockSpec((tm, D_MODEL), lambda i: (i, 0)),
        scratch_shapes=[pltpu.VMEM((tm, D_MODEL), F32)],
        input_output_aliases={0: 0},
        compiler_params=_cparams("arbitrary"),
        name="out_proj",
    )(x, mod, g_norm, oa, ob, oc, wgate, wo)


MIN_DENOM = 2.0 ** -64


def _key_slice(idx, tks):
    return pl.ds(pl.multiple_of(idx * tks, tks), tks)


def _score_bound(q, key_norm):
    qf = q.astype(F32)
    qn2 = _dot_nt(jnp.ones((8, q.shape[1]), F32), qf * qf)[:1]
    return jnp.sqrt(qn2) * key_norm


def _bounded_iteration(t, last, q, k_ref, v_ref, acct_ref, s_ref, *, unroll, tks, shift, bias_fn=None,
                       side_work=None):
    if side_work is not None:
        side_work()
    acct = acct_ref[...]
    s = s_ref[...]
    for u in range(unroll):
        idx = t * unroll + u
        s_next = None if (last and u == unroll - 1) else _dot_nt(k_ref[_key_slice(idx + 1, tks), :], q)
        if bias_fn is not None:
            s = bias_fn(idx, s)
        acct = acct + _dot(v_ref[:, _key_slice(idx, tks)], jnp.exp2(s - shift).astype(BF16))
        s = s_next
    acct_ref[...] = acct
    if not last:
        s_ref[...] = s


def _exact_pass(q, k_ref, v_ref, m_ref, acct_ref, *, n_sub, tks, bias_fn=None):
    m_ref[...] = jnp.full_like(m_ref, -jnp.inf)
    acct_ref[...] = jnp.zeros_like(acct_ref)

    def body(idx, carry):
        s = _dot_nt(k_ref[_key_slice(idx, tks), :], q)
        if bias_fn is not None:
            s = bias_fn(idx, s)
        m = m_ref[...]
        m_new = jnp.maximum(m, jnp.max(s, axis=0, keepdims=True))
        p = jnp.exp2(s - m_new).astype(BF16)
        acct_ref[...] = jnp.exp2(m - m_new) * acct_ref[...] + _dot(v_ref[:, _key_slice(idx, tks)], p)
        m_ref[...] = m_new
        return carry

    lax.fori_loop(0, n_sub, body, 0)


def _denominators_ok(acct_ref):
    return jnp.min(acct_ref[HEAD_DIM:HEAD_DIM + 1, :]) >= MIN_DENOM


def _attn_a_kernel(q_ref, k_ref, v_ref, kn_ref, wq_ref, wkp_ref, wkc_ref, wkn_ref, wvp_ref, wvc_ref, wvn_ref,
                   wbias_ref, wsink_ref, o_ref, wo_ref, m_ref, acct_ref, s_ref, *, n_iter, unroll, tks, nb):
    rows = acct_ref.shape[1]

    def window():
        _window_blocks(pl.program_id(2), wq_ref, wkp_ref, wkc_ref, wkn_ref, wvp_ref, wvc_ref, wvn_ref,
                       wbias_ref, wsink_ref, wo_ref, nb=nb, qb=wq_ref.shape[1] // BLOCK_Q)

    q = q_ref[...].reshape(rows, HEAD_DIM)
    shift = _score_bound(q, kn_ref[:, :1])
    acct_ref[...] = jnp.zeros_like(acct_ref)
    s_ref[...] = _dot_nt(k_ref[0:tks, :], q)
    refs = (q, k_ref, v_ref, acct_ref, s_ref)

    def body(t, carry):
        _bounded_iteration(t, False, *refs, unroll=unroll, tks=tks, shift=shift)
        return carry

    lax.fori_loop(0, n_iter - 1, body, 0)
    _bounded_iteration(n_iter - 1, True, *refs, unroll=unroll, tks=tks, shift=shift, side_work=window)

    @pl.when(jnp.logical_not(_denominators_ok(acct_ref)))
    def _():
        _exact_pass(q, k_ref, v_ref, m_ref, acct_ref, n_sub=n_iter * unroll, tks=tks)

    acct = acct_ref[...]
    o_t = acct[:HEAD_DIM] / acct[HEAD_DIM:HEAD_DIM + 1]
    o_ref[...] = o_t.T.astype(o_ref.dtype).reshape(o_ref.shape)


def _attn_axial_window(q, k, v1t, key_norm, wq, wk, wv1t, wbias_t, wsink_t, *, layer, batch, seq, tq, tks, unroll):
    n = q.shape[1]
    nq = seq // tq
    rows = Q_PER_KV * tq
    q_map = lambda b, g, i: (g, b * nq + i, 0)
    out = jax.ShapeDtypeStruct((N_Q_HEADS, n, HEAD_DIM), ATTN_OUT)
    o_spec = pl.BlockSpec((Q_PER_KV, tq, HEAD_DIM), q_map)
    return pl.pallas_call(
        functools.partial(_attn_a_kernel, n_iter=seq // (tks * unroll), unroll=unroll, tks=tks,
                          nb=seq // BLOCK_Q),
        out_shape=[out, out],
        grid=(batch, N_KV_HEADS, nq),
        in_specs=[
            pl.BlockSpec((Q_PER_KV, tq, HEAD_DIM), q_map),
            pl.BlockSpec((None, seq, HEAD_DIM), lambda b, g, i: (g, b, 0)),
            pl.BlockSpec((None, VT_ROWS, seq), lambda b, g, i: (g, 0, b)),
            pl.BlockSpec((None, 1, HEAD_DIM), lambda b, g, i: (layer, 0, 0)),
        ] + _window_specs(layer=layer, seq=seq, qb=tq // BLOCK_Q),
        out_specs=[o_spec, o_spec],
        scratch_shapes=[pltpu.VMEM((1, rows), F32), pltpu.VMEM((VT_ROWS, rows), F32),
                        pltpu.VMEM((tks, rows), F32)],
        compiler_params=_cparams("arbitrary", "arbitrary", "arbitrary"),
        name="attn_axial_window",
    )(q, k, v1t, key_norm, wq, wk, wk, wk, wv1t, wv1t, wv1t, wbias_t, wsink_t)


def _window_blocks(i, q_ref, kp_ref, kc_ref, kn_ref, vp_ref, vc_ref, vn_ref, bias_ref, sink_ref, o_ref,
                   *, nb, qb):
    rows = Q_PER_KV * BLOCK_Q
    kwin = jnp.concatenate([kp_ref[...], kc_ref[...], kn_ref[...]], axis=0)
    vwin = jnp.concatenate([vp_ref[...], vc_ref[...], vn_ref[...]], axis=1)
    bias = bias_ref[...]
    sink = sink_ref[...]
    for u in range(qb):
        blk = i * qb + u
        q = q_ref[:, u * BLOCK_Q:(u + 1) * BLOCK_Q, :].reshape(rows, HEAD_DIM)
        s = _dot_nt(kwin[u * BLOCK_Q:u * BLOCK_Q + NEAR], q) + bias
        s = jnp.concatenate([jnp.where(blk > 0, s[:BLOCK_Q], NEG), s[BLOCK_Q:2 * BLOCK_Q],
                             jnp.where(blk < nb - 1, s[2 * BLOCK_Q:], NEG)], axis=0)
        m = jnp.maximum(jnp.max(s, axis=0, keepdims=True), sink)
        p = jnp.exp2(s - m).astype(BF16)
        acct = _dot(vwin[:, u * BLOCK_Q:u * BLOCK_Q + NEAR], p)
        den = acct[HEAD_DIM:HEAD_DIM + 1] + jnp.exp2(sink - m)
        o_t = acct[:HEAD_DIM] / den
        o_ref[:, u * BLOCK_Q:(u + 1) * BLOCK_Q, :] = o_t.T.astype(o_ref.dtype).reshape(
            Q_PER_KV, BLOCK_Q, HEAD_DIM)


def _window_specs(*, layer, seq, qb):
    nb = seq // BLOCK_Q
    steps = nb // qb
    rows = Q_PER_KV * BLOCK_Q
    cur_map = lambda b, g, i: (g, b * steps + i, 0)
    prev_map = lambda b, g, i: (g, b * nb + jnp.maximum(i * qb - 1, 0), 0)
    next_map = lambda b, g, i: (g, b * nb + jnp.minimum(i * qb + qb, nb - 1), 0)
    t = lambda index_map: (lambda b, g, i: (index_map(b, g, i)[0], 0, index_map(b, g, i)[1]))
    return [
        pl.BlockSpec((Q_PER_KV, qb * BLOCK_Q, HEAD_DIM), cur_map),
        pl.BlockSpec((None, BLOCK_Q, HEAD_DIM), prev_map),
        pl.BlockSpec((None, qb * BLOCK_Q, HEAD_DIM), cur_map),
        pl.BlockSpec((None, BLOCK_Q, HEAD_DIM), next_map),
        pl.BlockSpec((None, VT_ROWS, BLOCK_Q), t(prev_map)),
        pl.BlockSpec((None, VT_ROWS, qb * BLOCK_Q), t(cur_map)),
        pl.BlockSpec((None, VT_ROWS, BLOCK_Q), t(next_map)),
        pl.BlockSpec((None, NEAR, rows), lambda b, g, i: (g, 0, 0)),
        pl.BlockSpec((None, None, 1, rows), lambda b, g, i: (layer, g, 0, 0)),
    ]


def _attn_c_kernel(q_ref, k_ref, v_ref, bias_ref, cfar_ref, kn_ref, lam_ref, gs_ref, o_ref,
                   m_ref, acct_ref, s_ref, *, n_iter, unroll, tks, nblk, nqb, out_scale):
    first_blk = pl.program_id(2) * nqb
    rows = Q_PER_KV * nqb * BLOCK_Q
    blocks_per_sub = tks // BLOCK_Q
    blocks_per_iter = unroll * blocks_per_sub

    q = q_ref[...].reshape(rows, HEAD_DIM)
    lo = lax.broadcasted_iota(jnp.int32, q.shape, 1) < DIFF_DIM
    zero = jnp.zeros_like(q)
    q2 = jnp.concatenate([jnp.where(lo, q, zero), jnp.where(lo, zero, q)], axis=0)
    acct_ref[...] = jnp.zeros_like(acct_ref)
    s_ref[...] = _dot_nt(k_ref[0:tks, :], q2)
    refs = (q2, k_ref, v_ref, acct_ref, s_ref)

    c_left, c_right, c_max = cfar_ref[0], cfar_ref[1], cfar_ref[2]
    t_lo = jnp.maximum(first_blk - 1, 0) // blocks_per_iter
    t_hi = jnp.minimum(first_blk + nqb, nblk - 1) // blocks_per_iter
    two = lambda x: jnp.concatenate([x, x], axis=1)
    shift = _score_bound(q2, kn_ref[:, :1]) + two(c_max)

    def add_bias(idx, s):
        blks = []
        for kb in range(blocks_per_sub):
            key_blk = idx * blocks_per_sub + kb
            tiles = [bias_ref[jnp.clip(key_blk - (first_blk + a) + 2, 0, NEAR // BLOCK_Q + 1)]
                     for a in range(nqb)]
            if nqb == 1:
                blks.append(tiles[0])
            else:
                blks.append(jnp.concatenate([tiles[a][:, h * BLOCK_Q:(h + 1) * BLOCK_Q]
                                             for h in range(Q_PER_KV) for a in range(nqb)], axis=1))
        bias = jnp.concatenate(blks, axis=0)
        return jnp.concatenate([s[:, :rows] + bias, s[:, rows:] + bias], axis=1)

    def iteration(t, last):
        is_near = (t >= t_lo) & (t <= t_hi)

        @pl.when(jnp.logical_not(is_near))
        def _():
            side = jnp.where(t < t_lo, c_left, c_right)
            _bounded_iteration(t, last, *refs, unroll=unroll, tks=tks, shift=shift - two(side))

        @pl.when(is_near)
        def _():
            _bounded_iteration(t, last, *refs, unroll=unroll, tks=tks, shift=shift, bias_fn=add_bias)

    def body(t, carry):
        iteration(t, False)
        return carry

    lax.fori_loop(0, n_iter - 1, body, 0)
    iteration(n_iter - 1, True)

    @pl.when(jnp.logical_not(_denominators_ok(acct_ref)))
    def _():
        _exact_pass(q2, k_ref, v_ref, m_ref, acct_ref, n_sub=n_iter * unroll, tks=tks, bias_fn=add_bias)

    acct = acct_ref[...]
    o1_t = acct[:HEAD_DIM, :rows] / acct[HEAD_DIM:HEAD_DIM + 1, :rows]
    o2_t = acct[:HEAD_DIM, rows:] / acct[HEAD_DIM:HEAD_DIM + 1, rows:]
    o_t = o1_t - lam_ref[:, :1] * o2_t
    ms = jnp.mean(o_t * o_t, axis=0, keepdims=True)
    o = (o_t * lax.rsqrt(ms + EPS)).T * gs_ref[...] * out_scale
    o_ref[...] = o.astype(o_ref.dtype).reshape(o_ref.shape)


def _attn_c(q, k, v1t, bias_t, cfar, key_norm, lam, g_subln, *, layer, batch, seq, tks, unroll, nqb):
    n = q.shape[1]
    nq = seq // (nqb * BLOCK_Q)
    rows = Q_PER_KV * nqb * BLOCK_Q
    lam_init = 0.8 - 0.6 * math.exp(-0.3 * layer)
    q_map = lambda b, g, i: (g, b * nq + i, 0)
    cfar_t = jnp.repeat(cfar, nqb * BLOCK_Q, axis=1).reshape(3, N_KV_HEADS, 1, rows)
    return pl.pallas_call(
        functools.partial(_attn_c_kernel, n_iter=seq // (tks * unroll), unroll=unroll, tks=tks,
                          nblk=seq // BLOCK_Q, nqb=nqb, out_scale=1.0 - lam_init),
        out_shape=jax.ShapeDtypeStruct((N_Q_HEADS, n, HEAD_DIM), ATTN_OUT),
        grid=(batch, N_KV_HEADS, nq),
        in_specs=[
            pl.BlockSpec((Q_PER_KV, nqb * BLOCK_Q, HEAD_DIM), q_map),
            pl.BlockSpec((None, seq, HEAD_DIM), lambda b, g, i: (g, b, 0)),
            pl.BlockSpec((None, VT_ROWS, seq), lambda b, g, i: (g, 0, b)),
            pl.BlockSpec((None, NEAR // BLOCK_Q + 2, BLOCK_Q, Q_PER_KV * BLOCK_Q), lambda b, g, i: (g, 0, 0, 0)),
            pl.BlockSpec((3, None, 1, rows), lambda b, g, i: (0, g, 0, 0)),
            pl.BlockSpec((None, 1, HEAD_DIM), lambda b, g, i: (layer, 0, 0)),
            pl.BlockSpec((None, 1, HEAD_DIM), lambda b, g, i: (layer, 0, 0)),
            pl.BlockSpec((None, 1, HEAD_DIM), lambda b, g, i: (layer, 0, 0)),
        ],
        out_specs=pl.BlockSpec((Q_PER_KV, nqb * BLOCK_Q, HEAD_DIM), q_map),
        scratch_shapes=[pltpu.VMEM((1, 2 * rows), F32), pltpu.VMEM((VT_ROWS, 2 * rows), F32),
                        pltpu.VMEM((tks, 2 * rows), F32)],
        compiler_params=_cparams("arbitrary", "arbitrary", "arbitrary"),
        name="attn_diff",
    )(q, k, v1t, bias_t, cfar_t, key_norm, lam, g_subln)


def _t5_bucket_np(rel):
    half = N_BUCKETS // 2
    max_exact = half // 2
    ret = np.where(rel > 0, half, 0)
    n = np.abs(rel)
    ratio = np.log(np.maximum(n, 1).astype(np.float32) / np.float32(max_exact)) / np.float32(
        math.log(MAX_DISTANCE / max_exact))
    large = max_exact + (ratio * np.float32(half - max_exact)).astype(np.int32)
    large = np.minimum(large, half - 1)
    return (ret + np.where(n < max_exact, n, large)).astype(np.int32)


def _near_bias(table):
    r = np.arange(BLOCK_Q)[:, None]
    c = np.arange(NEAR)[None, :]
    bucket = _t5_bucket_np(c - BLOCK_Q - r)
    onehot = (bucket.reshape(-1, 1) == np.arange(N_BUCKETS)[None, :]).astype(np.float32)
    rows = jnp.dot(jnp.asarray(onehot), table.astype(F32), precision=lax.Precision.HIGHEST)
    return rows.reshape(BLOCK_Q, NEAR, table.shape[1]).transpose(2, 0, 1)


def _rope_tables(seq):
    rows = seq // GRID_W
    nfreq = HEAD_DIM // 4
    inv = ROPE_THETA ** (-jnp.arange(nfreq, dtype=F32) / nfreq)
    ang_r = jnp.arange(rows).astype(F32)[:, None] * inv
    ang_c = jnp.arange(GRID_W).astype(F32)[:, None] * inv
    by_row = lambda t: jnp.broadcast_to(t[:, None, :], (rows, GRID_W, nfreq)).reshape(seq, nfreq)
    by_col = lambda t: jnp.broadcast_to(t[None, :, :], (rows, GRID_W, nfreq)).reshape(seq, nfreq)
    table = lambda fn: jnp.concatenate([by_row(fn(ang_r))] * 2 + [by_col(fn(ang_c))] * 2, axis=-1)
    cos, sin = table(jnp.cos), table(jnp.sin)
    first = (np.arange(HEAD_DIM) % (HEAD_DIM // 2)) < HEAD_DIM // 4
    return cos, jnp.where(first, -sin, 0.0), jnp.where(first, 0.0, sin)


def _trunk(x, mod, p, *, batch, seq):
    t = _tiles(seq)
    tm = t["tm"]
    rope_tabs = _rope_tables(seq)
    for l in range(DEPTH):
        x = _ffn(x, mod, p["g_norm"], p["wgu"], p["wout"], layer=l, which=0, seq=seq, tm=tm)
        qa, ka, va, qb, kb, vb, qc, kc, vc = _qkv(x, mod, p["g_norm"], p["wqkv"], rope_tabs, p["gh"],
                                                  layer=l, seq=seq, tm=tm)
        oa, ob = _attn_axial_window(qa, ka, va, p["kn_a"], qb, kb, vb, p["bias_b"], p["sink"], layer=l,
                                    batch=batch, seq=seq, tq=t["tq_a"], tks=t["tks"], unroll=t["unroll"])
        oc = _attn_c(qc, kc, vc, p["bias_c"], p["cfar_c"], p["kn_c"], p["lam"], p["g_subln"],
                     layer=l, batch=batch, seq=seq, tks=t["tks"], unroll=t["unroll"], nqb=t["nqb_c"])
        x = _out_proj(x, mod, p["g_norm"], oa, ob, oc, p["wgate"], p["wo"], layer=l, seq=seq,
                      tm=t["tm_out"])
        x = _ffn(x, mod, p["g_norm"], p["wgu"], p["wout"], layer=l, which=1, seq=seq, tm=tm)
    return x


def _prepare(w_ff_in, w_ff_out, w_in, w_o, g_qa, g_ka, g_qb, g_kb, g_qc, g_kc, sink,
             lam_q1, lam_k1, lam_q2, lam_k2, g_subln, rel_bias):
    wg = w_ff_in[..., :D_FF].reshape(DEPTH, 2, D_MODEL, N_FF_CHUNKS, FF_CHUNK)
    wu = w_ff_in[..., D_FF:].reshape(DEPTH, 2, D_MODEL, N_FF_CHUNKS, FF_CHUNK)
    wgu = jnp.concatenate([wg, wu], axis=-1).transpose(0, 1, 3, 2, 4).astype(BF16)
    wout = w_ff_out.reshape(DEPTH, 2, N_FF_CHUNKS, FF_CHUNK, D_MODEL).astype(BF16)
    zeros = jnp.zeros_like(g_qa)
    gh = jnp.stack([g_qa, g_ka, g_qb, g_kb, jnp.tile(g_qc, (1, 2)), jnp.tile(g_kc, (1, 2)), zeros, zeros],
                   axis=1).astype(F32)
    table_b, table_c = rel_bias[:, :N_Q_HEADS], rel_bias[:, N_Q_HEADS:]
    half = N_BUCKETS // 2
    c_left, c_right = table_c[half - 1] * LOG2E, table_c[N_BUCKETS - 1] * LOG2E
    lam_init = jnp.asarray([0.8 - 0.6 * math.exp(-0.3 * l) for l in range(DEPTH)], F32)
    lam = (jnp.exp(jnp.sum(lam_q1.astype(F32) * lam_k1.astype(F32), axis=-1))
           - jnp.exp(jnp.sum(lam_q2.astype(F32) * lam_k2.astype(F32), axis=-1)) + lam_init)

    band = np.abs(np.arange(NEAR)[None, :] - BLOCK_Q - np.arange(BLOCK_Q)[:, None]) <= WINDOW
    near_c = (_near_bias(table_c) * LOG2E).reshape(
        N_KV_HEADS, Q_PER_KV, BLOCK_Q, NEAR // BLOCK_Q, BLOCK_Q).transpose(0, 3, 4, 1, 2).reshape(
        N_KV_HEADS, NEAR // BLOCK_Q, BLOCK_Q, Q_PER_KV * BLOCK_Q)
    cfar_heads = jnp.stack([c_left, c_right, jnp.max(table_c, axis=0) * LOG2E]).astype(F32)
    cfar_c = jnp.repeat(cfar_heads, BLOCK_Q, axis=1).reshape(3, N_KV_HEADS, 1, Q_PER_KV * BLOCK_Q)

    def key_norm(g, dim):
        bound = 1.01 * math.sqrt(dim) * jnp.max(jnp.abs(g.astype(F32)), axis=-1)
        return jnp.broadcast_to(bound[:, None, None], (DEPTH, 1, HEAD_DIM))

    return dict(
        wgu=wgu, wout=wout,
        wqkv=w_in[:, :, :QKV_W].astype(BF16), wgate=w_in[:, :, QKV_W:].astype(BF16), wo=w_o.astype(BF16),
        gh=gh,
        bias_b=jnp.where(band, _near_bias(table_b) * LOG2E, NEG).reshape(
            N_KV_HEADS, Q_PER_KV, BLOCK_Q, NEAR).transpose(0, 3, 1, 2).reshape(
            N_KV_HEADS, NEAR, Q_PER_KV * BLOCK_Q),
        sink=jnp.repeat(sink.astype(F32) * LOG2E, BLOCK_Q, axis=1).reshape(
            DEPTH, N_KV_HEADS, 1, Q_PER_KV * BLOCK_Q),
        bias_c=jnp.concatenate([cfar_c[0][:, None] + jnp.zeros((1, 1, BLOCK_Q, 1), F32), near_c,
                                cfar_c[1][:, None] + jnp.zeros((1, 1, BLOCK_Q, 1), F32)], axis=1),
        cfar_c=cfar_heads,
        kn_a=key_norm(g_ka, HEAD_DIM), kn_c=key_norm(g_kc, DIFF_DIM),
        lam=jnp.broadcast_to(lam[:, None, None], (DEPTH, 1, HEAD_DIM)),
        g_subln=g_subln.astype(F32)[:, None, :],
    )


def kernel(x_prompt, x_sample, c_prompt, c_sample, w_ada, b_ada, g_norm, w_ff_in, w_ff_out, w_in, w_o,
           g_qa, g_ka, g_qb, g_kb, g_qc, g_kc, sink, lam_q1, lam_k1, lam_q2, lam_k2, g_subln, rel_bias):
    p = _prepare(w_ff_in, w_ff_out, w_in, w_o, g_qa, g_ka, g_qb, g_kb, g_qc, g_kc, sink,
                 lam_q1, lam_k1, lam_q2, lam_k2, g_subln, rel_bias)
    p["g_norm"] = g_norm.astype(F32)
    outs = []
    n_cond = 0
    conds = [c_prompt, c_sample]
    c_all = jnp.concatenate(conds + [jnp.zeros((ADA_ROWS - sum(c.shape[0] for c in conds), D_MODEL), F32)])
    mod_all = _ada(c_all, w_ada, b_ada)
    for x, c in ((x_prompt, c_prompt), (x_sample, c_sample)):
        batch, seq, _ = x.shape
        mod = mod_all[:, n_cond:n_cond + batch].reshape(DEPTH, batch, 9, D_MODEL)
        n_cond += batch
        y = _trunk(x.reshape(batch * seq, D_MODEL), mod, p, batch=batch, seq=seq)
        outs.append(y.reshape(batch, seq, D_MODEL))
    return tuple(outs)
```

```python
import functools
import math

import numpy as np
import jax
import jax.numpy as jnp
from jax import lax
from jax.experimental import pallas as pl
from jax.experimental.pallas import tpu as pltpu

F32 = jnp.float32
BF16 = jnp.bfloat16
ATTN_OUT = jnp.bfloat16

D_MODEL = 1024
DEPTH = 4
N_Q_HEADS = 8
N_KV_HEADS = 2
Q_PER_KV = N_Q_HEADS // N_KV_HEADS
HEAD_DIM = 128
DIFF_DIM = 64
D_FF = 2816
BLOCK_Q = 128
WINDOW = 128
GRID_W = 64
N_BUCKETS = 32
MAX_DISTANCE = 128
ROPE_THETA = 10000.0
EPS = 1e-6
NEG = -1e30
LOG2E = 1.4426950408889634

Q_W = N_Q_HEADS * HEAD_DIM
KV_W = N_KV_HEADS * HEAD_DIM
BRANCH_IN = Q_W + 2 * KV_W
QKV_W = 3 * BRANCH_IN
MXU_TILE = 256
FF_CHUNK = MXU_TILE
N_FF_CHUNKS = D_FF // FF_CHUNK
ADA_ROWS = 16
ADA_TN = 1536
NEAR = 3 * BLOCK_Q
VT_ROWS = HEAD_DIM + 16

F32_SUBLANES = 8
ROW_TILE = 512
STEPS_PER_REGION = 16
BOUND_MARGIN = 1.01

VMEM_LIMIT = 56 * 1024 * 1024


def _tiles(seq):
    tks = min(MXU_TILE, seq)
    return dict(
        tm=min(ROW_TILE, seq),
        tm_out=min(2 * ROW_TILE, seq),
        tq_a=2 * BLOCK_Q,
        tks=tks,
        qb=min(STEPS_PER_REGION, seq // BLOCK_Q),
        unroll=min(STEPS_PER_REGION, seq // tks),
    )


def _cparams(*sem):
    return pltpu.CompilerParams(dimension_semantics=sem, vmem_limit_bytes=VMEM_LIMIT)


def _dot(a, b):
    return jnp.dot(a, b, preferred_element_type=F32)


def _dot_nt(a, b):
    return lax.dot_general(a, b, (((1,), (1,)), ((), ())), preferred_element_type=F32)


def _ada_kernel(c_ref, w_ref, b_ref, o_ref):
    c = c_ref[...]
    a = (c * jax.nn.sigmoid(c)).astype(BF16)
    o_ref[0] = _dot(a, w_ref[0].astype(BF16)) + b_ref[0]


def _ada(c_all, w_ada, b_ada):
    n_out = w_ada.shape[-1]
    return pl.pallas_call(
        _ada_kernel,
        out_shape=jax.ShapeDtypeStruct((DEPTH, ADA_ROWS, n_out), F32),
        grid=(DEPTH, n_out // ADA_TN),
        in_specs=[
            pl.BlockSpec((ADA_ROWS, D_MODEL), lambda l, j: (0, 0)),
            pl.BlockSpec((1, D_MODEL, ADA_TN), lambda l, j: (l, 0, j)),
            pl.BlockSpec((1, 1, ADA_TN), lambda l, j: (l, 0, j)),
        ],
        out_specs=pl.BlockSpec((1, ADA_ROWS, ADA_TN), lambda l, j: (l, 0, j)),
        compiler_params=_cparams("arbitrary", "arbitrary"),
        name="ada",
    )(c_all, w_ada, b_ada.reshape(DEPTH, 1, n_out))


def _modulate(x, mod_ref, g, jj):
    ms = jnp.mean(x * x, axis=-1, keepdims=True)
    y = x * lax.rsqrt(ms + EPS) * g
    return y * (1.0 + mod_ref[0, 3 * jj + 1:3 * jj + 2, :]) + mod_ref[0, 3 * jj:3 * jj + 1, :]


def _ffn_kernel(x_ref, mod_ref, g_ref, wgu_ref, wout_ref, o_ref, nb_ref, acc_ref, *, jj):
    x = x_ref[...]
    nb_ref[...] = _modulate(x, mod_ref, g_ref[jj:jj + 1, :], jj).astype(BF16)
    acc_ref[...] = jnp.zeros_like(acc_ref)

    def body(c, carry):
        h = _dot(nb_ref[...], wgu_ref[c])
        hg = h[:, :FF_CHUNK]
        a = (hg * jax.nn.sigmoid(hg)) * h[:, FF_CHUNK:]
        acc_ref[...] += _dot(a.astype(BF16), wout_ref[c])
        return carry

    lax.fori_loop(0, N_FF_CHUNKS, body, 0, unroll=True)
    o_ref[...] = x + (0.5 * mod_ref[0, 3 * jj + 2:3 * jj + 3, :]) * acc_ref[...]


def _ffn(x, mod, g_norm, wgu, wout, *, layer, which, seq, tm):
    n = x.shape[0]
    jj = 2 * which
    const = dict(pipeline_mode=pl.Buffered(1))
    return pl.pallas_call(
        functools.partial(_ffn_kernel, jj=jj),
        out_shape=jax.ShapeDtypeStruct((n, D_MODEL), F32),
        grid=(n // tm,),
        in_specs=[
            pl.BlockSpec((tm, D_MODEL), lambda i: (i, 0)),
            pl.BlockSpec((None, 1, 9, D_MODEL), lambda i: (layer, (i * tm) // seq, 0, 0)),
            pl.BlockSpec((None, 3, D_MODEL), lambda i: (layer, 0, 0)),
            pl.BlockSpec((None, None, N_FF_CHUNKS, D_MODEL, 2 * FF_CHUNK),
                         lambda i: (layer, which, 0, 0, 0), **const),
            pl.BlockSpec((None, None, N_FF_CHUNKS, FF_CHUNK, D_MODEL),
                         lambda i: (layer, which, 0, 0, 0), **const),
        ],
        out_specs=pl.BlockSpec((tm, D_MODEL), lambda i: (i, 0)),
        scratch_shapes=[pltpu.VMEM((tm, D_MODEL), BF16), pltpu.VMEM((tm, D_MODEL), F32)],
        input_output_aliases={} if (layer == 0 and which == 0) else {0: 0},
        compiler_params=_cparams("arbitrary"),
        name=f"ffn{which}",
    )(x, mod, g_norm, wgu, wout)


def _head_norm(r, g):
    ms = jnp.mean(r * r, axis=-1, keepdims=True)
    return r * lax.rsqrt(ms + EPS) * g


def _half_norm(r, g2):
    sq = r * r
    lo = lax.broadcasted_iota(jnp.int32, r.shape, 1) < DIFF_DIM
    s_lo = jnp.sum(jnp.where(lo, sq, 0.0), axis=-1, keepdims=True)
    s_hi = jnp.sum(jnp.where(lo, 0.0, sq), axis=-1, keepdims=True)
    ms = jnp.where(lo, s_lo, s_hi) * (1.0 / DIFF_DIM)
    return r * lax.rsqrt(ms + EPS) * g2


def _qkv_kernel(x_ref, mod_ref, g_ref, w_ref, cos_ref, sa_ref, sb_ref, gh_ref,
                qa_ref, ka_ref, va_ref, qb_ref, kb_ref, vb_ref, qc_ref, kc_ref, vc_ref, nb_ref):
    nb_ref[...] = _modulate(x_ref[...], mod_ref, g_ref[1:2, :], 1).astype(BF16)
    cos, sa, sb = cos_ref[...], sa_ref[...], sb_ref[...]

    def rope(r):
        return (r * cos + pltpu.roll(r, HEAD_DIM - 32, 1) * sa + pltpu.roll(r, 32, 1) * sb)

    def proj(col, width):
        return _dot(nb_ref[...], w_ref[:, col:col + width])

    def heads(r, n_heads, fn, out_ref):
        for h in range(n_heads):
            out_ref[h] = fn(r[:, h * HEAD_DIM:(h + 1) * HEAD_DIM]).astype(BF16)

    def values_transposed(r, out_ref):
        for h in range(N_KV_HEADS):
            out_ref[h, :HEAD_DIM, :] = r[:, h * HEAD_DIM:(h + 1) * HEAD_DIM].T.astype(BF16)
            out_ref[h, HEAD_DIM:, :] = jnp.ones((VT_ROWS - HEAD_DIM, r.shape[0]), BF16)

    qa_scale = HEAD_DIM ** -0.5 * LOG2E
    qc_scale = DIFF_DIM ** -0.5 * LOG2E
    g_qa, g_ka, g_qb, g_kb = (gh_ref[i:i + 1, :] for i in range(4))
    g_qc, g_kc = gh_ref[4:5, :], gh_ref[5:6, :]
    col = 0
    heads(proj(col, Q_W), N_Q_HEADS, lambda r: rope(_head_norm(r, g_qa)) * qa_scale, qa_ref)
    col += Q_W
    heads(proj(col, KV_W), N_KV_HEADS, lambda r: rope(_head_norm(r, g_ka)), ka_ref)
    col += KV_W
    values_transposed(proj(col, KV_W),va_ref)
    col += KV_W
    heads(proj(col, Q_W), N_Q_HEADS, lambda r: _head_norm(r, g_qb) * qa_scale, qb_ref)
    col += Q_W
    heads(proj(col, KV_W), N_KV_HEADS, lambda r: _head_norm(r, g_kb), kb_ref)
    col += KV_W
    values_transposed(proj(col, KV_W), vb_ref)
    col += KV_W
    heads(proj(col, Q_W), N_Q_HEADS, lambda r: _half_norm(r, g_qc) * qc_scale, qc_ref)
    col += Q_W
    heads(proj(col, KV_W), N_KV_HEADS, lambda r: _half_norm(r, g_kc), kc_ref)
    col += KV_W
    values_transposed(proj(col, KV_W),vc_ref)


def _qkv(x, mod, g_norm, wqkv, rope_tabs, gh, *, layer, seq, tm):
    n = x.shape[0]
    tiles_per_seq = seq // tm
    q_shape = jax.ShapeDtypeStruct((N_Q_HEADS, n, HEAD_DIM), BF16)
    kv_shape = jax.ShapeDtypeStruct((N_KV_HEADS, n, HEAD_DIM), BF16)
    q_spec = pl.BlockSpec((N_Q_HEADS, tm, HEAD_DIM), lambda i: (0, i, 0))
    kv_spec = pl.BlockSpec((N_KV_HEADS, tm, HEAD_DIM), lambda i: (0, i, 0))
    tab_spec = pl.BlockSpec((tm, HEAD_DIM), lambda i: (i % tiles_per_seq, 0))
    v1_shape = jax.ShapeDtypeStruct((N_KV_HEADS, VT_ROWS, n), BF16)
    v1_spec = pl.BlockSpec((N_KV_HEADS, VT_ROWS, tm), lambda i: (0, 0, i))
    return pl.pallas_call(
        _qkv_kernel,
        out_shape=[q_shape, kv_shape, v1_shape] * 3,
        grid=(n // tm,),
        in_specs=[
            pl.BlockSpec((tm, D_MODEL), lambda i: (i, 0)),
            pl.BlockSpec((None, 1, 9, D_MODEL), lambda i: (layer, (i * tm) // seq, 0, 0)),
            pl.BlockSpec((None, 3, D_MODEL), lambda i: (layer, 0, 0)),
            pl.BlockSpec((None, D_MODEL, QKV_W), lambda i: (layer, 0, 0), pipeline_mode=pl.Buffered(1)),
            tab_spec, tab_spec, tab_spec,
            pl.BlockSpec((None, 8, HEAD_DIM), lambda i: (layer, 0, 0)),
        ],
        out_specs=[q_spec, kv_spec, v1_spec] * 3,
        scratch_shapes=[pltpu.VMEM((tm, D_MODEL), BF16)],
        compiler_params=_cparams("arbitrary"),
        name="qkv",
    )(x, mod, g_norm, wqkv, *rope_tabs, gh)


def _out_kernel(x_ref, mod_ref, g_ref, oa_ref, ob_ref, oc_ref, wg_ref, wo_ref, o_ref, mg_ref):
    x = x_ref[...]
    nb = _modulate(x, mod_ref, g_ref[1:2, :], 1).astype(BF16)
    for br, br_ref in enumerate((oa_ref, ob_ref, oc_ref)):
        gate = jax.nn.sigmoid(_dot(nb, wg_ref[:, br * D_MODEL:(br + 1) * D_MODEL]))
        for h in range(N_Q_HEADS):
            lanes = slice(h * HEAD_DIM, (h + 1) * HEAD_DIM)
            term = gate[:, lanes] * br_ref[h].astype(F32)
            if br == 0:
                mg_ref[:, lanes] = term
            else:
                mg_ref[:, lanes] += term
    o_ref[...] = x + mod_ref[0, 5:6, :] * _dot(mg_ref[...].astype(BF16), wo_ref[...])


def _out_proj(x, mod, g_norm, oa, ob, oc, wgate, wo, *, layer, seq, tm):
    n = x.shape[0]
    o_spec = pl.BlockSpec((N_Q_HEADS, tm, HEAD_DIM), lambda i: (0, i, 0))
    return pl.pallas_call(
        _out_kernel,
        out_shape=jax.ShapeDtypeStruct((n, D_MODEL), F32),
        grid=(n // tm,),
        in_specs=[
            pl.BlockSpec((tm, D_MODEL), lambda i: (i, 0)),
            pl.BlockSpec((None, 1, 9, D_MODEL), lambda i: (layer, (i * tm) // seq, 0, 0)),
            pl.BlockSpec((None, 3, D_MODEL), lambda i: (layer, 0, 0)),
            o_spec, o_spec, o_spec,
            pl.BlockSpec((None, D_MODEL, 3 * D_MODEL), lambda i: (layer, 0, 0), pipeline_mode=pl.Buffered(1)),
            pl.BlockSpec((None, D_MODEL, D_MODEL), lambda i: (layer, 0, 0), pipeline_mode=pl.Buffered(1)),
        ],
        out_specs=pl.BlockSpec((tm, D_MODEL), lambda i: (i, 0)),
        scratch_shapes=[pltpu.VMEM((tm, D_MODEL), F32)],
        input_output_aliases={0: 0},
        compiler_params=_cparams("arbitrary"),
        name="out_proj",
    )(x, mod, g_norm, oa, ob, oc, wgate, wo)


MIN_DENOM = 2.0 ** -64


def _key_slice(idx, tks):
    return pl.ds(pl.multiple_of(idx * tks, tks), tks)


def _score_bound(q, key_norm):
    qf = q.astype(F32)
    qn2 = _dot_nt(jnp.ones((F32_SUBLANES, q.shape[1]), F32), qf * qf)[:1]
    return jnp.sqrt(qn2) * key_norm


def _bounded_iteration(t, last, q, k_ref, v_ref, acct_ref, s_ref, *, unroll, tks, shift, bias_fn=None):
    acct = acct_ref[...]
    s = s_ref[...]
    for u in range(unroll):
        idx = t * unroll + u
        s_next = None if (last and u == unroll - 1) else _dot_nt(k_ref[_key_slice(idx + 1, tks), :], q)
        if bias_fn is not None:
            s = bias_fn(idx, s)
        acct = acct + _dot(v_ref[:, _key_slice(idx, tks)], jnp.exp2(s - shift).astype(BF16))
        s = s_next
    acct_ref[...] = acct
    if not last:
        s_ref[...] = s


def _exact_pass(q, k_ref, v_ref, m_ref, acct_ref, *, n_sub, tks, bias_fn=None):
    m_ref[...] = jnp.full_like(m_ref, -jnp.inf)
    acct_ref[...] = jnp.zeros_like(acct_ref)

    def body(idx, carry):
        s = _dot_nt(k_ref[_key_slice(idx, tks), :], q)
        if bias_fn is not None:
            s = bias_fn(idx, s)
        m = m_ref[...]
        m_new = jnp.maximum(m, jnp.max(s, axis=0, keepdims=True))
        p = jnp.exp2(s - m_new).astype(BF16)
        acct_ref[...] = jnp.exp2(m - m_new) * acct_ref[...] + _dot(v_ref[:, _key_slice(idx, tks)], p)
        m_ref[...] = m_new
        return carry

    lax.fori_loop(0, n_sub, body, 0)


def _denominators_ok(acct_ref):
    return jnp.min(acct_ref[HEAD_DIM:HEAD_DIM + 1, :]) >= MIN_DENOM


def _attn_a_kernel(q_ref, k_ref, v_ref, kn_ref, o_ref, m_ref, acct_ref, s_ref, *, n_iter, unroll, tks):
    rows = acct_ref.shape[1]
    q = q_ref[...].reshape(rows, HEAD_DIM)
    shift = _score_bound(q, kn_ref[:, :1])
    acct_ref[...] = jnp.zeros_like(acct_ref)
    s_ref[...] = _dot_nt(k_ref[0:tks, :], q)
    refs = (q, k_ref, v_ref, acct_ref, s_ref)

    def body(t, carry):
        _bounded_iteration(t, False, *refs, unroll=unroll, tks=tks, shift=shift)
        return carry

    lax.fori_loop(0, n_iter - 1, body, 0)
    _bounded_iteration(n_iter - 1, True, *refs, unroll=unroll, tks=tks, shift=shift)

    @pl.when(jnp.logical_not(_denominators_ok(acct_ref)))
    def _():
        _exact_pass(q, k_ref, v_ref, m_ref, acct_ref, n_sub=n_iter * unroll, tks=tks)

    acct = acct_ref[...]
    o_t = acct[:HEAD_DIM] / acct[HEAD_DIM:HEAD_DIM + 1]
    o_ref[...] = o_t.T.astype(o_ref.dtype).reshape(o_ref.shape)


def _attn_a(q, k, v1t, key_norm, *, layer, batch, seq, tq, tks, unroll):
    n = q.shape[1]
    nq = seq // tq
    rows = Q_PER_KV * tq
    q_map = lambda b, g, i: (g, b * nq + i, 0)
    return pl.pallas_call(
        functools.partial(_attn_a_kernel, n_iter=seq // (tks * unroll), unroll=unroll, tks=tks),
        out_shape=jax.ShapeDtypeStruct((N_Q_HEADS, n, HEAD_DIM), ATTN_OUT),
        grid=(batch, N_KV_HEADS, nq),
        in_specs=[
            pl.BlockSpec((Q_PER_KV, tq, HEAD_DIM), q_map),
            pl.BlockSpec((None, seq, HEAD_DIM), lambda b, g, i: (g, b, 0)),
            pl.BlockSpec((None, VT_ROWS, seq), lambda b, g, i: (g, 0, b)),
            pl.BlockSpec((None, 1, HEAD_DIM), lambda b, g, i: (layer, 0, 0)),
        ],
        out_specs=pl.BlockSpec((Q_PER_KV, tq, HEAD_DIM), q_map),
        scratch_shapes=[pltpu.VMEM((1, rows), F32), pltpu.VMEM((VT_ROWS, rows), F32),
                        pltpu.VMEM((tks, rows), F32)],
        compiler_params=_cparams("arbitrary", "arbitrary", "arbitrary"),
        name="attn_axial",
    )(q, k, v1t, key_norm)


def _attn_b_kernel(q_ref, kp_ref, kc_ref, kn_ref, vp_ref, vc_ref, vn_ref, bias_ref, sink_ref, o_ref,
                   *, nb, qb):
    i = pl.program_id(2)
    rows = Q_PER_KV * BLOCK_Q
    kwin = jnp.concatenate([kp_ref[...], kc_ref[...], kn_ref[...]], axis=0)
    vwin = jnp.concatenate([vp_ref[...], vc_ref[...], vn_ref[...]], axis=1)
    bias = bias_ref[...]
    sink = sink_ref[...]
    for u in range(qb):
        blk = i * qb + u
        q = q_ref[:, u * BLOCK_Q:(u + 1) * BLOCK_Q, :].reshape(rows, HEAD_DIM)
        s = _dot_nt(kwin[u * BLOCK_Q:u * BLOCK_Q + NEAR], q) + bias
        s = jnp.concatenate([jnp.where(blk > 0, s[:BLOCK_Q], NEG), s[BLOCK_Q:2 * BLOCK_Q],
                             jnp.where(blk < nb - 1, s[2 * BLOCK_Q:], NEG)], axis=0)
        m = jnp.maximum(jnp.max(s, axis=0, keepdims=True), sink)
        p = jnp.exp2(s - m).astype(BF16)
        acct = _dot(vwin[:, u * BLOCK_Q:u * BLOCK_Q + NEAR], p)
        den = acct[HEAD_DIM:HEAD_DIM + 1] + jnp.exp2(sink - m)
        o_t = acct[:HEAD_DIM] / den
        o_ref[:, u * BLOCK_Q:(u + 1) * BLOCK_Q, :] = o_t.T.astype(o_ref.dtype).reshape(
            Q_PER_KV, BLOCK_Q, HEAD_DIM)


def _attn_b(q, k, v1t, bias_t, sink_t, *, layer, batch, seq, qb):
    n = q.shape[1]
    nb = seq // BLOCK_Q
    steps = nb // qb
    rows = Q_PER_KV * BLOCK_Q
    cur_map = lambda b, g, i: (g, b * steps + i, 0)
    prev_map = lambda b, g, i: (g, b * nb + jnp.maximum(i * qb - 1, 0), 0)
    next_map = lambda b, g, i: (g, b * nb + jnp.minimum(i * qb + qb, nb - 1), 0)
    t = lambda index_map: (lambda b, g, i: (index_map(b, g, i)[0], 0, index_map(b, g, i)[1]))
    k_specs = [pl.BlockSpec((None, BLOCK_Q, HEAD_DIM), prev_map),
               pl.BlockSpec((None, qb * BLOCK_Q, HEAD_DIM), cur_map),
               pl.BlockSpec((None, BLOCK_Q, HEAD_DIM), next_map)]
    v_specs = [pl.BlockSpec((None, VT_ROWS, BLOCK_Q), t(prev_map)),
               pl.BlockSpec((None, VT_ROWS, qb * BLOCK_Q), t(cur_map)),
               pl.BlockSpec((None, VT_ROWS, BLOCK_Q), t(next_map))]
    return pl.pallas_call(
        functools.partial(_attn_b_kernel, nb=nb, qb=qb),
        out_shape=jax.ShapeDtypeStruct((N_Q_HEADS, n, HEAD_DIM), ATTN_OUT),
        grid=(batch, N_KV_HEADS, steps),
        in_specs=[pl.BlockSpec((Q_PER_KV, qb * BLOCK_Q, HEAD_DIM), cur_map)] + k_specs + v_specs + [
            pl.BlockSpec((None, NEAR, rows), lambda b, g, i: (g, 0, 0)),
            pl.BlockSpec((None, None, 1, rows), lambda b, g, i: (layer, g, 0, 0)),
        ],
        out_specs=pl.BlockSpec((Q_PER_KV, qb * BLOCK_Q, HEAD_DIM), cur_map),
        compiler_params=_cparams("arbitrary", "arbitrary", "arbitrary"),
        name="attn_window",
    )(q, k, k, k, v1t, v1t, v1t, bias_t, sink_t)


def _attn_c_kernel(q_ref, k_ref, v_ref, bias_ref, cfar_ref, kn_ref, lam_ref, gs_ref, o_ref,
                   m_ref, acct_ref, s_ref, *, n_iter, unroll, tks, nblk, out_scale):
    n_blk = pl.program_id(2)
    rows = Q_PER_KV * BLOCK_Q
    blocks_per_sub = tks // BLOCK_Q
    blocks_per_iter = unroll * blocks_per_sub

    q = q_ref[...].reshape(rows, HEAD_DIM)
    lo = lax.broadcasted_iota(jnp.int32, q.shape, 1) < DIFF_DIM
    zero = jnp.zeros_like(q)
    q2 = jnp.concatenate([jnp.where(lo, q, zero), jnp.where(lo, zero, q)], axis=0)
    acct_ref[...] = jnp.zeros_like(acct_ref)
    s_ref[...] = _dot_nt(k_ref[0:tks, :], q2)
    refs = (q2, k_ref, v_ref, acct_ref, s_ref)

    c_left, c_right, c_max = cfar_ref[0], cfar_ref[1], cfar_ref[2]
    t_lo = jnp.maximum(n_blk - 1, 0) // blocks_per_iter
    t_hi = jnp.minimum(n_blk + 1, nblk - 1) // blocks_per_iter
    two = lambda x: jnp.concatenate([x, x], axis=1)
    shift = _score_bound(q2, kn_ref[:, :1]) + two(c_max)

    def add_bias(idx, s):
        tile_of = lambda key_blk: jnp.clip(key_blk - n_blk + 2, 0, NEAR // BLOCK_Q + 1)
        bias = jnp.concatenate([bias_ref[tile_of(idx * blocks_per_sub + kb)] for kb in range(blocks_per_sub)],
                               axis=0)
        return jnp.concatenate([s[:, :rows] + bias, s[:, rows:] + bias], axis=1)

    def iteration(t, last):
        is_near = (t >= t_lo) & (t <= t_hi)

        @pl.when(jnp.logical_not(is_near))
        def _():
            side = jnp.where(t < t_lo, c_left, c_right)
            _bounded_iteration(t, last, *refs, unroll=unroll, tks=tks, shift=shift - two(side))

        @pl.when(is_near)
        def _():
            _bounded_iteration(t, last, *refs, unroll=unroll, tks=tks, shift=shift, bias_fn=add_bias)

    def body(t, carry):
        iteration(t, False)
        return carry

    lax.fori_loop(0, n_iter - 1, body, 0)
    iteration(n_iter - 1, True)

    @pl.when(jnp.logical_not(_denominators_ok(acct_ref)))
    def _():
        _exact_pass(q2, k_ref, v_ref, m_ref, acct_ref, n_sub=n_iter * unroll, tks=tks, bias_fn=add_bias)

    acct = acct_ref[...]
    o1_t = acct[:HEAD_DIM, :rows] / acct[HEAD_DIM:HEAD_DIM + 1, :rows]
    o2_t = acct[:HEAD_DIM, rows:] / acct[HEAD_DIM:HEAD_DIM + 1, rows:]
    o_t = o1_t - lam_ref[:, :1] * o2_t
    ms = jnp.mean(o_t * o_t, axis=0, keepdims=True)
    o = (o_t * lax.rsqrt(ms + EPS)).T * gs_ref[...] * out_scale
    o_ref[...] = o.astype(o_ref.dtype).reshape(o_ref.shape)


def _attn_c(q, k, v1t, bias_t, cfar_t, key_norm, lam, g_subln, *, layer, batch, seq, tks, unroll):
    n = q.shape[1]
    nq = seq // BLOCK_Q
    rows = Q_PER_KV * BLOCK_Q
    lam_init = 0.8 - 0.6 * math.exp(-0.3 * layer)
    q_map = lambda b, g, i: (g, b * nq + i, 0)
    return pl.pallas_call(
        functools.partial(_attn_c_kernel, n_iter=seq // (tks * unroll), unroll=unroll, tks=tks,
                          nblk=nq, out_scale=1.0 - lam_init),
        out_shape=jax.ShapeDtypeStruct((N_Q_HEADS, n, HEAD_DIM), ATTN_OUT),
        grid=(batch, N_KV_HEADS, nq),
        in_specs=[
            pl.BlockSpec((Q_PER_KV, BLOCK_Q, HEAD_DIM), q_map),
            pl.BlockSpec((None, seq, HEAD_DIM), lambda b, g, i: (g, b, 0)),
            pl.BlockSpec((None, VT_ROWS, seq), lambda b, g, i: (g, 0, b)),
            pl.BlockSpec((None, NEAR // BLOCK_Q + 2, BLOCK_Q, rows), lambda b, g, i: (g, 0, 0, 0)),
            pl.BlockSpec((3, None, 1, rows), lambda b, g, i: (0, g, 0, 0)),
            pl.BlockSpec((None, 1, HEAD_DIM), lambda b, g, i: (layer, 0, 0)),
            pl.BlockSpec((None, 1, HEAD_DIM), lambda b, g, i: (layer, 0, 0)),
            pl.BlockSpec((None, 1, HEAD_DIM), lambda b, g, i: (layer, 0, 0)),
        ],
        out_specs=pl.BlockSpec((Q_PER_KV, BLOCK_Q, HEAD_DIM), q_map),
        scratch_shapes=[pltpu.VMEM((1, 2 * rows), F32), pltpu.VMEM((VT_ROWS, 2 * rows), F32),
                        pltpu.VMEM((tks, 2 * rows), F32)],
        compiler_params=_cparams("arbitrary", "arbitrary", "arbitrary"),
        name="attn_diff",
    )(q, k, v1t, bias_t, cfar_t, key_norm, lam, g_subln)


def _t5_bucket_np(rel):
    half = N_BUCKETS // 2
    max_exact = half // 2
    ret = np.where(rel > 0, half, 0)
    n = np.abs(rel)
    ratio = np.log(np.maximum(n, 1).astype(np.float32) / np.float32(max_exact)) / np.float32(
        math.log(MAX_DISTANCE / max_exact))
    large = max_exact + (ratio * np.float32(half - max_exact)).astype(np.int32)
    large = np.minimum(large, half - 1)
    return (ret + np.where(n < max_exact, n, large)).astype(np.int32)


def _near_bias(table):
    r = np.arange(BLOCK_Q)[:, None]
    c = np.arange(NEAR)[None, :]
    bucket = _t5_bucket_np(c - BLOCK_Q - r)
    onehot = (bucket.reshape(-1, 1) == np.arange(N_BUCKETS)[None, :]).astype(np.float32)
    rows = jnp.dot(jnp.asarray(onehot), table.astype(F32), precision=lax.Precision.HIGHEST)
    return rows.reshape(BLOCK_Q, NEAR, table.shape[1]).transpose(2, 0, 1)


def _rope_tables(seq):
    rows = seq // GRID_W
    nfreq = HEAD_DIM // 4
    inv = ROPE_THETA ** (-jnp.arange(nfreq, dtype=F32) / nfreq)
    ang_r = jnp.arange(rows).astype(F32)[:, None] * inv
    ang_c = jnp.arange(GRID_W).astype(F32)[:, None] * inv
    by_row = lambda t: jnp.broadcast_to(t[:, None, :], (rows, GRID_W, nfreq)).reshape(seq, nfreq)
    by_col = lambda t: jnp.broadcast_to(t[None, :, :], (rows, GRID_W, nfreq)).reshape(seq, nfreq)
    table = lambda fn: jnp.concatenate([by_row(fn(ang_r))] * 2 + [by_col(fn(ang_c))] * 2, axis=-1)
    cos, sin = table(jnp.cos), table(jnp.sin)
    first = (np.arange(HEAD_DIM) % (HEAD_DIM // 2)) < HEAD_DIM // 4
    return cos, jnp.where(first, -sin, 0.0), jnp.where(first, 0.0, sin)


def _trunk(x, mod, p, *, batch, seq):
    t = _tiles(seq)
    tm = t["tm"]
    rope_tabs = _rope_tables(seq)
    for l in range(DEPTH):
        x = _ffn(x, mod, p["g_norm"], p["wgu"], p["wout"], layer=l, which=0, seq=seq, tm=tm)
        qa, ka, va, qb, kb, vb, qc, kc, vc = _qkv(x, mod, p["g_norm"], p["wqkv"], rope_tabs, p["gh"],
                                                  layer=l, seq=seq, tm=tm)
        oa = _attn_a(qa, ka, va, p["kn_a"], layer=l, batch=batch, seq=seq, tq=t["tq_a"], tks=t["tks"],
                     unroll=t["unroll"])
        ob = _attn_b(qb, kb, vb, p["bias_b"], p["sink"], layer=l, batch=batch, seq=seq, qb=t["qb"])
        oc = _attn_c(qc, kc, vc, p["bias_c"], p["cfar_c"], p["kn_c"], p["lam"], p["g_subln"],
                     layer=l, batch=batch, seq=seq, tks=t["tks"], unroll=t["unroll"])
        x = _out_proj(x, mod, p["g_norm"], oa, ob, oc, p["wgate"], p["wo"], layer=l, seq=seq,
                      tm=t["tm_out"])
        x = _ffn(x, mod, p["g_norm"], p["wgu"], p["wout"], layer=l, which=1, seq=seq, tm=tm)
    return x


def _prepare(w_ff_in, w_ff_out, w_in, w_o, g_qa, g_ka, g_qb, g_kb, g_qc, g_kc, sink,
             lam_q1, lam_k1, lam_q2, lam_k2, g_subln, rel_bias):
    wg = w_ff_in[..., :D_FF].reshape(DEPTH, 2, D_MODEL, N_FF_CHUNKS, FF_CHUNK)
    wu = w_ff_in[..., D_FF:].reshape(DEPTH, 2, D_MODEL, N_FF_CHUNKS, FF_CHUNK)
    wgu = jnp.concatenate([wg, wu], axis=-1).transpose(0, 1, 3, 2, 4).astype(BF16)
    wout = w_ff_out.reshape(DEPTH, 2, N_FF_CHUNKS, FF_CHUNK, D_MODEL).astype(BF16)
    zeros = jnp.zeros_like(g_qa)
    gh = jnp.stack([g_qa, g_ka, g_qb, g_kb, jnp.tile(g_qc, (1, 2)), jnp.tile(g_kc, (1, 2)), zeros, zeros],
                   axis=1).astype(F32)
    table_b, table_c = rel_bias[:, :N_Q_HEADS], rel_bias[:, N_Q_HEADS:]
    half = N_BUCKETS // 2
    c_left, c_right = table_c[half - 1] * LOG2E, table_c[N_BUCKETS - 1] * LOG2E
    lam_init = jnp.asarray([0.8 - 0.6 * math.exp(-0.3 * l) for l in range(DEPTH)], F32)
    lam = (jnp.exp(jnp.sum(lam_q1.astype(F32) * lam_k1.astype(F32), axis=-1))
           - jnp.exp(jnp.sum(lam_q2.astype(F32) * lam_k2.astype(F32), axis=-1)) + lam_init)

    band = np.abs(np.arange(NEAR)[None, :] - BLOCK_Q - np.arange(BLOCK_Q)[:, None]) <= WINDOW
    near_c = (_near_bias(table_c) * LOG2E).reshape(
        N_KV_HEADS, Q_PER_KV, BLOCK_Q, NEAR // BLOCK_Q, BLOCK_Q).transpose(0, 3, 4, 1, 2).reshape(
        N_KV_HEADS, NEAR // BLOCK_Q, BLOCK_Q, Q_PER_KV * BLOCK_Q)
    cfar_c = jnp.repeat(jnp.stack([c_left, c_right, jnp.max(table_c, axis=0) * LOG2E]).astype(F32),
                        BLOCK_Q, axis=1).reshape(3, N_KV_HEADS, 1, Q_PER_KV * BLOCK_Q)

    def key_norm(g, dim):
        bound = BOUND_MARGIN * math.sqrt(dim) * jnp.max(jnp.abs(g.astype(F32)), axis=-1)
        return jnp.broadcast_to(bound[:, None, None], (DEPTH, 1, HEAD_DIM))

    return dict(
        wgu=wgu, wout=wout,
        wqkv=w_in[:, :, :QKV_W].astype(BF16), wgate=w_in[:, :, QKV_W:].astype(BF16), wo=w_o.astype(BF16),
        gh=gh,
        bias_b=jnp.where(band, _near_bias(table_b) * LOG2E, NEG).reshape(
            N_KV_HEADS, Q_PER_KV, BLOCK_Q, NEAR).transpose(0, 3, 1, 2).reshape(
            N_KV_HEADS, NEAR, Q_PER_KV * BLOCK_Q),
        sink=jnp.repeat(sink.astype(F32) * LOG2E, BLOCK_Q, axis=1).reshape(
            DEPTH, N_KV_HEADS, 1, Q_PER_KV * BLOCK_Q),
        bias_c=jnp.concatenate([cfar_c[0][:, None] + jnp.zeros((1, 1, BLOCK_Q, 1), F32), near_c,
                                cfar_c[1][:, None] + jnp.zeros((1, 1, BLOCK_Q, 1), F32)], axis=1),
        cfar_c=cfar_c,
        kn_a=key_norm(g_ka, HEAD_DIM), kn_c=key_norm(g_kc, DIFF_DIM),
        lam=jnp.broadcast_to(lam[:, None, None], (DEPTH, 1, HEAD_DIM)),
        g_subln=g_subln.astype(F32)[:, None, :],
    )


def kernel(x_prompt, x_sample, c_prompt, c_sample, w_ada, b_ada, g_norm, w_ff_in, w_ff_out, w_in, w_o,
           g_qa, g_ka, g_qb, g_kb, g_qc, g_kc, sink, lam_q1, lam_k1, lam_q2, lam_k2, g_subln, rel_bias):
    p = _prepare(w_ff_in, w_ff_out, w_in, w_o, g_qa, g_ka, g_qb, g_kb, g_qc, g_kc, sink,
                 lam_q1, lam_k1, lam_q2, lam_k2, g_subln, rel_bias)
    p["g_norm"] = g_norm.astype(F32)
    outs = []
    n_cond = 0
    conds = [c_prompt, c_sample]
    c_all = jnp.concatenate(conds + [jnp.zeros((ADA_ROWS - sum(c.shape[0] for c in conds), D_MODEL), F32)])
    mod_all = _ada(c_all, w_ada, b_ada)
    for x, c in ((x_prompt, c_prompt), (x_sample, c_sample)):
        batch, seq, _ = x.shape
        mod = mod_all[:, n_cond:n_cond + batch].reshape(DEPTH, batch, 9, D_MODEL)
        n_cond += batch
        y = _trunk(x.reshape(batch * seq, D_MODEL), mod, p, batch=batch, seq=seq)
        outs.append(y.reshape(batch, seq, D_MODEL))
    return tuple(outs)
```

```python
import functools
import math

import numpy as np
import jax
import jax.numpy as jnp
from jax import lax
from jax.experimental import pallas as pl
from jax.experimental.pallas import tpu as pltpu

F32 = jnp.float32
BF16 = jnp.bfloat16
ATTN_OUT = jnp.bfloat16

D_MODEL = 1024
DEPTH = 4
N_Q_HEADS = 8
N_KV_HEADS = 2
Q_PER_KV = N_Q_HEADS // N_KV_HEADS
HEAD_DIM = 128
DIFF_DIM = 64
D_FF = 2816
BLOCK_Q = 128
WINDOW = 128
GRID_W = 64
N_BUCKETS = 32
MAX_DISTANCE = 128
ROPE_THETA = 10000.0
EPS = 1e-6
NEG = -1e30
LOG2E = 1.4426950408889634

Q_W = N_Q_HEADS * HEAD_DIM
KV_W = N_KV_HEADS * HEAD_DIM
BRANCH_IN = Q_W + 2 * KV_W
QKV_W = 3 * BRANCH_IN
MXU_TILE = 256
FF_CHUNK = MXU_TILE
N_FF_CHUNKS = D_FF // FF_CHUNK
ADA_ROWS = 16
ADA_TN = 1536
NEAR = 3 * BLOCK_Q
VT_ROWS = HEAD_DIM + 16

F32_SUBLANES = 8
ROW_TILE = 512
STEPS_PER_REGION = 32
BOUND_MARGIN = 1.01

VMEM_LIMIT = 56 * 1024 * 1024


def _tiles(seq):
    tks = min(MXU_TILE, seq)
    return dict(
        tm=min(ROW_TILE, seq),
        tm_out=min(2 * ROW_TILE, seq),
        tq_a=2 * BLOCK_Q,
        tks=tks,
        qb=min(STEPS_PER_REGION, seq // BLOCK_Q),
        unroll=min(STEPS_PER_REGION, seq // tks),
    )


def _cparams(*sem):
    return pltpu.CompilerParams(dimension_semantics=sem, vmem_limit_bytes=VMEM_LIMIT)


def _dot(a, b):
    return jnp.dot(a, b, preferred_element_type=F32)


def _dot_nt(a, b):
    return lax.dot_general(a, b, (((1,), (1,)), ((), ())), preferred_element_type=F32)


def _ada_kernel(c_ref, w_ref, b_ref, o_ref):
    c = c_ref[...]
    a = (c * jax.nn.sigmoid(c)).astype(BF16)
    o_ref[0] = _dot(a, w_ref[0].astype(BF16)) + b_ref[0]


def _ada(c_all, w_ada, b_ada):
    n_out = w_ada.shape[-1]
    return pl.pallas_call(
        _ada_kernel,
        out_shape=jax.ShapeDtypeStruct((DEPTH, ADA_ROWS, n_out), F32),
        grid=(DEPTH, n_out // ADA_TN),
        in_specs=[
            pl.BlockSpec((ADA_ROWS, D_MODEL), lambda l, j: (0, 0)),
            pl.BlockSpec((1, D_MODEL, ADA_TN), lambda l, j: (l, 0, j)),
            pl.BlockSpec((1, 1, ADA_TN), lambda l, j: (l, 0, j)),
        ],
        out_specs=pl.BlockSpec((1, ADA_ROWS, ADA_TN), lambda l, j: (l, 0, j)),
        compiler_params=_cparams("arbitrary", "arbitrary"),
        name="ada",
    )(c_all, w_ada, b_ada.reshape(DEPTH, 1, n_out))


def _modulate(x, mod_ref, g, jj):
    ms = jnp.mean(x * x, axis=-1, keepdims=True)
    y = x * lax.rsqrt(ms + EPS) * g
    return y * (1.0 + mod_ref[0, 3 * jj + 1:3 * jj + 2, :]) + mod_ref[0, 3 * jj:3 * jj + 1, :]


def _ffn_kernel(x_ref, mod_ref, g_ref, wgu_ref, wout_ref, o_ref, nb_ref, acc_ref, *, jj):
    x = x_ref[...]
    nb_ref[...] = _modulate(x, mod_ref, g_ref[jj:jj + 1, :], jj).astype(BF16)
    acc_ref[...] = jnp.zeros_like(acc_ref)

    def body(c, carry):
        h = _dot(nb_ref[...], wgu_ref[c])
        hg = h[:, :FF_CHUNK]
        a = (hg * jax.nn.sigmoid(hg)) * h[:, FF_CHUNK:]
        acc_ref[...] += _dot(a.astype(BF16), wout_ref[c])
        return carry

    lax.fori_loop(0, N_FF_CHUNKS, body, 0, unroll=True)
    o_ref[...] = x + (0.5 * mod_ref[0, 3 * jj + 2:3 * jj + 3, :]) * acc_ref[...]


def _ffn(x, mod, g_norm, wgu, wout, *, layer, which, seq, tm):
    n = x.shape[0]
    jj = 2 * which
    const = dict(pipeline_mode=pl.Buffered(1))
    return pl.pallas_call(
        functools.partial(_ffn_kernel, jj=jj),
        out_shape=jax.ShapeDtypeStruct((n, D_MODEL), F32),
        grid=(n // tm,),
        in_specs=[
            pl.BlockSpec((tm, D_MODEL), lambda i: (i, 0)),
            pl.BlockSpec((None, 1, 9, D_MODEL), lambda i: (layer, (i * tm) // seq, 0, 0)),
            pl.BlockSpec((None, 3, D_MODEL), lambda i: (layer, 0, 0)),
            pl.BlockSpec((None, None, N_FF_CHUNKS, D_MODEL, 2 * FF_CHUNK),
                         lambda i: (layer, which, 0, 0, 0), **const),
            pl.BlockSpec((None, None, N_FF_CHUNKS, FF_CHUNK, D_MODEL),
                         lambda i: (layer, which, 0, 0, 0), **const),
        ],
        out_specs=pl.BlockSpec((tm, D_MODEL), lambda i: (i, 0)),
        scratch_shapes=[pltpu.VMEM((tm, D_MODEL), BF16), pltpu.VMEM((tm, D_MODEL), F32)],
        input_output_aliases={} if (layer == 0 and which == 0) else {0: 0},
        compiler_params=_cparams("arbitrary"),
        name=f"ffn{which}",
    )(x, mod, g_norm, wgu, wout)


def _head_norm(r, g):
    ms = jnp.mean(r * r, axis=-1, keepdims=True)
    return r * lax.rsqrt(ms + EPS) * g


def _half_norm(r, g2):
    sq = r * r
    lo = lax.broadcasted_iota(jnp.int32, r.shape, 1) < DIFF_DIM
    s_lo = jnp.sum(jnp.where(lo, sq, 0.0), axis=-1, keepdims=True)
    s_hi = jnp.sum(jnp.where(lo, 0.0, sq), axis=-1, keepdims=True)
    ms = jnp.where(lo, s_lo, s_hi) * (1.0 / DIFF_DIM)
    return r * lax.rsqrt(ms + EPS) * g2


def _qkv_kernel(x_ref, mod_ref, g_ref, w_ref, cos_ref, sa_ref, sb_ref, gh_ref,
                qa_ref, ka_ref, va_ref, qb_ref, kb_ref, vb_ref, qc_ref, kc_ref, vc_ref, nb_ref):
    nb_ref[...] = _modulate(x_ref[...], mod_ref, g_ref[1:2, :], 1).astype(BF16)
    cos, sa, sb = cos_ref[...], sa_ref[...], sb_ref[...]

    def rope(r):
        return (r * cos + pltpu.roll(r, HEAD_DIM - 32, 1) * sa + pltpu.roll(r, 32, 1) * sb)

    def proj(col, width):
        return _dot(nb_ref[...], w_ref[:, col:col + width])

    def heads(r, n_heads, fn, out_ref):
        for h in range(n_heads):
            out_ref[h] = fn(r[:, h * HEAD_DIM:(h + 1) * HEAD_DIM]).astype(BF16)

    def values_transposed(r, out_ref):
        for h in range(N_KV_HEADS):
            out_ref[h, :HEAD_DIM, :] = r[:, h * HEAD_DIM:(h + 1) * HEAD_DIM].T.astype(BF16)
            out_ref[h, HEAD_DIM:, :] = jnp.ones((VT_ROWS - HEAD_DIM, r.shape[0]), BF16)

    qa_scale = HEAD_DIM ** -0.5 * LOG2E
    qc_scale = DIFF_DIM ** -0.5 * LOG2E
    g_qa, g_ka, g_qb, g_kb = (gh_ref[i:i + 1, :] for i in range(4))
    g_qc, g_kc = gh_ref[4:5, :], gh_ref[5:6, :]
    col = 0
    heads(proj(col, Q_W), N_Q_HEADS, lambda r: rope(_head_norm(r, g_qa)) * qa_scale, qa_ref)
    col += Q_W
    heads(proj(col, KV_W), N_KV_HEADS, lambda r: rope(_head_norm(r, g_ka)), ka_ref)
    col += KV_W
    values_transposed(proj(col, KV_W),va_ref)
    col += KV_W
    heads(proj(col, Q_W), N_Q_HEADS, lambda r: _head_norm(r, g_qb) * qa_scale, qb_ref)
    col += Q_W
    heads(proj(col, KV_W), N_KV_HEADS, lambda r: _head_norm(r, g_kb), kb_ref)
    col += KV_W
    values_transposed(proj(col, KV_W), vb_ref)
    col += KV_W
    heads(proj(col, Q_W), N_Q_HEADS, lambda r: _half_norm(r, g_qc) * qc_scale, qc_ref)
    col += Q_W
    heads(proj(col, KV_W), N_KV_HEADS, lambda r: _half_norm(r, g_kc), kc_ref)
    col += KV_W
    values_transposed(proj(col, KV_W),vc_ref)


def _qkv(x, mod, g_norm, wqkv, rope_tabs, gh, *, layer, seq, tm):
    n = x.shape[0]
    tiles_per_seq = seq // tm
    q_shape = jax.ShapeDtypeStruct((N_Q_HEADS, n, HEAD_DIM), BF16)
    kv_shape = jax.ShapeDtypeStruct((N_KV_HEADS, n, HEAD_DIM), BF16)
    q_spec = pl.BlockSpec((N_Q_HEADS, tm, HEAD_DIM), lambda i: (0, i, 0))
    kv_spec = pl.BlockSpec((N_KV_HEADS, tm, HEAD_DIM), lambda i: (0, i, 0))
    tab_spec = pl.BlockSpec((tm, HEAD_DIM), lambda i: (i % tiles_per_seq, 0))
    v1_shape = jax.ShapeDtypeStruct((N_KV_HEADS, VT_ROWS, n), BF16)
    v1_spec = pl.BlockSpec((N_KV_HEADS, VT_ROWS, tm), lambda i: (0, 0, i))
    return pl.pallas_call(
        _qkv_kernel,
        out_shape=[q_shape, kv_shape, v1_shape] * 3,
        grid=(n // tm,),
        in_specs=[
            pl.BlockSpec((tm, D_MODEL), lambda i: (i, 0)),
            pl.BlockSpec((None, 1, 9, D_MODEL), lambda i: (layer, (i * tm) // seq, 0, 0)),
            pl.BlockSpec((None, 3, D_MODEL), lambda i: (layer, 0, 0)),
            pl.BlockSpec((None, D_MODEL, QKV_W), lambda i: (layer, 0, 0), pipeline_mode=pl.Buffered(1)),
            tab_spec, tab_spec, tab_spec,
            pl.BlockSpec((None, 8, HEAD_DIM), lambda i: (layer, 0, 0)),
        ],
        out_specs=[q_spec, kv_spec, v1_spec] * 3,
        scratch_shapes=[pltpu.VMEM((tm, D_MODEL), BF16)],
        compiler_params=_cparams("arbitrary"),
        name="qkv",
    )(x, mod, g_norm, wqkv, *rope_tabs, gh)


def _out_kernel(x_ref, mod_ref, g_ref, oa_ref, ob_ref, oc_ref, wg_ref, wo_ref, o_ref, mg_ref):
    x = x_ref[...]
    nb = _modulate(x, mod_ref, g_ref[1:2, :], 1).astype(BF16)
    for br, br_ref in enumerate((oa_ref, ob_ref, oc_ref)):
        gate = jax.nn.sigmoid(_dot(nb, wg_ref[:, br * D_MODEL:(br + 1) * D_MODEL]))
        for h in range(N_Q_HEADS):
            lanes = slice(h * HEAD_DIM, (h + 1) * HEAD_DIM)
            term = gate[:, lanes] * br_ref[h].astype(F32)
            if br == 0:
                mg_ref[:, lanes] = term
            else:
                mg_ref[:, lanes] += term
    o_ref[...] = x + mod_ref[0, 5:6, :] * _dot(mg_ref[...].astype(BF16), wo_ref[...])


def _out_proj(x, mod, g_norm, oa, ob, oc, wgate, wo, *, layer, seq, tm):
    n = x.shape[0]
    o_spec = pl.BlockSpec((N_Q_HEADS, tm, HEAD_DIM), lambda i: (0, i, 0))
    return pl.pallas_call(
        _out_kernel,
        out_shape=jax.ShapeDtypeStruct((n, D_MODEL), F32),
        grid=(n // tm,),
        in_specs=[
            pl.BlockSpec((tm, D_MODEL), lambda i: (i, 0)),
            pl.BlockSpec((None, 1, 9, D_MODEL), lambda i: (layer, (i * tm) // seq, 0, 0)),
            pl.BlockSpec((None, 3, D_MODEL), lambda i: (layer, 0, 0)),
            o_spec, o_spec, o_spec,
            pl.BlockSpec((None, D_MODEL, 3 * D_MODEL), lambda i: (layer, 0, 0), pipeline_mode=pl.Buffered(1)),
            pl.BlockSpec((None, D_MODEL, D_MODEL), lambda i: (layer, 0, 0), pipeline_mode=pl.Buffered(1)),
        ],
        out_specs=pl.BlockSpec((tm, D_MODEL), lambda i: (i, 0)),
        scratch_shapes=[pltpu.VMEM((tm, D_MODEL), F32)],
        input_output_aliases={0: 0},
        compiler_params=_cparams("arbitrary"),
        name="out_proj",
    )(x, mod, g_norm, oa, ob, oc, wgate, wo)


MIN_DENOM = 2.0 ** -64


def _key_slice(idx, tks):
    return pl.ds(pl.multiple_of(idx * tks, tks), tks)


def _score_bound(q, key_norm):
    qf = q.astype(F32)
    qn2 = _dot_nt(jnp.ones((F32_SUBLANES, q.shape[1]), F32), qf * qf)[:1]
    return jnp.sqrt(qn2) * key_norm


def _bounded_iteration(t, last, q, k_ref, v_ref, acct_ref, s_ref, *, unroll, tks, shift, bias_fn=None):
    acct = acct_ref[...]
    s = s_ref[...]
    for u in range(unroll):
        idx = t * unroll + u
        s_next = None if (last and u == unroll - 1) else _dot_nt(k_ref[_key_slice(idx + 1, tks), :], q)
        if bias_fn is not None:
            s = bias_fn(idx, s)
        acct = acct + _dot(v_ref[:, _key_slice(idx, tks)], jnp.exp2(s - shift).astype(BF16))
        s = s_next
    acct_ref[...] = acct
    if not last:
        s_ref[...] = s


def _exact_pass(q, k_ref, v_ref, m_ref, acct_ref, *, n_sub, tks, bias_fn=None):
    m_ref[...] = jnp.full_like(m_ref, -jnp.inf)
    acct_ref[...] = jnp.zeros_like(acct_ref)

    def body(idx, carry):
        s = _dot_nt(k_ref[_key_slice(idx, tks), :], q)
        if bias_fn is not None:
            s = bias_fn(idx, s)
        m = m_ref[...]
        m_new = jnp.maximum(m, jnp.max(s, axis=0, keepdims=True))
        p = jnp.exp2(s - m_new).astype(BF16)
        acct_ref[...] = jnp.exp2(m - m_new) * acct_ref[...] + _dot(v_ref[:, _key_slice(idx, tks)], p)
        m_ref[...] = m_new
        return carry

    lax.fori_loop(0, n_sub, body, 0)


def _denominators_ok(acct_ref):
    return jnp.min(acct_ref[HEAD_DIM:HEAD_DIM + 1, :]) >= MIN_DENOM


def _attn_a_kernel(q_ref, k_ref, v_ref, kn_ref, o_ref, m_ref, acct_ref, s_ref, *, n_iter, unroll, tks):
    rows = acct_ref.shape[1]
    q = q_ref[...].reshape(rows, HEAD_DIM)
    shift = _score_bound(q, kn_ref[:, :1])
    acct_ref[...] = jnp.zeros_like(acct_ref)
    s_ref[...] = _dot_nt(k_ref[0:tks, :], q)
    refs = (q, k_ref, v_ref, acct_ref, s_ref)

    def body(t, carry):
        _bounded_iteration(t, False, *refs, unroll=unroll, tks=tks, shift=shift)
        return carry

    lax.fori_loop(0, n_iter - 1, body, 0)
    _bounded_iteration(n_iter - 1, True, *refs, unroll=unroll, tks=tks, shift=shift)

    @pl.when(jnp.logical_not(_denominators_ok(acct_ref)))
    def _():
        _exact_pass(q, k_ref, v_ref, m_ref, acct_ref, n_sub=n_iter * unroll, tks=tks)

    acct = acct_ref[...]
    o_t = acct[:HEAD_DIM] / acct[HEAD_DIM:HEAD_DIM + 1]
    o_ref[...] = o_t.T.astype(o_ref.dtype).reshape(o_ref.shape)


def _attn_a(q, k, v1t, key_norm, *, layer, batch, seq, tq, tks, unroll):
    n = q.shape[1]
    nq = seq // tq
    rows = Q_PER_KV * tq
    q_map = lambda b, g, i: (g, b * nq + i, 0)
    return pl.pallas_call(
        functools.partial(_attn_a_kernel, n_iter=seq // (tks * unroll), unroll=unroll, tks=tks),
        out_shape=jax.ShapeDtypeStruct((N_Q_HEADS, n, HEAD_DIM), ATTN_OUT),
        grid=(batch, N_KV_HEADS, nq),
        in_specs=[
            pl.BlockSpec((Q_PER_KV, tq, HEAD_DIM), q_map),
            pl.BlockSpec((None, seq, HEAD_DIM), lambda b, g, i: (g, b, 0)),
            pl.BlockSpec((None, VT_ROWS, seq), lambda b, g, i: (g, 0, b)),
            pl.BlockSpec((None, 1, HEAD_DIM), lambda b, g, i: (layer, 0, 0)),
        ],
        out_specs=pl.BlockSpec((Q_PER_KV, tq, HEAD_DIM), q_map),
        scratch_shapes=[pltpu.VMEM((1, rows), F32), pltpu.VMEM((VT_ROWS, rows), F32),
                        pltpu.VMEM((tks, rows), F32)],
        compiler_params=_cparams("arbitrary", "arbitrary", "arbitrary"),
        name="attn_axial",
    )(q, k, v1t, key_norm)


def _attn_b_kernel(q_ref, kp_ref, kc_ref, kn_ref, vp_ref, vc_ref, vn_ref, bias_ref, sink_ref, o_ref,
                   *, nb, qb):
    i = pl.program_id(2)
    rows = Q_PER_KV * BLOCK_Q
    kwin = jnp.concatenate([kp_ref[...], kc_ref[...], kn_ref[...]], axis=0)
    vwin = jnp.concatenate([vp_ref[...], vc_ref[...], vn_ref[...]], axis=1)
    bias = bias_ref[...]
    sink = sink_ref[...]
    for u in range(qb):
        blk = i * qb + u
        q = q_ref[:, u * BLOCK_Q:(u + 1) * BLOCK_Q, :].reshape(rows, HEAD_DIM)
        s = _dot_nt(kwin[u * BLOCK_Q:u * BLOCK_Q + NEAR], q) + bias
        s = jnp.concatenate([jnp.where(blk > 0, s[:BLOCK_Q], NEG), s[BLOCK_Q:2 * BLOCK_Q],
                             jnp.where(blk < nb - 1, s[2 * BLOCK_Q:], NEG)], axis=0)
        m = jnp.maximum(jnp.max(s, axis=0, keepdims=True), sink)
        p = jnp.exp2(s - m).astype(BF16)
        acct = _dot(vwin[:, u * BLOCK_Q:u * BLOCK_Q + NEAR], p)
        den = acct[HEAD_DIM:HEAD_DIM + 1] + jnp.exp2(sink - m)
        o_t = acct[:HEAD_DIM] / den
        o_ref[:, u * BLOCK_Q:(u + 1) * BLOCK_Q, :] = o_t.T.astype(o_ref.dtype).reshape(
            Q_PER_KV, BLOCK_Q, HEAD_DIM)


def _attn_b(q, k, v1t, bias_t, sink_t, *, layer, batch, seq, qb):
    n = q.shape[1]
    nb = seq // BLOCK_Q
    steps = nb // qb
    rows = Q_PER_KV * BLOCK_Q
    cur_map = lambda b, g, i: (g, b * steps + i, 0)
    prev_map = lambda b, g, i: (g, b * nb + jnp.maximum(i * qb - 1, 0), 0)
    next_map = lambda b, g, i: (g, b * nb + jnp.minimum(i * qb + qb, nb - 1), 0)
    t = lambda index_map: (lambda b, g, i: (index_map(b, g, i)[0], 0, index_map(b, g, i)[1]))
    k_specs = [pl.BlockSpec((None, BLOCK_Q, HEAD_DIM), prev_map),
               pl.BlockSpec((None, qb * BLOCK_Q, HEAD_DIM), cur_map),
               pl.BlockSpec((None, BLOCK_Q, HEAD_DIM), next_map)]
    v_specs = [pl.BlockSpec((None, VT_ROWS, BLOCK_Q), t(prev_map)),
               pl.BlockSpec((None, VT_ROWS, qb * BLOCK_Q), t(cur_map)),
               pl.BlockSpec((None, VT_ROWS, BLOCK_Q), t(next_map))]
    return pl.pallas_call(
        functools.partial(_attn_b_kernel, nb=nb, qb=qb),
        out_shape=jax.ShapeDtypeStruct((N_Q_HEADS, n, HEAD_DIM), ATTN_OUT),
        grid=(batch, N_KV_HEADS, steps),
        in_specs=[pl.BlockSpec((Q_PER_KV, qb * BLOCK_Q, HEAD_DIM), cur_map)] + k_specs + v_specs + [
            pl.BlockSpec((None, NEAR, rows), lambda b, g, i: (g, 0, 0)),
            pl.BlockSpec((None, None, 1, rows), lambda b, g, i: (layer, g, 0, 0)),
        ],
        out_specs=pl.BlockSpec((Q_PER_KV, qb * BLOCK_Q, HEAD_DIM), cur_map),
        compiler_params=_cparams("arbitrary", "arbitrary", "arbitrary"),
        name="attn_window",
    )(q, k, k, k, v1t, v1t, v1t, bias_t, sink_t)


def _attn_c_kernel(q_ref, k_ref, v_ref, bias_ref, cfar_ref, kn_ref, lam_ref, gs_ref, o_ref,
                   m_ref, acct_ref, s_ref, *, n_iter, unroll, tks, nblk, out_scale):
    n_blk = pl.program_id(2)
    rows = Q_PER_KV * BLOCK_Q
    blocks_per_sub = tks // BLOCK_Q
    blocks_per_iter = unroll * blocks_per_sub

    q = q_ref[...].reshape(rows, HEAD_DIM)
    lo = lax.broadcasted_iota(jnp.int32, q.shape, 1) < DIFF_DIM
    zero = jnp.zeros_like(q)
    q2 = jnp.concatenate([jnp.where(lo, q, zero), jnp.where(lo, zero, q)], axis=0)
    acct_ref[...] = jnp.zeros_like(acct_ref)
    s_ref[...] = _dot_nt(k_ref[0:tks, :], q2)
    refs = (q2, k_ref, v_ref, acct_ref, s_ref)

    c_left, c_right, c_max = cfar_ref[0], cfar_ref[1], cfar_ref[2]
    t_lo = jnp.maximum(n_blk - 1, 0) // blocks_per_iter
    t_hi = jnp.minimum(n_blk + 1, nblk - 1) // blocks_per_iter
    two = lambda x: jnp.concatenate([x, x], axis=1)
    shift = _score_bound(q2, kn_ref[:, :1]) + two(c_max)

    def add_bias(idx, s):
        tile_of = lambda key_blk: jnp.clip(key_blk - n_blk + 2, 0, NEAR // BLOCK_Q + 1)
        bias = jnp.concatenate([bias_ref[tile_of(idx * blocks_per_sub + kb)] for kb in range(blocks_per_sub)],
                               axis=0)
        return jnp.concatenate([s[:, :rows] + bias, s[:, rows:] + bias], axis=1)

    def iteration(t, last):
        is_near = (t >= t_lo) & (t <= t_hi)

        @pl.when(jnp.logical_not(is_near))
        def _():
            side = jnp.where(t < t_lo, c_left, c_right)
            _bounded_iteration(t, last, *refs, unroll=unroll, tks=tks, shift=shift - two(side))

        @pl.when(is_near)
        def _():
            _bounded_iteration(t, last, *refs, unroll=unroll, tks=tks, shift=shift, bias_fn=add_bias)

    def body(t, carry):
        iteration(t, False)
        return carry

    lax.fori_loop(0, n_iter - 1, body, 0)
    iteration(n_iter - 1, True)

    @pl.when(jnp.logical_not(_denominators_ok(acct_ref)))
    def _():
        _exact_pass(q2, k_ref, v_ref, m_ref, acct_ref, n_sub=n_iter * unroll, tks=tks, bias_fn=add_bias)

    acct = acct_ref[...]
    o1_t = acct[:HEAD_DIM, :rows] / acct[HEAD_DIM:HEAD_DIM + 1, :rows]
    o2_t = acct[:HEAD_DIM, rows:] / acct[HEAD_DIM:HEAD_DIM + 1, rows:]
    o_t = o1_t - lam_ref[:, :1] * o2_t
    ms = jnp.mean(o_t * o_t, axis=0, keepdims=True)
    o = (o_t * lax.rsqrt(ms + EPS)).T * gs_ref[...] * out_scale
    o_ref[...] = o.astype(o_ref.dtype).reshape(o_ref.shape)


def _attn_c(q, k, v1t, bias_t, cfar_t, key_norm, lam, g_subln, *, layer, batch, seq, tks, unroll):
    n = q.shape[1]
    nq = seq // BLOCK_Q
    rows = Q_PER_KV * BLOCK_Q
    lam_init = 0.8 - 0.6 * math.exp(-0.3 * layer)
    q_map = lambda b, g, i: (g, b * nq + i, 0)
    return pl.pallas_call(
        functools.partial(_attn_c_kernel, n_iter=seq // (tks * unroll), unroll=unroll, tks=tks,
                          nblk=nq, out_scale=1.0 - lam_init),
        out_shape=jax.ShapeDtypeStruct((N_Q_HEADS, n, HEAD_DIM), ATTN_OUT),
        grid=(batch, N_KV_HEADS, nq),
        in_specs=[
            pl.BlockSpec((Q_PER_KV, BLOCK_Q, HEAD_DIM), q_map),
            pl.BlockSpec((None, seq, HEAD_DIM), lambda b, g, i: (g, b, 0)),
            pl.BlockSpec((None, VT_ROWS, seq), lambda b, g, i: (g, 0, b)),
            pl.BlockSpec((None, NEAR // BLOCK_Q + 2, BLOCK_Q, rows), lambda b, g, i: (g, 0, 0, 0)),
            pl.BlockSpec((3, None, 1, rows), lambda b, g, i: (0, g, 0, 0)),
            pl.BlockSpec((None, 1, HEAD_DIM), lambda b, g, i: (layer, 0, 0)),
            pl.BlockSpec((None, 1, HEAD_DIM), lambda b, g, i: (layer, 0, 0)),
            pl.BlockSpec((None, 1, HEAD_DIM), lambda b, g, i: (layer, 0, 0)),
        ],
        out_specs=pl.BlockSpec((Q_PER_KV, BLOCK_Q, HEAD_DIM), q_map),
        scratch_shapes=[pltpu.VMEM((1, 2 * rows), F32), pltpu.VMEM((VT_ROWS, 2 * rows), F32),
                        pltpu.VMEM((tks, 2 * rows), F32)],
        compiler_params=_cparams("arbitrary", "arbitrary", "arbitrary"),
        name="attn_diff",
    )(q, k, v1t, bias_t, cfar_t, key_norm, lam, g_subln)


def _t5_bucket_np(rel):
    half = N_BUCKETS // 2
    max_exact = half // 2
    ret = np.where(rel > 0, half, 0)
    n = np.abs(rel)
    ratio = np.log(np.maximum(n, 1).astype(np.float32) / np.float32(max_exact)) / np.float32(
        math.log(MAX_DISTANCE / max_exact))
    large = max_exact + (ratio * np.float32(half - max_exact)).astype(np.int32)
    large = np.minimum(large, half - 1)
    return (ret + np.where(n < max_exact, n, large)).astype(np.int32)


def _near_bias(table):
    r = np.arange(BLOCK_Q)[:, None]
    c = np.arange(NEAR)[None, :]
    bucket = _t5_bucket_np(c - BLOCK_Q - r)
    onehot = (bucket.reshape(-1, 1) == np.arange(N_BUCKETS)[None, :]).astype(np.float32)
    rows = jnp.dot(jnp.asarray(onehot), table.astype(F32), precision=lax.Precision.HIGHEST)
    return rows.reshape(BLOCK_Q, NEAR, table.shape[1]).transpose(2, 0, 1)


def _rope_tables(seq):
    rows = seq // GRID_W
    nfreq = HEAD_DIM // 4
    inv = ROPE_THETA ** (-jnp.arange(nfreq, dtype=F32) / nfreq)
    ang_r = jnp.arange(rows).astype(F32)[:, None] * inv
    ang_c = jnp.arange(GRID_W).astype(F32)[:, None] * inv
    by_row = lambda t: jnp.broadcast_to(t[:, None, :], (rows, GRID_W, nfreq)).reshape(seq, nfreq)
    by_col = lambda t: jnp.broadcast_to(t[None, :, :], (rows, GRID_W, nfreq)).reshape(seq, nfreq)
    table = lambda fn: jnp.concatenate([by_row(fn(ang_r))] * 2 + [by_col(fn(ang_c))] * 2, axis=-1)
    cos, sin = table(jnp.cos), table(jnp.sin)
    first = (np.arange(HEAD_DIM) % (HEAD_DIM // 2)) < HEAD_DIM // 4
    return cos, jnp.where(first, -sin, 0.0), jnp.where(first, 0.0, sin)


def _trunk(x, mod, p, *, batch, seq):
    t = _tiles(seq)
    tm = t["tm"]
    rope_tabs = _rope_tables(seq)
    for l in range(DEPTH):
        x = _ffn(x, mod, p["g_norm"], p["wgu"], p["wout"], layer=l, which=0, seq=seq, tm=tm)
        qa, ka, va, qb, kb, vb, qc, kc, vc = _qkv(x, mod, p["g_norm"], p["wqkv"], rope_tabs, p["gh"],
                                                  layer=l, seq=seq, tm=tm)
        oa = _attn_a(qa, ka, va, p["kn_a"], layer=l, batch=batch, seq=seq, tq=t["tq_a"], tks=t["tks"],
                     unroll=t["unroll"])
        ob = _attn_b(qb, kb, vb, p["bias_b"], p["sink"], layer=l, batch=batch, seq=seq, qb=t["qb"])
        oc = _attn_c(qc, kc, vc, p["bias_c"], p["cfar_c"], p["kn_c"], p["lam"], p["g_subln"],
                     layer=l, batch=batch, seq=seq, tks=t["tks"], unroll=t["unroll"])
        x = _out_proj(x, mod, p["g_norm"], oa, ob, oc, p["wgate"], p["wo"], layer=l, seq=seq,
                      tm=t["tm_out"])
        x = _ffn(x, mod, p["g_norm"], p["wgu"], p["wout"], layer=l, which=1, seq=seq, tm=tm)
    return x


def _prepare(w_ff_in, w_ff_out, w_in, w_o, g_qa, g_ka, g_qb, g_kb, g_qc, g_kc, sink,
             lam_q1, lam_k1, lam_q2, lam_k2, g_subln, rel_bias):
    wg = w_ff_in[..., :D_FF].reshape(DEPTH, 2, D_MODEL, N_FF_CHUNKS, FF_CHUNK)
    wu = w_ff_in[..., D_FF:].reshape(DEPTH, 2, D_MODEL, N_FF_CHUNKS, FF_CHUNK)
    wgu = jnp.concatenate([wg, wu], axis=-1).transpose(0, 1, 3, 2, 4).astype(BF16)
    wout = w_ff_out.reshape(DEPTH, 2, N_FF_CHUNKS, FF_CHUNK, D_MODEL).astype(BF16)
    zeros = jnp.zeros_like(g_qa)
    gh = jnp.stack([g_qa, g_ka, g_qb, g_kb, jnp.tile(g_qc, (1, 2)), jnp.tile(g_kc, (1, 2)), zeros, zeros],
                   axis=1).astype(F32)
    table_b, table_c = rel_bias[:, :N_Q_HEADS], rel_bias[:, N_Q_HEADS:]
    half = N_BUCKETS // 2
    c_left, c_right = table_c[half - 1] * LOG2E, table_c[N_BUCKETS - 1] * LOG2E
    lam_init = jnp.asarray([0.8 - 0.6 * math.exp(-0.3 * l) for l in range(DEPTH)], F32)
    lam = (jnp.exp(jnp.sum(lam_q1.astype(F32) * lam_k1.astype(F32), axis=-1))
           - jnp.exp(jnp.sum(lam_q2.astype(F32) * lam_k2.astype(F32), axis=-1)) + lam_init)

    band = np.abs(np.arange(NEAR)[None, :] - BLOCK_Q - np.arange(BLOCK_Q)[:, None]) <= WINDOW
    near_c = (_near_bias(table_c) * LOG2E).reshape(
        N_KV_HEADS, Q_PER_KV, BLOCK_Q, NEAR // BLOCK_Q, BLOCK_Q).transpose(0, 3, 4, 1, 2).reshape(
        N_KV_HEADS, NEAR // BLOCK_Q, BLOCK_Q, Q_PER_KV * BLOCK_Q)
    cfar_c = jnp.repeat(jnp.stack([c_left, c_right, jnp.max(table_c, axis=0) * LOG2E]).astype(F32),
                        BLOCK_Q, axis=1).reshape(3, N_KV_HEADS, 1, Q_PER_KV * BLOCK_Q)

    def key_norm(g, dim):
        bound = BOUND_MARGIN * math.sqrt(dim) * jnp.max(jnp.abs(g.astype(F32)), axis=-1)
        return jnp.broadcast_to(bound[:, None, None], (DEPTH, 1, HEAD_DIM))

    return dict(
        wgu=wgu, wout=wout,
        wqkv=w_in[:, :, :QKV_W].astype(BF16), wgate=w_in[:, :, QKV_W:].astype(BF16), wo=w_o.astype(BF16),
        gh=gh,
        bias_b=jnp.where(band, _near_bias(table_b) * LOG2E, NEG).reshape(
            N_KV_HEADS, Q_PER_KV, BLOCK_Q, NEAR).transpose(0, 3, 1, 2).reshape(
            N_KV_HEADS, NEAR, Q_PER_KV * BLOCK_Q),
        sink=jnp.repeat(sink.astype(F32) * LOG2E, BLOCK_Q, axis=1).reshape(
            DEPTH, N_KV_HEADS, 1, Q_PER_KV * BLOCK_Q),
        bias_c=jnp.concatenate([cfar_c[0][:, None] + jnp.zeros((1, 1, BLOCK_Q, 1), F32), near_c,
                                cfar_c[1][:, None] + jnp.zeros((1, 1, BLOCK_Q, 1), F32)], axis=1),
        cfar_c=cfar_c,
        kn_a=key_norm(g_ka, HEAD_DIM), kn_c=key_norm(g_kc, DIFF_DIM),
        lam=jnp.broadcast_to(lam[:, None, None], (DEPTH, 1, HEAD_DIM)),
        g_subln=g_subln.astype(F32)[:, None, :],
    )


def kernel(x_prompt, x_sample, c_prompt, c_sample, w_ada, b_ada, g_norm, w_ff_in, w_ff_out, w_in, w_o,
           g_qa, g_ka, g_qb, g_kb, g_qc, g_kc, sink, lam_q1, lam_k1, lam_q2, lam_k2, g_subln, rel_bias):
    p = _prepare(w_ff_in, w_ff_out, w_in, w_o, g_qa, g_ka, g_qb, g_kb, g_qc, g_kc, sink,
                 lam_q1, lam_k1, lam_q2, lam_k2, g_subln, rel_bias)
    p["g_norm"] = g_norm.astype(F32)
    outs = []
    n_cond = 0
    conds = [c_prompt, c_sample]
    c_all = jnp.concatenate(conds + [jnp.zeros((ADA_ROWS - sum(c.shape[0] for c in conds), D_MODEL), F32)])
    mod_all = _ada(c_all, w_ada, b_ada)
    for x, c in ((x_prompt, c_prompt), (x_sample, c_sample)):
        batch, seq, _ = x.shape
        mod = mod_all[:, n_cond:n_cond + batch].reshape(DEPTH, batch, 9, D_MODEL)
        n_cond += batch
        y = _trunk(x.reshape(batch * seq, D_MODEL), mod, p, batch=batch, seq=seq)
        outs.append(y.reshape(batch, seq, D_MODEL))
    return tuple(outs)
```

```python
import functools
import math

import numpy as np
import jax
import jax.numpy as jnp
from jax import lax
from jax.experimental import pallas as pl
from jax.experimental.pallas import tpu as pltpu

F32 = jnp.float32
BF16 = jnp.bfloat16
ATTN_OUT = jnp.bfloat16

D_MODEL = 1024
DEPTH = 4
N_Q_HEADS = 8
N_KV_HEADS = 2
Q_PER_KV = N_Q_HEADS // N_KV_HEADS
HEAD_DIM = 128
DIFF_DIM = 64
D_FF = 2816
BLOCK_Q = 128
WINDOW = 128
GRID_W = 64
N_BUCKETS = 32
MAX_DISTANCE = 128
ROPE_THETA = 10000.0
EPS = 1e-6
NEG = -1e30
LOG2E = 1.4426950408889634

Q_W = N_Q_HEADS * HEAD_DIM
KV_W = N_KV_HEADS * HEAD_DIM
BRANCH_IN = Q_W + 2 * KV_W
QKV_W = 3 * BRANCH_IN
MXU_TILE = 256
FF_CHUNK = MXU_TILE
N_FF_CHUNKS = D_FF // FF_CHUNK
ADA_ROWS = 16
ADA_TN = 1536
NEAR = 3 * BLOCK_Q
VT_ROWS = HEAD_DIM + 16

F32_SUBLANES = 8
ROW_TILE = 512
STEPS_PER_REGION = 32
BOUND_MARGIN = 1.01

VMEM_LIMIT = 56 * 1024 * 1024


def _tiles(seq):
    tks = min(MXU_TILE, seq)
    return dict(
        tm=min(ROW_TILE, seq),
        tm_out=min(2 * ROW_TILE, seq),
        tq_a=2 * BLOCK_Q,
        tks=tks,
        qb=min(STEPS_PER_REGION, seq // BLOCK_Q),
        unroll=min(STEPS_PER_REGION, seq // tks),
    )


def _cparams(*sem):
    return pltpu.CompilerParams(dimension_semantics=sem, vmem_limit_bytes=VMEM_LIMIT)


def _dot(a, b):
    return jnp.dot(a, b, preferred_element_type=F32)


def _dot_nt(a, b):
    return lax.dot_general(a, b, (((1,), (1,)), ((), ())), preferred_element_type=F32)


def _ada_kernel(c_ref, w_ref, b_ref, o_ref):
    c = c_ref[...]
    a = (c * jax.nn.sigmoid(c)).astype(BF16)
    o_ref[0] = _dot(a, w_ref[0].astype(BF16)) + b_ref[0]


def _ada(c_all, w_ada, b_ada):
    n_out = w_ada.shape[-1]
    return pl.pallas_call(
        _ada_kernel,
        out_shape=jax.ShapeDtypeStruct((DEPTH, ADA_ROWS, n_out), F32),
        grid=(DEPTH, n_out // ADA_TN),
        in_specs=[
            pl.BlockSpec((ADA_ROWS, D_MODEL), lambda l, j: (0, 0)),
            pl.BlockSpec((1, D_MODEL, ADA_TN), lambda l, j: (l, 0, j)),
            pl.BlockSpec((1, 1, ADA_TN), lambda l, j: (l, 0, j)),
        ],
        out_specs=pl.BlockSpec((1, ADA_ROWS, ADA_TN), lambda l, j: (l, 0, j)),
        compiler_params=_cparams("arbitrary", "arbitrary"),
        name="ada",
    )(c_all, w_ada, b_ada.reshape(DEPTH, 1, n_out))


def _modulate(x, mod_ref, g, jj):
    ms = jnp.mean(x * x, axis=-1, keepdims=True)
    y = x * lax.rsqrt(ms + EPS) * g
    return y * (1.0 + mod_ref[0, 3 * jj + 1:3 * jj + 2, :]) + mod_ref[0, 3 * jj:3 * jj + 1, :]


def _ffn_kernel(x_ref, mod_ref, g_ref, wgu_ref, wout_ref, o_ref, nb_ref, acc_ref, *, jj):
    x = x_ref[...]
    nb_ref[...] = _modulate(x, mod_ref, g_ref[jj:jj + 1, :], jj).astype(BF16)
    acc_ref[...] = jnp.zeros_like(acc_ref)

    def body(c, carry):
        h = _dot(nb_ref[...], wgu_ref[c])
        hg = h[:, :FF_CHUNK]
        a = (hg * jax.nn.sigmoid(hg)) * h[:, FF_CHUNK:]
        acc_ref[...] += _dot(a.astype(BF16), wout_ref[c])
        return carry

    lax.fori_loop(0, N_FF_CHUNKS, body, 0, unroll=True)
    o_ref[...] = x + (0.5 * mod_ref[0, 3 * jj + 2:3 * jj + 3, :]) * acc_ref[...]


def _ffn(x, mod, g_norm, wgu, wout, *, layer, which, seq, tm):
    n = x.shape[0]
    jj = 2 * which
    const = dict(pipeline_mode=pl.Buffered(1))
    return pl.pallas_call(
        functools.partial(_ffn_kernel, jj=jj),
        out_shape=jax.ShapeDtypeStruct((n, D_MODEL), F32),
        grid=(n // tm,),
        in_specs=[
            pl.BlockSpec((tm, D_MODEL), lambda i: (i, 0)),
            pl.BlockSpec((None, 1, 9, D_MODEL), lambda i: (layer, (i * tm) // seq, 0, 0)),
            pl.BlockSpec((None, 3, D_MODEL), lambda i: (layer, 0, 0)),
            pl.BlockSpec((None, None, N_FF_CHUNKS, D_MODEL, 2 * FF_CHUNK),
                         lambda i: (layer, which, 0, 0, 0), **const),
            pl.BlockSpec((None, None, N_FF_CHUNKS, FF_CHUNK, D_MODEL),
                         lambda i: (layer, which, 0, 0, 0), **const),
        ],
        out_specs=pl.BlockSpec((tm, D_MODEL), lambda i: (i, 0)),
        scratch_shapes=[pltpu.VMEM((tm, D_MODEL), BF16), pltpu.VMEM((tm, D_MODEL), F32)],
        input_output_aliases={} if (layer == 0 and which == 0) else {0: 0},
        compiler_params=_cparams("arbitrary"),
        name=f"ffn{which}",
    )(x, mod, g_norm, wgu, wout)


def _head_norm(r, g):
    ms = jnp.mean(r * r, axis=-1, keepdims=True)
    return r * lax.rsqrt(ms + EPS) * g


def _half_norm(r, g2):
    sq = r * r
    lo = lax.broadcasted_iota(jnp.int32, r.shape, 1) < DIFF_DIM
    s_lo = jnp.sum(jnp.where(lo, sq, 0.0), axis=-1, keepdims=True)
    s_hi = jnp.sum(jnp.where(lo, 0.0, sq), axis=-1, keepdims=True)
    ms = jnp.where(lo, s_lo, s_hi) * (1.0 / DIFF_DIM)
    return r * lax.rsqrt(ms + EPS) * g2


def _qkv_kernel(x_ref, mod_ref, g_ref, w_ref, cos_ref, sa_ref, sb_ref, gh_ref,
                qa_ref, ka_ref, va_ref, qb_ref, kb_ref, vb_ref, qc_ref, kc_ref, vc_ref, nb_ref):
    nb_ref[...] = _modulate(x_ref[...], mod_ref, g_ref[1:2, :], 1).astype(BF16)
    cos, sa, sb = cos_ref[...], sa_ref[...], sb_ref[...]

    def rope(r):
        return (r * cos + pltpu.roll(r, HEAD_DIM - 32, 1) * sa + pltpu.roll(r, 32, 1) * sb)

    def proj(col, width):
        return _dot(nb_ref[...], w_ref[:, col:col + width])

    def heads(r, n_heads, fn, out_ref):
        for h in range(n_heads):
            out_ref[h] = fn(r[:, h * HEAD_DIM:(h + 1) * HEAD_DIM]).astype(BF16)

    def values_transposed(r, out_ref):
        for h in range(N_KV_HEADS):
            out_ref[h, :HEAD_DIM, :] = r[:, h * HEAD_DIM:(h + 1) * HEAD_DIM].T.astype(BF16)
            out_ref[h, HEAD_DIM:, :] = jnp.ones((VT_ROWS - HEAD_DIM, r.shape[0]), BF16)

    qa_scale = HEAD_DIM ** -0.5 * LOG2E
    qc_scale = DIFF_DIM ** -0.5 * LOG2E
    g_qa, g_ka, g_qb, g_kb = (gh_ref[i:i + 1, :] for i in range(4))
    g_qc, g_kc = gh_ref[4:5, :], gh_ref[5:6, :]
    col = 0
    heads(proj(col, Q_W), N_Q_HEADS, lambda r: rope(_head_norm(r, g_qa)) * qa_scale, qa_ref)
    col += Q_W
    heads(proj(col, KV_W), N_KV_HEADS, lambda r: rope(_head_norm(r, g_ka)), ka_ref)
    col += KV_W
    values_transposed(proj(col, KV_W),va_ref)
    col += KV_W
    heads(proj(col, Q_W), N_Q_HEADS, lambda r: _head_norm(r, g_qb) * qa_scale, qb_ref)
    col += Q_W
    heads(proj(col, KV_W), N_KV_HEADS, lambda r: _head_norm(r, g_kb), kb_ref)
    col += KV_W
    values_transposed(proj(col, KV_W), vb_ref)
    col += KV_W
    heads(proj(col, Q_W), N_Q_HEADS, lambda r: _half_norm(r, g_qc) * qc_scale, qc_ref)
    col += Q_W
    heads(proj(col, KV_W), N_KV_HEADS, lambda r: _half_norm(r, g_kc), kc_ref)
    col += KV_W
    values_transposed(proj(col, KV_W),vc_ref)


def _qkv(x, mod, g_norm, wqkv, rope_tabs, gh, *, layer, seq, tm):
    n = x.shape[0]
    tiles_per_seq = seq // tm
    q_shape = jax.ShapeDtypeStruct((N_Q_HEADS, n, HEAD_DIM), BF16)
    kv_shape = jax.ShapeDtypeStruct((N_KV_HEADS, n, HEAD_DIM), BF16)
    q_spec = pl.BlockSpec((N_Q_HEADS, tm, HEAD_DIM), lambda i: (0, i, 0))
    kv_spec = pl.BlockSpec((N_KV_HEADS, tm, HEAD_DIM), lambda i: (0, i, 0))
    tab_spec = pl.BlockSpec((tm, HEAD_DIM), lambda i: (i % tiles_per_seq, 0))
    v1_shape = jax.ShapeDtypeStruct((N_KV_HEADS, VT_ROWS, n), BF16)
    v1_spec = pl.BlockSpec((N_KV_HEADS, VT_ROWS, tm), lambda i: (0, 0, i))
    return pl.pallas_call(
        _qkv_kernel,
        out_shape=[q_shape, kv_shape, v1_shape] * 3,
        grid=(n // tm,),
        in_specs=[
            pl.BlockSpec((tm, D_MODEL), lambda i: (i, 0)),
            pl.BlockSpec((None, 1, 9, D_MODEL), lambda i: (layer, (i * tm) // seq, 0, 0)),
            pl.BlockSpec((None, 3, D_MODEL), lambda i: (layer, 0, 0)),
            pl.BlockSpec((None, D_MODEL, QKV_W), lambda i: (layer, 0, 0), pipeline_mode=pl.Buffered(1)),
            tab_spec, tab_spec, tab_spec,
            pl.BlockSpec((None, 8, HEAD_DIM), lambda i: (layer, 0, 0)),
        ],
        out_specs=[q_spec, kv_spec, v1_spec] * 3,
        scratch_shapes=[pltpu.VMEM((tm, D_MODEL), BF16)],
        compiler_params=_cparams("arbitrary"),
        name="qkv",
    )(x, mod, g_norm, wqkv, *rope_tabs, gh)


def _out_kernel(x_ref, mod_ref, g_ref, oa_ref, ob_ref, oc_ref, wg_ref, wo_ref, o_ref, mg_ref):
    x = x_ref[...]
    nb = _modulate(x, mod_ref, g_ref[1:2, :], 1).astype(BF16)
    for br, br_ref in enumerate((oa_ref, ob_ref, oc_ref)):
        gate = jax.nn.sigmoid(_dot(nb, wg_ref[:, br * D_MODEL:(br + 1) * D_MODEL]))
        for h in range(N_Q_HEADS):
            lanes = slice(h * HEAD_DIM, (h + 1) * HEAD_DIM)
            term = gate[:, lanes] * br_ref[h].astype(F32)
            if br == 0:
                mg_ref[:, lanes] = term
            else:
                mg_ref[:, lanes] += term
    o_ref[...] = x + mod_ref[0, 5:6, :] * _dot(mg_ref[...].astype(BF16), wo_ref[...])


def _out_proj(x, mod, g_norm, oa, ob, oc, wgate, wo, *, layer, seq, tm):
    n = x.shape[0]
    o_spec = pl.BlockSpec((N_Q_HEADS, tm, HEAD_DIM), lambda i: (0, i, 0))
    return pl.pallas_call(
        _out_kernel,
        out_shape=jax.ShapeDtypeStruct((n, D_MODEL), F32),
        grid=(n // tm,),
        in_specs=[
            pl.BlockSpec((tm, D_MODEL), lambda i: (i, 0)),
            pl.BlockSpec((None, 1, 9, D_MODEL), lambda i: (layer, (i * tm) // seq, 0, 0)),
            pl.BlockSpec((None, 3, D_MODEL), lambda i: (layer, 0, 0)),
            o_spec, o_spec, o_spec,
            pl.BlockSpec((None, D_MODEL, 3 * D_MODEL), lambda i: (layer, 0, 0), pipeline_mode=pl.Buffered(1)),
            pl.BlockSpec((None, D_MODEL, D_MODEL), lambda i: (layer, 0, 0), pipeline_mode=pl.Buffered(1)),
        ],
        out_specs=pl.BlockSpec((tm, D_MODEL), lambda i: (i, 0)),
        scratch_shapes=[pltpu.VMEM((tm, D_MODEL), F32)],
        input_output_aliases={0: 0},
        compiler_params=_cparams("arbitrary"),
        name="out_proj",
    )(x, mod, g_norm, oa, ob, oc, wgate, wo)


MIN_DENOM = 2.0 ** -64


def _key_slice(idx, tks):
    return pl.ds(pl.multiple_of(idx * tks, tks), tks)


def _score_bound(q, key_norm):
    qf = q.astype(F32)
    qn2 = _dot_nt(jnp.ones((F32_SUBLANES, q.shape[1]), F32), qf * qf)[:1]
    return jnp.sqrt(qn2) * key_norm


def _bounded_iteration(t, last, q, k_ref, v_ref, acct_ref, s_ref, *, unroll, tks, shift, bias_fn=None):
    acct = acct_ref[...]
    s = s_ref[...]
    for u in range(unroll):
        idx = t * unroll + u
        s_next = None if (last and u == unroll - 1) else _dot_nt(k_ref[_key_slice(idx + 1, tks), :], q)
        if bias_fn is not None:
            s = bias_fn(idx, s)
        acct = acct + _dot(v_ref[:, _key_slice(idx, tks)], jnp.exp2(s - shift).astype(BF16))
        s = s_next
    acct_ref[...] = acct
    if not last:
        s_ref[...] = s


def _exact_pass(q, k_ref, v_ref, m_ref, acct_ref, *, n_sub, tks, bias_fn=None):
    m_ref[...] = jnp.full_like(m_ref, -jnp.inf)
    acct_ref[...] = jnp.zeros_like(acct_ref)

    def body(idx, carry):
        s = _dot_nt(k_ref[_key_slice(idx, tks), :], q)
        if bias_fn is not None:
            s = bias_fn(idx, s)
        m = m_ref[...]
        m_new = jnp.maximum(m, jnp.max(s, axis=0, keepdims=True))
        p = jnp.exp2(s - m_new).astype(BF16)
        acct_ref[...] = jnp.exp2(m - m_new) * acct_ref[...] + _dot(v_ref[:, _key_slice(idx, tks)], p)
        m_ref[...] = m_new
        return carry

    lax.fori_loop(0, n_sub, body, 0)


def _denominators_ok(acct_ref):
    return jnp.min(acct_ref[HEAD_DIM:HEAD_DIM + 1, :]) >= MIN_DENOM


def _attn_a_kernel(q_ref, k_ref, v_ref, kn_ref, o_ref, m_ref, acct_ref, s_ref, *, n_iter, unroll, tks):
    rows = acct_ref.shape[1]
    q = q_ref[...].reshape(rows, HEAD_DIM)
    shift = _score_bound(q, kn_ref[:, :1])
    acct_ref[...] = jnp.zeros_like(acct_ref)
    s_ref[...] = _dot_nt(k_ref[0:tks, :], q)
    refs = (q, k_ref, v_ref, acct_ref, s_ref)

    def body(t, carry):
        _bounded_iteration(t, False, *refs, unroll=unroll, tks=tks, shift=shift)
        return carry

    lax.fori_loop(0, n_iter - 1, body, 0)
    _bounded_iteration(n_iter - 1, True, *refs, unroll=unroll, tks=tks, shift=shift)

    @pl.when(jnp.logical_not(_denominators_ok(acct_ref)))
    def _():
        _exact_pass(q, k_ref, v_ref, m_ref, acct_ref, n_sub=n_iter * unroll, tks=tks)

    acct = acct_ref[...]
    o_t = acct[:HEAD_DIM] / acct[HEAD_DIM:HEAD_DIM + 1]
    o_ref[...] = o_t.T.astype(o_ref.dtype).reshape(o_ref.shape)


def _attn_a(q, k, v1t, key_norm, *, layer, batch, seq, tq, tks, unroll):
    n = q.shape[1]
    nq = seq // tq
    rows = Q_PER_KV * tq
    q_map = lambda b, g, i: (g, b * nq + i, 0)
    return pl.pallas_call(
        functools.partial(_attn_a_kernel, n_iter=seq // (tks * unroll), unroll=unroll, tks=tks),
        out_shape=jax.ShapeDtypeStruct((N_Q_HEADS, n, HEAD_DIM), ATTN_OUT),
        grid=(batch, N_KV_HEADS, nq),
        in_specs=[
            pl.BlockSpec((Q_PER_KV, tq, HEAD_DIM), q_map),
            pl.BlockSpec((None, seq, HEAD_DIM), lambda b, g, i: (g, b, 0)),
            pl.BlockSpec((None, VT_ROWS, seq), lambda b, g, i: (g, 0, b)),
            pl.BlockSpec((None, 1, HEAD_DIM), lambda b, g, i: (layer, 0, 0)),
        ],
        out_specs=pl.BlockSpec((Q_PER_KV, tq, HEAD_DIM), q_map),
        scratch_shapes=[pltpu.VMEM((1, rows), F32), pltpu.VMEM((VT_ROWS, rows), F32),
                        pltpu.VMEM((tks, rows), F32)],
        compiler_params=_cparams("arbitrary", "arbitrary", "arbitrary"),
        name="attn_axial",
    )(q, k, v1t, key_norm)


def _attn_b_kernel(q_ref, kp_ref, kc_ref, kn_ref, vp_ref, vc_ref, vn_ref, bias_ref, sink_ref, o_ref,
                   *, nb, qb):
    i = pl.program_id(2)
    rows = Q_PER_KV * BLOCK_Q
    kwin = jnp.concatenate([kp_ref[...], kc_ref[...], kn_ref[...]], axis=0)
    vwin = jnp.concatenate([vp_ref[...], vc_ref[...], vn_ref[...]], axis=1)
    bias = bias_ref[...]
    sink = sink_ref[...]
    for u in range(qb):
        blk = i * qb + u
        q = q_ref[:, u * BLOCK_Q:(u + 1) * BLOCK_Q, :].reshape(rows, HEAD_DIM)
        s = _dot_nt(kwin[u * BLOCK_Q:u * BLOCK_Q + NEAR], q) + bias
        s = jnp.concatenate([jnp.where(blk > 0, s[:BLOCK_Q], NEG), s[BLOCK_Q:2 * BLOCK_Q],
                             jnp.where(blk < nb - 1, s[2 * BLOCK_Q:], NEG)], axis=0)
        m = jnp.maximum(jnp.max(s, axis=0, keepdims=True), sink)
        p = jnp.exp2(s - m).astype(BF16)
        acct = _dot(vwin[:, u * BLOCK_Q:u * BLOCK_Q + NEAR], p)
        den = acct[HEAD_DIM:HEAD_DIM + 1] + jnp.exp2(sink - m)
        o_t = acct[:HEAD_DIM] / den
        o_ref[:, u * BLOCK_Q:(u + 1) * BLOCK_Q, :] = o_t.T.astype(o_ref.dtype).reshape(
            Q_PER_KV, BLOCK_Q, HEAD_DIM)


def _attn_b(q, k, v1t, bias_t, sink_t, *, layer, batch, seq, qb):
    n = q.shape[1]
    nb = seq // BLOCK_Q
    steps = nb // qb
    rows = Q_PER_KV * BLOCK_Q
    cur_map = lambda b, g, i: (g, b * steps + i, 0)
    prev_map = lambda b, g, i: (g, b * nb + jnp.maximum(i * qb - 1, 0), 0)
    next_map = lambda b, g, i: (g, b * nb + jnp.minimum(i * qb + qb, nb - 1), 0)
    t = lambda index_map: (lambda b, g, i: (index_map(b, g, i)[0], 0, index_map(b, g, i)[1]))
    k_specs = [pl.BlockSpec((None, BLOCK_Q, HEAD_DIM), prev_map),
               pl.BlockSpec((None, qb * BLOCK_Q, HEAD_DIM), cur_map),
               pl.BlockSpec((None, BLOCK_Q, HEAD_DIM), next_map)]
    v_specs = [pl.BlockSpec((None, VT_ROWS, BLOCK_Q), t(prev_map)),
               pl.BlockSpec((None, VT_ROWS, qb * BLOCK_Q), t(cur_map)),
               pl.BlockSpec((None, VT_ROWS, BLOCK_Q), t(next_map))]
    return pl.pallas_call(
        functools.partial(_attn_b_kernel, nb=nb, qb=qb),
        out_shape=jax.ShapeDtypeStruct((N_Q_HEADS, n, HEAD_DIM), ATTN_OUT),
        grid=(batch, N_KV_HEADS, steps),
        in_specs=[pl.BlockSpec((Q_PER_KV, qb * BLOCK_Q, HEAD_DIM), cur_map)] + k_specs + v_specs + [
            pl.BlockSpec((None, NEAR, rows), lambda b, g, i: (g, 0, 0)),
            pl.BlockSpec((None, None, 1, rows), lambda b, g, i: (layer, g, 0, 0)),
        ],
        out_specs=pl.BlockSpec((Q_PER_KV, qb * BLOCK_Q, HEAD_DIM), cur_map),
        compiler_params=_cparams("arbitrary", "arbitrary", "arbitrary"),
        name="attn_window",
    )(q, k, k, k, v1t, v1t, v1t, bias_t, sink_t)


def _attn_c_kernel(q_ref, k_ref, v_ref, bias_ref, cmax_ref, kn_ref, lam_ref, gs_ref, o_ref,
                   m_ref, acct_ref, s_ref, *, n_iter, unroll, tks, out_scale):
    n_blk = pl.program_id(2)
    rows = Q_PER_KV * BLOCK_Q
    blocks_per_sub = tks // BLOCK_Q

    q = q_ref[...].reshape(rows, HEAD_DIM)
    lo = lax.broadcasted_iota(jnp.int32, q.shape, 1) < DIFF_DIM
    zero = jnp.zeros_like(q)
    q2 = jnp.concatenate([jnp.where(lo, q, zero), jnp.where(lo, zero, q)], axis=0)
    acct_ref[...] = jnp.zeros_like(acct_ref)
    s_ref[...] = _dot_nt(k_ref[0:tks, :], q2)
    refs = (q2, k_ref, v_ref, acct_ref, s_ref)

    c_max = cmax_ref[...]
    shift = _score_bound(q2, kn_ref[:, :1]) + jnp.concatenate([c_max, c_max], axis=1)

    def add_bias(idx, s):
        tile_of = lambda key_blk: jnp.clip(key_blk - n_blk + 2, 0, NEAR // BLOCK_Q + 1)
        bias = jnp.concatenate([bias_ref[tile_of(idx * blocks_per_sub + kb)] for kb in range(blocks_per_sub)],
                               axis=0)
        return jnp.concatenate([s[:, :rows] + bias, s[:, rows:] + bias], axis=1)

    def body(t, carry):
        _bounded_iteration(t, False, *refs, unroll=unroll, tks=tks, shift=shift, bias_fn=add_bias)
        return carry

    lax.fori_loop(0, n_iter - 1, body, 0)
    _bounded_iteration(n_iter - 1, True, *refs, unroll=unroll, tks=tks, shift=shift, bias_fn=add_bias)

    @pl.when(jnp.logical_not(_denominators_ok(acct_ref)))
    def _():
        _exact_pass(q2, k_ref, v_ref, m_ref, acct_ref, n_sub=n_iter * unroll, tks=tks, bias_fn=add_bias)

    acct = acct_ref[...]
    o1_t = acct[:HEAD_DIM, :rows] / acct[HEAD_DIM:HEAD_DIM + 1, :rows]
    o2_t = acct[:HEAD_DIM, rows:] / acct[HEAD_DIM:HEAD_DIM + 1, rows:]
    o_t = o1_t - lam_ref[:, :1] * o2_t
    ms = jnp.mean(o_t * o_t, axis=0, keepdims=True)
    o = (o_t * lax.rsqrt(ms + EPS)).T * gs_ref[...] * out_scale
    o_ref[...] = o.astype(o_ref.dtype).reshape(o_ref.shape)


def _attn_c(q, k, v1t, bias_t, cmax_t, key_norm, lam, g_subln, *, layer, batch, seq, tks, unroll):
    n = q.shape[1]
    nq = seq // BLOCK_Q
    rows = Q_PER_KV * BLOCK_Q
    lam_init = 0.8 - 0.6 * math.exp(-0.3 * layer)
    q_map = lambda b, g, i: (g, b * nq + i, 0)
    return pl.pallas_call(
        functools.partial(_attn_c_kernel, n_iter=seq // (tks * unroll), unroll=unroll, tks=tks,
                          out_scale=1.0 - lam_init),
        out_shape=jax.ShapeDtypeStruct((N_Q_HEADS, n, HEAD_DIM), ATTN_OUT),
        grid=(batch, N_KV_HEADS, nq),
        in_specs=[
            pl.BlockSpec((Q_PER_KV, BLOCK_Q, HEAD_DIM), q_map),
            pl.BlockSpec((None, seq, HEAD_DIM), lambda b, g, i: (g, b, 0)),
            pl.BlockSpec((None, VT_ROWS, seq), lambda b, g, i: (g, 0, b)),
            pl.BlockSpec((None, NEAR // BLOCK_Q + 2, BLOCK_Q, rows), lambda b, g, i: (g, 0, 0, 0)),
            pl.BlockSpec((None, 1, rows), lambda b, g, i: (g, 0, 0)),
            pl.BlockSpec((None, 1, HEAD_DIM), lambda b, g, i: (layer, 0, 0)),
            pl.BlockSpec((None, 1, HEAD_DIM), lambda b, g, i: (layer, 0, 0)),
            pl.BlockSpec((None, 1, HEAD_DIM), lambda b, g, i: (layer, 0, 0)),
        ],
        out_specs=pl.BlockSpec((Q_PER_KV, BLOCK_Q, HEAD_DIM), q_map),
        scratch_shapes=[pltpu.VMEM((1, 2 * rows), F32), pltpu.VMEM((VT_ROWS, 2 * rows), F32),
                        pltpu.VMEM((tks, 2 * rows), F32)],
        compiler_params=_cparams("arbitrary", "arbitrary", "arbitrary"),
        name="attn_diff",
    )(q, k, v1t, bias_t, cmax_t, key_norm, lam, g_subln)


def _t5_bucket_np(rel):
    half = N_BUCKETS // 2
    max_exact = half // 2
    ret = np.where(rel > 0, half, 0)
    n = np.abs(rel)
    ratio = np.log(np.maximum(n, 1).astype(np.float32) / np.float32(max_exact)) / np.float32(
        math.log(MAX_DISTANCE / max_exact))
    large = max_exact + (ratio * np.float32(half - max_exact)).astype(np.int32)
    large = np.minimum(large, half - 1)
    return (ret + np.where(n < max_exact, n, large)).astype(np.int32)


def _near_bias(table):
    r = np.arange(BLOCK_Q)[:, None]
    c = np.arange(NEAR)[None, :]
    bucket = _t5_bucket_np(c - BLOCK_Q - r)
    onehot = (bucket.reshape(-1, 1) == np.arange(N_BUCKETS)[None, :]).astype(np.float32)
    rows = jnp.dot(jnp.asarray(onehot), table.astype(F32), precision=lax.Precision.HIGHEST)
    return rows.reshape(BLOCK_Q, NEAR, table.shape[1]).transpose(2, 0, 1)


def _rope_tables(seq):
    rows = seq // GRID_W
    nfreq = HEAD_DIM // 4
    inv = ROPE_THETA ** (-jnp.arange(nfreq, dtype=F32) / nfreq)
    ang_r = jnp.arange(rows).astype(F32)[:, None] * inv
    ang_c = jnp.arange(GRID_W).astype(F32)[:, None] * inv
    by_row = lambda t: jnp.broadcast_to(t[:, None, :], (rows, GRID_W, nfreq)).reshape(seq, nfreq)
    by_col = lambda t: jnp.broadcast_to(t[None, :, :], (rows, GRID_W, nfreq)).reshape(seq, nfreq)
    table = lambda fn: jnp.concatenate([by_row(fn(ang_r))] * 2 + [by_col(fn(ang_c))] * 2, axis=-1)
    cos, sin = table(jnp.cos), table(jnp.sin)
    first = (np.arange(HEAD_DIM) % (HEAD_DIM // 2)) < HEAD_DIM // 4
    return cos, jnp.where(first, -sin, 0.0), jnp.where(first, 0.0, sin)


def _trunk(x, mod, p, *, batch, seq):
    t = _tiles(seq)
    tm = t["tm"]
    rope_tabs = _rope_tables(seq)
    for l in range(DEPTH):
        x = _ffn(x, mod, p["g_norm"], p["wgu"], p["wout"], layer=l, which=0, seq=seq, tm=tm)
        qa, ka, va, qb, kb, vb, qc, kc, vc = _qkv(x, mod, p["g_norm"], p["wqkv"], rope_tabs, p["gh"],
                                                  layer=l, seq=seq, tm=tm)
        oa = _attn_a(qa, ka, va, p["kn_a"], layer=l, batch=batch, seq=seq, tq=t["tq_a"], tks=t["tks"],
                     unroll=t["unroll"])
        ob = _attn_b(qb, kb, vb, p["bias_b"], p["sink"], layer=l, batch=batch, seq=seq, qb=t["qb"])
        oc = _attn_c(qc, kc, vc, p["bias_c"], p["cmax_c"], p["kn_c"], p["lam"], p["g_subln"],
                     layer=l, batch=batch, seq=seq, tks=t["tks"], unroll=t["unroll"])
        x = _out_proj(x, mod, p["g_norm"], oa, ob, oc, p["wgate"], p["wo"], layer=l, seq=seq,
                      tm=t["tm_out"])
        x = _ffn(x, mod, p["g_norm"], p["wgu"], p["wout"], layer=l, which=1, seq=seq, tm=tm)
    return x


def _prepare(w_ff_in, w_ff_out, w_in, w_o, g_qa, g_ka, g_qb, g_kb, g_qc, g_kc, sink,
             lam_q1, lam_k1, lam_q2, lam_k2, g_subln, rel_bias):
    wg = w_ff_in[..., :D_FF].reshape(DEPTH, 2, D_MODEL, N_FF_CHUNKS, FF_CHUNK)
    wu = w_ff_in[..., D_FF:].reshape(DEPTH, 2, D_MODEL, N_FF_CHUNKS, FF_CHUNK)
    wgu = jnp.concatenate([wg, wu], axis=-1).transpose(0, 1, 3, 2, 4).astype(BF16)
    wout = w_ff_out.reshape(DEPTH, 2, N_FF_CHUNKS, FF_CHUNK, D_MODEL).astype(BF16)
    zeros = jnp.zeros_like(g_qa)
    gh = jnp.stack([g_qa, g_ka, g_qb, g_kb, jnp.tile(g_qc, (1, 2)), jnp.tile(g_kc, (1, 2)), zeros, zeros],
                   axis=1).astype(F32)
    table_b, table_c = rel_bias[:, :N_Q_HEADS], rel_bias[:, N_Q_HEADS:]
    half = N_BUCKETS // 2
    c_left, c_right = table_c[half - 1] * LOG2E, table_c[N_BUCKETS - 1] * LOG2E
    lam_init = jnp.asarray([0.8 - 0.6 * math.exp(-0.3 * l) for l in range(DEPTH)], F32)
    lam = (jnp.exp(jnp.sum(lam_q1.astype(F32) * lam_k1.astype(F32), axis=-1))
           - jnp.exp(jnp.sum(lam_q2.astype(F32) * lam_k2.astype(F32), axis=-1)) + lam_init)

    band = np.abs(np.arange(NEAR)[None, :] - BLOCK_Q - np.arange(BLOCK_Q)[:, None]) <= WINDOW
    near_c = (_near_bias(table_c) * LOG2E).reshape(
        N_KV_HEADS, Q_PER_KV, BLOCK_Q, NEAR // BLOCK_Q, BLOCK_Q).transpose(0, 3, 4, 1, 2).reshape(
        N_KV_HEADS, NEAR // BLOCK_Q, BLOCK_Q, Q_PER_KV * BLOCK_Q)
    cfar_c = jnp.repeat(jnp.stack([c_left, c_right, jnp.max(table_c, axis=0) * LOG2E]).astype(F32),
                        BLOCK_Q, axis=1).reshape(3, N_KV_HEADS, 1, Q_PER_KV * BLOCK_Q)

    def key_norm(g, dim):
        bound = BOUND_MARGIN * math.sqrt(dim) * jnp.max(jnp.abs(g.astype(F32)), axis=-1)
        return jnp.broadcast_to(bound[:, None, None], (DEPTH, 1, HEAD_DIM))

    return dict(
        wgu=wgu, wout=wout,
        wqkv=w_in[:, :, :QKV_W].astype(BF16), wgate=w_in[:, :, QKV_W:].astype(BF16), wo=w_o.astype(BF16),
        gh=gh,
        bias_b=jnp.where(band, _near_bias(table_b) * LOG2E, NEG).reshape(
            N_KV_HEADS, Q_PER_KV, BLOCK_Q, NEAR).transpose(0, 3, 1, 2).reshape(
            N_KV_HEADS, NEAR, Q_PER_KV * BLOCK_Q),
        sink=jnp.repeat(sink.astype(F32) * LOG2E, BLOCK_Q, axis=1).reshape(
            DEPTH, N_KV_HEADS, 1, Q_PER_KV * BLOCK_Q),
        bias_c=jnp.concatenate([cfar_c[0][:, None] + jnp.zeros((1, 1, BLOCK_Q, 1), F32), near_c,
                                cfar_c[1][:, None] + jnp.zeros((1, 1, BLOCK_Q, 1), F32)], axis=1),
        cmax_c=cfar_c[2],
        kn_a=key_norm(g_ka, HEAD_DIM), kn_c=key_norm(g_kc, DIFF_DIM),
        lam=jnp.broadcast_to(lam[:, None, None], (DEPTH, 1, HEAD_DIM)),
        g_subln=g_subln.astype(F32)[:, None, :],
    )


def kernel(x_prompt, x_sample, c_prompt, c_sample, w_ada, b_ada, g_norm, w_ff_in, w_ff_out, w_in, w_o,
           g_qa, g_ka, g_qb, g_kb, g_qc, g_kc, sink, lam_q1, lam_k1, lam_q2, lam_k2, g_subln, rel_bias):
    p = _prepare(w_ff_in, w_ff_out, w_in, w_o, g_qa, g_ka, g_qb, g_kb, g_qc, g_kc, sink,
                 lam_q1, lam_k1, lam_q2, lam_k2, g_subln, rel_bias)
    p["g_norm"] = g_norm.astype(F32)
    outs = []
    n_cond = 0
    conds = [c_prompt, c_sample]
    c_all = jnp.concatenate(conds + [jnp.zeros((ADA_ROWS - sum(c.shape[0] for c in conds), D_MODEL), F32)])
    mod_all = _ada(c_all, w_ada, b_ada)
    for x, c in ((x_prompt, c_prompt), (x_sample, c_sample)):
        batch, seq, _ = x.shape
        mod = mod_all[:, n_cond:n_cond + batch].reshape(DEPTH, batch, 9, D_MODEL)
        n_cond += batch
        y = _trunk(x.reshape(batch * seq, D_MODEL), mod, p, batch=batch, seq=seq)
        outs.append(y.reshape(batch, seq, D_MODEL))
    return tuple(outs)
```

```python
import functools
import math

import numpy as np
import jax
import jax.numpy as jnp
from jax import lax
from jax.experimental import pallas as pl
from jax.experimental.pallas import tpu as pltpu

F32 = jnp.float32
BF16 = jnp.bfloat16
ATTN_OUT = jnp.bfloat16

D_MODEL = 1024
DEPTH = 4
N_Q_HEADS = 8
N_KV_HEADS = 2
Q_PER_KV = N_Q_HEADS // N_KV_HEADS
HEAD_DIM = 128
DIFF_DIM = 64
D_FF = 2816
BLOCK_Q = 128
WINDOW = 128
GRID_W = 64
N_BUCKETS = 32
MAX_DISTANCE = 128
ROPE_THETA = 10000.0
EPS = 1e-6
NEG = -1e30
LOG2E = 1.4426950408889634
QA_SCALE = HEAD_DIM ** -0.5 * LOG2E
QC_SCALE = DIFF_DIM ** -0.5 * LOG2E

Q_W = N_Q_HEADS * HEAD_DIM
KV_W = N_KV_HEADS * HEAD_DIM
BRANCH_IN = Q_W + 2 * KV_W
QKV_W = 3 * BRANCH_IN
MXU_TILE = 256
FF_CHUNK = MXU_TILE
N_FF_CHUNKS = D_FF // FF_CHUNK
ADA_ROWS = 16
ADA_TN = 1536
NEAR = 3 * BLOCK_Q
VT_ROWS = HEAD_DIM + 16

F32_SUBLANES = 8
ROW_TILE = 512
STEPS_PER_REGION = 32
BOUND_MARGIN = 1.01

VMEM_LIMIT = 56 * 1024 * 1024


def _tiles(seq):
    tks = min(MXU_TILE, seq)
    return dict(
        tm=min(ROW_TILE, seq),
        tm_out=min(2 * ROW_TILE, seq),
        tq_a=2 * BLOCK_Q,
        tks=tks,
        qb=min(STEPS_PER_REGION, seq // BLOCK_Q),
        unroll=min(STEPS_PER_REGION, seq // tks),
    )


def _cparams(*sem):
    return pltpu.CompilerParams(dimension_semantics=sem, vmem_limit_bytes=VMEM_LIMIT)


def _dot(a, b):
    return jnp.dot(a, b, preferred_element_type=F32)


def _dot_nt(a, b):
    return lax.dot_general(a, b, (((1,), (1,)), ((), ())), preferred_element_type=F32)


def _ada_kernel(c_ref, w_ref, b_ref, o_ref):
    c = c_ref[...]
    a = (c * jax.nn.sigmoid(c)).astype(BF16)
    o_ref[0] = _dot(a, w_ref[0].astype(BF16)) + b_ref[0]


def _ada(c_all, w_ada, b_ada):
    n_out = w_ada.shape[-1]
    return pl.pallas_call(
        _ada_kernel,
        out_shape=jax.ShapeDtypeStruct((DEPTH, ADA_ROWS, n_out), F32),
        grid=(DEPTH, n_out // ADA_TN),
        in_specs=[
            pl.BlockSpec((ADA_ROWS, D_MODEL), lambda l, j: (0, 0)),
            pl.BlockSpec((1, D_MODEL, ADA_TN), lambda l, j: (l, 0, j)),
            pl.BlockSpec((1, 1, ADA_TN), lambda l, j: (l, 0, j)),
        ],
        out_specs=pl.BlockSpec((1, ADA_ROWS, ADA_TN), lambda l, j: (l, 0, j)),
        compiler_params=_cparams("arbitrary", "arbitrary"),
        name="ada",
    )(c_all, w_ada, b_ada.reshape(DEPTH, 1, n_out))


def _modulate(x, mod_ref, g, jj):
    ms = jnp.mean(x * x, axis=-1, keepdims=True)
    y = x * lax.rsqrt(ms + EPS) * g
    return y * (1.0 + mod_ref[0, 3 * jj + 1:3 * jj + 2, :]) + mod_ref[0, 3 * jj:3 * jj + 1, :]


def _ffn_kernel(x_ref, mod_ref, g_ref, wgu_ref, wout_ref, o_ref, nb_ref, acc_ref, *, jj):
    x = x_ref[...]
    nb_ref[...] = _modulate(x, mod_ref, g_ref[jj:jj + 1, :], jj).astype(BF16)
    acc_ref[...] = jnp.zeros_like(acc_ref)

    def body(c, carry):
        h = _dot(nb_ref[...], wgu_ref[c])
        hg = h[:, :FF_CHUNK]
        a = (hg * jax.nn.sigmoid(hg)) * h[:, FF_CHUNK:]
        acc_ref[...] += _dot(a.astype(BF16), wout_ref[c])
        return carry

    lax.fori_loop(0, N_FF_CHUNKS, body, 0, unroll=True)
    o_ref[...] = x + (0.5 * mod_ref[0, 3 * jj + 2:3 * jj + 3, :]) * acc_ref[...]


def _ffn(x, mod, g_norm, wgu, wout, *, layer, which, seq, tm):
    n = x.shape[0]
    jj = 2 * which
    const = dict(pipeline_mode=pl.Buffered(1))
    return pl.pallas_call(
        functools.partial(_ffn_kernel, jj=jj),
        out_shape=jax.ShapeDtypeStruct((n, D_MODEL), F32),
        grid=(n // tm,),
        in_specs=[
            pl.BlockSpec((tm, D_MODEL), lambda i: (i, 0)),
            pl.BlockSpec((None, 1, 9, D_MODEL), lambda i: (layer, (i * tm) // seq, 0, 0)),
            pl.BlockSpec((None, 3, D_MODEL), lambda i: (layer, 0, 0)),
            pl.BlockSpec((None, None, N_FF_CHUNKS, D_MODEL, 2 * FF_CHUNK),
                         lambda i: (layer, which, 0, 0, 0), **const),
            pl.BlockSpec((None, None, N_FF_CHUNKS, FF_CHUNK, D_MODEL),
                         lambda i: (layer, which, 0, 0, 0), **const),
        ],
        out_specs=pl.BlockSpec((tm, D_MODEL), lambda i: (i, 0)),
        scratch_shapes=[pltpu.VMEM((tm, D_MODEL), BF16), pltpu.VMEM((tm, D_MODEL), F32)],
        input_output_aliases={} if (layer == 0 and which == 0) else {0: 0},
        compiler_params=_cparams("arbitrary"),
        name=f"ffn{which}",
    )(x, mod, g_norm, wgu, wout)


def _head_norm(r, g):
    ms = jnp.mean(r * r, axis=-1, keepdims=True)
    return r * lax.rsqrt(ms + EPS) * g


def _half_norm(r, g2):
    sq = r * r
    lo = lax.broadcasted_iota(jnp.int32, r.shape, 1) < DIFF_DIM
    s_lo = jnp.sum(jnp.where(lo, sq, 0.0), axis=-1, keepdims=True)
    s_hi = jnp.sum(jnp.where(lo, 0.0, sq), axis=-1, keepdims=True)
    ms = jnp.where(lo, s_lo, s_hi) * (1.0 / DIFF_DIM)
    return r * lax.rsqrt(ms + EPS) * g2


def _qkv_kernel(x_ref, mod_ref, g_ref, w_ref, cos_ref, sa_ref, sb_ref, gh_ref,
                qa_ref, ka_ref, va_ref, qb_ref, kb_ref, vb_ref, qc_ref, kc_ref, vc_ref, nb_ref):
    nb_ref[...] = _modulate(x_ref[...], mod_ref, g_ref[1:2, :], 1).astype(BF16)
    cos, sa, sb = cos_ref[...], sa_ref[...], sb_ref[...]

    def rope(r):
        return (r * cos + pltpu.roll(r, HEAD_DIM - 32, 1) * sa + pltpu.roll(r, 32, 1) * sb)

    def proj(col, width):
        return _dot(nb_ref[...], w_ref[:, col:col + width])

    def heads(r, n_heads, fn, out_ref):
        for h in range(n_heads):
            out_ref[h] = fn(r[:, h * HEAD_DIM:(h + 1) * HEAD_DIM]).astype(BF16)

    def values_transposed(r, out_ref):
        for h in range(N_KV_HEADS):
            out_ref[h, :HEAD_DIM, :] = r[:, h * HEAD_DIM:(h + 1) * HEAD_DIM].T.astype(BF16)
            out_ref[h, HEAD_DIM:, :] = jnp.ones((VT_ROWS - HEAD_DIM, r.shape[0]), BF16)

    qa_scale, qc_scale = QA_SCALE, QC_SCALE
    g_qa, g_ka, g_qb, g_kb = (gh_ref[i:i + 1, :] for i in range(4))
    g_qc, g_kc = gh_ref[4:5, :], gh_ref[5:6, :]
    col = 0
    heads(proj(col, Q_W), N_Q_HEADS, lambda r: rope(_head_norm(r, g_qa)) * qa_scale, qa_ref)
    col += Q_W
    heads(proj(col, KV_W), N_KV_HEADS, lambda r: rope(_head_norm(r, g_ka)), ka_ref)
    col += KV_W
    values_transposed(proj(col, KV_W),va_ref)
    col += KV_W
    heads(proj(col, Q_W), N_Q_HEADS, lambda r: _head_norm(r, g_qb) * qa_scale, qb_ref)
    col += Q_W
    heads(proj(col, KV_W), N_KV_HEADS, lambda r: _head_norm(r, g_kb), kb_ref)
    col += KV_W
    values_transposed(proj(col, KV_W), vb_ref)
    col += KV_W
    heads(proj(col, Q_W), N_Q_HEADS, lambda r: _half_norm(r, g_qc) * qc_scale, qc_ref)
    col += Q_W
    heads(proj(col, KV_W), N_KV_HEADS, lambda r: _half_norm(r, g_kc), kc_ref)
    col += KV_W
    values_transposed(proj(col, KV_W),vc_ref)


def _qkv(x, mod, g_norm, wqkv, rope_tabs, gh, *, layer, seq, tm):
    n = x.shape[0]
    tiles_per_seq = seq // tm
    q_shape = jax.ShapeDtypeStruct((N_Q_HEADS, n, HEAD_DIM), BF16)
    kv_shape = jax.ShapeDtypeStruct((N_KV_HEADS, n, HEAD_DIM), BF16)
    q_spec = pl.BlockSpec((N_Q_HEADS, tm, HEAD_DIM), lambda i: (0, i, 0))
    kv_spec = pl.BlockSpec((N_KV_HEADS, tm, HEAD_DIM), lambda i: (0, i, 0))
    tab_spec = pl.BlockSpec((tm, HEAD_DIM), lambda i: (i % tiles_per_seq, 0))
    v1_shape = jax.ShapeDtypeStruct((N_KV_HEADS, VT_ROWS, n), BF16)
    v1_spec = pl.BlockSpec((N_KV_HEADS, VT_ROWS, tm), lambda i: (0, 0, i))
    return pl.pallas_call(
        _qkv_kernel,
        out_shape=[q_shape, kv_shape, v1_shape] * 3,
        grid=(n // tm,),
        in_specs=[
            pl.BlockSpec((tm, D_MODEL), lambda i: (i, 0)),
            pl.BlockSpec((None, 1, 9, D_MODEL), lambda i: (layer, (i * tm) // seq, 0, 0)),
            pl.BlockSpec((None, 3, D_MODEL), lambda i: (layer, 0, 0)),
            pl.BlockSpec((None, D_MODEL, QKV_W), lambda i: (layer, 0, 0), pipeline_mode=pl.Buffered(1)),
            tab_spec, tab_spec, tab_spec,
            pl.BlockSpec((None, 8, HEAD_DIM), lambda i: (layer, 0, 0)),
        ],
        out_specs=[q_spec, kv_spec, v1_spec] * 3,
        scratch_shapes=[pltpu.VMEM((tm, D_MODEL), BF16)],
        compiler_params=_cparams("arbitrary"),
        name="qkv",
    )(x, mod, g_norm, wqkv, *rope_tabs, gh)


def _out_kernel(x_ref, mod_ref, g_ref, oa_ref, ob_ref, oc_ref, wg_ref, wo_ref, o_ref, mg_ref):
    x = x_ref[...]
    nb = _modulate(x, mod_ref, g_ref[1:2, :], 1).astype(BF16)
    for br, br_ref in enumerate((oa_ref, ob_ref, oc_ref)):
        gate = jax.nn.sigmoid(_dot(nb, wg_ref[:, br * D_MODEL:(br + 1) * D_MODEL]))
        for h in range(N_Q_HEADS):
            lanes = slice(h * HEAD_DIM, (h + 1) * HEAD_DIM)
            term = gate[:, lanes] * br_ref[h].astype(F32)
            if br == 0:
                mg_ref[:, lanes] = term
            else:
                mg_ref[:, lanes] += term
    o_ref[...] = x + mod_ref[0, 5:6, :] * _dot(mg_ref[...].astype(BF16), wo_ref[...])


def _out_proj(x, mod, g_norm, oa, ob, oc, wgate, wo, *, layer, seq, tm):
    n = x.shape[0]
    o_spec = pl.BlockSpec((N_Q_HEADS, tm, HEAD_DIM), lambda i: (0, i, 0))
    return pl.pallas_call(
        _out_kernel,
        out_shape=jax.ShapeDtypeStruct((n, D_MODEL), F32),
        grid=(n // tm,),
        in_specs=[
            pl.BlockSpec((tm, D_MODEL), lambda i: (i, 0)),
            pl.BlockSpec((None, 1, 9, D_MODEL), lambda i: (layer, (i * tm) // seq, 0, 0)),
            pl.BlockSpec((None, 3, D_MODEL), lambda i: (layer, 0, 0)),
            o_spec, o_spec, o_spec,
            pl.BlockSpec((None, D_MODEL, 3 * D_MODEL), lambda i: (layer, 0, 0), pipeline_mode=pl.Buffered(1)),
            pl.BlockSpec((None, D_MODEL, D_MODEL), lambda i: (layer, 0, 0), pipeline_mode=pl.Buffered(1)),
        ],
        out_specs=pl.BlockSpec((tm, D_MODEL), lambda i: (i, 0)),
        scratch_shapes=[pltpu.VMEM((tm, D_MODEL), F32)],
        input_output_aliases={0: 0},
        compiler_params=_cparams("arbitrary"),
        name="out_proj",
    )(x, mod, g_norm, oa, ob, oc, wgate, wo)


MIN_DENOM = 2.0 ** -64
SAFE_SPAN = 60.0


def _key_slice(idx, tks):
    return pl.ds(pl.multiple_of(idx * tks, tks), tks)


def _score_bound(q, key_norm):
    qf = q.astype(F32)
    qn2 = _dot_nt(jnp.ones((F32_SUBLANES, q.shape[1]), F32), qf * qf)[:1]
    return jnp.sqrt(qn2) * key_norm


def _bounded_iteration(t, last, q, k_ref, v_ref, acct_ref, s_ref, *, unroll, tks, shift, bias_fn=None):
    acct = acct_ref[...]
    s = s_ref[...]
    for u in range(unroll):
        idx = t * unroll + u
        s_next = None if (last and u == unroll - 1) else _dot_nt(k_ref[_key_slice(idx + 1, tks), :], q)
        if bias_fn is not None:
            s = bias_fn(idx, s)
        acct = acct + _dot(v_ref[:, _key_slice(idx, tks)], jnp.exp2(s - shift).astype(BF16))
        s = s_next
    acct_ref[...] = acct
    if not last:
        s_ref[...] = s


def _exact_pass(q, k_ref, v_ref, m_ref, acct_ref, *, n_sub, tks, bias_fn=None):
    m_ref[...] = jnp.full_like(m_ref, -jnp.inf)
    acct_ref[...] = jnp.zeros_like(acct_ref)

    def body(idx, carry):
        s = _dot_nt(k_ref[_key_slice(idx, tks), :], q)
        if bias_fn is not None:
            s = bias_fn(idx, s)
        m = m_ref[...]
        m_new = jnp.maximum(m, jnp.max(s, axis=0, keepdims=True))
        p = jnp.exp2(s - m_new).astype(BF16)
        acct_ref[...] = jnp.exp2(m - m_new) * acct_ref[...] + _dot(v_ref[:, _key_slice(idx, tks)], p)
        m_ref[...] = m_new
        return carry

    lax.fori_loop(0, n_sub, body, 0)


def _recompute_if_flushed(check, acct_ref, exact_pass):
    @pl.when(check != 0)
    def _():
        @pl.when(jnp.min(acct_ref[HEAD_DIM:HEAD_DIM + 1, :]) < MIN_DENOM)
        def _():
            exact_pass()


def _attn_a_kernel(check_ref, q_ref, k_ref, v_ref, kn_ref, o_ref, m_ref, acct_ref, s_ref,
                   *, layer, n_iter, unroll, tks):
    rows = acct_ref.shape[1]
    q = q_ref[...].reshape(rows, HEAD_DIM)
    shift = _score_bound(q, kn_ref[:, :1])
    acct_ref[...] = jnp.zeros_like(acct_ref)
    s_ref[...] = _dot_nt(k_ref[0:tks, :], q)
    refs = (q, k_ref, v_ref, acct_ref, s_ref)

    def body(t, carry):
        _bounded_iteration(t, False, *refs, unroll=unroll, tks=tks, shift=shift)
        return carry

    lax.fori_loop(0, n_iter - 1, body, 0)
    _bounded_iteration(n_iter - 1, True, *refs, unroll=unroll, tks=tks, shift=shift)

    _recompute_if_flushed(check_ref[layer], acct_ref, lambda: _exact_pass(
        q, k_ref, v_ref, m_ref, acct_ref, n_sub=n_iter * unroll, tks=tks))

    acct = acct_ref[...]
    o_t = acct[:HEAD_DIM] / acct[HEAD_DIM:HEAD_DIM + 1]
    o_ref[...] = o_t.T.astype(o_ref.dtype).reshape(o_ref.shape)


def _attn_a(check, q, k, v1t, key_norm, *, layer, batch, seq, tq, tks, unroll):
    n = q.shape[1]
    nq = seq // tq
    rows = Q_PER_KV * tq
    q_map = lambda b, g, i: (g, b * nq + i, 0)
    return pl.pallas_call(
        functools.partial(_attn_a_kernel, layer=layer, n_iter=seq // (tks * unroll), unroll=unroll, tks=tks),
        out_shape=jax.ShapeDtypeStruct((N_Q_HEADS, n, HEAD_DIM), ATTN_OUT),
        grid=(batch, N_KV_HEADS, nq),
        in_specs=[
            pl.BlockSpec(memory_space=pltpu.SMEM),
            pl.BlockSpec((Q_PER_KV, tq, HEAD_DIM), q_map),
            pl.BlockSpec((None, seq, HEAD_DIM), lambda b, g, i: (g, b, 0)),
            pl.BlockSpec((None, VT_ROWS, seq), lambda b, g, i: (g, 0, b)),
            pl.BlockSpec((None, 1, HEAD_DIM), lambda b, g, i: (layer, 0, 0)),
        ],
        out_specs=pl.BlockSpec((Q_PER_KV, tq, HEAD_DIM), q_map),
        scratch_shapes=[pltpu.VMEM((1, rows), F32), pltpu.VMEM((VT_ROWS, rows), F32),
                        pltpu.VMEM((tks, rows), F32)],
        compiler_params=_cparams("arbitrary", "arbitrary", "arbitrary"),
        name="attn_axial",
    )(check, q, k, v1t, key_norm)


def _attn_b_kernel(q_ref, kp_ref, kc_ref, kn_ref, vp_ref, vc_ref, vn_ref, bias_ref, sink_ref, o_ref,
                   *, nb, qb):
    i = pl.program_id(2)
    rows = Q_PER_KV * BLOCK_Q
    kwin = jnp.concatenate([kp_ref[...], kc_ref[...], kn_ref[...]], axis=0)
    vwin = jnp.concatenate([vp_ref[...], vc_ref[...], vn_ref[...]], axis=1)
    bias = bias_ref[...]
    sink = sink_ref[...]
    for u in range(qb):
        blk = i * qb + u
        q = q_ref[:, u * BLOCK_Q:(u + 1) * BLOCK_Q, :].reshape(rows, HEAD_DIM)
        s = _dot_nt(kwin[u * BLOCK_Q:u * BLOCK_Q + NEAR], q) + bias
        s = jnp.concatenate([jnp.where(blk > 0, s[:BLOCK_Q], NEG), s[BLOCK_Q:2 * BLOCK_Q],
                             jnp.where(blk < nb - 1, s[2 * BLOCK_Q:], NEG)], axis=0)
        m = jnp.maximum(jnp.max(s, axis=0, keepdims=True), sink)
        p = jnp.exp2(s - m).astype(BF16)
        acct = _dot(vwin[:, u * BLOCK_Q:u * BLOCK_Q + NEAR], p)
        den = acct[HEAD_DIM:HEAD_DIM + 1] + jnp.exp2(sink - m)
        o_t = acct[:HEAD_DIM] / den
        o_ref[:, u * BLOCK_Q:(u + 1) * BLOCK_Q, :] = o_t.T.astype(o_ref.dtype).reshape(
            Q_PER_KV, BLOCK_Q, HEAD_DIM)


def _attn_b(q, k, v1t, bias_t, sink_t, *, layer, batch, seq, qb):
    n = q.shape[1]
    nb = seq // BLOCK_Q
    steps = nb // qb
    rows = Q_PER_KV * BLOCK_Q
    cur_map = lambda b, g, i: (g, b * steps + i, 0)
    prev_map = lambda b, g, i: (g, b * nb + jnp.maximum(i * qb - 1, 0), 0)
    next_map = lambda b, g, i: (g, b * nb + jnp.minimum(i * qb + qb, nb - 1), 0)
    t = lambda index_map: (lambda b, g, i: (index_map(b, g, i)[0], 0, index_map(b, g, i)[1]))
    k_specs = [pl.BlockSpec((None, BLOCK_Q, HEAD_DIM), prev_map),
               pl.BlockSpec((None, qb * BLOCK_Q, HEAD_DIM), cur_map),
               pl.BlockSpec((None, BLOCK_Q, HEAD_DIM), next_map)]
    v_specs = [pl.BlockSpec((None, VT_ROWS, BLOCK_Q), t(prev_map)),
               pl.BlockSpec((None, VT_ROWS, qb * BLOCK_Q), t(cur_map)),
               pl.BlockSpec((None, VT_ROWS, BLOCK_Q), t(next_map))]
    return pl.pallas_call(
        functools.partial(_attn_b_kernel, nb=nb, qb=qb),
        out_shape=jax.ShapeDtypeStruct((N_Q_HEADS, n, HEAD_DIM), ATTN_OUT),
        grid=(batch, N_KV_HEADS, steps),
        in_specs=[pl.BlockSpec((Q_PER_KV, qb * BLOCK_Q, HEAD_DIM), cur_map)] + k_specs + v_specs + [
            pl.BlockSpec((None, NEAR, rows), lambda b, g, i: (g, 0, 0)),
            pl.BlockSpec((None, None, 1, rows), lambda b, g, i: (layer, g, 0, 0)),
        ],
        out_specs=pl.BlockSpec((Q_PER_KV, qb * BLOCK_Q, HEAD_DIM), cur_map),
        compiler_params=_cparams("arbitrary", "arbitrary", "arbitrary"),
        name="attn_window",
    )(q, k, k, k, v1t, v1t, v1t, bias_t, sink_t)


def _attn_c_kernel(check_ref, q_ref, k_ref, v_ref, bias_ref, cfar_ref, kn_ref, lam_ref, gs_ref, o_ref,
                   m_ref, acct_ref, s_ref, *, layer, n_iter, unroll, tks, nblk, out_scale):
    n_blk = pl.program_id(2)
    rows = Q_PER_KV * BLOCK_Q
    blocks_per_sub = tks // BLOCK_Q
    blocks_per_iter = unroll * blocks_per_sub

    q = q_ref[...].reshape(rows, HEAD_DIM)
    lo = lax.broadcasted_iota(jnp.int32, q.shape, 1) < DIFF_DIM
    zero = jnp.zeros_like(q)
    q2 = jnp.concatenate([jnp.where(lo, q, zero), jnp.where(lo, zero, q)], axis=0)
    acct_ref[...] = jnp.zeros_like(acct_ref)
    s_ref[...] = _dot_nt(k_ref[0:tks, :], q2)
    refs = (q2, k_ref, v_ref, acct_ref, s_ref)

    c_left, c_right, c_max = cfar_ref[0], cfar_ref[1], cfar_ref[2]
    t_lo = jnp.maximum(n_blk - 1, 0) // blocks_per_iter
    t_hi = jnp.minimum(n_blk + 1, nblk - 1) // blocks_per_iter
    two = lambda x: jnp.concatenate([x, x], axis=1)
    shift = _score_bound(q2, kn_ref[:, :1]) + two(c_max)

    def add_bias(idx, s):
        tile_of = lambda key_blk: jnp.clip(key_blk - n_blk + 2, 0, NEAR // BLOCK_Q + 1)
        bias = jnp.concatenate([bias_ref[tile_of(idx * blocks_per_sub + kb)] for kb in range(blocks_per_sub)],
                               axis=0)
        return jnp.concatenate([s[:, :rows] + bias, s[:, rows:] + bias], axis=1)

    def iteration(t, last):
        is_near = (t >= t_lo) & (t <= t_hi)

        @pl.when(jnp.logical_not(is_near))
        def _():
            side = jnp.where(t < t_lo, c_left, c_right)
            _bounded_iteration(t, last, *refs, unroll=unroll, tks=tks, shift=shift - two(side))

        @pl.when(is_near)
        def _():
            _bounded_iteration(t, last, *refs, unroll=unroll, tks=tks, shift=shift, bias_fn=add_bias)

    def body(t, carry):
        iteration(t, False)
        return carry

    lax.fori_loop(0, n_iter - 1, body, 0)
    iteration(n_iter - 1, True)

    _recompute_if_flushed(check_ref[layer], acct_ref, lambda: _exact_pass(
        q2, k_ref, v_ref, m_ref, acct_ref, n_sub=n_iter * unroll, tks=tks, bias_fn=add_bias))

    acct = acct_ref[...]
    o1_t = acct[:HEAD_DIM, :rows] / acct[HEAD_DIM:HEAD_DIM + 1, :rows]
    o2_t = acct[:HEAD_DIM, rows:] / acct[HEAD_DIM:HEAD_DIM + 1, rows:]
    o_t = o1_t - lam_ref[:, :1] * o2_t
    ms = jnp.mean(o_t * o_t, axis=0, keepdims=True)
    o = (o_t * lax.rsqrt(ms + EPS)).T * gs_ref[...] * out_scale
    o_ref[...] = o.astype(o_ref.dtype).reshape(o_ref.shape)


def _attn_c(check, q, k, v1t, bias_t, cfar_t, key_norm, lam, g_subln, *, layer, batch, seq, tks, unroll):
    n = q.shape[1]
    nq = seq // BLOCK_Q
    rows = Q_PER_KV * BLOCK_Q
    lam_init = 0.8 - 0.6 * math.exp(-0.3 * layer)
    q_map = lambda b, g, i: (g, b * nq + i, 0)
    return pl.pallas_call(
        functools.partial(_attn_c_kernel, layer=layer, n_iter=seq // (tks * unroll), unroll=unroll, tks=tks,
                          nblk=nq, out_scale=1.0 - lam_init),
        out_shape=jax.ShapeDtypeStruct((N_Q_HEADS, n, HEAD_DIM), ATTN_OUT),
        grid=(batch, N_KV_HEADS, nq),
        in_specs=[
            pl.BlockSpec(memory_space=pltpu.SMEM),
            pl.BlockSpec((Q_PER_KV, BLOCK_Q, HEAD_DIM), q_map),
            pl.BlockSpec((None, seq, HEAD_DIM), lambda b, g, i: (g, b, 0)),
            pl.BlockSpec((None, VT_ROWS, seq), lambda b, g, i: (g, 0, b)),
            pl.BlockSpec((None, NEAR // BLOCK_Q + 2, BLOCK_Q, rows), lambda b, g, i: (g, 0, 0, 0)),
            pl.BlockSpec((3, None, 1, rows), lambda b, g, i: (0, g, 0, 0)),
            pl.BlockSpec((None, 1, HEAD_DIM), lambda b, g, i: (layer, 0, 0)),
            pl.BlockSpec((None, 1, HEAD_DIM), lambda b, g, i: (layer, 0, 0)),
            pl.BlockSpec((None, 1, HEAD_DIM), lambda b, g, i: (layer, 0, 0)),
        ],
        out_specs=pl.BlockSpec((Q_PER_KV, BLOCK_Q, HEAD_DIM), q_map),
        scratch_shapes=[pltpu.VMEM((1, 2 * rows), F32), pltpu.VMEM((VT_ROWS, 2 * rows), F32),
                        pltpu.VMEM((tks, 2 * rows), F32)],
        compiler_params=_cparams("arbitrary", "arbitrary", "arbitrary"),
        name="attn_diff",
    )(check, q, k, v1t, bias_t, cfar_t, key_norm, lam, g_subln)


def _t5_bucket_np(rel):
    half = N_BUCKETS // 2
    max_exact = half // 2
    ret = np.where(rel > 0, half, 0)
    n = np.abs(rel)
    ratio = np.log(np.maximum(n, 1).astype(np.float32) / np.float32(max_exact)) / np.float32(
        math.log(MAX_DISTANCE / max_exact))
    large = max_exact + (ratio * np.float32(half - max_exact)).astype(np.int32)
    large = np.minimum(large, half - 1)
    return (ret + np.where(n < max_exact, n, large)).astype(np.int32)


def _near_bias(table):
    r = np.arange(BLOCK_Q)[:, None]
    c = np.arange(NEAR)[None, :]
    bucket = _t5_bucket_np(c - BLOCK_Q - r)
    onehot = (bucket.reshape(-1, 1) == np.arange(N_BUCKETS)[None, :]).astype(np.float32)
    rows = jnp.dot(jnp.asarray(onehot), table.astype(F32), precision=lax.Precision.HIGHEST)
    return rows.reshape(BLOCK_Q, NEAR, table.shape[1]).transpose(2, 0, 1)


def _rope_tables(seq):
    rows = seq // GRID_W
    nfreq = HEAD_DIM // 4
    inv = ROPE_THETA ** (-jnp.arange(nfreq, dtype=F32) / nfreq)
    ang_r = jnp.arange(rows).astype(F32)[:, None] * inv
    ang_c = jnp.arange(GRID_W).astype(F32)[:, None] * inv
    by_row = lambda t: jnp.broadcast_to(t[:, None, :], (rows, GRID_W, nfreq)).reshape(seq, nfreq)
    by_col = lambda t: jnp.broadcast_to(t[None, :, :], (rows, GRID_W, nfreq)).reshape(seq, nfreq)
    table = lambda fn: jnp.concatenate([by_row(fn(ang_r))] * 2 + [by_col(fn(ang_c))] * 2, axis=-1)
    cos, sin = table(jnp.cos), table(jnp.sin)
    first = (np.arange(HEAD_DIM) % (HEAD_DIM // 2)) < HEAD_DIM // 4
    return cos, jnp.where(first, -sin, 0.0), jnp.where(first, 0.0, sin)


def _trunk(x, mod, p, *, batch, seq):
    t = _tiles(seq)
    tm = t["tm"]
    rope_tabs = _rope_tables(seq)
    for l in range(DEPTH):
        x = _ffn(x, mod, p["g_norm"], p["wgu"], p["wout"], layer=l, which=0, seq=seq, tm=tm)
        qa, ka, va, qb, kb, vb, qc, kc, vc = _qkv(x, mod, p["g_norm"], p["wqkv"], rope_tabs, p["gh"],
                                                  layer=l, seq=seq, tm=tm)
        oa = _attn_a(p["check_a"], qa, ka, va, p["kn_a"], layer=l, batch=batch, seq=seq, tq=t["tq_a"], tks=t["tks"],
                     unroll=t["unroll"])
        ob = _attn_b(qb, kb, vb, p["bias_b"], p["sink"], layer=l, batch=batch, seq=seq, qb=t["qb"])
        oc = _attn_c(p["check_c"], qc, kc, vc, p["bias_c"], p["cfar_c"], p["kn_c"], p["lam"], p["g_subln"],
                     layer=l, batch=batch, seq=seq, tks=t["tks"], unroll=t["unroll"])
        x = _out_proj(x, mod, p["g_norm"], oa, ob, oc, p["wgate"], p["wo"], layer=l, seq=seq,
                      tm=t["tm_out"])
        x = _ffn(x, mod, p["g_norm"], p["wgu"], p["wout"], layer=l, which=1, seq=seq, tm=tm)
    return x


def _prepare(w_ff_in, w_ff_out, w_in, w_o, g_qa, g_ka, g_qb, g_kb, g_qc, g_kc, sink,
             lam_q1, lam_k1, lam_q2, lam_k2, g_subln, rel_bias):
    wg = w_ff_in[..., :D_FF].reshape(DEPTH, 2, D_MODEL, N_FF_CHUNKS, FF_CHUNK)
    wu = w_ff_in[..., D_FF:].reshape(DEPTH, 2, D_MODEL, N_FF_CHUNKS, FF_CHUNK)
    wgu = jnp.concatenate([wg, wu], axis=-1).transpose(0, 1, 3, 2, 4).astype(BF16)
    wout = w_ff_out.reshape(DEPTH, 2, N_FF_CHUNKS, FF_CHUNK, D_MODEL).astype(BF16)
    zeros = jnp.zeros_like(g_qa)
    gh = jnp.stack([g_qa, g_ka, g_qb, g_kb, jnp.tile(g_qc, (1, 2)), jnp.tile(g_kc, (1, 2)), zeros, zeros],
                   axis=1).astype(F32)
    table_b, table_c = rel_bias[:, :N_Q_HEADS], rel_bias[:, N_Q_HEADS:]
    half = N_BUCKETS // 2
    c_left, c_right = table_c[half - 1] * LOG2E, table_c[N_BUCKETS - 1] * LOG2E
    lam_init = jnp.asarray([0.8 - 0.6 * math.exp(-0.3 * l) for l in range(DEPTH)], F32)
    lam = (jnp.exp(jnp.sum(lam_q1.astype(F32) * lam_k1.astype(F32), axis=-1))
           - jnp.exp(jnp.sum(lam_q2.astype(F32) * lam_k2.astype(F32), axis=-1)) + lam_init)

    band = np.abs(np.arange(NEAR)[None, :] - BLOCK_Q - np.arange(BLOCK_Q)[:, None]) <= WINDOW
    near_c = (_near_bias(table_c) * LOG2E).reshape(
        N_KV_HEADS, Q_PER_KV, BLOCK_Q, NEAR // BLOCK_Q, BLOCK_Q).transpose(0, 3, 4, 1, 2).reshape(
        N_KV_HEADS, NEAR // BLOCK_Q, BLOCK_Q, Q_PER_KV * BLOCK_Q)
    cfar_c = jnp.repeat(jnp.stack([c_left, c_right, jnp.max(table_c, axis=0) * LOG2E]).astype(F32),
                        BLOCK_Q, axis=1).reshape(3, N_KV_HEADS, 1, Q_PER_KV * BLOCK_Q)

    def norm_bound(g, dim):
        return BOUND_MARGIN * math.sqrt(dim) * jnp.max(jnp.abs(g.astype(F32)), axis=-1)

    def key_norm(g, dim):
        return jnp.broadcast_to(norm_bound(g, dim)[:, None, None], (DEPTH, 1, HEAD_DIM))

    span_a = 2.0 * BOUND_MARGIN * QA_SCALE * norm_bound(g_qa, HEAD_DIM) * norm_bound(g_ka, HEAD_DIM)
    span_c = (2.0 * BOUND_MARGIN * QC_SCALE * norm_bound(g_qc, DIFF_DIM) * norm_bound(g_kc, DIFF_DIM)
              + (jnp.max(table_c) - jnp.min(table_c)) * LOG2E)

    return dict(
        wgu=wgu, wout=wout,
        wqkv=w_in[:, :, :QKV_W].astype(BF16), wgate=w_in[:, :, QKV_W:].astype(BF16), wo=w_o.astype(BF16),
        gh=gh,
        bias_b=jnp.where(band, _near_bias(table_b) * LOG2E, NEG).reshape(
            N_KV_HEADS, Q_PER_KV, BLOCK_Q, NEAR).transpose(0, 3, 1, 2).reshape(
            N_KV_HEADS, NEAR, Q_PER_KV * BLOCK_Q),
        sink=jnp.repeat(sink.astype(F32) * LOG2E, BLOCK_Q, axis=1).reshape(
            DEPTH, N_KV_HEADS, 1, Q_PER_KV * BLOCK_Q),
        bias_c=jnp.concatenate([cfar_c[0][:, None] + jnp.zeros((1, 1, BLOCK_Q, 1), F32), near_c,
                                cfar_c[1][:, None] + jnp.zeros((1, 1, BLOCK_Q, 1), F32)], axis=1),
        cfar_c=cfar_c,
        kn_a=key_norm(g_ka, HEAD_DIM), kn_c=key_norm(g_kc, DIFF_DIM),
        check_a=(span_a > SAFE_SPAN).astype(jnp.int32), check_c=(span_c > SAFE_SPAN).astype(jnp.int32),
        lam=jnp.broadcast_to(lam[:, None, None], (DEPTH, 1, HEAD_DIM)),
        g_subln=g_subln.astype(F32)[:, None, :],
    )


def kernel(x_prompt, x_sample, c_prompt, c_sample, w_ada, b_ada, g_norm, w_ff_in, w_ff_out, w_in, w_o,
           g_qa, g_ka, g_qb, g_kb, g_qc, g_kc, sink, lam_q1, lam_k1, lam_q2, lam_k2, g_subln, rel_bias):
    p = _prepare(w_ff_in, w_ff_out, w_in, w_o, g_qa, g_ka, g_qb, g_kb, g_qc, g_kc, sink,
                 lam_q1, lam_k1, lam_q2, lam_k2, g_subln, rel_bias)
    p["g_norm"] = g_norm.astype(F32)
    outs = []
    n_cond = 0
    conds = [c_prompt, c_sample]
    c_all = jnp.concatenate(conds + [jnp.zeros((ADA_ROWS - sum(c.shape[0] for c in conds), D_MODEL), F32)])
    mod_all = _ada(c_all, w_ada, b_ada)
    for x, c in ((x_prompt, c_prompt), (x_sample, c_sample)):
        batch, seq, _ = x.shape
        mod = mod_all[:, n_cond:n_cond + batch].reshape(DEPTH, batch, 9, D_MODEL)
        n_cond += batch
        y = _trunk(x.reshape(batch * seq, D_MODEL), mod, p, batch=batch, seq=seq)
        outs.append(y.reshape(batch, seq, D_MODEL))
    return tuple(outs)
```

```python
import functools
import math

import numpy as np
import jax
import jax.numpy as jnp
from jax import lax
from jax.experimental import pallas as pl
from jax.experimental.pallas import tpu as pltpu

F32 = jnp.float32
BF16 = jnp.bfloat16
ATTN_OUT = jnp.bfloat16

D_MODEL = 1024
DEPTH = 4
N_Q_HEADS = 8
N_KV_HEADS = 2
Q_PER_KV = N_Q_HEADS // N_KV_HEADS
HEAD_DIM = 128
DIFF_DIM = 64
D_FF = 2816
BLOCK_Q = 128
WINDOW = 128
GRID_W = 64
N_BUCKETS = 32
MAX_DISTANCE = 128
ROPE_THETA = 10000.0
EPS = 1e-6
NEG = -1e30
LOG2E = 1.4426950408889634
QA_SCALE = HEAD_DIM ** -0.5 * LOG2E
QC_SCALE = DIFF_DIM ** -0.5 * LOG2E

Q_W = N_Q_HEADS * HEAD_DIM
KV_W = N_KV_HEADS * HEAD_DIM
BRANCH_IN = Q_W + 2 * KV_W
QKV_W = 3 * BRANCH_IN
MXU_TILE = 256
FF_CHUNK = MXU_TILE
N_FF_CHUNKS = D_FF // FF_CHUNK
ADA_ROWS = 16
ADA_TN = 1536
NEAR = 3 * BLOCK_Q
VT_ROWS = HEAD_DIM + 16

F32_SUBLANES = 8
ROW_TILE = 512
STEPS_PER_REGION = 32
BOUND_MARGIN = 1.01

VMEM_LIMIT = 56 * 1024 * 1024


def _tiles(seq):
    tks = min(MXU_TILE, seq)
    return dict(
        tm=min(ROW_TILE, seq),
        tm_out=min(2 * ROW_TILE, seq),
        tq_a=2 * BLOCK_Q,
        tks=tks,
        qb=min(STEPS_PER_REGION, seq // BLOCK_Q),
        unroll=min(STEPS_PER_REGION, seq // tks),
    )


def _cparams(*sem):
    return pltpu.CompilerParams(dimension_semantics=sem, vmem_limit_bytes=VMEM_LIMIT)


def _dot(a, b):
    return jnp.dot(a, b, preferred_element_type=F32)


def _dot_nt(a, b):
    return lax.dot_general(a, b, (((1,), (1,)), ((), ())), preferred_element_type=F32)


def _ada_kernel(c_ref, w_ref, b_ref, o_ref):
    c = c_ref[...]
    a = (c * jax.nn.sigmoid(c)).astype(BF16)
    o_ref[0] = _dot(a, w_ref[0].astype(BF16)) + b_ref[0]


def _ada(c_all, w_ada, b_ada):
    n_out = w_ada.shape[-1]
    return pl.pallas_call(
        _ada_kernel,
        out_shape=jax.ShapeDtypeStruct((DEPTH, ADA_ROWS, n_out), F32),
        grid=(DEPTH, n_out // ADA_TN),
        in_specs=[
            pl.BlockSpec((ADA_ROWS, D_MODEL), lambda l, j: (0, 0)),
            pl.BlockSpec((1, D_MODEL, ADA_TN), lambda l, j: (l, 0, j)),
            pl.BlockSpec((1, 1, ADA_TN), lambda l, j: (l, 0, j)),
        ],
        out_specs=pl.BlockSpec((1, ADA_ROWS, ADA_TN), lambda l, j: (l, 0, j)),
        compiler_params=_cparams("arbitrary", "arbitrary"),
        name="ada",
    )(c_all, w_ada, b_ada.reshape(DEPTH, 1, n_out))


def _modulate(x, mod_ref, g, jj):
    ms = jnp.mean(x * x, axis=-1, keepdims=True)
    y = x * lax.rsqrt(ms + EPS) * g
    return y * (1.0 + mod_ref[0, 3 * jj + 1:3 * jj + 2, :]) + mod_ref[0, 3 * jj:3 * jj + 1, :]


def _ffn_kernel(x_ref, mod_ref, g_ref, win_ref, wout_ref, o_ref, nb_ref, acc_ref, *, jj):
    x = x_ref[...]
    nb_ref[...] = _modulate(x, mod_ref, g_ref[jj:jj + 1, :], jj).astype(BF16)
    acc_ref[...] = jnp.zeros_like(acc_ref)
    for c in range(N_FF_CHUNKS):
        cols = slice(c * FF_CHUNK, (c + 1) * FF_CHUNK)
        hg = _dot(nb_ref[...], win_ref[:, cols])
        hu = _dot(nb_ref[...], win_ref[:, D_FF + c * FF_CHUNK:D_FF + (c + 1) * FF_CHUNK])
        a = (hg * jax.nn.sigmoid(hg)) * hu
        acc_ref[...] += _dot(a.astype(BF16), wout_ref[cols, :])
    o_ref[...] = x + (0.5 * mod_ref[0, 3 * jj + 2:3 * jj + 3, :]) * acc_ref[...]


def _ffn(x, mod, g_norm, w_in, w_out, *, layer, which, seq, tm):
    n = x.shape[0]
    jj = 2 * which
    const = dict(pipeline_mode=pl.Buffered(1))
    return pl.pallas_call(
        functools.partial(_ffn_kernel, jj=jj),
        out_shape=jax.ShapeDtypeStruct((n, D_MODEL), F32),
        grid=(n // tm,),
        in_specs=[
            pl.BlockSpec((tm, D_MODEL), lambda i: (i, 0)),
            pl.BlockSpec((None, 1, 9, D_MODEL), lambda i: (layer, (i * tm) // seq, 0, 0)),
            pl.BlockSpec((None, 3, D_MODEL), lambda i: (layer, 0, 0)),
            pl.BlockSpec((None, None, D_MODEL, 2 * D_FF), lambda i: (layer, which, 0, 0), **const),
            pl.BlockSpec((None, None, D_FF, D_MODEL), lambda i: (layer, which, 0, 0), **const),
        ],
        out_specs=pl.BlockSpec((tm, D_MODEL), lambda i: (i, 0)),
        scratch_shapes=[pltpu.VMEM((tm, D_MODEL), BF16), pltpu.VMEM((tm, D_MODEL), F32)],
        input_output_aliases={} if (layer == 0 and which == 0) else {0: 0},
        compiler_params=_cparams("arbitrary"),
        name=f"ffn{which}",
    )(x, mod, g_norm, w_in, w_out)


def _head_norm(r, g):
    ms = jnp.mean(r * r, axis=-1, keepdims=True)
    return r * lax.rsqrt(ms + EPS) * g


def _half_norm(r, g2):
    sq = r * r
    lo = lax.broadcasted_iota(jnp.int32, r.shape, 1) < DIFF_DIM
    s_lo = jnp.sum(jnp.where(lo, sq, 0.0), axis=-1, keepdims=True)
    s_hi = jnp.sum(jnp.where(lo, 0.0, sq), axis=-1, keepdims=True)
    ms = jnp.where(lo, s_lo, s_hi) * (1.0 / DIFF_DIM)
    return r * lax.rsqrt(ms + EPS) * g2


def _qkv_kernel(x_ref, mod_ref, g_ref, w_ref, cos_ref, sa_ref, sb_ref, gh_ref,
                qa_ref, ka_ref, va_ref, qb_ref, kb_ref, vb_ref, qc_ref, kc_ref, vc_ref, nb_ref):
    nb_ref[...] = _modulate(x_ref[...], mod_ref, g_ref[1:2, :], 1).astype(BF16)
    cos, sa, sb = cos_ref[...], sa_ref[...], sb_ref[...]

    def rope(r):
        return (r * cos + pltpu.roll(r, HEAD_DIM - 32, 1) * sa + pltpu.roll(r, 32, 1) * sb)

    def proj(col, width):
        return _dot(nb_ref[...], w_ref[:, col:col + width])

    def heads(r, n_heads, fn, out_ref):
        for h in range(n_heads):
            out_ref[h] = fn(r[:, h * HEAD_DIM:(h + 1) * HEAD_DIM]).astype(BF16)

    def values_transposed(r, out_ref):
        for h in range(N_KV_HEADS):
            out_ref[h, :HEAD_DIM, :] = r[:, h * HEAD_DIM:(h + 1) * HEAD_DIM].T.astype(BF16)
            out_ref[h, HEAD_DIM:, :] = jnp.ones((VT_ROWS - HEAD_DIM, r.shape[0]), BF16)

    g_qa, g_ka, g_qb, g_kb = (gh_ref[i:i + 1, :] for i in range(4))
    g_qc, g_kc = gh_ref[4:5, :], gh_ref[5:6, :]
    col = 0
    heads(proj(col, Q_W), N_Q_HEADS, lambda r: rope(_head_norm(r, g_qa)) * QA_SCALE,qa_ref)
    col += Q_W
    heads(proj(col, KV_W), N_KV_HEADS, lambda r: rope(_head_norm(r, g_ka)), ka_ref)
    col += KV_W
    values_transposed(proj(col, KV_W), va_ref)
    col += KV_W
    heads(proj(col, Q_W), N_Q_HEADS, lambda r: _head_norm(r, g_qb) * QA_SCALE,qb_ref)
    col += Q_W
    heads(proj(col, KV_W), N_KV_HEADS, lambda r: _head_norm(r, g_kb), kb_ref)
    col += KV_W
    values_transposed(proj(col, KV_W), vb_ref)
    col += KV_W
    heads(proj(col, Q_W), N_Q_HEADS, lambda r: _half_norm(r, g_qc) * QC_SCALE,qc_ref)
    col += Q_W
    heads(proj(col, KV_W), N_KV_HEADS, lambda r: _half_norm(r, g_kc), kc_ref)
    col += KV_W
    values_transposed(proj(col, KV_W), vc_ref)


def _qkv(x, mod, g_norm, wqkv, rope_tabs, gh, *, layer, seq, tm):
    n = x.shape[0]
    tiles_per_seq = seq // tm
    q_shape = jax.ShapeDtypeStruct((N_Q_HEADS, n, HEAD_DIM), BF16)
    kv_shape = jax.ShapeDtypeStruct((N_KV_HEADS, n, HEAD_DIM), BF16)
    q_spec = pl.BlockSpec((N_Q_HEADS, tm, HEAD_DIM), lambda i: (0, i, 0))
    kv_spec = pl.BlockSpec((N_KV_HEADS, tm, HEAD_DIM), lambda i: (0, i, 0))
    tab_spec = pl.BlockSpec((tm, HEAD_DIM), lambda i: (i % tiles_per_seq, 0))
    v1_shape = jax.ShapeDtypeStruct((N_KV_HEADS, VT_ROWS, n), BF16)
    v1_spec = pl.BlockSpec((N_KV_HEADS, VT_ROWS, tm), lambda i: (0, 0, i))
    return pl.pallas_call(
        _qkv_kernel,
        out_shape=[q_shape, kv_shape, v1_shape] * 3,
        grid=(n // tm,),
        in_specs=[
            pl.BlockSpec((tm, D_MODEL), lambda i: (i, 0)),
            pl.BlockSpec((None, 1, 9, D_MODEL), lambda i: (layer, (i * tm) // seq, 0, 0)),
            pl.BlockSpec((None, 3, D_MODEL), lambda i: (layer, 0, 0)),
            pl.BlockSpec((None, D_MODEL, QKV_W), lambda i: (layer, 0, 0), pipeline_mode=pl.Buffered(1)),
            tab_spec, tab_spec, tab_spec,
            pl.BlockSpec((None, 8, HEAD_DIM), lambda i: (layer, 0, 0)),
        ],
        out_specs=[q_spec, kv_spec, v1_spec] * 3,
        scratch_shapes=[pltpu.VMEM((tm, D_MODEL), BF16)],
        compiler_params=_cparams("arbitrary"),
        name="qkv",
    )(x, mod, g_norm, wqkv, *rope_tabs, gh)


def _out_kernel(x_ref, mod_ref, g_ref, oa_ref, ob_ref, oc_ref, wg_ref, wo_ref, o_ref, mg_ref):
    x = x_ref[...]
    nb = _modulate(x, mod_ref, g_ref[1:2, :], 1).astype(BF16)
    for br, br_ref in enumerate((oa_ref, ob_ref, oc_ref)):
        gate = jax.nn.sigmoid(_dot(nb, wg_ref[:, br * D_MODEL:(br + 1) * D_MODEL]))
        for h in range(N_Q_HEADS):
            lanes = slice(h * HEAD_DIM, (h + 1) * HEAD_DIM)
            term = gate[:, lanes] * br_ref[h].astype(F32)
            if br == 0:
                mg_ref[:, lanes] = term
            else:
                mg_ref[:, lanes] += term
    o_ref[...] = x + mod_ref[0, 5:6, :] * _dot(mg_ref[...].astype(BF16), wo_ref[...])


def _out_proj(x, mod, g_norm, oa, ob, oc, wgate, wo, *, layer, seq, tm):
    n = x.shape[0]
    o_spec = pl.BlockSpec((N_Q_HEADS, tm, HEAD_DIM), lambda i: (0, i, 0))
    return pl.pallas_call(
        _out_kernel,
        out_shape=jax.ShapeDtypeStruct((n, D_MODEL), F32),
        grid=(n // tm,),
        in_specs=[
            pl.BlockSpec((tm, D_MODEL), lambda i: (i, 0)),
            pl.BlockSpec((None, 1, 9, D_MODEL), lambda i: (layer, (i * tm) // seq, 0, 0)),
            pl.BlockSpec((None, 3, D_MODEL), lambda i: (layer, 0, 0)),
            o_spec, o_spec, o_spec,
            pl.BlockSpec((None, D_MODEL, 3 * D_MODEL), lambda i: (layer, 0, 0), pipeline_mode=pl.Buffered(1)),
            pl.BlockSpec((None, D_MODEL, D_MODEL), lambda i: (layer, 0, 0), pipeline_mode=pl.Buffered(1)),
        ],
        out_specs=pl.BlockSpec((tm, D_MODEL), lambda i: (i, 0)),
        scratch_shapes=[pltpu.VMEM((tm, D_MODEL), F32)],
        input_output_aliases={0: 0},
        compiler_params=_cparams("arbitrary"),
        name="out_proj",
    )(x, mod, g_norm, oa, ob, oc, wgate, wo)


MIN_DENOM = 2.0 ** -64
SAFE_SPAN = 60.0


def _key_slice(idx, tks):
    return pl.ds(pl.multiple_of(idx * tks, tks), tks)


def _score_bound(q, key_norm):
    qf = q.astype(F32)
    qn2 = _dot_nt(jnp.ones((F32_SUBLANES, q.shape[1]), F32), qf * qf)[:1]
    return jnp.sqrt(qn2) * key_norm


def _bounded_iteration(t, last, q, k_ref, v_ref, acct_ref, s_ref, *, unroll, tks, shift, bias_fn=None):
    acct = acct_ref[...]
    s = s_ref[...]
    for u in range(unroll):
        idx = t * unroll + u
        s_next = None if (last and u == unroll - 1) else _dot_nt(k_ref[_key_slice(idx + 1, tks), :], q)
        if bias_fn is not None:
            s = bias_fn(idx, s)
        acct = acct + _dot(v_ref[:, _key_slice(idx, tks)], jnp.exp2(s - shift).astype(BF16))
        s = s_next
    acct_ref[...] = acct
    if not last:
        s_ref[...] = s


def _exact_pass(q, k_ref, v_ref, m_ref, acct_ref, *, n_sub, tks, bias_fn=None):
    m_ref[...] = jnp.full_like(m_ref, -jnp.inf)
    acct_ref[...] = jnp.zeros_like(acct_ref)

    def body(idx, carry):
        s = _dot_nt(k_ref[_key_slice(idx, tks), :], q)
        if bias_fn is not None:
            s = bias_fn(idx, s)
        m = m_ref[...]
        m_new = jnp.maximum(m, jnp.max(s, axis=0, keepdims=True))
        p = jnp.exp2(s - m_new).astype(BF16)
        acct_ref[...] = jnp.exp2(m - m_new) * acct_ref[...] + _dot(v_ref[:, _key_slice(idx, tks)], p)
        m_ref[...] = m_new
        return carry

    lax.fori_loop(0, n_sub, body, 0)


def _recompute_if_flushed(check, acct_ref, exact_pass):
    @pl.when(check != 0)
    def _():
        @pl.when(jnp.min(acct_ref[HEAD_DIM:HEAD_DIM + 1, :]) < MIN_DENOM)
        def _():
            exact_pass()


def _attn_a_kernel(check_ref, q_ref, k_ref, v_ref, kn_ref, o_ref, m_ref, acct_ref, s_ref,
                   *, layer, n_iter, unroll, tks):
    rows = acct_ref.shape[1]
    q = q_ref[...].reshape(rows, HEAD_DIM)
    shift = _score_bound(q, kn_ref[:, :1])
    acct_ref[...] = jnp.zeros_like(acct_ref)
    s_ref[...] = _dot_nt(k_ref[0:tks, :], q)
    refs = (q, k_ref, v_ref, acct_ref, s_ref)

    def body(t, carry):
        _bounded_iteration(t, False, *refs, unroll=unroll, tks=tks, shift=shift)
        return carry

    lax.fori_loop(0, n_iter - 1, body, 0)
    _bounded_iteration(n_iter - 1, True, *refs, unroll=unroll, tks=tks, shift=shift)

    _recompute_if_flushed(check_ref[layer], acct_ref, lambda: _exact_pass(
        q, k_ref, v_ref, m_ref, acct_ref, n_sub=n_iter * unroll, tks=tks))

    acct = acct_ref[...]
    o_t = acct[:HEAD_DIM] / acct[HEAD_DIM:HEAD_DIM + 1]
    o_ref[...] = o_t.T.astype(o_ref.dtype).reshape(o_ref.shape)


def _attn_a(check, q, k, v1t, key_norm, *, layer, batch, seq, tq, tks, unroll):
    n = q.shape[1]
    nq = seq // tq
    rows = Q_PER_KV * tq
    q_map = lambda b, g, i: (g, b * nq + i, 0)
    return pl.pallas_call(
        functools.partial(_attn_a_kernel, layer=layer, n_iter=seq // (tks * unroll), unroll=unroll, tks=tks),
        out_shape=jax.ShapeDtypeStruct((N_Q_HEADS, n, HEAD_DIM), ATTN_OUT),
        grid=(batch, N_KV_HEADS, nq),
        in_specs=[
            pl.BlockSpec(memory_space=pltpu.SMEM),
            pl.BlockSpec((Q_PER_KV, tq, HEAD_DIM), q_map),
            pl.BlockSpec((None, seq, HEAD_DIM), lambda b, g, i: (g, b, 0)),
            pl.BlockSpec((None, VT_ROWS, seq), lambda b, g, i: (g, 0, b)),
            pl.BlockSpec((None, 1, HEAD_DIM), lambda b, g, i: (layer, 0, 0)),
        ],
        out_specs=pl.BlockSpec((Q_PER_KV, tq, HEAD_DIM), q_map),
        scratch_shapes=[pltpu.VMEM((1, rows), F32), pltpu.VMEM((VT_ROWS, rows), F32),
                        pltpu.VMEM((tks, rows), F32)],
        compiler_params=_cparams("arbitrary", "arbitrary", "arbitrary"),
        name="attn_axial",
    )(check, q, k, v1t, key_norm)


def _attn_b_kernel(q_ref, kp_ref, kc_ref, kn_ref, vp_ref, vc_ref, vn_ref, bias_ref, sink_ref, o_ref,
                   *, nb, qb):
    i = pl.program_id(2)
    rows = Q_PER_KV * BLOCK_Q
    kwin = jnp.concatenate([kp_ref[...], kc_ref[...], kn_ref[...]], axis=0)
    vwin = jnp.concatenate([vp_ref[...], vc_ref[...], vn_ref[...]], axis=1)
    bias = bias_ref[...]
    sink = sink_ref[...]
    for u in range(qb):
        blk = i * qb + u
        q = q_ref[:, u * BLOCK_Q:(u + 1) * BLOCK_Q, :].reshape(rows, HEAD_DIM)
        s = _dot_nt(kwin[u * BLOCK_Q:u * BLOCK_Q + NEAR], q) + bias
        s = jnp.concatenate([jnp.where(blk > 0, s[:BLOCK_Q], NEG), s[BLOCK_Q:2 * BLOCK_Q],
                             jnp.where(blk < nb - 1, s[2 * BLOCK_Q:], NEG)], axis=0)
        m = jnp.maximum(jnp.max(s, axis=0, keepdims=True), sink)
        p = jnp.exp2(s - m).astype(BF16)
        acct = _dot(vwin[:, u * BLOCK_Q:u * BLOCK_Q + NEAR], p)
        den = acct[HEAD_DIM:HEAD_DIM + 1] + jnp.exp2(sink - m)
        o_t = acct[:HEAD_DIM] / den
        o_ref[:, u * BLOCK_Q:(u + 1) * BLOCK_Q, :] = o_t.T.astype(o_ref.dtype).reshape(
            Q_PER_KV, BLOCK_Q, HEAD_DIM)


def _attn_b(q, k, v1t, bias_t, sink_t, *, layer, batch, seq, qb):
    n = q.shape[1]
    nb = seq // BLOCK_Q
    steps = nb // qb
    rows = Q_PER_KV * BLOCK_Q
    cur_map = lambda b, g, i: (g, b * steps + i, 0)
    prev_map = lambda b, g, i: (g, b * nb + jnp.maximum(i * qb - 1, 0), 0)
    next_map = lambda b, g, i: (g, b * nb + jnp.minimum(i * qb + qb, nb - 1), 0)
    t = lambda index_map: (lambda b, g, i: (index_map(b, g, i)[0], 0, index_map(b, g, i)[1]))
    k_specs = [pl.BlockSpec((None, BLOCK_Q, HEAD_DIM), prev_map),
               pl.BlockSpec((None, qb * BLOCK_Q, HEAD_DIM), cur_map),
               pl.BlockSpec((None, BLOCK_Q, HEAD_DIM), next_map)]
    v_specs = [pl.BlockSpec((None, VT_ROWS, BLOCK_Q), t(prev_map)),
               pl.BlockSpec((None, VT_ROWS, qb * BLOCK_Q), t(cur_map)),
               pl.BlockSpec((None, VT_ROWS, BLOCK_Q), t(next_map))]
    return pl.pallas_call(
        functools.partial(_attn_b_kernel, nb=nb, qb=qb),
        out_shape=jax.ShapeDtypeStruct((N_Q_HEADS, n, HEAD_DIM), ATTN_OUT),
        grid=(batch, N_KV_HEADS, steps),
        in_specs=[pl.BlockSpec((Q_PER_KV, qb * BLOCK_Q, HEAD_DIM), cur_map)] + k_specs + v_specs + [
            pl.BlockSpec((None, NEAR, rows), lambda b, g, i: (g, 0, 0)),
            pl.BlockSpec((None, None, 1, rows), lambda b, g, i: (layer, g, 0, 0)),
        ],
        out_specs=pl.BlockSpec((Q_PER_KV, qb * BLOCK_Q, HEAD_DIM), cur_map),
        compiler_params=_cparams("arbitrary", "arbitrary", "arbitrary"),
        name="attn_window",
    )(q, k, k, k, v1t, v1t, v1t, bias_t, sink_t)


def _attn_c_kernel(check_ref, q_ref, k_ref, v_ref, bias_ref, cfar_ref, kn_ref, lam_ref, gs_ref, o_ref,
                   m_ref, acct_ref, s_ref, *, layer, n_iter, unroll, tks, nblk, out_scale):
    n_blk = pl.program_id(2)
    rows = Q_PER_KV * BLOCK_Q
    blocks_per_sub = tks // BLOCK_Q
    blocks_per_iter = unroll * blocks_per_sub

    q = q_ref[...].reshape(rows, HEAD_DIM)
    lo = lax.broadcasted_iota(jnp.int32, q.shape, 1) < DIFF_DIM
    zero = jnp.zeros_like(q)
    q2 = jnp.concatenate([jnp.where(lo, q, zero), jnp.where(lo, zero, q)], axis=0)
    acct_ref[...] = jnp.zeros_like(acct_ref)
    s_ref[...] = _dot_nt(k_ref[0:tks, :], q2)
    refs = (q2, k_ref, v_ref, acct_ref, s_ref)

    c_left, c_right, c_max = cfar_ref[0], cfar_ref[1], cfar_ref[2]
    t_lo = jnp.maximum(n_blk - 1, 0) // blocks_per_iter
    t_hi = jnp.minimum(n_blk + 1, nblk - 1) // blocks_per_iter
    two = lambda x: jnp.concatenate([x, x], axis=1)
    shift = _score_bound(q2, kn_ref[:, :1]) + two(c_max)

    def add_bias(idx, s):
        tile_of = lambda key_blk: jnp.clip(key_blk - n_blk + 2, 0, NEAR // BLOCK_Q + 1)
        bias = jnp.concatenate([bias_ref[tile_of(idx * blocks_per_sub + kb)] for kb in range(blocks_per_sub)],
                               axis=0)
        return jnp.concatenate([s[:, :rows] + bias, s[:, rows:] + bias], axis=1)

    def iteration(t, last):
        is_near = (t >= t_lo) & (t <= t_hi)

        @pl.when(jnp.logical_not(is_near))
        def _():
            side = jnp.where(t < t_lo, c_left, c_right)
            _bounded_iteration(t, last, *refs, unroll=unroll, tks=tks, shift=shift - two(side))

        @pl.when(is_near)
        def _():
            _bounded_iteration(t, last, *refs, unroll=unroll, tks=tks, shift=shift, bias_fn=add_bias)

    def body(t, carry):
        iteration(t, False)
        return carry

    lax.fori_loop(0, n_iter - 1, body, 0)
    iteration(n_iter - 1, True)

    _recompute_if_flushed(check_ref[layer], acct_ref, lambda: _exact_pass(
        q2, k_ref, v_ref, m_ref, acct_ref, n_sub=n_iter * unroll, tks=tks, bias_fn=add_bias))

    acct = acct_ref[...]
    o1_t = acct[:HEAD_DIM, :rows] / acct[HEAD_DIM:HEAD_DIM + 1, :rows]
    o2_t = acct[:HEAD_DIM, rows:] / acct[HEAD_DIM:HEAD_DIM + 1, rows:]
    o_t = o1_t - lam_ref[:, :1] * o2_t
    ms = jnp.mean(o_t * o_t, axis=0, keepdims=True)
    o = (o_t * lax.rsqrt(ms + EPS)).T * gs_ref[...] * out_scale
    o_ref[...] = o.astype(o_ref.dtype).reshape(o_ref.shape)


def _attn_c(check, q, k, v1t, bias_t, cfar_t, key_norm, lam, g_subln, *, layer, batch, seq, tks, unroll):
    n = q.shape[1]
    nq = seq // BLOCK_Q
    rows = Q_PER_KV * BLOCK_Q
    lam_init = 0.8 - 0.6 * math.exp(-0.3 * layer)
    q_map = lambda b, g, i: (g, b * nq + i, 0)
    return pl.pallas_call(
        functools.partial(_attn_c_kernel, layer=layer, n_iter=seq // (tks * unroll), unroll=unroll, tks=tks,
                          nblk=nq, out_scale=1.0 - lam_init),
        out_shape=jax.ShapeDtypeStruct((N_Q_HEADS, n, HEAD_DIM), ATTN_OUT),
        grid=(batch, N_KV_HEADS, nq),
        in_specs=[
            pl.BlockSpec(memory_space=pltpu.SMEM),
            pl.BlockSpec((Q_PER_KV, BLOCK_Q, HEAD_DIM), q_map),
            pl.BlockSpec((None, seq, HEAD_DIM), lambda b, g, i: (g, b, 0)),
            pl.BlockSpec((None, VT_ROWS, seq), lambda b, g, i: (g, 0, b)),
            pl.BlockSpec((None, NEAR // BLOCK_Q + 2, BLOCK_Q, rows), lambda b, g, i: (g, 0, 0, 0)),
            pl.BlockSpec((3, None, 1, rows), lambda b, g, i: (0, g, 0, 0)),
            pl.BlockSpec((None, 1, HEAD_DIM), lambda b, g, i: (layer, 0, 0)),
            pl.BlockSpec((None, 1, HEAD_DIM), lambda b, g, i: (layer, 0, 0)),
            pl.BlockSpec((None, 1, HEAD_DIM), lambda b, g, i: (layer, 0, 0)),
        ],
        out_specs=pl.BlockSpec((Q_PER_KV, BLOCK_Q, HEAD_DIM), q_map),
        scratch_shapes=[pltpu.VMEM((1, 2 * rows), F32), pltpu.VMEM((VT_ROWS, 2 * rows), F32),
                        pltpu.VMEM((tks, 2 * rows), F32)],
        compiler_params=_cparams("arbitrary", "arbitrary", "arbitrary"),
        name="attn_diff",
    )(check, q, k, v1t, bias_t, cfar_t, key_norm, lam, g_subln)


def _t5_bucket_np(rel):
    half = N_BUCKETS // 2
    max_exact = half // 2
    ret = np.where(rel > 0, half, 0)
    n = np.abs(rel)
    ratio = np.log(np.maximum(n, 1).astype(np.float32) / np.float32(max_exact)) / np.float32(
        math.log(MAX_DISTANCE / max_exact))
    large = max_exact + (ratio * np.float32(half - max_exact)).astype(np.int32)
    large = np.minimum(large, half - 1)
    return (ret + np.where(n < max_exact, n, large)).astype(np.int32)


def _near_bias(table):
    r = np.arange(BLOCK_Q)[:, None]
    c = np.arange(NEAR)[None, :]
    bucket = _t5_bucket_np(c - BLOCK_Q - r)
    onehot = (bucket.reshape(-1, 1) == np.arange(N_BUCKETS)[None, :]).astype(np.float32)
    rows = jnp.dot(jnp.asarray(onehot), table.astype(F32), precision=lax.Precision.HIGHEST)
    return rows.reshape(BLOCK_Q, NEAR, table.shape[1]).transpose(2, 0, 1)


def _rope_tables(seq):
    rows = seq // GRID_W
    nfreq = HEAD_DIM // 4
    inv = ROPE_THETA ** (-jnp.arange(nfreq, dtype=F32) / nfreq)
    ang_r = jnp.arange(rows).astype(F32)[:, None] * inv
    ang_c = jnp.arange(GRID_W).astype(F32)[:, None] * inv
    by_row = lambda t: jnp.broadcast_to(t[:, None, :], (rows, GRID_W, nfreq)).reshape(seq, nfreq)
    by_col = lambda t: jnp.broadcast_to(t[None, :, :], (rows, GRID_W, nfreq)).reshape(seq, nfreq)
    table = lambda fn: jnp.concatenate([by_row(fn(ang_r))] * 2 + [by_col(fn(ang_c))] * 2, axis=-1)
    cos, sin = table(jnp.cos), table(jnp.sin)
    first = (np.arange(HEAD_DIM) % (HEAD_DIM // 2)) < HEAD_DIM // 4
    return cos, jnp.where(first, -sin, 0.0), jnp.where(first, 0.0, sin)


def _trunk(x, mod, p, *, batch, seq):
    t = _tiles(seq)
    tm = t["tm"]
    rope_tabs = _rope_tables(seq)
    for l in range(DEPTH):
        x = _ffn(x, mod, p["g_norm"], p["w_ff_in"], p["w_ff_out"], layer=l, which=0, seq=seq, tm=tm)
        qa, ka, va, qb, kb, vb, qc, kc, vc = _qkv(x, mod, p["g_norm"], p["wqkv"], rope_tabs, p["gh"],
                                                  layer=l, seq=seq, tm=tm)
        oa = _attn_a(p["check_a"], qa, ka, va, p["kn_a"], layer=l, batch=batch, seq=seq, tq=t["tq_a"], tks=t["tks"],
                     unroll=t["unroll"])
        ob = _attn_b(qb, kb, vb, p["bias_b"], p["sink"], layer=l, batch=batch, seq=seq, qb=t["qb"])
        oc = _attn_c(p["check_c"], qc, kc, vc, p["bias_c"], p["cfar_c"], p["kn_c"], p["lam"], p["g_subln"],
                     layer=l, batch=batch, seq=seq, tks=t["tks"], unroll=t["unroll"])
        x = _out_proj(x, mod, p["g_norm"], oa, ob, oc, p["wgate"], p["wo"], layer=l, seq=seq,
                      tm=t["tm_out"])
        x = _ffn(x, mod, p["g_norm"], p["w_ff_in"], p["w_ff_out"], layer=l, which=1, seq=seq, tm=tm)
    return x


def _prepare(w_ff_in, w_ff_out, w_in, w_o, g_qa, g_ka, g_qb, g_kb, g_qc, g_kc, sink,
             lam_q1, lam_k1, lam_q2, lam_k2, g_subln, rel_bias):
    zeros = jnp.zeros_like(g_qa)
    gh = jnp.stack([g_qa, g_ka, g_qb, g_kb, jnp.tile(g_qc, (1, 2)), jnp.tile(g_kc, (1, 2)), zeros, zeros],
                   axis=1).astype(F32)
    table_b, table_c = rel_bias[:, :N_Q_HEADS], rel_bias[:, N_Q_HEADS:]
    half = N_BUCKETS // 2
    c_left, c_right = table_c[half - 1] * LOG2E, table_c[N_BUCKETS - 1] * LOG2E
    lam_init = jnp.asarray([0.8 - 0.6 * math.exp(-0.3 * l) for l in range(DEPTH)], F32)
    lam = (jnp.exp(jnp.sum(lam_q1.astype(F32) * lam_k1.astype(F32), axis=-1))
           - jnp.exp(jnp.sum(lam_q2.astype(F32) * lam_k2.astype(F32), axis=-1)) + lam_init)

    band = np.abs(np.arange(NEAR)[None, :] - BLOCK_Q - np.arange(BLOCK_Q)[:, None]) <= WINDOW
    near_c = (_near_bias(table_c) * LOG2E).reshape(
        N_KV_HEADS, Q_PER_KV, BLOCK_Q, NEAR // BLOCK_Q, BLOCK_Q).transpose(0, 3, 4, 1, 2).reshape(
        N_KV_HEADS, NEAR // BLOCK_Q, BLOCK_Q, Q_PER_KV * BLOCK_Q)
    cfar_c = jnp.repeat(jnp.stack([c_left, c_right, jnp.max(table_c, axis=0) * LOG2E]).astype(F32),
                        BLOCK_Q, axis=1).reshape(3, N_KV_HEADS, 1, Q_PER_KV * BLOCK_Q)

    def norm_bound(g, dim):
        return BOUND_MARGIN * math.sqrt(dim) * jnp.max(jnp.abs(g.astype(F32)), axis=-1)

    def key_norm(g, dim):
        return jnp.broadcast_to(norm_bound(g, dim)[:, None, None], (DEPTH, 1, HEAD_DIM))

    span_a = 2.0 * BOUND_MARGIN * QA_SCALE * norm_bound(g_qa, HEAD_DIM) * norm_bound(g_ka, HEAD_DIM)
    span_c = (2.0 * BOUND_MARGIN * QC_SCALE * norm_bound(g_qc, DIFF_DIM) * norm_bound(g_kc, DIFF_DIM)
              + (jnp.max(table_c) - jnp.min(table_c)) * LOG2E)

    return dict(
        w_ff_in=w_ff_in.astype(BF16), w_ff_out=w_ff_out.astype(BF16),
        wqkv=w_in[:, :, :QKV_W].astype(BF16), wgate=w_in[:, :, QKV_W:].astype(BF16), wo=w_o.astype(BF16),
        gh=gh,
        bias_b=jnp.where(band, _near_bias(table_b) * LOG2E, NEG).reshape(
            N_KV_HEADS, Q_PER_KV, BLOCK_Q, NEAR).transpose(0, 3, 1, 2).reshape(
            N_KV_HEADS, NEAR, Q_PER_KV * BLOCK_Q),
        sink=jnp.repeat(sink.astype(F32) * LOG2E, BLOCK_Q, axis=1).reshape(
            DEPTH, N_KV_HEADS, 1, Q_PER_KV * BLOCK_Q),
        bias_c=jnp.concatenate([cfar_c[0][:, None] + jnp.zeros((1, 1, BLOCK_Q, 1), F32), near_c,
                                cfar_c[1][:, None] + jnp.zeros((1, 1, BLOCK_Q, 1), F32)], axis=1),
        cfar_c=cfar_c,
        kn_a=key_norm(g_ka, HEAD_DIM), kn_c=key_norm(g_kc, DIFF_DIM),
        check_a=(span_a > SAFE_SPAN).astype(jnp.int32), check_c=(span_c > SAFE_SPAN).astype(jnp.int32),
        lam=jnp.broadcast_to(lam[:, None, None], (DEPTH, 1, HEAD_DIM)),
        g_subln=g_subln.astype(F32)[:, None, :],
    )


def kernel(x_prompt, x_sample, c_prompt, c_sample, w_ada, b_ada, g_norm, w_ff_in, w_ff_out, w_in, w_o,
           g_qa, g_ka, g_qb, g_kb, g_qc, g_kc, sink, lam_q1, lam_k1, lam_q2, lam_k2, g_subln, rel_bias):
    p = _prepare(w_ff_in, w_ff_out, w_in, w_o, g_qa, g_ka, g_qb, g_kb, g_qc, g_kc, sink,
                 lam_q1, lam_k1, lam_q2, lam_k2, g_subln, rel_bias)
    p["g_norm"] = g_norm.astype(F32)
    outs = []
    n_cond = 0
    conds = [c_prompt, c_sample]
    c_all = jnp.concatenate(conds + [jnp.zeros((ADA_ROWS - sum(c.shape[0] for c in conds), D_MODEL), F32)])
    mod_all = _ada(c_all, w_ada, b_ada)
    for x, c in ((x_prompt, c_prompt), (x_sample, c_sample)):
        batch, seq, _ = x.shape
        mod = mod_all[:, n_cond:n_cond + batch].reshape(DEPTH, batch, 9, D_MODEL)
        n_cond += batch
        y = _trunk(x.reshape(batch * seq, D_MODEL), mod, p, batch=batch, seq=seq)
        outs.append(y.reshape(batch, seq, D_MODEL))
    return tuple(outs)
```

```python
import functools
import math

import numpy as np
import jax
import jax.numpy as jnp
from jax import lax
from jax.experimental import pallas as pl
from jax.experimental.pallas import tpu as pltpu

F32 = jnp.float32
BF16 = jnp.bfloat16
ATTN_OUT = jnp.bfloat16

D_MODEL = 1024
DEPTH = 4
N_Q_HEADS = 8
N_KV_HEADS = 2
Q_PER_KV = N_Q_HEADS // N_KV_HEADS
HEAD_DIM = 128
DIFF_DIM = 64
D_FF = 2816
BLOCK_Q = 128
WINDOW = 128
GRID_W = 64
N_BUCKETS = 32
MAX_DISTANCE = 128
ROPE_THETA = 10000.0
EPS = 1e-6
NEG = -1e30
LOG2E = 1.4426950408889634
QA_SCALE = HEAD_DIM ** -0.5 * LOG2E
QC_SCALE = DIFF_DIM ** -0.5 * LOG2E

Q_W = N_Q_HEADS * HEAD_DIM
KV_W = N_KV_HEADS * HEAD_DIM
BRANCH_IN = Q_W + 2 * KV_W
QKV_W = 3 * BRANCH_IN
MXU_TILE = 256
FF_CHUNK = MXU_TILE
N_FF_CHUNKS = D_FF // FF_CHUNK
ADA_ROWS = 16
ADA_TN = 1536
NEAR = 3 * BLOCK_Q
VT_ROWS = HEAD_DIM + 16

F32_SUBLANES = 8
ROW_TILE = 512
STEPS_PER_REGION = 32
BOUND_MARGIN = 1.01

VMEM_LIMIT = 56 * 1024 * 1024


def _tiles(seq):
    tks = min(MXU_TILE, seq)
    return dict(
        tm=min(ROW_TILE, seq),
        tm_out=min(2 * ROW_TILE, seq),
        tq_a=2 * BLOCK_Q,
        tks=tks,
        qb=min(STEPS_PER_REGION, seq // BLOCK_Q),
        unroll=min(STEPS_PER_REGION, seq // tks),
    )


def _cparams(*sem):
    return pltpu.CompilerParams(dimension_semantics=sem, vmem_limit_bytes=VMEM_LIMIT)


def _dot(a, b):
    return jnp.dot(a, b, preferred_element_type=F32)


def _dot_nt(a, b):
    return lax.dot_general(a, b, (((1,), (1,)), ((), ())), preferred_element_type=F32)


def _ada_kernel(c_ref, w_ref, b_ref, o_ref):
    c = c_ref[...]
    a = (c * jax.nn.sigmoid(c)).astype(BF16)
    o_ref[0] = _dot(a, w_ref[0].astype(BF16)) + b_ref[0]


def _ada(c_all, w_ada, b_ada):
    n_out = w_ada.shape[-1]
    return pl.pallas_call(
        _ada_kernel,
        out_shape=jax.ShapeDtypeStruct((DEPTH, ADA_ROWS, n_out), F32),
        grid=(DEPTH, n_out // ADA_TN),
        in_specs=[
            pl.BlockSpec((ADA_ROWS, D_MODEL), lambda l, j: (0, 0)),
            pl.BlockSpec((1, D_MODEL, ADA_TN), lambda l, j: (l, 0, j)),
            pl.BlockSpec((1, 1, ADA_TN), lambda l, j: (l, 0, j)),
        ],
        out_specs=pl.BlockSpec((1, ADA_ROWS, ADA_TN), lambda l, j: (l, 0, j)),
        compiler_params=_cparams("arbitrary", "arbitrary"),
        name="ada",
    )(c_all, w_ada, b_ada.reshape(DEPTH, 1, n_out))


def _modulate(x, mod_ref, g, jj):
    ms = jnp.mean(x * x, axis=-1, keepdims=True)
    y = x * lax.rsqrt(ms + EPS) * g
    return y * (1.0 + mod_ref[0, 3 * jj + 1:3 * jj + 2, :]) + mod_ref[0, 3 * jj:3 * jj + 1, :]


def _ffn_kernel(x_ref, mod_ref, g_ref, win_ref, wout_ref, o_ref, nb_ref, acc_ref, *, jj):
    x = x_ref[...]
    nb_ref[...] = _modulate(x, mod_ref, g_ref[jj:jj + 1, :], jj).astype(BF16)
    acc_ref[...] = jnp.zeros_like(acc_ref)
    for c in range(N_FF_CHUNKS):
        cols = slice(c * FF_CHUNK, (c + 1) * FF_CHUNK)
        hg = _dot(nb_ref[...], win_ref[:, cols])
        hu = _dot(nb_ref[...], win_ref[:, D_FF + c * FF_CHUNK:D_FF + (c + 1) * FF_CHUNK])
        a = (hg * jax.nn.sigmoid(hg)) * hu
        acc_ref[...] += _dot(a.astype(BF16), wout_ref[cols, :])
    o_ref[...] = x + (0.5 * mod_ref[0, 3 * jj + 2:3 * jj + 3, :]) * acc_ref[...]


def _ffn(x, mod, g_norm, w_in, w_out, *, layer, which, seq, tm):
    n = x.shape[0]
    jj = 2 * which
    const = dict(pipeline_mode=pl.Buffered(1))
    return pl.pallas_call(
        functools.partial(_ffn_kernel, jj=jj),
        out_shape=jax.ShapeDtypeStruct((n, D_MODEL), F32),
        grid=(n // tm,),
        in_specs=[
            pl.BlockSpec((tm, D_MODEL), lambda i: (i, 0)),
            pl.BlockSpec((None, 1, 9, D_MODEL), lambda i: (layer, (i * tm) // seq, 0, 0)),
            pl.BlockSpec((None, 3, D_MODEL), lambda i: (layer, 0, 0)),
            pl.BlockSpec((None, None, D_MODEL, 2 * D_FF), lambda i: (layer, which, 0, 0), **const),
            pl.BlockSpec((None, None, D_FF, D_MODEL), lambda i: (layer, which, 0, 0), **const),
        ],
        out_specs=pl.BlockSpec((tm, D_MODEL), lambda i: (i, 0)),
        scratch_shapes=[pltpu.VMEM((tm, D_MODEL), BF16), pltpu.VMEM((tm, D_MODEL), F32)],
        input_output_aliases={} if (layer == 0 and which == 0) else {0: 0},
        compiler_params=_cparams("arbitrary"),
        name=f"ffn{which}",
    )(x, mod, g_norm, w_in, w_out)


def _head_norm(r, g):
    ms = jnp.mean(r * r, axis=-1, keepdims=True)
    return r * lax.rsqrt(ms + EPS) * g


def _half_norm(r, g2):
    sq = r * r
    lo = lax.broadcasted_iota(jnp.int32, r.shape, 1) < DIFF_DIM
    s_lo = jnp.sum(jnp.where(lo, sq, 0.0), axis=-1, keepdims=True)
    s_hi = jnp.sum(jnp.where(lo, 0.0, sq), axis=-1, keepdims=True)
    ms = jnp.where(lo, s_lo, s_hi) * (1.0 / DIFF_DIM)
    return r * lax.rsqrt(ms + EPS) * g2


def _qkv_kernel(x_ref, mod_ref, g_ref, w_ref, cos_ref, sa_ref, sb_ref, gh_ref,
                qa_ref, ka_ref, va_ref, qb_ref, kb_ref, vb_ref, qc_ref, kc_ref, vc_ref, nb_ref):
    nb_ref[...] = _modulate(x_ref[...], mod_ref, g_ref[1:2, :], 1).astype(BF16)
    cos, sa, sb = cos_ref[...], sa_ref[...], sb_ref[...]

    def rope(r):
        return (r * cos + pltpu.roll(r, HEAD_DIM - 32, 1) * sa + pltpu.roll(r, 32, 1) * sb)

    def proj(col, width):
        return _dot(nb_ref[...], w_ref[:, col:col + width])

    def heads(r, n_heads, fn, out_ref):
        for h in range(n_heads):
            out_ref[h] = fn(r[:, h * HEAD_DIM:(h + 1) * HEAD_DIM]).astype(BF16)

    def values_transposed(r, out_ref):
        for h in range(N_KV_HEADS):
            out_ref[h, :HEAD_DIM, :] = r[:, h * HEAD_DIM:(h + 1) * HEAD_DIM].T.astype(BF16)
            out_ref[h, HEAD_DIM:, :] = jnp.ones((VT_ROWS - HEAD_DIM, r.shape[0]), BF16)

    g_qa, g_ka, g_qb, g_kb = (gh_ref[i:i + 1, :] for i in range(4))
    g_qc, g_kc = gh_ref[4:5, :], gh_ref[5:6, :]
    col = 0
    heads(proj(col, Q_W), N_Q_HEADS, lambda r: rope(_head_norm(r, g_qa)) * QA_SCALE,qa_ref)
    col += Q_W
    heads(proj(col, KV_W), N_KV_HEADS, lambda r: rope(_head_norm(r, g_ka)), ka_ref)
    col += KV_W
    values_transposed(proj(col, KV_W), va_ref)
    col += KV_W
    heads(proj(col, Q_W), N_Q_HEADS, lambda r: _head_norm(r, g_qb) * QA_SCALE,qb_ref)
    col += Q_W
    heads(proj(col, KV_W), N_KV_HEADS, lambda r: _head_norm(r, g_kb), kb_ref)
    col += KV_W
    values_transposed(proj(col, KV_W), vb_ref)
    col += KV_W
    heads(proj(col, Q_W), N_Q_HEADS, lambda r: _half_norm(r, g_qc) * QC_SCALE,qc_ref)
    col += Q_W
    heads(proj(col, KV_W), N_KV_HEADS, lambda r: _half_norm(r, g_kc), kc_ref)
    col += KV_W
    values_transposed(proj(col, KV_W), vc_ref)


def _qkv(x, mod, g_norm, wqkv, rope_tabs, gh, *, layer, seq, tm):
    n = x.shape[0]
    tiles_per_seq = seq // tm
    q_shape = jax.ShapeDtypeStruct((N_Q_HEADS, n, HEAD_DIM), BF16)
    kv_shape = jax.ShapeDtypeStruct((N_KV_HEADS, n, HEAD_DIM), BF16)
    q_spec = pl.BlockSpec((N_Q_HEADS, tm, HEAD_DIM), lambda i: (0, i, 0))
    kv_spec = pl.BlockSpec((N_KV_HEADS, tm, HEAD_DIM), lambda i: (0, i, 0))
    tab_spec = pl.BlockSpec((tm, HEAD_DIM), lambda i: (i % tiles_per_seq, 0))
    v1_shape = jax.ShapeDtypeStruct((N_KV_HEADS, VT_ROWS, n), BF16)
    v1_spec = pl.BlockSpec((N_KV_HEADS, VT_ROWS, tm), lambda i: (0, 0, i))
    return pl.pallas_call(
        _qkv_kernel,
        out_shape=[q_shape, kv_shape, v1_shape] * 3,
        grid=(n // tm,),
        in_specs=[
            pl.BlockSpec((tm, D_MODEL), lambda i: (i, 0)),
            pl.BlockSpec((None, 1, 9, D_MODEL), lambda i: (layer, (i * tm) // seq, 0, 0)),
            pl.BlockSpec((None, 3, D_MODEL), lambda i: (layer, 0, 0)),
            pl.BlockSpec((None, D_MODEL, QKV_W), lambda i: (layer, 0, 0), pipeline_mode=pl.Buffered(1)),
            tab_spec, tab_spec, tab_spec,
            pl.BlockSpec((None, 8, HEAD_DIM), lambda i: (layer, 0, 0)),
        ],
        out_specs=[q_spec, kv_spec, v1_spec] * 3,
        scratch_shapes=[pltpu.VMEM((tm, D_MODEL), BF16)],
        compiler_params=_cparams("arbitrary"),
        name="qkv",
    )(x, mod, g_norm, wqkv, *rope_tabs, gh)


def _out_kernel(x_ref, mod_ref, g_ref, oa_ref, ob_ref, oc_ref, wg_ref, wo_ref, o_ref, mg_ref):
    x = x_ref[...]
    nb = _modulate(x, mod_ref, g_ref[1:2, :], 1).astype(BF16)
    for br, br_ref in enumerate((oa_ref, ob_ref, oc_ref)):
        gate = jax.nn.sigmoid(_dot(nb, wg_ref[:, br * D_MODEL:(br + 1) * D_MODEL]))
        for h in range(N_Q_HEADS):
            lanes = slice(h * HEAD_DIM, (h + 1) * HEAD_DIM)
            term = gate[:, lanes] * br_ref[h].astype(F32)
            if br == 0:
                mg_ref[:, lanes] = term
            else:
                mg_ref[:, lanes] += term
    o_ref[...] = x + mod_ref[0, 5:6, :] * _dot(mg_ref[...].astype(BF16), wo_ref[...])


def _out_proj(x, mod, g_norm, oa, ob, oc, wgate, wo, *, layer, seq, tm):
    n = x.shape[0]
    o_spec = pl.BlockSpec((N_Q_HEADS, tm, HEAD_DIM), lambda i: (0, i, 0))
    return pl.pallas_call(
        _out_kernel,
        out_shape=jax.ShapeDtypeStruct((n, D_MODEL), F32),
        grid=(n // tm,),
        in_specs=[
            pl.BlockSpec((tm, D_MODEL), lambda i: (i, 0)),
            pl.BlockSpec((None, 1, 9, D_MODEL), lambda i: (layer, (i * tm) // seq, 0, 0)),
            pl.BlockSpec((None, 3, D_MODEL), lambda i: (layer, 0, 0)),
            o_spec, o_spec, o_spec,
            pl.BlockSpec((None, D_MODEL, 3 * D_MODEL), lambda i: (layer, 0, 0), pipeline_mode=pl.Buffered(1)),
            pl.BlockSpec((None, D_MODEL, D_MODEL), lambda i: (layer, 0, 0), pipeline_mode=pl.Buffered(1)),
        ],
        out_specs=pl.BlockSpec((tm, D_MODEL), lambda i: (i, 0)),
        scratch_shapes=[pltpu.VMEM((tm, D_MODEL), F32)],
        input_output_aliases={0: 0},
        compiler_params=_cparams("arbitrary"),
        name="out_proj",
    )(x, mod, g_norm, oa, ob, oc, wgate, wo)


MIN_DENOM = 2.0 ** -64
SAFE_SPAN = 60.0


def _key_slice(idx, tks):
    return pl.ds(pl.multiple_of(idx * tks, tks), tks)


def _score_bound(q, key_norm):
    qf = q.astype(F32)
    qn2 = _dot_nt(jnp.ones((F32_SUBLANES, q.shape[1]), F32), qf * qf)[:1]
    return jnp.sqrt(qn2) * key_norm


def _bounded_iteration(t, last, q, k_ref, v_ref, acct_ref, s_ref, *, unroll, tks, shift, bias_fn=None):
    acct = acct_ref[...]
    s = s_ref[...]
    for u in range(unroll):
        idx = t * unroll + u
        s_next = None if (last and u == unroll - 1) else _dot_nt(k_ref[_key_slice(idx + 1, tks), :], q)
        if bias_fn is not None:
            s = bias_fn(idx, s)
        acct = acct + _dot(v_ref[:, _key_slice(idx, tks)], jnp.exp2(s - shift).astype(BF16))
        s = s_next
    acct_ref[...] = acct
    if not last:
        s_ref[...] = s


def _exact_pass(q, k_ref, v_ref, m_ref, acct_ref, *, n_sub, tks, bias_fn=None):
    m_ref[...] = jnp.full_like(m_ref, -jnp.inf)
    acct_ref[...] = jnp.zeros_like(acct_ref)

    def body(idx, carry):
        s = _dot_nt(k_ref[_key_slice(idx, tks), :], q)
        if bias_fn is not None:
            s = bias_fn(idx, s)
        m = m_ref[...]
        m_new = jnp.maximum(m, jnp.max(s, axis=0, keepdims=True))
        p = jnp.exp2(s - m_new).astype(BF16)
        acct_ref[...] = jnp.exp2(m - m_new) * acct_ref[...] + _dot(v_ref[:, _key_slice(idx, tks)], p)
        m_ref[...] = m_new
        return carry

    lax.fori_loop(0, n_sub, body, 0)


def _recompute_if_flushed(check, acct_ref, exact_pass):
    @pl.when(check != 0)
    def _():
        @pl.when(jnp.min(acct_ref[HEAD_DIM:HEAD_DIM + 1, :]) < MIN_DENOM)
        def _():
            exact_pass()


def _attn_a_kernel(check_ref, q_ref, k_ref, v_ref, kn_ref, o_ref, m_ref, acct_ref, s_ref,
                   *, layer, n_iter, unroll, tks):
    rows = acct_ref.shape[1]
    q = q_ref[...].reshape(rows, HEAD_DIM)
    shift = _score_bound(q, kn_ref[:, :1])
    acct_ref[...] = jnp.zeros_like(acct_ref)
    s_ref[...] = _dot_nt(k_ref[0:tks, :], q)
    refs = (q, k_ref, v_ref, acct_ref, s_ref)

    def body(t, carry):
        _bounded_iteration(t, False, *refs, unroll=unroll, tks=tks, shift=shift)
        return carry

    lax.fori_loop(0, n_iter - 1, body, 0)
    _bounded_iteration(n_iter - 1, True, *refs, unroll=unroll, tks=tks, shift=shift)

    _recompute_if_flushed(check_ref[layer], acct_ref, lambda: _exact_pass(
        q, k_ref, v_ref, m_ref, acct_ref, n_sub=n_iter * unroll, tks=tks))

    acct = acct_ref[...]
    o_t = acct[:HEAD_DIM] / acct[HEAD_DIM:HEAD_DIM + 1]
    o_ref[...] = o_t.T.astype(o_ref.dtype).reshape(o_ref.shape)


def _attn_a(check, q, k, v1t, key_norm, *, layer, batch, seq, tq, tks, unroll):
    n = q.shape[1]
    nq = seq // tq
    rows = Q_PER_KV * tq
    q_map = lambda b, g, i: (g, b * nq + i, 0)
    return pl.pallas_call(
        functools.partial(_attn_a_kernel, layer=layer, n_iter=seq // (tks * unroll), unroll=unroll, tks=tks),
        out_shape=jax.ShapeDtypeStruct((N_Q_HEADS, n, HEAD_DIM), ATTN_OUT),
        grid=(batch, N_KV_HEADS, nq),
        in_specs=[
            pl.BlockSpec(memory_space=pltpu.SMEM),
            pl.BlockSpec((Q_PER_KV, tq, HEAD_DIM), q_map),
            pl.BlockSpec((None, seq, HEAD_DIM), lambda b, g, i: (g, b, 0)),
            pl.BlockSpec((None, VT_ROWS, seq), lambda b, g, i: (g, 0, b)),
            pl.BlockSpec((None, 1, HEAD_DIM), lambda b, g, i: (layer, 0, 0)),
        ],
        out_specs=pl.BlockSpec((Q_PER_KV, tq, HEAD_DIM), q_map),
        scratch_shapes=[pltpu.VMEM((1, rows), F32), pltpu.VMEM((VT_ROWS, rows), F32),
                        pltpu.VMEM((tks, rows), F32)],
        compiler_params=_cparams("arbitrary", "arbitrary", "arbitrary"),
        name="attn_axial",
    )(check, q, k, v1t, key_norm)


def _attn_b_kernel(q_ref, kp_ref, kc_ref, kn_ref, vp_ref, vc_ref, vn_ref, bias_ref, sink_ref, o_ref,
                   *, nb, qb):
    i = pl.program_id(2)
    rows = Q_PER_KV * BLOCK_Q
    kwin = jnp.concatenate([kp_ref[...], kc_ref[...], kn_ref[...]], axis=0)
    vwin = jnp.concatenate([vp_ref[...], vc_ref[...], vn_ref[...]], axis=1)
    bias = bias_ref[...]
    sink = sink_ref[...]
    for u in range(qb):
        blk = i * qb + u
        q = q_ref[:, u * BLOCK_Q:(u + 1) * BLOCK_Q, :].reshape(rows, HEAD_DIM)
        s = _dot_nt(kwin[u * BLOCK_Q:u * BLOCK_Q + NEAR], q) + bias
        s = jnp.concatenate([jnp.where(blk > 0, s[:BLOCK_Q], NEG), s[BLOCK_Q:2 * BLOCK_Q],
                             jnp.where(blk < nb - 1, s[2 * BLOCK_Q:], NEG)], axis=0)
        m = jnp.maximum(jnp.max(s, axis=0, keepdims=True), sink)
        p = jnp.exp2(s - m).astype(BF16)
        acct = _dot(vwin[:, u * BLOCK_Q:u * BLOCK_Q + NEAR], p)
        den = acct[HEAD_DIM:HEAD_DIM + 1] + jnp.exp2(sink - m)
        o_t = acct[:HEAD_DIM] / den
        o_ref[:, u * BLOCK_Q:(u + 1) * BLOCK_Q, :] = o_t.T.astype(o_ref.dtype).reshape(
            Q_PER_KV, BLOCK_Q, HEAD_DIM)


def _attn_b(q, k, v1t, bias_t, sink_t, *, layer, batch, seq, qb):
    n = q.shape[1]
    nb = seq // BLOCK_Q
    steps = nb // qb
    rows = Q_PER_KV * BLOCK_Q
    cur_map = lambda b, g, i: (g, b * steps + i, 0)
    prev_map = lambda b, g, i: (g, b * nb + jnp.maximum(i * qb - 1, 0), 0)
    next_map = lambda b, g, i: (g, b * nb + jnp.minimum(i * qb + qb, nb - 1), 0)
    t = lambda index_map: (lambda b, g, i: (index_map(b, g, i)[0], 0, index_map(b, g, i)[1]))
    k_specs = [pl.BlockSpec((None, BLOCK_Q, HEAD_DIM), prev_map),
               pl.BlockSpec((None, qb * BLOCK_Q, HEAD_DIM), cur_map),
               pl.BlockSpec((None, BLOCK_Q, HEAD_DIM), next_map)]
    v_specs = [pl.BlockSpec((None, VT_ROWS, BLOCK_Q), t(prev_map)),
               pl.BlockSpec((None, VT_ROWS, qb * BLOCK_Q), t(cur_map)),
               pl.BlockSpec((None, VT_ROWS, BLOCK_Q), t(next_map))]
    return pl.pallas_call(
        functools.partial(_attn_b_kernel, nb=nb, qb=qb),
        out_shape=jax.ShapeDtypeStruct((N_Q_HEADS, n, HEAD_DIM), ATTN_OUT),
        grid=(batch, N_KV_HEADS, steps),
        in_specs=[pl.BlockSpec((Q_PER_KV, qb * BLOCK_Q, HEAD_DIM), cur_map)] + k_specs + v_specs + [
            pl.BlockSpec((None, NEAR, rows), lambda b, g, i: (g, 0, 0)),
            pl.BlockSpec((None, None, 1, rows), lambda b, g, i: (layer, g, 0, 0)),
        ],
        out_specs=pl.BlockSpec((Q_PER_KV, qb * BLOCK_Q, HEAD_DIM), cur_map),
        compiler_params=_cparams("arbitrary", "arbitrary", "arbitrary"),
        name="attn_window",
    )(q, k, k, k, v1t, v1t, v1t, bias_t, sink_t)


def _attn_c_kernel(check_ref, q_ref, k_ref, v_ref, bias_ref, cfar_ref, kn_ref, lam_ref, gs_ref, o_ref,
                   m_ref, acct_ref, s_ref, *, layer, n_iter, unroll, tks, nblk, out_scale):
    n_blk = pl.program_id(2)
    rows = Q_PER_KV * BLOCK_Q
    blocks_per_sub = tks // BLOCK_Q
    blocks_per_iter = unroll * blocks_per_sub

    q = q_ref[...].reshape(rows, HEAD_DIM)
    lo = lax.broadcasted_iota(jnp.int32, q.shape, 1) < DIFF_DIM
    zero = jnp.zeros_like(q)
    q2 = jnp.concatenate([jnp.where(lo, q, zero), jnp.where(lo, zero, q)], axis=0)
    acct_ref[...] = jnp.zeros_like(acct_ref)
    s_ref[...] = _dot_nt(k_ref[0:tks, :], q2)
    refs = (q2, k_ref, v_ref, acct_ref, s_ref)

    c_left, c_right, c_max = cfar_ref[0], cfar_ref[1], cfar_ref[2]
    t_lo = jnp.maximum(n_blk - 1, 0) // blocks_per_iter
    t_hi = jnp.minimum(n_blk + 1, nblk - 1) // blocks_per_iter
    two = lambda x: jnp.concatenate([x, x], axis=1)
    shift = _score_bound(q2, kn_ref[:, :1]) + two(c_max)

    def add_bias(idx, s):
        tile_of = lambda key_blk: jnp.clip(key_blk - n_blk + 2, 0, NEAR // BLOCK_Q + 1)
        bias = jnp.concatenate([bias_ref[tile_of(idx * blocks_per_sub + kb)] for kb in range(blocks_per_sub)],
                               axis=0)
        return jnp.concatenate([s[:, :rows] + bias, s[:, rows:] + bias], axis=1)

    def iteration(t, last):
        if n_iter == 1:
            return _bounded_iteration(t, last, *refs, unroll=unroll, tks=tks, shift=shift, bias_fn=add_bias)
        is_near = (t >= t_lo) & (t <= t_hi)

        @pl.when(jnp.logical_not(is_near))
        def _():
            side = jnp.where(t < t_lo, c_left, c_right)
            _bounded_iteration(t, last, *refs, unroll=unroll, tks=tks, shift=shift - two(side))

        @pl.when(is_near)
        def _():
            _bounded_iteration(t, last, *refs, unroll=unroll, tks=tks, shift=shift, bias_fn=add_bias)

    def body(t, carry):
        iteration(t, False)
        return carry

    lax.fori_loop(0, n_iter - 1, body, 0)
    iteration(n_iter - 1, True)

    _recompute_if_flushed(check_ref[layer], acct_ref, lambda: _exact_pass(
        q2, k_ref, v_ref, m_ref, acct_ref, n_sub=n_iter * unroll, tks=tks, bias_fn=add_bias))

    acct = acct_ref[...]
    o1_t = acct[:HEAD_DIM, :rows] / acct[HEAD_DIM:HEAD_DIM + 1, :rows]
    o2_t = acct[:HEAD_DIM, rows:] / acct[HEAD_DIM:HEAD_DIM + 1, rows:]
    o_t = o1_t - lam_ref[:, :1] * o2_t
    ms = jnp.mean(o_t * o_t, axis=0, keepdims=True)
    o = (o_t * lax.rsqrt(ms + EPS)).T * gs_ref[...] * out_scale
    o_ref[...] = o.astype(o_ref.dtype).reshape(o_ref.shape)


def _attn_c(check, q, k, v1t, bias_t, cfar_t, key_norm, lam, g_subln, *, layer, batch, seq, tks, unroll):
    n = q.shape[1]
    nq = seq // BLOCK_Q
    rows = Q_PER_KV * BLOCK_Q
    lam_init = 0.8 - 0.6 * math.exp(-0.3 * layer)
    q_map = lambda b, g, i: (g, b * nq + i, 0)
    return pl.pallas_call(
        functools.partial(_attn_c_kernel, layer=layer, n_iter=seq // (tks * unroll), unroll=unroll, tks=tks,
                          nblk=nq, out_scale=1.0 - lam_init),
        out_shape=jax.ShapeDtypeStruct((N_Q_HEADS, n, HEAD_DIM), ATTN_OUT),
        grid=(batch, N_KV_HEADS, nq),
        in_specs=[
            pl.BlockSpec(memory_space=pltpu.SMEM),
            pl.BlockSpec((Q_PER_KV, BLOCK_Q, HEAD_DIM), q_map),
            pl.BlockSpec((None, seq, HEAD_DIM), lambda b, g, i: (g, b, 0)),
            pl.BlockSpec((None, VT_ROWS, seq), lambda b, g, i: (g, 0, b)),
            pl.BlockSpec((None, NEAR // BLOCK_Q + 2, BLOCK_Q, rows), lambda b, g, i: (g, 0, 0, 0)),
            pl.BlockSpec((3, None, 1, rows), lambda b, g, i: (0, g, 0, 0)),
            pl.BlockSpec((None, 1, HEAD_DIM), lambda b, g, i: (layer, 0, 0)),
            pl.BlockSpec((None, 1, HEAD_DIM), lambda b, g, i: (layer, 0, 0)),
            pl.BlockSpec((None, 1, HEAD_DIM), lambda b, g, i: (layer, 0, 0)),
        ],
        out_specs=pl.BlockSpec((Q_PER_KV, BLOCK_Q, HEAD_DIM), q_map),
        scratch_shapes=[pltpu.VMEM((1, 2 * rows), F32), pltpu.VMEM((VT_ROWS, 2 * rows), F32),
                        pltpu.VMEM((tks, 2 * rows), F32)],
        compiler_params=_cparams("arbitrary", "arbitrary", "arbitrary"),
        name="attn_diff",
    )(check, q, k, v1t, bias_t, cfar_t, key_norm, lam, g_subln)


def _t5_bucket_np(rel):
    half = N_BUCKETS // 2
    max_exact = half // 2
    ret = np.where(rel > 0, half, 0)
    n = np.abs(rel)
    ratio = np.log(np.maximum(n, 1).astype(np.float32) / np.float32(max_exact)) / np.float32(
        math.log(MAX_DISTANCE / max_exact))
    large = max_exact + (ratio * np.float32(half - max_exact)).astype(np.int32)
    large = np.minimum(large, half - 1)
    return (ret + np.where(n < max_exact, n, large)).astype(np.int32)


def _near_bias(table):
    r = np.arange(BLOCK_Q)[:, None]
    c = np.arange(NEAR)[None, :]
    bucket = _t5_bucket_np(c - BLOCK_Q - r)
    onehot = (bucket.reshape(-1, 1) == np.arange(N_BUCKETS)[None, :]).astype(np.float32)
    rows = jnp.dot(jnp.asarray(onehot), table.astype(F32), precision=lax.Precision.HIGHEST)
    return rows.reshape(BLOCK_Q, NEAR, table.shape[1]).transpose(2, 0, 1)


def _rope_tables(seq):
    rows = seq // GRID_W
    nfreq = HEAD_DIM // 4
    inv = ROPE_THETA ** (-jnp.arange(nfreq, dtype=F32) / nfreq)
    ang_r = jnp.arange(rows).astype(F32)[:, None] * inv
    ang_c = jnp.arange(GRID_W).astype(F32)[:, None] * inv
    by_row = lambda t: jnp.broadcast_to(t[:, None, :], (rows, GRID_W, nfreq)).reshape(seq, nfreq)
    by_col = lambda t: jnp.broadcast_to(t[None, :, :], (rows, GRID_W, nfreq)).reshape(seq, nfreq)
    table = lambda fn: jnp.concatenate([by_row(fn(ang_r))] * 2 + [by_col(fn(ang_c))] * 2, axis=-1)
    cos, sin = table(jnp.cos), table(jnp.sin)
    first = (np.arange(HEAD_DIM) % (HEAD_DIM // 2)) < HEAD_DIM // 4
    return cos, jnp.where(first, -sin, 0.0), jnp.where(first, 0.0, sin)


def _trunk(x, mod, p, *, batch, seq):
    t = _tiles(seq)
    tm = t["tm"]
    rope_tabs = _rope_tables(seq)
    for l in range(DEPTH):
        x = _ffn(x, mod, p["g_norm"], p["w_ff_in"], p["w_ff_out"], layer=l, which=0, seq=seq, tm=tm)
        qa, ka, va, qb, kb, vb, qc, kc, vc = _qkv(x, mod, p["g_norm"], p["wqkv"], rope_tabs, p["gh"],
                                                  layer=l, seq=seq, tm=tm)
        oa = _attn_a(p["check_a"], qa, ka, va, p["kn_a"], layer=l, batch=batch, seq=seq, tq=t["tq_a"], tks=t["tks"],
                     unroll=t["unroll"])
        ob = _attn_b(qb, kb, vb, p["bias_b"], p["sink"], layer=l, batch=batch, seq=seq, qb=t["qb"])
        oc = _attn_c(p["check_c"], qc, kc, vc, p["bias_c"], p["cfar_c"], p["kn_c"], p["lam"], p["g_subln"],
                     layer=l, batch=batch, seq=seq, tks=t["tks"], unroll=t["unroll"])
        x = _out_proj(x, mod, p["g_norm"], oa, ob, oc, p["wgate"], p["wo"], layer=l, seq=seq,
                      tm=t["tm_out"])
        x = _ffn(x, mod, p["g_norm"], p["w_ff_in"], p["w_ff_out"], layer=l, which=1, seq=seq, tm=tm)
    return x


def _prepare(w_ff_in, w_ff_out, w_in, w_o, g_qa, g_ka, g_qb, g_kb, g_qc, g_kc, sink,
             lam_q1, lam_k1, lam_q2, lam_k2, g_subln, rel_bias):
    zeros = jnp.zeros_like(g_qa)
    gh = jnp.stack([g_qa, g_ka, g_qb, g_kb, jnp.tile(g_qc, (1, 2)), jnp.tile(g_kc, (1, 2)), zeros, zeros],
                   axis=1).astype(F32)
    table_b, table_c = rel_bias[:, :N_Q_HEADS], rel_bias[:, N_Q_HEADS:]
    half = N_BUCKETS // 2
    c_left, c_right = table_c[half - 1] * LOG2E, table_c[N_BUCKETS - 1] * LOG2E
    lam_init = jnp.asarray([0.8 - 0.6 * math.exp(-0.3 * l) for l in range(DEPTH)], F32)
    lam = (jnp.exp(jnp.sum(lam_q1.astype(F32) * lam_k1.astype(F32), axis=-1))
           - jnp.exp(jnp.sum(lam_q2.astype(F32) * lam_k2.astype(F32), axis=-1)) + lam_init)

    band = np.abs(np.arange(NEAR)[None, :] - BLOCK_Q - np.arange(BLOCK_Q)[:, None]) <= WINDOW
    near_c = (_near_bias(table_c) * LOG2E).reshape(
        N_KV_HEADS, Q_PER_KV, BLOCK_Q, NEAR // BLOCK_Q, BLOCK_Q).transpose(0, 3, 4, 1, 2).reshape(
        N_KV_HEADS, NEAR // BLOCK_Q, BLOCK_Q, Q_PER_KV * BLOCK_Q)
    cfar_c = jnp.repeat(jnp.stack([c_left, c_right, jnp.max(table_c, axis=0) * LOG2E]).astype(F32),
                        BLOCK_Q, axis=1).reshape(3, N_KV_HEADS, 1, Q_PER_KV * BLOCK_Q)

    def norm_bound(g, dim):
        return BOUND_MARGIN * math.sqrt(dim) * jnp.max(jnp.abs(g.astype(F32)), axis=-1)

    def key_norm(g, dim):
        return jnp.broadcast_to(norm_bound(g, dim)[:, None, None], (DEPTH, 1, HEAD_DIM))

    span_a = 2.0 * BOUND_MARGIN * QA_SCALE * norm_bound(g_qa, HEAD_DIM) * norm_bound(g_ka, HEAD_DIM)
    span_c = (2.0 * BOUND_MARGIN * QC_SCALE * norm_bound(g_qc, DIFF_DIM) * norm_bound(g_kc, DIFF_DIM)
              + (jnp.max(table_c) - jnp.min(table_c)) * LOG2E)

    return dict(
        w_ff_in=w_ff_in.astype(BF16), w_ff_out=w_ff_out.astype(BF16),
        wqkv=w_in[:, :, :QKV_W].astype(BF16), wgate=w_in[:, :, QKV_W:].astype(BF16), wo=w_o.astype(BF16),
        gh=gh,
        bias_b=jnp.where(band, _near_bias(table_b) * LOG2E, NEG).reshape(
            N_KV_HEADS, Q_PER_KV, BLOCK_Q, NEAR).transpose(0, 3, 1, 2).reshape(
            N_KV_HEADS, NEAR, Q_PER_KV * BLOCK_Q),
        sink=jnp.repeat(sink.astype(F32) * LOG2E, BLOCK_Q, axis=1).reshape(
            DEPTH, N_KV_HEADS, 1, Q_PER_KV * BLOCK_Q),
        bias_c=jnp.concatenate([cfar_c[0][:, None] + jnp.zeros((1, 1, BLOCK_Q, 1), F32), near_c,
                                cfar_c[1][:, None] + jnp.zeros((1, 1, BLOCK_Q, 1), F32)], axis=1),
        cfar_c=cfar_c,
        kn_a=key_norm(g_ka, HEAD_DIM), kn_c=key_norm(g_kc, DIFF_DIM),
        check_a=(span_a > SAFE_SPAN).astype(jnp.int32), check_c=(span_c > SAFE_SPAN).astype(jnp.int32),
        lam=jnp.broadcast_to(lam[:, None, None], (DEPTH, 1, HEAD_DIM)),
        g_subln=g_subln.astype(F32)[:, None, :],
    )


def kernel(x_prompt, x_sample, c_prompt, c_sample, w_ada, b_ada, g_norm, w_ff_in, w_ff_out, w_in, w_o,
           g_qa, g_ka, g_qb, g_kb, g_qc, g_kc, sink, lam_q1, lam_k1, lam_q2, lam_k2, g_subln, rel_bias):
    p = _prepare(w_ff_in, w_ff_out, w_in, w_o, g_qa, g_ka, g_qb, g_kb, g_qc, g_kc, sink,
                 lam_q1, lam_k1, lam_q2, lam_k2, g_subln, rel_bias)
    p["g_norm"] = g_norm.astype(F32)
    outs = []
    n_cond = 0
    conds = [c_prompt, c_sample]
    c_all = jnp.concatenate(conds + [jnp.zeros((ADA_ROWS - sum(c.shape[0] for c in conds), D_MODEL), F32)])
    mod_all = _ada(c_all, w_ada, b_ada)
    for x, c in ((x_prompt, c_prompt), (x_sample, c_sample)):
        batch, seq, _ = x.shape
        mod = mod_all[:, n_cond:n_cond + batch].reshape(DEPTH, batch, 9, D_MODEL)
        n_cond += batch
        y = _trunk(x.reshape(batch * seq, D_MODEL), mod, p, batch=batch, seq=seq)
        outs.append(y.reshape(batch, seq, D_MODEL))
    return tuple(outs)
```

```python
import functools
import math

import numpy as np
import jax
import jax.numpy as jnp
from jax import lax
from jax.experimental import pallas as pl
from jax.experimental.pallas import tpu as pltpu

F32 = jnp.float32
BF16 = jnp.bfloat16
ATTN_OUT = jnp.bfloat16

D_MODEL = 1024
DEPTH = 4
N_Q_HEADS = 8
N_KV_HEADS = 2
Q_PER_KV = N_Q_HEADS // N_KV_HEADS
HEAD_DIM = 128
DIFF_DIM = 64
D_FF = 2816
BLOCK_Q = 128
WINDOW = 128
GRID_W = 64
N_BUCKETS = 32
MAX_DISTANCE = 128
ROPE_THETA = 10000.0
EPS = 1e-6
NEG = -1e30
LOG2E = 1.4426950408889634
QA_SCALE = HEAD_DIM ** -0.5 * LOG2E
QC_SCALE = DIFF_DIM ** -0.5 * LOG2E

Q_W = N_Q_HEADS * HEAD_DIM
KV_W = N_KV_HEADS * HEAD_DIM
BRANCH_IN = Q_W + 2 * KV_W
QKV_W = 3 * BRANCH_IN
MXU_TILE = 256
FF_CHUNK = MXU_TILE
N_FF_CHUNKS = D_FF // FF_CHUNK
ADA_ROWS = 16
ADA_TN = 1536
NEAR = 3 * BLOCK_Q
VT_ROWS = HEAD_DIM + 16

F32_SUBLANES = 8
ROW_TILE = 512
STEPS_PER_REGION = 32
BOUND_MARGIN = 1.01

VMEM_LIMIT = 56 * 1024 * 1024


def _tiles(seq):
    tks = min(MXU_TILE, seq)
    return dict(
        tm=min(ROW_TILE, seq),
        tm_out=min(2 * ROW_TILE, seq),
        tq_a=(4 if seq <= STEPS_PER_REGION * tks else 2) * BLOCK_Q,
        tks=tks,
        qb=min(STEPS_PER_REGION, seq // BLOCK_Q),
        unroll=min(STEPS_PER_REGION, seq // tks),
    )


def _cparams(*sem):
    return pltpu.CompilerParams(dimension_semantics=sem, vmem_limit_bytes=VMEM_LIMIT)


def _dot(a, b):
    return jnp.dot(a, b, preferred_element_type=F32)


def _dot_nt(a, b):
    return lax.dot_general(a, b, (((1,), (1,)), ((), ())), preferred_element_type=F32)


def _ada_kernel(c_ref, w_ref, b_ref, o_ref):
    c = c_ref[...]
    a = (c * jax.nn.sigmoid(c)).astype(BF16)
    o_ref[0] = _dot(a, w_ref[0].astype(BF16)) + b_ref[0]


def _ada(c_all, w_ada, b_ada):
    n_out = w_ada.shape[-1]
    return pl.pallas_call(
        _ada_kernel,
        out_shape=jax.ShapeDtypeStruct((DEPTH, ADA_ROWS, n_out), F32),
        grid=(DEPTH, n_out // ADA_TN),
        in_specs=[
            pl.BlockSpec((ADA_ROWS, D_MODEL), lambda l, j: (0, 0)),
            pl.BlockSpec((1, D_MODEL, ADA_TN), lambda l, j: (l, 0, j)),
            pl.BlockSpec((1, 1, ADA_TN), lambda l, j: (l, 0, j)),
        ],
        out_specs=pl.BlockSpec((1, ADA_ROWS, ADA_TN), lambda l, j: (l, 0, j)),
        compiler_params=_cparams("arbitrary", "arbitrary"),
        name="ada",
    )(c_all, w_ada, b_ada.reshape(DEPTH, 1, n_out))


def _modulate(x, mod_ref, g, jj):
    ms = jnp.mean(x * x, axis=-1, keepdims=True)
    y = x * lax.rsqrt(ms + EPS) * g
    return y * (1.0 + mod_ref[0, 3 * jj + 1:3 * jj + 2, :]) + mod_ref[0, 3 * jj:3 * jj + 1, :]


def _ffn_kernel(x_ref, mod_ref, g_ref, win_ref, wout_ref, o_ref, nb_ref, acc_ref, *, jj):
    x = x_ref[...]
    nb_ref[...] = _modulate(x, mod_ref, g_ref[jj:jj + 1, :], jj).astype(BF16)
    acc_ref[...] = jnp.zeros_like(acc_ref)
    for c in range(N_FF_CHUNKS):
        cols = slice(c * FF_CHUNK, (c + 1) * FF_CHUNK)
        hg = _dot(nb_ref[...], win_ref[:, cols])
        hu = _dot(nb_ref[...], win_ref[:, D_FF + c * FF_CHUNK:D_FF + (c + 1) * FF_CHUNK])
        a = (hg * jax.nn.sigmoid(hg)) * hu
        acc_ref[...] += _dot(a.astype(BF16), wout_ref[cols, :])
    o_ref[...] = x + (0.5 * mod_ref[0, 3 * jj + 2:3 * jj + 3, :]) * acc_ref[...]


def _ffn(x, mod, g_norm, w_in, w_out, *, layer, which, seq, tm):
    n = x.shape[0]
    jj = 2 * which
    const = dict(pipeline_mode=pl.Buffered(1))
    return pl.pallas_call(
        functools.partial(_ffn_kernel, jj=jj),
        out_shape=jax.ShapeDtypeStruct((n, D_MODEL), F32),
        grid=(n // tm,),
        in_specs=[
            pl.BlockSpec((tm, D_MODEL), lambda i: (i, 0)),
            pl.BlockSpec((None, 1, 9, D_MODEL), lambda i: (layer, (i * tm) // seq, 0, 0)),
            pl.BlockSpec((None, 3, D_MODEL), lambda i: (layer, 0, 0)),
            pl.BlockSpec((None, None, D_MODEL, 2 * D_FF), lambda i: (layer, which, 0, 0), **const),
            pl.BlockSpec((None, None, D_FF, D_MODEL), lambda i: (layer, which, 0, 0), **const),
        ],
        out_specs=pl.BlockSpec((tm, D_MODEL), lambda i: (i, 0)),
        scratch_shapes=[pltpu.VMEM((tm, D_MODEL), BF16), pltpu.VMEM((tm, D_MODEL), F32)],
        input_output_aliases={} if (layer == 0 and which == 0) else {0: 0},
        compiler_params=_cparams("arbitrary"),
        name=f"ffn{which}",
    )(x, mod, g_norm, w_in, w_out)


def _head_norm(r, g):
    ms = jnp.mean(r * r, axis=-1, keepdims=True)
    return r * lax.rsqrt(ms + EPS) * g


def _half_norm(r, g2):
    sq = r * r
    lo = lax.broadcasted_iota(jnp.int32, r.shape, 1) < DIFF_DIM
    s_lo = jnp.sum(jnp.where(lo, sq, 0.0), axis=-1, keepdims=True)
    s_hi = jnp.sum(jnp.where(lo, 0.0, sq), axis=-1, keepdims=True)
    ms = jnp.where(lo, s_lo, s_hi) * (1.0 / DIFF_DIM)
    return r * lax.rsqrt(ms + EPS) * g2


def _qkv_kernel(x_ref, mod_ref, g_ref, w_ref, cos_ref, sa_ref, sb_ref, gh_ref,
                qa_ref, ka_ref, va_ref, qb_ref, kb_ref, vb_ref, qc_ref, kc_ref, vc_ref, nb_ref):
    nb_ref[...] = _modulate(x_ref[...], mod_ref, g_ref[1:2, :], 1).astype(BF16)
    cos, sa, sb = cos_ref[...], sa_ref[...], sb_ref[...]

    def rope(r):
        return (r * cos + pltpu.roll(r, HEAD_DIM - 32, 1) * sa + pltpu.roll(r, 32, 1) * sb)

    def proj(col, width):
        return _dot(nb_ref[...], w_ref[:, col:col + width])

    def heads(r, n_heads, fn, out_ref):
        for h in range(n_heads):
            out_ref[h] = fn(r[:, h * HEAD_DIM:(h + 1) * HEAD_DIM]).astype(BF16)

    def values_transposed(r, out_ref):
        for h in range(N_KV_HEADS):
            out_ref[h, :HEAD_DIM, :] = r[:, h * HEAD_DIM:(h + 1) * HEAD_DIM].T.astype(BF16)
            out_ref[h, HEAD_DIM:, :] = jnp.ones((VT_ROWS - HEAD_DIM, r.shape[0]), BF16)

    g_qa, g_ka, g_qb, g_kb = (gh_ref[i:i + 1, :] for i in range(4))
    g_qc, g_kc = gh_ref[4:5, :], gh_ref[5:6, :]
    col = 0
    heads(proj(col, Q_W), N_Q_HEADS, lambda r: rope(_head_norm(r, g_qa)) * QA_SCALE,qa_ref)
    col += Q_W
    heads(proj(col, KV_W), N_KV_HEADS, lambda r: rope(_head_norm(r, g_ka)), ka_ref)
    col += KV_W
    values_transposed(proj(col, KV_W), va_ref)
    col += KV_W
    heads(proj(col, Q_W), N_Q_HEADS, lambda r: _head_norm(r, g_qb) * QA_SCALE,qb_ref)
    col += Q_W
    heads(proj(col, KV_W), N_KV_HEADS, lambda r: _head_norm(r, g_kb), kb_ref)
    col += KV_W
    values_transposed(proj(col, KV_W), vb_ref)
    col += KV_W
    heads(proj(col, Q_W), N_Q_HEADS, lambda r: _half_norm(r, g_qc) * QC_SCALE,qc_ref)
    col += Q_W
    heads(proj(col, KV_W), N_KV_HEADS, lambda r: _half_norm(r, g_kc), kc_ref)
    col += KV_W
    values_transposed(proj(col, KV_W), vc_ref)


def _qkv(x, mod, g_norm, wqkv, rope_tabs, gh, *, layer, seq, tm):
    n = x.shape[0]
    tiles_per_seq = seq // tm
    q_shape = jax.ShapeDtypeStruct((N_Q_HEADS, n, HEAD_DIM), BF16)
    kv_shape = jax.ShapeDtypeStruct((N_KV_HEADS, n, HEAD_DIM), BF16)
    q_spec = pl.BlockSpec((N_Q_HEADS, tm, HEAD_DIM), lambda i: (0, i, 0))
    kv_spec = pl.BlockSpec((N_KV_HEADS, tm, HEAD_DIM), lambda i: (0, i, 0))
    tab_spec = pl.BlockSpec((tm, HEAD_DIM), lambda i: (i % tiles_per_seq, 0))
    v1_shape = jax.ShapeDtypeStruct((N_KV_HEADS, VT_ROWS, n), BF16)
    v1_spec = pl.BlockSpec((N_KV_HEADS, VT_ROWS, tm), lambda i: (0, 0, i))
    return pl.pallas_call(
        _qkv_kernel,
        out_shape=[q_shape, kv_shape, v1_shape] * 3,
        grid=(n // tm,),
        in_specs=[
            pl.BlockSpec((tm, D_MODEL), lambda i: (i, 0)),
            pl.BlockSpec((None, 1, 9, D_MODEL), lambda i: (layer, (i * tm) // seq, 0, 0)),
            pl.BlockSpec((None, 3, D_MODEL), lambda i: (layer, 0, 0)),
            pl.BlockSpec((None, D_MODEL, QKV_W), lambda i: (layer, 0, 0), pipeline_mode=pl.Buffered(1)),
            tab_spec, tab_spec, tab_spec,
            pl.BlockSpec((None, 8, HEAD_DIM), lambda i: (layer, 0, 0)),
        ],
        out_specs=[q_spec, kv_spec, v1_spec] * 3,
        scratch_shapes=[pltpu.VMEM((tm, D_MODEL), BF16)],
        compiler_params=_cparams("arbitrary"),
        name="qkv",
    )(x, mod, g_norm, wqkv, *rope_tabs, gh)


def _out_kernel(x_ref, mod_ref, g_ref, oa_ref, ob_ref, oc_ref, wg_ref, wo_ref, o_ref, mg_ref):
    x = x_ref[...]
    nb = _modulate(x, mod_ref, g_ref[1:2, :], 1).astype(BF16)
    for br, br_ref in enumerate((oa_ref, ob_ref, oc_ref)):
        gate = jax.nn.sigmoid(_dot(nb, wg_ref[:, br * D_MODEL:(br + 1) * D_MODEL]))
        for h in range(N_Q_HEADS):
            lanes = slice(h * HEAD_DIM, (h + 1) * HEAD_DIM)
            term = gate[:, lanes] * br_ref[h].astype(F32)
            if br == 0:
                mg_ref[:, lanes] = term
            else:
                mg_ref[:, lanes] += term
    o_ref[...] = x + mod_ref[0, 5:6, :] * _dot(mg_ref[...].astype(BF16), wo_ref[...])


def _out_proj(x, mod, g_norm, oa, ob, oc, wgate, wo, *, layer, seq, tm):
    n = x.shape[0]
    o_spec = pl.BlockSpec((N_Q_HEADS, tm, HEAD_DIM), lambda i: (0, i, 0))
    return pl.pallas_call(
        _out_kernel,
        out_shape=jax.ShapeDtypeStruct((n, D_MODEL), F32),
        grid=(n // tm,),
        in_specs=[
            pl.BlockSpec((tm, D_MODEL), lambda i: (i, 0)),
            pl.BlockSpec((None, 1, 9, D_MODEL), lambda i: (layer, (i * tm) // seq, 0, 0)),
            pl.BlockSpec((None, 3, D_MODEL), lambda i: (layer, 0, 0)),
            o_spec, o_spec, o_spec,
            pl.BlockSpec((None, D_MODEL, 3 * D_MODEL), lambda i: (layer, 0, 0), pipeline_mode=pl.Buffered(1)),
            pl.BlockSpec((None, D_MODEL, D_MODEL), lambda i: (layer, 0, 0), pipeline_mode=pl.Buffered(1)),
        ],
        out_specs=pl.BlockSpec((tm, D_MODEL), lambda i: (i, 0)),
        scratch_shapes=[pltpu.VMEM((tm, D_MODEL), F32)],
        input_output_aliases={0: 0},
        compiler_params=_cparams("arbitrary"),
        name="out_proj",
    )(x, mod, g_norm, oa, ob, oc, wgate, wo)


MIN_DENOM = 2.0 ** -64
SAFE_SPAN = 60.0


def _key_slice(idx, tks):
    return pl.ds(pl.multiple_of(idx * tks, tks), tks)


def _score_bound(q, key_norm):
    qf = q.astype(F32)
    qn2 = _dot_nt(jnp.ones((F32_SUBLANES, q.shape[1]), F32), qf * qf)[:1]
    return jnp.sqrt(qn2) * key_norm


def _bounded_iteration(t, last, q, k_ref, v_ref, acct_ref, s_ref, *, unroll, tks, shift, bias_fn=None):
    acct = acct_ref[...]
    s = s_ref[...]
    for u in range(unroll):
        idx = t * unroll + u
        s_next = None if (last and u == unroll - 1) else _dot_nt(k_ref[_key_slice(idx + 1, tks), :], q)
        if bias_fn is not None:
            s = bias_fn(idx, s)
        acct = acct + _dot(v_ref[:, _key_slice(idx, tks)], jnp.exp2(s - shift).astype(BF16))
        s = s_next
    acct_ref[...] = acct
    if not last:
        s_ref[...] = s


def _exact_pass(q, k_ref, v_ref, m_ref, acct_ref, *, n_sub, tks, bias_fn=None):
    m_ref[...] = jnp.full_like(m_ref, -jnp.inf)
    acct_ref[...] = jnp.zeros_like(acct_ref)

    def body(idx, carry):
        s = _dot_nt(k_ref[_key_slice(idx, tks), :], q)
        if bias_fn is not None:
            s = bias_fn(idx, s)
        m = m_ref[...]
        m_new = jnp.maximum(m, jnp.max(s, axis=0, keepdims=True))
        p = jnp.exp2(s - m_new).astype(BF16)
        acct_ref[...] = jnp.exp2(m - m_new) * acct_ref[...] + _dot(v_ref[:, _key_slice(idx, tks)], p)
        m_ref[...] = m_new
        return carry

    lax.fori_loop(0, n_sub, body, 0)


def _recompute_if_flushed(check, acct_ref, exact_pass):
    @pl.when(check != 0)
    def _():
        @pl.when(jnp.min(acct_ref[HEAD_DIM:HEAD_DIM + 1, :]) < MIN_DENOM)
        def _():
            exact_pass()


def _attn_a_kernel(check_ref, q_ref, k_ref, v_ref, kn_ref, o_ref, m_ref, acct_ref, s_ref,
                   *, layer, n_iter, unroll, tks):
    rows = acct_ref.shape[1]
    q = q_ref[...].reshape(rows, HEAD_DIM)
    shift = _score_bound(q, kn_ref[:, :1])
    acct_ref[...] = jnp.zeros_like(acct_ref)
    s_ref[...] = _dot_nt(k_ref[0:tks, :], q)
    refs = (q, k_ref, v_ref, acct_ref, s_ref)

    def body(t, carry):
        _bounded_iteration(t, False, *refs, unroll=unroll, tks=tks, shift=shift)
        return carry

    lax.fori_loop(0, n_iter - 1, body, 0)
    _bounded_iteration(n_iter - 1, True, *refs, unroll=unroll, tks=tks, shift=shift)

    _recompute_if_flushed(check_ref[layer], acct_ref, lambda: _exact_pass(
        q, k_ref, v_ref, m_ref, acct_ref, n_sub=n_iter * unroll, tks=tks))

    acct = acct_ref[...]
    o_t = acct[:HEAD_DIM] / acct[HEAD_DIM:HEAD_DIM + 1]
    o_ref[...] = o_t.T.astype(o_ref.dtype).reshape(o_ref.shape)


def _attn_a(check, q, k, v1t, key_norm, *, layer, batch, seq, tq, tks, unroll):
    n = q.shape[1]
    nq = seq // tq
    rows = Q_PER_KV * tq
    q_map = lambda b, g, i: (g, b * nq + i, 0)
    return pl.pallas_call(
        functools.partial(_attn_a_kernel, layer=layer, n_iter=seq // (tks * unroll), unroll=unroll, tks=tks),
        out_shape=jax.ShapeDtypeStruct((N_Q_HEADS, n, HEAD_DIM), ATTN_OUT),
        grid=(batch, N_KV_HEADS, nq),
        in_specs=[
            pl.BlockSpec(memory_space=pltpu.SMEM),
            pl.BlockSpec((Q_PER_KV, tq, HEAD_DIM), q_map),
            pl.BlockSpec((None, seq, HEAD_DIM), lambda b, g, i: (g, b, 0)),
            pl.BlockSpec((None, VT_ROWS, seq), lambda b, g, i: (g, 0, b)),
            pl.BlockSpec((None, 1, HEAD_DIM), lambda b, g, i: (layer, 0, 0)),
        ],
        out_specs=pl.BlockSpec((Q_PER_KV, tq, HEAD_DIM), q_map),
        scratch_shapes=[pltpu.VMEM((1, rows), F32), pltpu.VMEM((VT_ROWS, rows), F32),
                        pltpu.VMEM((tks, rows), F32)],
        compiler_params=_cparams("arbitrary", "arbitrary", "arbitrary"),
        name="attn_axial",
    )(check, q, k, v1t, key_norm)


def _attn_b_kernel(q_ref, kp_ref, kc_ref, kn_ref, vp_ref, vc_ref, vn_ref, bias_ref, sink_ref, o_ref,
                   *, nb, qb):
    i = pl.program_id(2)
    rows = Q_PER_KV * BLOCK_Q
    kwin = jnp.concatenate([kp_ref[...], kc_ref[...], kn_ref[...]], axis=0)
    vwin = jnp.concatenate([vp_ref[...], vc_ref[...], vn_ref[...]], axis=1)
    bias = bias_ref[...]
    sink = sink_ref[...]
    for u in range(qb):
        blk = i * qb + u
        q = q_ref[:, u * BLOCK_Q:(u + 1) * BLOCK_Q, :].reshape(rows, HEAD_DIM)
        s = _dot_nt(kwin[u * BLOCK_Q:u * BLOCK_Q + NEAR], q) + bias
        s = jnp.concatenate([jnp.where(blk > 0, s[:BLOCK_Q], NEG), s[BLOCK_Q:2 * BLOCK_Q],
                             jnp.where(blk < nb - 1, s[2 * BLOCK_Q:], NEG)], axis=0)
        m = jnp.maximum(jnp.max(s, axis=0, keepdims=True), sink)
        p = jnp.exp2(s - m).astype(BF16)
        acct = _dot(vwin[:, u * BLOCK_Q:u * BLOCK_Q + NEAR], p)
        den = acct[HEAD_DIM:HEAD_DIM + 1] + jnp.exp2(sink - m)
        o_t = acct[:HEAD_DIM] / den
        o_ref[:, u * BLOCK_Q:(u + 1) * BLOCK_Q, :] = o_t.T.astype(o_ref.dtype).reshape(
            Q_PER_KV, BLOCK_Q, HEAD_DIM)


def _attn_b(q, k, v1t, bias_t, sink_t, *, layer, batch, seq, qb):
    n = q.shape[1]
    nb = seq // BLOCK_Q
    steps = nb // qb
    rows = Q_PER_KV * BLOCK_Q
    cur_map = lambda b, g, i: (g, b * steps + i, 0)
    prev_map = lambda b, g, i: (g, b * nb + jnp.maximum(i * qb - 1, 0), 0)
    next_map = lambda b, g, i: (g, b * nb + jnp.minimum(i * qb + qb, nb - 1), 0)
    t = lambda index_map: (lambda b, g, i: (index_map(b, g, i)[0], 0, index_map(b, g, i)[1]))
    k_specs = [pl.BlockSpec((None, BLOCK_Q, HEAD_DIM), prev_map),
               pl.BlockSpec((None, qb * BLOCK_Q, HEAD_DIM), cur_map),
               pl.BlockSpec((None, BLOCK_Q, HEAD_DIM), next_map)]
    v_specs = [pl.BlockSpec((None, VT_ROWS, BLOCK_Q), t(prev_map)),
               pl.BlockSpec((None, VT_ROWS, qb * BLOCK_Q), t(cur_map)),
               pl.BlockSpec((None, VT_ROWS, BLOCK_Q), t(next_map))]
    return pl.pallas_call(
        functools.partial(_attn_b_kernel, nb=nb, qb=qb),
        out_shape=jax.ShapeDtypeStruct((N_Q_HEADS, n, HEAD_DIM), ATTN_OUT),
        grid=(batch, N_KV_HEADS, steps),
        in_specs=[pl.BlockSpec((Q_PER_KV, qb * BLOCK_Q, HEAD_DIM), cur_map)] + k_specs + v_specs + [
            pl.BlockSpec((None, NEAR, rows), lambda b, g, i: (g, 0, 0)),
            pl.BlockSpec((None, None, 1, rows), lambda b, g, i: (layer, g, 0, 0)),
        ],
        out_specs=pl.BlockSpec((Q_PER_KV, qb * BLOCK_Q, HEAD_DIM), cur_map),
        compiler_params=_cparams("arbitrary", "arbitrary", "arbitrary"),
        name="attn_window",
    )(q, k, k, k, v1t, v1t, v1t, bias_t, sink_t)


def _attn_c_kernel(check_ref, q_ref, k_ref, v_ref, bias_ref, cfar_ref, kn_ref, lam_ref, gs_ref, o_ref,
                   m_ref, acct_ref, s_ref, *, layer, n_iter, unroll, tks, nblk, out_scale):
    n_blk = pl.program_id(2)
    rows = Q_PER_KV * BLOCK_Q
    blocks_per_sub = tks // BLOCK_Q
    blocks_per_iter = unroll * blocks_per_sub

    q = q_ref[...].reshape(rows, HEAD_DIM)
    lo = lax.broadcasted_iota(jnp.int32, q.shape, 1) < DIFF_DIM
    zero = jnp.zeros_like(q)
    q2 = jnp.concatenate([jnp.where(lo, q, zero), jnp.where(lo, zero, q)], axis=0)
    acct_ref[...] = jnp.zeros_like(acct_ref)
    s_ref[...] = _dot_nt(k_ref[0:tks, :], q2)
    refs = (q2, k_ref, v_ref, acct_ref, s_ref)

    c_left, c_right, c_max = cfar_ref[0], cfar_ref[1], cfar_ref[2]
    t_lo = jnp.maximum(n_blk - 1, 0) // blocks_per_iter
    t_hi = jnp.minimum(n_blk + 1, nblk - 1) // blocks_per_iter
    two = lambda x: jnp.concatenate([x, x], axis=1)
    shift = _score_bound(q2, kn_ref[:, :1]) + two(c_max)

    def add_bias(idx, s):
        tile_of = lambda key_blk: jnp.clip(key_blk - n_blk + 2, 0, NEAR // BLOCK_Q + 1)
        bias = jnp.concatenate([bias_ref[tile_of(idx * blocks_per_sub + kb)] for kb in range(blocks_per_sub)],
                               axis=0)
        return jnp.concatenate([s[:, :rows] + bias, s[:, rows:] + bias], axis=1)

    def iteration(t, last):
        is_near = (t >= t_lo) & (t <= t_hi)

        @pl.when(jnp.logical_not(is_near))
        def _():
            side = jnp.where(t < t_lo, c_left, c_right)
            _bounded_iteration(t, last, *refs, unroll=unroll, tks=tks, shift=shift - two(side))

        @pl.when(is_near)
        def _():
            _bounded_iteration(t, last, *refs, unroll=unroll, tks=tks, shift=shift, bias_fn=add_bias)

    def body(t, carry):
        iteration(t, False)
        return carry

    lax.fori_loop(0, n_iter - 1, body, 0)
    iteration(n_iter - 1, True)

    _recompute_if_flushed(check_ref[layer], acct_ref, lambda: _exact_pass(
        q2, k_ref, v_ref, m_ref, acct_ref, n_sub=n_iter * unroll, tks=tks, bias_fn=add_bias))

    acct = acct_ref[...]
    o1_t = acct[:HEAD_DIM, :rows] / acct[HEAD_DIM:HEAD_DIM + 1, :rows]
    o2_t = acct[:HEAD_DIM, rows:] / acct[HEAD_DIM:HEAD_DIM + 1, rows:]
    o_t = o1_t - lam_ref[:, :1] * o2_t
    ms = jnp.mean(o_t * o_t, axis=0, keepdims=True)
    o = (o_t * lax.rsqrt(ms + EPS)).T * gs_ref[...] * out_scale
    o_ref[...] = o.astype(o_ref.dtype).reshape(o_ref.shape)


def _attn_c(check, q, k, v1t, bias_t, cfar_t, key_norm, lam, g_subln, *, layer, batch, seq, tks, unroll):
    n = q.shape[1]
    nq = seq // BLOCK_Q
    rows = Q_PER_KV * BLOCK_Q
    lam_init = 0.8 - 0.6 * math.exp(-0.3 * layer)
    q_map = lambda b, g, i: (g, b * nq + i, 0)
    return pl.pallas_call(
        functools.partial(_attn_c_kernel, layer=layer, n_iter=seq // (tks * unroll), unroll=unroll, tks=tks,
                          nblk=nq, out_scale=1.0 - lam_init),
        out_shape=jax.ShapeDtypeStruct((N_Q_HEADS, n, HEAD_DIM), ATTN_OUT),
        grid=(batch, N_KV_HEADS, nq),
        in_specs=[
            pl.BlockSpec(memory_space=pltpu.SMEM),
            pl.BlockSpec((Q_PER_KV, BLOCK_Q, HEAD_DIM), q_map),
            pl.BlockSpec((None, seq, HEAD_DIM), lambda b, g, i: (g, b, 0)),
            pl.BlockSpec((None, VT_ROWS, seq), lambda b, g, i: (g, 0, b)),
            pl.BlockSpec((None, NEAR // BLOCK_Q + 2, BLOCK_Q, rows), lambda b, g, i: (g, 0, 0, 0)),
            pl.BlockSpec((3, None, 1, rows), lambda b, g, i: (0, g, 0, 0)),
            pl.BlockSpec((None, 1, HEAD_DIM), lambda b, g, i: (layer, 0, 0)),
            pl.BlockSpec((None, 1, HEAD_DIM), lambda b, g, i: (layer, 0, 0)),
            pl.BlockSpec((None, 1, HEAD_DIM), lambda b, g, i: (layer, 0, 0)),
        ],
        out_specs=pl.BlockSpec((Q_PER_KV, BLOCK_Q, HEAD_DIM), q_map),
        scratch_shapes=[pltpu.VMEM((1, 2 * rows), F32), pltpu.VMEM((VT_ROWS, 2 * rows), F32),
                        pltpu.VMEM((tks, 2 * rows), F32)],
        compiler_params=_cparams("arbitrary", "arbitrary", "arbitrary"),
        name="attn_diff",
    )(check, q, k, v1t, bias_t, cfar_t, key_norm, lam, g_subln)


def _t5_bucket_np(rel):
    half = N_BUCKETS // 2
    max_exact = half // 2
    ret = np.where(rel > 0, half, 0)
    n = np.abs(rel)
    ratio = np.log(np.maximum(n, 1).astype(np.float32) / np.float32(max_exact)) / np.float32(
        math.log(MAX_DISTANCE / max_exact))
    large = max_exact + (ratio * np.float32(half - max_exact)).astype(np.int32)
    large = np.minimum(large, half - 1)
    return (ret + np.where(n < max_exact, n, large)).astype(np.int32)


def _near_bias(table):
    r = np.arange(BLOCK_Q)[:, None]
    c = np.arange(NEAR)[None, :]
    bucket = _t5_bucket_np(c - BLOCK_Q - r)
    onehot = (bucket.reshape(-1, 1) == np.arange(N_BUCKETS)[None, :]).astype(np.float32)
    rows = jnp.dot(jnp.asarray(onehot), table.astype(F32), precision=lax.Precision.HIGHEST)
    return rows.reshape(BLOCK_Q, NEAR, table.shape[1]).transpose(2, 0, 1)


def _rope_tables(seq):
    rows = seq // GRID_W
    nfreq = HEAD_DIM // 4
    inv = ROPE_THETA ** (-jnp.arange(nfreq, dtype=F32) / nfreq)
    ang_r = jnp.arange(rows).astype(F32)[:, None] * inv
    ang_c = jnp.arange(GRID_W).astype(F32)[:, None] * inv
    by_row = lambda t: jnp.broadcast_to(t[:, None, :], (rows, GRID_W, nfreq)).reshape(seq, nfreq)
    by_col = lambda t: jnp.broadcast_to(t[None, :, :], (rows, GRID_W, nfreq)).reshape(seq, nfreq)
    table = lambda fn: jnp.concatenate([by_row(fn(ang_r))] * 2 + [by_col(fn(ang_c))] * 2, axis=-1)
    cos, sin = table(jnp.cos), table(jnp.sin)
    first = (np.arange(HEAD_DIM) % (HEAD_DIM // 2)) < HEAD_DIM // 4
    return cos, jnp.where(first, -sin, 0.0), jnp.where(first, 0.0, sin)


def _trunk(x, mod, p, *, batch, seq):
    t = _tiles(seq)
    tm = t["tm"]
    rope_tabs = _rope_tables(seq)
    for l in range(DEPTH):
        x = _ffn(x, mod, p["g_norm"], p["w_ff_in"], p["w_ff_out"], layer=l, which=0, seq=seq, tm=tm)
        qa, ka, va, qb, kb, vb, qc, kc, vc = _qkv(x, mod, p["g_norm"], p["wqkv"], rope_tabs, p["gh"],
                                                  layer=l, seq=seq, tm=tm)
        oa = _attn_a(p["check_a"], qa, ka, va, p["kn_a"], layer=l, batch=batch, seq=seq, tq=t["tq_a"], tks=t["tks"],
                     unroll=t["unroll"])
        ob = _attn_b(qb, kb, vb, p["bias_b"], p["sink"], layer=l, batch=batch, seq=seq, qb=t["qb"])
        oc = _attn_c(p["check_c"], qc, kc, vc, p["bias_c"], p["cfar_c"], p["kn_c"], p["lam"], p["g_subln"],
                     layer=l, batch=batch, seq=seq, tks=t["tks"], unroll=t["unroll"])
        x = _out_proj(x, mod, p["g_norm"], oa, ob, oc, p["wgate"], p["wo"], layer=l, seq=seq,
                      tm=t["tm_out"])
        x = _ffn(x, mod, p["g_norm"], p["w_ff_in"], p["w_ff_out"], layer=l, which=1, seq=seq, tm=tm)
    return x


def _prepare(w_ff_in, w_ff_out, w_in, w_o, g_qa, g_ka, g_qb, g_kb, g_qc, g_kc, sink,
             lam_q1, lam_k1, lam_q2, lam_k2, g_subln, rel_bias):
    zeros = jnp.zeros_like(g_qa)
    gh = jnp.stack([g_qa, g_ka, g_qb, g_kb, jnp.tile(g_qc, (1, 2)), jnp.tile(g_kc, (1, 2)), zeros, zeros],
                   axis=1).astype(F32)
    table_b, table_c = rel_bias[:, :N_Q_HEADS], rel_bias[:, N_Q_HEADS:]
    half = N_BUCKETS // 2
    c_left, c_right = table_c[half - 1] * LOG2E, table_c[N_BUCKETS - 1] * LOG2E
    lam_init = jnp.asarray([0.8 - 0.6 * math.exp(-0.3 * l) for l in range(DEPTH)], F32)
    lam = (jnp.exp(jnp.sum(lam_q1.astype(F32) * lam_k1.astype(F32), axis=-1))
           - jnp.exp(jnp.sum(lam_q2.astype(F32) * lam_k2.astype(F32), axis=-1)) + lam_init)

    band = np.abs(np.arange(NEAR)[None, :] - BLOCK_Q - np.arange(BLOCK_Q)[:, None]) <= WINDOW
    near_c = (_near_bias(table_c) * LOG2E).reshape(
        N_KV_HEADS, Q_PER_KV, BLOCK_Q, NEAR // BLOCK_Q, BLOCK_Q).transpose(0, 3, 4, 1, 2).reshape(
        N_KV_HEADS, NEAR // BLOCK_Q, BLOCK_Q, Q_PER_KV * BLOCK_Q)
    cfar_c = jnp.repeat(jnp.stack([c_left, c_right, jnp.max(table_c, axis=0) * LOG2E]).astype(F32),
                        BLOCK_Q, axis=1).reshape(3, N_KV_HEADS, 1, Q_PER_KV * BLOCK_Q)

    def norm_bound(g, dim):
        return BOUND_MARGIN * math.sqrt(dim) * jnp.max(jnp.abs(g.astype(F32)), axis=-1)

    def key_norm(g, dim):
        return jnp.broadcast_to(norm_bound(g, dim)[:, None, None], (DEPTH, 1, HEAD_DIM))

    span_a = 2.0 * BOUND_MARGIN * QA_SCALE * norm_bound(g_qa, HEAD_DIM) * norm_bound(g_ka, HEAD_DIM)
    span_c = (2.0 * BOUND_MARGIN * QC_SCALE * norm_bound(g_qc, DIFF_DIM) * norm_bound(g_kc, DIFF_DIM)
              + (jnp.max(table_c) - jnp.min(table_c)) * LOG2E)

    return dict(
        w_ff_in=w_ff_in.astype(BF16), w_ff_out=w_ff_out.astype(BF16),
        wqkv=w_in[:, :, :QKV_W].astype(BF16), wgate=w_in[:, :, QKV_W:].astype(BF16), wo=w_o.astype(BF16),
        gh=gh,
        bias_b=jnp.where(band, _near_bias(table_b) * LOG2E, NEG).reshape(
            N_KV_HEADS, Q_PER_KV, BLOCK_Q, NEAR).transpose(0, 3, 1, 2).reshape(
            N_KV_HEADS, NEAR, Q_PER_KV * BLOCK_Q),
        sink=jnp.repeat(sink.astype(F32) * LOG2E, BLOCK_Q, axis=1).reshape(
            DEPTH, N_KV_HEADS, 1, Q_PER_KV * BLOCK_Q),
        bias_c=jnp.concatenate([cfar_c[0][:, None] + jnp.zeros((1, 1, BLOCK_Q, 1), F32), near_c,
                                cfar_c[1][:, None] + jnp.zeros((1, 1, BLOCK_Q, 1), F32)], axis=1),
        cfar_c=cfar_c,
        kn_a=key_norm(g_ka, HEAD_DIM), kn_c=key_norm(g_kc, DIFF_DIM),
        check_a=(span_a > SAFE_SPAN).astype(jnp.int32), check_c=(span_c > SAFE_SPAN).astype(jnp.int32),
        lam=jnp.broadcast_to(lam[:, None, None], (DEPTH, 1, HEAD_DIM)),
        g_subln=g_subln.astype(F32)[:, None, :],
    )


def kernel(x_prompt, x_sample, c_prompt, c_sample, w_ada, b_ada, g_norm, w_ff_in, w_ff_out, w_in, w_o,
           g_qa, g_ka, g_qb, g_kb, g_qc, g_kc, sink, lam_q1, lam_k1, lam_q2, lam_k2, g_subln, rel_bias):
    p = _prepare(w_ff_in, w_ff_out, w_in, w_o, g_qa, g_ka, g_qb, g_kb, g_qc, g_kc, sink,
                 lam_q1, lam_k1, lam_q2, lam_k2, g_subln, rel_bias)
    p["g_norm"] = g_norm.astype(F32)
    outs = []
    n_cond = 0
    conds = [c_prompt, c_sample]
    c_all = jnp.concatenate(conds + [jnp.zeros((ADA_ROWS - sum(c.shape[0] for c in conds), D_MODEL), F32)])
    mod_all = _ada(c_all, w_ada, b_ada)
    for x, c in ((x_prompt, c_prompt), (x_sample, c_sample)):
        batch, seq, _ = x.shape
        mod = mod_all[:, n_cond:n_cond + batch].reshape(DEPTH, batch, 9, D_MODEL)
        n_cond += batch
        y = _trunk(x.reshape(batch * seq, D_MODEL), mod, p, batch=batch, seq=seq)
        outs.append(y.reshape(batch, seq, D_MODEL))
    return tuple(outs)
```

```python
import functools
import math

import numpy as np
import jax
import jax.numpy as jnp
from jax import lax
from jax.experimental import pallas as pl
from jax.experimental.pallas import tpu as pltpu

F32 = jnp.float32
BF16 = jnp.bfloat16
ATTN_OUT = jnp.bfloat16

D_MODEL = 1024
DEPTH = 4
N_Q_HEADS = 8
N_KV_HEADS = 2
Q_PER_KV = N_Q_HEADS // N_KV_HEADS
HEAD_DIM = 128
DIFF_DIM = 64
D_FF = 2816
BLOCK_Q = 128
WINDOW = 128
GRID_W = 64
N_BUCKETS = 32
MAX_DISTANCE = 128
ROPE_THETA = 10000.0
EPS = 1e-6
NEG = -1e30
LOG2E = 1.4426950408889634
QA_SCALE = HEAD_DIM ** -0.5 * LOG2E
QC_SCALE = DIFF_DIM ** -0.5 * LOG2E

Q_W = N_Q_HEADS * HEAD_DIM
KV_W = N_KV_HEADS * HEAD_DIM
BRANCH_IN = Q_W + 2 * KV_W
QKV_W = 3 * BRANCH_IN
MXU_TILE = 256
FF_CHUNK = MXU_TILE
N_FF_CHUNKS = D_FF // FF_CHUNK
ADA_ROWS = 16
ADA_TN = 1536
NEAR = 3 * BLOCK_Q
VT_ROWS = HEAD_DIM + 16

F32_SUBLANES = 8
ROW_TILE = 512
STEPS_PER_REGION = 32
BOUND_MARGIN = 1.01

VMEM_LIMIT = 56 * 1024 * 1024


def _tiles(seq):
    tks = min(MXU_TILE, seq)
    return dict(
        tm=min(ROW_TILE, seq),
        tm_out=min(2 * ROW_TILE, seq),
        tq_a=(4 if seq <= STEPS_PER_REGION * tks else 2) * BLOCK_Q,
        tks=tks,
        qb=min(STEPS_PER_REGION, seq // BLOCK_Q),
        unroll=min(STEPS_PER_REGION, seq // tks),
    )


def _cparams(*sem):
    return pltpu.CompilerParams(dimension_semantics=sem, vmem_limit_bytes=VMEM_LIMIT)


def _dot(a, b):
    return jnp.dot(a, b, preferred_element_type=F32)


def _dot_nt(a, b):
    return lax.dot_general(a, b, (((1,), (1,)), ((), ())), preferred_element_type=F32)


def _ada_kernel(c_ref, w_ref, b_ref, o_ref):
    c = c_ref[...]
    a = (c * jax.nn.sigmoid(c)).astype(BF16)
    o_ref[0] = _dot(a, w_ref[0].astype(BF16)) + b_ref[0]


def _ada(c_all, w_ada, b_ada):
    n_out = w_ada.shape[-1]
    return pl.pallas_call(
        _ada_kernel,
        out_shape=jax.ShapeDtypeStruct((DEPTH, ADA_ROWS, n_out), F32),
        grid=(DEPTH, n_out // ADA_TN),
        in_specs=[
            pl.BlockSpec((ADA_ROWS, D_MODEL), lambda l, j: (0, 0)),
            pl.BlockSpec((1, D_MODEL, ADA_TN), lambda l, j: (l, 0, j)),
            pl.BlockSpec((1, 1, ADA_TN), lambda l, j: (l, 0, j)),
        ],
        out_specs=pl.BlockSpec((1, ADA_ROWS, ADA_TN), lambda l, j: (l, 0, j)),
        compiler_params=_cparams("arbitrary", "arbitrary"),
        name="ada",
    )(c_all, w_ada, b_ada.reshape(DEPTH, 1, n_out))


def _modulate(x, mod_ref, g, jj):
    ms = jnp.mean(x * x, axis=-1, keepdims=True)
    y = x * lax.rsqrt(ms + EPS) * g
    return y * (1.0 + mod_ref[0, 3 * jj + 1:3 * jj + 2, :]) + mod_ref[0, 3 * jj:3 * jj + 1, :]


def _ffn_kernel(x_ref, mod_ref, g_ref, win_ref, wout_ref, o_ref, nb_ref, acc_ref, *, jj):
    x = x_ref[...]
    nb_ref[...] = _modulate(x, mod_ref, g_ref[jj:jj + 1, :], jj).astype(BF16)
    acc_ref[...] = jnp.zeros_like(acc_ref)
    for c in range(N_FF_CHUNKS):
        cols = slice(c * FF_CHUNK, (c + 1) * FF_CHUNK)
        hg = _dot(nb_ref[...], win_ref[:, cols])
        hu = _dot(nb_ref[...], win_ref[:, D_FF + c * FF_CHUNK:D_FF + (c + 1) * FF_CHUNK])
        a = (hg * jax.nn.sigmoid(hg)) * hu
        acc_ref[...] += _dot(a.astype(BF16), wout_ref[cols, :])
    o_ref[...] = x + (0.5 * mod_ref[0, 3 * jj + 2:3 * jj + 3, :]) * acc_ref[...]


def _ffn(x, mod, g_norm, w_in, w_out, *, layer, which, seq, tm):
    n = x.shape[0]
    jj = 2 * which
    const = dict(pipeline_mode=pl.Buffered(1))
    return pl.pallas_call(
        functools.partial(_ffn_kernel, jj=jj),
        out_shape=jax.ShapeDtypeStruct((n, D_MODEL), F32),
        grid=(n // tm,),
        in_specs=[
            pl.BlockSpec((tm, D_MODEL), lambda i: (i, 0)),
            pl.BlockSpec((None, 1, 9, D_MODEL), lambda i: (layer, (i * tm) // seq, 0, 0)),
            pl.BlockSpec((None, 3, D_MODEL), lambda i: (layer, 0, 0)),
            pl.BlockSpec((None, None, D_MODEL, 2 * D_FF), lambda i: (layer, which, 0, 0), **const),
            pl.BlockSpec((None, None, D_FF, D_MODEL), lambda i: (layer, which, 0, 0), **const),
        ],
        out_specs=pl.BlockSpec((tm, D_MODEL), lambda i: (i, 0)),
        scratch_shapes=[pltpu.VMEM((tm, D_MODEL), BF16), pltpu.VMEM((tm, D_MODEL), F32)],
        input_output_aliases={} if (layer == 0 and which == 0) else {0: 0},
        compiler_params=_cparams("arbitrary"),
        name=f"ffn{which}",
    )(x, mod, g_norm, w_in, w_out)


def _head_norm(r, g):
    ms = jnp.mean(r * r, axis=-1, keepdims=True)
    return r * lax.rsqrt(ms + EPS) * g


def _half_norm(r, g2):
    sq = r * r
    lo = lax.broadcasted_iota(jnp.int32, r.shape, 1) < DIFF_DIM
    s_lo = jnp.sum(jnp.where(lo, sq, 0.0), axis=-1, keepdims=True)
    s_hi = jnp.sum(jnp.where(lo, 0.0, sq), axis=-1, keepdims=True)
    ms = jnp.where(lo, s_lo, s_hi) * (1.0 / DIFF_DIM)
    return r * lax.rsqrt(ms + EPS) * g2


def _qkv_kernel(x_ref, mod_ref, g_ref, w_ref, cos_ref, sin_ref, gh_ref,
                qa_ref, ka_ref, va_ref, qb_ref, kb_ref, vb_ref, qc_ref, kc_ref, vc_ref, nb_ref):
    nb_ref[...] = _modulate(x_ref[...], mod_ref, g_ref[1:2, :], 1).astype(BF16)
    cos, sin = cos_ref[...], sin_ref[...]

    def rope(r):
        return r * cos + pltpu.roll(r, HEAD_DIM // 2, 1) * sin

    def proj(col, width):
        return _dot(nb_ref[...], w_ref[:, col:col + width])

    def heads(r, n_heads, fn, out_ref):
        for h in range(n_heads):
            out_ref[h] = fn(r[:, h * HEAD_DIM:(h + 1) * HEAD_DIM]).astype(BF16)

    def values_transposed(r, out_ref):
        for h in range(N_KV_HEADS):
            out_ref[h, :HEAD_DIM, :] = r[:, h * HEAD_DIM:(h + 1) * HEAD_DIM].T.astype(BF16)
            out_ref[h, HEAD_DIM:, :] = jnp.ones((VT_ROWS - HEAD_DIM, r.shape[0]), BF16)

    g_qa, g_ka, g_qb, g_kb = (gh_ref[i:i + 1, :] for i in range(4))
    g_qc, g_kc = gh_ref[4:5, :], gh_ref[5:6, :]
    col = 0
    heads(proj(col, Q_W), N_Q_HEADS, lambda r: rope(_head_norm(r, g_qa)) * QA_SCALE,qa_ref)
    col += Q_W
    heads(proj(col, KV_W), N_KV_HEADS, lambda r: rope(_head_norm(r, g_ka)), ka_ref)
    col += KV_W
    values_transposed(proj(col, KV_W), va_ref)
    col += KV_W
    heads(proj(col, Q_W), N_Q_HEADS, lambda r: _head_norm(r, g_qb) * QA_SCALE,qb_ref)
    col += Q_W
    heads(proj(col, KV_W), N_KV_HEADS, lambda r: _head_norm(r, g_kb), kb_ref)
    col += KV_W
    values_transposed(proj(col, KV_W), vb_ref)
    col += KV_W
    heads(proj(col, Q_W), N_Q_HEADS, lambda r: _half_norm(r, g_qc) * QC_SCALE,qc_ref)
    col += Q_W
    heads(proj(col, KV_W), N_KV_HEADS, lambda r: _half_norm(r, g_kc), kc_ref)
    col += KV_W
    values_transposed(proj(col, KV_W), vc_ref)


def _qkv(x, mod, g_norm, wqkv, rope_tabs, gh, *, layer, seq, tm):
    n = x.shape[0]
    tiles_per_seq = seq // tm
    q_shape = jax.ShapeDtypeStruct((N_Q_HEADS, n, HEAD_DIM), BF16)
    kv_shape = jax.ShapeDtypeStruct((N_KV_HEADS, n, HEAD_DIM), BF16)
    q_spec = pl.BlockSpec((N_Q_HEADS, tm, HEAD_DIM), lambda i: (0, i, 0))
    kv_spec = pl.BlockSpec((N_KV_HEADS, tm, HEAD_DIM), lambda i: (0, i, 0))
    tab_spec = pl.BlockSpec((tm, HEAD_DIM), lambda i: (i % tiles_per_seq, 0))
    v1_shape = jax.ShapeDtypeStruct((N_KV_HEADS, VT_ROWS, n), BF16)
    v1_spec = pl.BlockSpec((N_KV_HEADS, VT_ROWS, tm), lambda i: (0, 0, i))
    return pl.pallas_call(
        _qkv_kernel,
        out_shape=[q_shape, kv_shape, v1_shape] * 3,
        grid=(n // tm,),
        in_specs=[
            pl.BlockSpec((tm, D_MODEL), lambda i: (i, 0)),
            pl.BlockSpec((None, 1, 9, D_MODEL), lambda i: (layer, (i * tm) // seq, 0, 0)),
            pl.BlockSpec((None, 3, D_MODEL), lambda i: (layer, 0, 0)),
            pl.BlockSpec((None, D_MODEL, QKV_W), lambda i: (layer, 0, 0), pipeline_mode=pl.Buffered(1)),
            tab_spec, tab_spec,
            pl.BlockSpec((None, 8, HEAD_DIM), lambda i: (layer, 0, 0)),
        ],
        out_specs=[q_spec, kv_spec, v1_spec] * 3,
        scratch_shapes=[pltpu.VMEM((tm, D_MODEL), BF16)],
        compiler_params=_cparams("arbitrary"),
        name="qkv",
    )(x, mod, g_norm, wqkv, *rope_tabs, gh)


def _out_kernel(x_ref, mod_ref, g_ref, oa_ref, ob_ref, oc_ref, wg_ref, wo_ref, o_ref, mg_ref):
    x = x_ref[...]
    nb = _modulate(x, mod_ref, g_ref[1:2, :], 1).astype(BF16)
    for br, br_ref in enumerate((oa_ref, ob_ref, oc_ref)):
        gate = jax.nn.sigmoid(_dot(nb, wg_ref[:, br * D_MODEL:(br + 1) * D_MODEL]))
        for h in range(N_Q_HEADS):
            lanes = slice(h * HEAD_DIM, (h + 1) * HEAD_DIM)
            term = gate[:, lanes] * br_ref[h].astype(F32)
            if br == 0:
                mg_ref[:, lanes] = term
            else:
                mg_ref[:, lanes] += term
    o_ref[...] = x + mod_ref[0, 5:6, :] * _dot(mg_ref[...].astype(BF16), wo_ref[...])


def _out_proj(x, mod, g_norm, oa, ob, oc, wgate, wo, *, layer, seq, tm):
    n = x.shape[0]
    o_spec = pl.BlockSpec((N_Q_HEADS, tm, HEAD_DIM), lambda i: (0, i, 0))
    return pl.pallas_call(
        _out_kernel,
        out_shape=jax.ShapeDtypeStruct((n, D_MODEL), F32),
        grid=(n // tm,),
        in_specs=[
            pl.BlockSpec((tm, D_MODEL), lambda i: (i, 0)),
            pl.BlockSpec((None, 1, 9, D_MODEL), lambda i: (layer, (i * tm) // seq, 0, 0)),
            pl.BlockSpec((None, 3, D_MODEL), lambda i: (layer, 0, 0)),
            o_spec, o_spec, o_spec,
            pl.BlockSpec((None, D_MODEL, 3 * D_MODEL), lambda i: (layer, 0, 0), pipeline_mode=pl.Buffered(1)),
            pl.BlockSpec((None, D_MODEL, D_MODEL), lambda i: (layer, 0, 0), pipeline_mode=pl.Buffered(1)),
        ],
        out_specs=pl.BlockSpec((tm, D_MODEL), lambda i: (i, 0)),
        scratch_shapes=[pltpu.VMEM((tm, D_MODEL), F32)],
        input_output_aliases={0: 0},
        compiler_params=_cparams("arbitrary"),
        name="out_proj",
    )(x, mod, g_norm, oa, ob, oc, wgate, wo)


MIN_DENOM = 2.0 ** -64
SAFE_SPAN = 60.0


def _key_slice(idx, tks):
    return pl.ds(pl.multiple_of(idx * tks, tks), tks)


def _score_bound(q, key_norm):
    qf = q.astype(F32)
    qn2 = _dot_nt(jnp.ones((F32_SUBLANES, q.shape[1]), F32), qf * qf)[:1]
    return jnp.sqrt(qn2) * key_norm


def _bounded_iteration(t, last, q, k_ref, v_ref, acct_ref, s_ref, *, unroll, tks, shift, bias_fn=None):
    acct = acct_ref[...]
    s = s_ref[...]
    for u in range(unroll):
        idx = t * unroll + u
        s_next = None if (last and u == unroll - 1) else _dot_nt(k_ref[_key_slice(idx + 1, tks), :], q)
        if bias_fn is not None:
            s = bias_fn(idx, s)
        acct = acct + _dot(v_ref[:, _key_slice(idx, tks)], jnp.exp2(s - shift).astype(BF16))
        s = s_next
    acct_ref[...] = acct
    if not last:
        s_ref[...] = s


def _exact_pass(q, k_ref, v_ref, m_ref, acct_ref, *, n_sub, tks, bias_fn=None):
    m_ref[...] = jnp.full_like(m_ref, -jnp.inf)
    acct_ref[...] = jnp.zeros_like(acct_ref)

    def body(idx, carry):
        s = _dot_nt(k_ref[_key_slice(idx, tks), :], q)
        if bias_fn is not None:
            s = bias_fn(idx, s)
        m = m_ref[...]
        m_new = jnp.maximum(m, jnp.max(s, axis=0, keepdims=True))
        p = jnp.exp2(s - m_new).astype(BF16)
        acct_ref[...] = jnp.exp2(m - m_new) * acct_ref[...] + _dot(v_ref[:, _key_slice(idx, tks)], p)
        m_ref[...] = m_new
        return carry

    lax.fori_loop(0, n_sub, body, 0)


def _recompute_if_flushed(check, acct_ref, exact_pass):
    @pl.when(check != 0)
    def _():
        @pl.when(jnp.min(acct_ref[HEAD_DIM:HEAD_DIM + 1, :]) < MIN_DENOM)
        def _():
            exact_pass()


def _attn_a_kernel(check_ref, q_ref, k_ref, v_ref, kn_ref, o_ref, m_ref, acct_ref, s_ref,
                   *, layer, n_iter, unroll, tks):
    rows = acct_ref.shape[1]
    q = q_ref[...].reshape(rows, HEAD_DIM)
    shift = _score_bound(q, kn_ref[:, :1])
    acct_ref[...] = jnp.zeros_like(acct_ref)
    s_ref[...] = _dot_nt(k_ref[0:tks, :], q)
    refs = (q, k_ref, v_ref, acct_ref, s_ref)

    def body(t, carry):
        _bounded_iteration(t, False, *refs, unroll=unroll, tks=tks, shift=shift)
        return carry

    lax.fori_loop(0, n_iter - 1, body, 0)
    _bounded_iteration(n_iter - 1, True, *refs, unroll=unroll, tks=tks, shift=shift)

    _recompute_if_flushed(check_ref[layer], acct_ref, lambda: _exact_pass(
        q, k_ref, v_ref, m_ref, acct_ref, n_sub=n_iter * unroll, tks=tks))

    acct = acct_ref[...]
    o_t = acct[:HEAD_DIM] / acct[HEAD_DIM:HEAD_DIM + 1]
    o_ref[...] = o_t.T.astype(o_ref.dtype).reshape(o_ref.shape)


def _attn_a(check, q, k, v1t, key_norm, *, layer, batch, seq, tq, tks, unroll):
    n = q.shape[1]
    nq = seq // tq
    rows = Q_PER_KV * tq
    q_map = lambda b, g, i: (g, b * nq + i, 0)
    return pl.pallas_call(
        functools.partial(_attn_a_kernel, layer=layer, n_iter=seq // (tks * unroll), unroll=unroll, tks=tks),
        out_shape=jax.ShapeDtypeStruct((N_Q_HEADS, n, HEAD_DIM), ATTN_OUT),
        grid=(batch, N_KV_HEADS, nq),
        in_specs=[
            pl.BlockSpec(memory_space=pltpu.SMEM),
            pl.BlockSpec((Q_PER_KV, tq, HEAD_DIM), q_map),
            pl.BlockSpec((None, seq, HEAD_DIM), lambda b, g, i: (g, b, 0)),
            pl.BlockSpec((None, VT_ROWS, seq), lambda b, g, i: (g, 0, b)),
            pl.BlockSpec((None, 1, HEAD_DIM), lambda b, g, i: (layer, 0, 0)),
        ],
        out_specs=pl.BlockSpec((Q_PER_KV, tq, HEAD_DIM), q_map),
        scratch_shapes=[pltpu.VMEM((1, rows), F32), pltpu.VMEM((VT_ROWS, rows), F32),
                        pltpu.VMEM((tks, rows), F32)],
        compiler_params=_cparams("arbitrary", "arbitrary", "arbitrary"),
        name="attn_axial",
    )(check, q, k, v1t, key_norm)


def _attn_b_kernel(q_ref, kp_ref, kc_ref, kn_ref, vp_ref, vc_ref, vn_ref, bias_ref, sink_ref, o_ref,
                   *, nb, qb):
    i = pl.program_id(2)
    rows = Q_PER_KV * BLOCK_Q
    kwin = jnp.concatenate([kp_ref[...], kc_ref[...], kn_ref[...]], axis=0)
    vwin = jnp.concatenate([vp_ref[...], vc_ref[...], vn_ref[...]], axis=1)
    bias = bias_ref[...]
    sink = sink_ref[...]
    for u in range(qb):
        blk = i * qb + u
        q = q_ref[:, u * BLOCK_Q:(u + 1) * BLOCK_Q, :].reshape(rows, HEAD_DIM)
        s = _dot_nt(kwin[u * BLOCK_Q:u * BLOCK_Q + NEAR], q) + bias
        s = jnp.concatenate([jnp.where(blk > 0, s[:BLOCK_Q], NEG), s[BLOCK_Q:2 * BLOCK_Q],
                             jnp.where(blk < nb - 1, s[2 * BLOCK_Q:], NEG)], axis=0)
        m = jnp.maximum(jnp.max(s, axis=0, keepdims=True), sink)
        p = jnp.exp2(s - m).astype(BF16)
        acct = _dot(vwin[:, u * BLOCK_Q:u * BLOCK_Q + NEAR], p)
        den = acct[HEAD_DIM:HEAD_DIM + 1] + jnp.exp2(sink - m)
        o_t = acct[:HEAD_DIM] / den
        o_ref[:, u * BLOCK_Q:(u + 1) * BLOCK_Q, :] = o_t.T.astype(o_ref.dtype).reshape(
            Q_PER_KV, BLOCK_Q, HEAD_DIM)


def _attn_b(q, k, v1t, bias_t, sink_t, *, layer, batch, seq, qb):
    n = q.shape[1]
    nb = seq // BLOCK_Q
    steps = nb // qb
    rows = Q_PER_KV * BLOCK_Q
    cur_map = lambda b, g, i: (g, b * steps + i, 0)
    prev_map = lambda b, g, i: (g, b * nb + jnp.maximum(i * qb - 1, 0), 0)
    next_map = lambda b, g, i: (g, b * nb + jnp.minimum(i * qb + qb, nb - 1), 0)
    t = lambda index_map: (lambda b, g, i: (index_map(b, g, i)[0], 0, index_map(b, g, i)[1]))
    k_specs = [pl.BlockSpec((None, BLOCK_Q, HEAD_DIM), prev_map),
               pl.BlockSpec((None, qb * BLOCK_Q, HEAD_DIM), cur_map),
               pl.BlockSpec((None, BLOCK_Q, HEAD_DIM), next_map)]
    v_specs = [pl.BlockSpec((None, VT_ROWS, BLOCK_Q), t(prev_map)),
               pl.BlockSpec((None, VT_ROWS, qb * BLOCK_Q), t(cur_map)),
               pl.BlockSpec((None, VT_ROWS, BLOCK_Q), t(next_map))]
    return pl.pallas_call(
        functools.partial(_attn_b_kernel, nb=nb, qb=qb),
        out_shape=jax.ShapeDtypeStruct((N_Q_HEADS, n, HEAD_DIM), ATTN_OUT),
        grid=(batch, N_KV_HEADS, steps),
        in_specs=[pl.BlockSpec((Q_PER_KV, qb * BLOCK_Q, HEAD_DIM), cur_map)] + k_specs + v_specs + [
            pl.BlockSpec((None, NEAR, rows), lambda b, g, i: (g, 0, 0)),
            pl.BlockSpec((None, None, 1, rows), lambda b, g, i: (layer, g, 0, 0)),
        ],
        out_specs=pl.BlockSpec((Q_PER_KV, qb * BLOCK_Q, HEAD_DIM), cur_map),
        compiler_params=_cparams("arbitrary", "arbitrary", "arbitrary"),
        name="attn_window",
    )(q, k, k, k, v1t, v1t, v1t, bias_t, sink_t)


def _attn_c_kernel(check_ref, q_ref, k_ref, v_ref, bias_ref, cfar_ref, kn_ref, lam_ref, gs_ref, o_ref,
                   m_ref, acct_ref, s_ref, *, layer, n_iter, unroll, tks, nblk, out_scale):
    n_blk = pl.program_id(2)
    rows = Q_PER_KV * BLOCK_Q
    blocks_per_sub = tks // BLOCK_Q
    blocks_per_iter = unroll * blocks_per_sub

    q = q_ref[...].reshape(rows, HEAD_DIM)
    lo = lax.broadcasted_iota(jnp.int32, q.shape, 1) < DIFF_DIM
    zero = jnp.zeros_like(q)
    q2 = jnp.concatenate([jnp.where(lo, q, zero), jnp.where(lo, zero, q)], axis=0)
    acct_ref[...] = jnp.zeros_like(acct_ref)
    s_ref[...] = _dot_nt(k_ref[0:tks, :], q2)
    refs = (q2, k_ref, v_ref, acct_ref, s_ref)

    c_left, c_right, c_max = cfar_ref[0], cfar_ref[1], cfar_ref[2]
    t_lo = jnp.maximum(n_blk - 1, 0) // blocks_per_iter
    t_hi = jnp.minimum(n_blk + 1, nblk - 1) // blocks_per_iter
    two = lambda x: jnp.concatenate([x, x], axis=1)
    shift = _score_bound(q2, kn_ref[:, :1]) + two(c_max)

    def add_bias(idx, s):
        tile_of = lambda key_blk: jnp.clip(key_blk - n_blk + 2, 0, NEAR // BLOCK_Q + 1)
        bias = jnp.concatenate([bias_ref[tile_of(idx * blocks_per_sub + kb)] for kb in range(blocks_per_sub)],
                               axis=0)
        return jnp.concatenate([s[:, :rows] + bias, s[:, rows:] + bias], axis=1)

    def iteration(t, last):
        is_near = (t >= t_lo) & (t <= t_hi)

        @pl.when(jnp.logical_not(is_near))
        def _():
            side = jnp.where(t < t_lo, c_left, c_right)
            _bounded_iteration(t, last, *refs, unroll=unroll, tks=tks, shift=shift - two(side))

        @pl.when(is_near)
        def _():
            _bounded_iteration(t, last, *refs, unroll=unroll, tks=tks, shift=shift, bias_fn=add_bias)

    def body(t, carry):
        iteration(t, False)
        return carry

    lax.fori_loop(0, n_iter - 1, body, 0)
    iteration(n_iter - 1, True)

    _recompute_if_flushed(check_ref[layer], acct_ref, lambda: _exact_pass(
        q2, k_ref, v_ref, m_ref, acct_ref, n_sub=n_iter * unroll, tks=tks, bias_fn=add_bias))

    acct = acct_ref[...]
    o1_t = acct[:HEAD_DIM, :rows] / acct[HEAD_DIM:HEAD_DIM + 1, :rows]
    o2_t = acct[:HEAD_DIM, rows:] / acct[HEAD_DIM:HEAD_DIM + 1, rows:]
    o_t = o1_t - lam_ref[:, :1] * o2_t
    ms = jnp.mean(o_t * o_t, axis=0, keepdims=True)
    o = (o_t * lax.rsqrt(ms + EPS)).T * gs_ref[...] * out_scale
    o_ref[...] = o.astype(o_ref.dtype).reshape(o_ref.shape)


def _attn_c(check, q, k, v1t, bias_t, cfar_t, key_norm, lam, g_subln, *, layer, batch, seq, tks, unroll):
    n = q.shape[1]
    nq = seq // BLOCK_Q
    rows = Q_PER_KV * BLOCK_Q
    lam_init = 0.8 - 0.6 * math.exp(-0.3 * layer)
    q_map = lambda b, g, i: (g, b * nq + i, 0)
    return pl.pallas_call(
        functools.partial(_attn_c_kernel, layer=layer, n_iter=seq // (tks * unroll), unroll=unroll, tks=tks,
                          nblk=nq, out_scale=1.0 - lam_init),
        out_shape=jax.ShapeDtypeStruct((N_Q_HEADS, n, HEAD_DIM), ATTN_OUT),
        grid=(batch, N_KV_HEADS, nq),
        in_specs=[
            pl.BlockSpec(memory_space=pltpu.SMEM),
            pl.BlockSpec((Q_PER_KV, BLOCK_Q, HEAD_DIM), q_map),
            pl.BlockSpec((None, seq, HEAD_DIM), lambda b, g, i: (g, b, 0)),
            pl.BlockSpec((None, VT_ROWS, seq), lambda b, g, i: (g, 0, b)),
            pl.BlockSpec((None, NEAR // BLOCK_Q + 2, BLOCK_Q, rows), lambda b, g, i: (g, 0, 0, 0)),
            pl.BlockSpec((3, None, 1, rows), lambda b, g, i: (0, g, 0, 0)),
            pl.BlockSpec((None, 1, HEAD_DIM), lambda b, g, i: (layer, 0, 0)),
            pl.BlockSpec((None, 1, HEAD_DIM), lambda b, g, i: (layer, 0, 0)),
            pl.BlockSpec((None, 1, HEAD_DIM), lambda b, g, i: (layer, 0, 0)),
        ],
        out_specs=pl.BlockSpec((Q_PER_KV, BLOCK_Q, HEAD_DIM), q_map),
        scratch_shapes=[pltpu.VMEM((1, 2 * rows), F32), pltpu.VMEM((VT_ROWS, 2 * rows), F32),
                        pltpu.VMEM((tks, 2 * rows), F32)],
        compiler_params=_cparams("arbitrary", "arbitrary", "arbitrary"),
        name="attn_diff",
    )(check, q, k, v1t, bias_t, cfar_t, key_norm, lam, g_subln)


def _t5_bucket_np(rel):
    half = N_BUCKETS // 2
    max_exact = half // 2
    ret = np.where(rel > 0, half, 0)
    n = np.abs(rel)
    ratio = np.log(np.maximum(n, 1).astype(np.float32) / np.float32(max_exact)) / np.float32(
        math.log(MAX_DISTANCE / max_exact))
    large = max_exact + (ratio * np.float32(half - max_exact)).astype(np.int32)
    large = np.minimum(large, half - 1)
    return (ret + np.where(n < max_exact, n, large)).astype(np.int32)


def _near_bias(table):
    r = np.arange(BLOCK_Q)[:, None]
    c = np.arange(NEAR)[None, :]
    bucket = _t5_bucket_np(c - BLOCK_Q - r)
    onehot = (bucket.reshape(-1, 1) == np.arange(N_BUCKETS)[None, :]).astype(np.float32)
    rows = jnp.dot(jnp.asarray(onehot), table.astype(F32), precision=lax.Precision.HIGHEST)
    return rows.reshape(BLOCK_Q, NEAR, table.shape[1]).transpose(2, 0, 1)


def _rope_tables(seq):
    rows = seq // GRID_W
    nfreq = HEAD_DIM // 4
    inv = ROPE_THETA ** (-jnp.arange(nfreq, dtype=F32) / nfreq)
    ang_r = jnp.arange(rows).astype(F32)[:, None] * inv
    ang_c = jnp.arange(GRID_W).astype(F32)[:, None] * inv
    by_row = lambda t: jnp.broadcast_to(t[:, None, :], (rows, GRID_W, nfreq)).reshape(seq, nfreq)
    by_col = lambda t: jnp.broadcast_to(t[None, :, :], (rows, GRID_W, nfreq)).reshape(seq, nfreq)
    table = lambda fn: jnp.concatenate([by_row(fn(ang_r)), by_col(fn(ang_c))] * 2, axis=-1)
    cos, sin = table(jnp.cos), table(jnp.sin)
    first = np.arange(HEAD_DIM) < HEAD_DIM // 2
    return cos, jnp.where(first, -sin, sin)


def _pair_layout(a):
    lead = a.shape[:-1]
    nfreq = HEAD_DIM // 4
    return a.reshape(*lead, -1, 2, 2, nfreq).swapaxes(-3, -2).reshape(*lead, -1)


def _trunk(x, mod, p, *, batch, seq):
    t = _tiles(seq)
    tm = t["tm"]
    rope_tabs = _rope_tables(seq)
    for l in range(DEPTH):
        x = _ffn(x, mod, p["g_norm"], p["w_ff_in"], p["w_ff_out"], layer=l, which=0, seq=seq, tm=tm)
        qa, ka, va, qb, kb, vb, qc, kc, vc = _qkv(x, mod, p["g_norm"], p["wqkv"], rope_tabs, p["gh"],
                                                  layer=l, seq=seq, tm=tm)
        oa = _attn_a(p["check_a"], qa, ka, va, p["kn_a"], layer=l, batch=batch, seq=seq, tq=t["tq_a"], tks=t["tks"],
                     unroll=t["unroll"])
        ob = _attn_b(qb, kb, vb, p["bias_b"], p["sink"], layer=l, batch=batch, seq=seq, qb=t["qb"])
        oc = _attn_c(p["check_c"], qc, kc, vc, p["bias_c"], p["cfar_c"], p["kn_c"], p["lam"], p["g_subln"],
                     layer=l, batch=batch, seq=seq, tks=t["tks"], unroll=t["unroll"])
        x = _out_proj(x, mod, p["g_norm"], oa, ob, oc, p["wgate"], p["wo"], layer=l, seq=seq,
                      tm=t["tm_out"])
        x = _ffn(x, mod, p["g_norm"], p["w_ff_in"], p["w_ff_out"], layer=l, which=1, seq=seq, tm=tm)
    return x


def _prepare(w_ff_in, w_ff_out, w_in, w_o, g_qa, g_ka, g_qb, g_kb, g_qc, g_kc, sink,
             lam_q1, lam_k1, lam_q2, lam_k2, g_subln, rel_bias):
    zeros = jnp.zeros_like(g_qa)
    gh = jnp.stack([_pair_layout(g_qa), _pair_layout(g_ka), g_qb, g_kb, jnp.tile(g_qc, (1, 2)), jnp.tile(g_kc, (1, 2)), zeros, zeros],
                   axis=1).astype(F32)
    table_b, table_c = rel_bias[:, :N_Q_HEADS], rel_bias[:, N_Q_HEADS:]
    half = N_BUCKETS // 2
    c_left, c_right = table_c[half - 1] * LOG2E, table_c[N_BUCKETS - 1] * LOG2E
    lam_init = jnp.asarray([0.8 - 0.6 * math.exp(-0.3 * l) for l in range(DEPTH)], F32)
    lam = (jnp.exp(jnp.sum(lam_q1.astype(F32) * lam_k1.astype(F32), axis=-1))
           - jnp.exp(jnp.sum(lam_q2.astype(F32) * lam_k2.astype(F32), axis=-1)) + lam_init)

    band = np.abs(np.arange(NEAR)[None, :] - BLOCK_Q - np.arange(BLOCK_Q)[:, None]) <= WINDOW
    near_c = (_near_bias(table_c) * LOG2E).reshape(
        N_KV_HEADS, Q_PER_KV, BLOCK_Q, NEAR // BLOCK_Q, BLOCK_Q).transpose(0, 3, 4, 1, 2).reshape(
        N_KV_HEADS, NEAR // BLOCK_Q, BLOCK_Q, Q_PER_KV * BLOCK_Q)
    cfar_c = jnp.repeat(jnp.stack([c_left, c_right, jnp.max(table_c, axis=0) * LOG2E]).astype(F32),
                        BLOCK_Q, axis=1).reshape(3, N_KV_HEADS, 1, Q_PER_KV * BLOCK_Q)

    def norm_bound(g, dim):
        return BOUND_MARGIN * math.sqrt(dim) * jnp.max(jnp.abs(g.astype(F32)), axis=-1)

    def key_norm(g, dim):
        return jnp.broadcast_to(norm_bound(g, dim)[:, None, None], (DEPTH, 1, HEAD_DIM))

    span_a = 2.0 * BOUND_MARGIN * QA_SCALE * norm_bound(g_qa, HEAD_DIM) * norm_bound(g_ka, HEAD_DIM)
    span_c = (2.0 * BOUND_MARGIN * QC_SCALE * norm_bound(g_qc, DIFF_DIM) * norm_bound(g_kc, DIFF_DIM)
              + (jnp.max(table_c) - jnp.min(table_c)) * LOG2E)

    return dict(
        w_ff_in=w_ff_in.astype(BF16), w_ff_out=w_ff_out.astype(BF16),
        wqkv=jnp.concatenate([_pair_layout(w_in[:, :, :Q_W + KV_W]), w_in[:, :, Q_W + KV_W:QKV_W]],
                             axis=-1).astype(BF16),
        wgate=w_in[:, :, QKV_W:].astype(BF16), wo=w_o.astype(BF16),
        gh=gh,
        bias_b=jnp.where(band, _near_bias(table_b) * LOG2E, NEG).reshape(
            N_KV_HEADS, Q_PER_KV, BLOCK_Q, NEAR).transpose(0, 3, 1, 2).reshape(
            N_KV_HEADS, NEAR, Q_PER_KV * BLOCK_Q),
        sink=jnp.repeat(sink.astype(F32) * LOG2E, BLOCK_Q, axis=1).reshape(
            DEPTH, N_KV_HEADS, 1, Q_PER_KV * BLOCK_Q),
        bias_c=jnp.concatenate([cfar_c[0][:, None] + jnp.zeros((1, 1, BLOCK_Q, 1), F32), near_c,
                                cfar_c[1][:, None] + jnp.zeros((1, 1, BLOCK_Q, 1), F32)], axis=1),
        cfar_c=cfar_c,
        kn_a=key_norm(g_ka, HEAD_DIM), kn_c=key_norm(g_kc, DIFF_DIM),
        check_a=(span_a > SAFE_SPAN).astype(jnp.int32), check_c=(span_c > SAFE_SPAN).astype(jnp.int32),
        lam=jnp.broadcast_to(lam[:, None, None], (DEPTH, 1, HEAD_DIM)),
        g_subln=g_subln.astype(F32)[:, None, :],
    )


def kernel(x_prompt, x_sample, c_prompt, c_sample, w_ada, b_ada, g_norm, w_ff_in, w_ff_out, w_in, w_o,
           g_qa, g_ka, g_qb, g_kb, g_qc, g_kc, sink, lam_q1, lam_k1, lam_q2, lam_k2, g_subln, rel_bias):
    p = _prepare(w_ff_in, w_ff_out, w_in, w_o, g_qa, g_ka, g_qb, g_kb, g_qc, g_kc, sink,
                 lam_q1, lam_k1, lam_q2, lam_k2, g_subln, rel_bias)
    p["g_norm"] = g_norm.astype(F32)
    outs = []
    n_cond = 0
    conds = [c_prompt, c_sample]
    c_all = jnp.concatenate(conds + [jnp.zeros((ADA_ROWS - sum(c.shape[0] for c in conds), D_MODEL), F32)])
    mod_all = _ada(c_all, w_ada, b_ada)
    for x, c in ((x_prompt, c_prompt), (x_sample, c_sample)):
        batch, seq, _ = x.shape
        mod = mod_all[:, n_cond:n_cond + batch].reshape(DEPTH, batch, 9, D_MODEL)
        n_cond += batch
        y = _trunk(x.reshape(batch * seq, D_MODEL), mod, p, batch=batch, seq=seq)
        outs.append(y.reshape(batch, seq, D_MODEL))
    return tuple(outs)
```

```python
import functools
import math

import numpy as np
import jax
import jax.numpy as jnp
from jax import lax
from jax.experimental import pallas as pl
from jax.experimental.pallas import tpu as pltpu

F32 = jnp.float32
BF16 = jnp.bfloat16
ATTN_OUT = jnp.bfloat16

D_MODEL = 1024
DEPTH = 4
N_Q_HEADS = 8
N_KV_HEADS = 2
Q_PER_KV = N_Q_HEADS // N_KV_HEADS
HEAD_DIM = 128
DIFF_DIM = 64
D_FF = 2816
BLOCK_Q = 128
WINDOW = 128
GRID_W = 64
N_BUCKETS = 32
MAX_DISTANCE = 128
ROPE_THETA = 10000.0
EPS = 1e-6
NEG = -1e30
LOG2E = 1.4426950408889634
QA_SCALE = HEAD_DIM ** -0.5 * LOG2E
QC_SCALE = DIFF_DIM ** -0.5 * LOG2E

Q_W = N_Q_HEADS * HEAD_DIM
KV_W = N_KV_HEADS * HEAD_DIM
BRANCH_IN = Q_W + 2 * KV_W
QKV_W = 3 * BRANCH_IN
MXU_TILE = 256
FF_CHUNK = MXU_TILE
N_FF_CHUNKS = D_FF // FF_CHUNK
ADA_ROWS = 16
ADA_TN = 1536
NEAR = 3 * BLOCK_Q
VT_ROWS = HEAD_DIM + 16

F32_SUBLANES = 8
ROW_TILE = 512
STEPS_PER_REGION = 32
BOUND_MARGIN = 1.01

VMEM_LIMIT = 56 * 1024 * 1024


def _tiles(seq):
    tks = min(MXU_TILE, seq)
    return dict(
        tm=min(ROW_TILE, seq),
        tm_out=min(2 * ROW_TILE, seq),
        tq_a=(4 if seq <= STEPS_PER_REGION * tks else 2) * BLOCK_Q,
        tks=tks,
        qb=min(STEPS_PER_REGION, seq // BLOCK_Q),
        unroll=min(STEPS_PER_REGION, seq // tks),
    )


def _cparams(*sem):
    return pltpu.CompilerParams(dimension_semantics=sem, vmem_limit_bytes=VMEM_LIMIT)


def _dot(a, b):
    return jnp.dot(a, b, preferred_element_type=F32)


def _dot_nt(a, b):
    return lax.dot_general(a, b, (((1,), (1,)), ((), ())), preferred_element_type=F32)


def _ada_kernel(c_ref, w_ref, b_ref, o_ref):
    c = c_ref[...]
    a = (c * jax.nn.sigmoid(c)).astype(BF16)
    o_ref[0] = _dot(a, w_ref[0].astype(BF16)) + b_ref[0]


def _ada(c_all, w_ada, b_ada):
    n_out = w_ada.shape[-1]
    return pl.pallas_call(
        _ada_kernel,
        out_shape=jax.ShapeDtypeStruct((DEPTH, ADA_ROWS, n_out), F32),
        grid=(DEPTH, n_out // ADA_TN),
        in_specs=[
            pl.BlockSpec((ADA_ROWS, D_MODEL), lambda l, j: (0, 0)),
            pl.BlockSpec((1, D_MODEL, ADA_TN), lambda l, j: (l, 0, j)),
            pl.BlockSpec((1, 1, ADA_TN), lambda l, j: (l, 0, j)),
        ],
        out_specs=pl.BlockSpec((1, ADA_ROWS, ADA_TN), lambda l, j: (l, 0, j)),
        compiler_params=_cparams("arbitrary", "arbitrary"),
        name="ada",
    )(c_all, w_ada, b_ada.reshape(DEPTH, 1, n_out))


def _modulate(x, mod_ref, g, jj):
    ms = jnp.mean(x * x, axis=-1, keepdims=True)
    y = x * lax.rsqrt(ms + EPS) * g
    return y * (1.0 + mod_ref[0, 3 * jj + 1:3 * jj + 2, :]) + mod_ref[0, 3 * jj:3 * jj + 1, :]


def _ffn_kernel(x_ref, mod_ref, g_ref, win_ref, wout_ref, o_ref, nb_ref, acc_ref, *, jj):
    x = x_ref[...]
    nb_ref[...] = _modulate(x, mod_ref, g_ref[jj:jj + 1, :], jj).astype(BF16)
    acc_ref[...] = jnp.zeros_like(acc_ref)
    for c in range(N_FF_CHUNKS):
        cols = slice(c * FF_CHUNK, (c + 1) * FF_CHUNK)
        hg = _dot(nb_ref[...], win_ref[:, cols])
        hu = _dot(nb_ref[...], win_ref[:, D_FF + c * FF_CHUNK:D_FF + (c + 1) * FF_CHUNK])
        a = (hg * jax.nn.sigmoid(hg)) * hu
        acc_ref[...] += _dot(a.astype(BF16), wout_ref[cols, :])
    o_ref[...] = x + (0.5 * mod_ref[0, 3 * jj + 2:3 * jj + 3, :]) * acc_ref[...]


def _ffn(x, mod, g_norm, w_in, w_out, *, layer, which, seq, tm):
    n = x.shape[0]
    jj = 2 * which
    const = dict(pipeline_mode=pl.Buffered(1))
    return pl.pallas_call(
        functools.partial(_ffn_kernel, jj=jj),
        out_shape=jax.ShapeDtypeStruct((n, D_MODEL), F32),
        grid=(n // tm,),
        in_specs=[
            pl.BlockSpec((tm, D_MODEL), lambda i: (i, 0)),
            pl.BlockSpec((None, 1, 9, D_MODEL), lambda i: (layer, (i * tm) // seq, 0, 0)),
            pl.BlockSpec((None, 3, D_MODEL), lambda i: (layer, 0, 0)),
            pl.BlockSpec((None, None, D_MODEL, 2 * D_FF), lambda i: (layer, which, 0, 0), **const),
            pl.BlockSpec((None, None, D_FF, D_MODEL), lambda i: (layer, which, 0, 0), **const),
        ],
        out_specs=pl.BlockSpec((tm, D_MODEL), lambda i: (i, 0)),
        scratch_shapes=[pltpu.VMEM((tm, D_MODEL), BF16), pltpu.VMEM((tm, D_MODEL), F32)],
        input_output_aliases={} if (layer == 0 and which == 0) else {0: 0},
        compiler_params=_cparams("arbitrary"),
        name=f"ffn{which}",
    )(x, mod, g_norm, w_in, w_out)


def _head_norm(r, g):
    ms = jnp.mean(r * r, axis=-1, keepdims=True)
    return r * lax.rsqrt(ms + EPS) * g


def _half_norm(r, g2):
    sq = r * r
    lo = lax.broadcasted_iota(jnp.int32, r.shape, 1) < DIFF_DIM
    s_lo = jnp.sum(jnp.where(lo, sq, 0.0), axis=-1, keepdims=True)
    s_hi = jnp.sum(jnp.where(lo, 0.0, sq), axis=-1, keepdims=True)
    ms = jnp.where(lo, s_lo, s_hi) * (1.0 / DIFF_DIM)
    return r * lax.rsqrt(ms + EPS) * g2


def _qkv_kernel(x_ref, mod_ref, g_ref, w_ref, cos_ref, sa_ref, sb_ref, gh_ref,
                qa_ref, ka_ref, va_ref, qb_ref, kb_ref, vb_ref, qc_ref, kc_ref, vc_ref, nb_ref):
    nb_ref[...] = _modulate(x_ref[...], mod_ref, g_ref[1:2, :], 1).astype(BF16)
    cos, sa, sb = cos_ref[...], sa_ref[...], sb_ref[...]

    def rope(r):
        return (r * cos + pltpu.roll(r, HEAD_DIM - 32, 1) * sa + pltpu.roll(r, 32, 1) * sb)

    def proj(col, width):
        return _dot(nb_ref[...], w_ref[:, col:col + width])

    def heads(r, n_heads, fn, out_ref):
        for h in range(n_heads):
            out_ref[h] = fn(r[:, h * HEAD_DIM:(h + 1) * HEAD_DIM]).astype(BF16)

    def values_transposed(r, out_ref):
        for h in range(N_KV_HEADS):
            out_ref[h, :HEAD_DIM, :] = r[:, h * HEAD_DIM:(h + 1) * HEAD_DIM].T.astype(BF16)
            out_ref[h, HEAD_DIM:, :] = jnp.ones((VT_ROWS - HEAD_DIM, r.shape[0]), BF16)

    g_qa, g_ka, g_qb, g_kb = (gh_ref[i:i + 1, :] for i in range(4))
    g_qc, g_kc = gh_ref[4:5, :], gh_ref[5:6, :]
    col = 0
    heads(proj(col, Q_W), N_Q_HEADS, lambda r: rope(_head_norm(r, g_qa)) * QA_SCALE,qa_ref)
    col += Q_W
    heads(proj(col, KV_W), N_KV_HEADS, lambda r: rope(_head_norm(r, g_ka)), ka_ref)
    col += KV_W
    values_transposed(proj(col, KV_W), va_ref)
    col += KV_W
    heads(proj(col, Q_W), N_Q_HEADS, lambda r: _head_norm(r, g_qb) * QA_SCALE,qb_ref)
    col += Q_W
    heads(proj(col, KV_W), N_KV_HEADS, lambda r: _head_norm(r, g_kb), kb_ref)
    col += KV_W
    values_transposed(proj(col, KV_W), vb_ref)
    col += KV_W
    heads(proj(col, Q_W), N_Q_HEADS, lambda r: _half_norm(r, g_qc) * QC_SCALE,qc_ref)
    col += Q_W
    heads(proj(col, KV_W), N_KV_HEADS, lambda r: _half_norm(r, g_kc), kc_ref)
    col += KV_W
    values_transposed(proj(col, KV_W), vc_ref)


def _qkv(x, mod, g_norm, wqkv, rope_tabs, gh, *, layer, seq, tm):
    n = x.shape[0]
    tiles_per_seq = seq // tm
    q_shape = jax.ShapeDtypeStruct((N_Q_HEADS, n, HEAD_DIM), BF16)
    kv_shape = jax.ShapeDtypeStruct((N_KV_HEADS, n, HEAD_DIM), BF16)
    q_spec = pl.BlockSpec((N_Q_HEADS, tm, HEAD_DIM), lambda i: (0, i, 0))
    kv_spec = pl.BlockSpec((N_KV_HEADS, tm, HEAD_DIM), lambda i: (0, i, 0))
    tab_spec = pl.BlockSpec((tm, HEAD_DIM), lambda i: (i % tiles_per_seq, 0))
    v1_shape = jax.ShapeDtypeStruct((N_KV_HEADS, VT_ROWS, n), BF16)
    v1_spec = pl.BlockSpec((N_KV_HEADS, VT_ROWS, tm), lambda i: (0, 0, i))
    return pl.pallas_call(
        _qkv_kernel,
        out_shape=[q_shape, kv_shape, v1_shape] * 3,
        grid=(n // tm,),
        in_specs=[
            pl.BlockSpec((tm, D_MODEL), lambda i: (i, 0)),
            pl.BlockSpec((None, 1, 9, D_MODEL), lambda i: (layer, (i * tm) // seq, 0, 0)),
            pl.BlockSpec((None, 3, D_MODEL), lambda i: (layer, 0, 0)),
            pl.BlockSpec((None, D_MODEL, QKV_W), lambda i: (layer, 0, 0), pipeline_mode=pl.Buffered(1)),
            tab_spec, tab_spec, tab_spec,
            pl.BlockSpec((None, 8, HEAD_DIM), lambda i: (layer, 0, 0)),
        ],
        out_specs=[q_spec, kv_spec, v1_spec] * 3,
        scratch_shapes=[pltpu.VMEM((tm, D_MODEL), BF16)],
        compiler_params=_cparams("arbitrary"),
        name="qkv",
    )(x, mod, g_norm, wqkv, *rope_tabs, gh)


def _out_kernel(x_ref, mod_ref, g_ref, oa_ref, ob_ref, oc_ref, wg_ref, wo_ref, o_ref, mg_ref):
    x = x_ref[...]
    nb = _modulate(x, mod_ref, g_ref[1:2, :], 1).astype(BF16)
    for br, br_ref in enumerate((oa_ref, ob_ref, oc_ref)):
        gate = jax.nn.sigmoid(_dot(nb, wg_ref[:, br * D_MODEL:(br + 1) * D_MODEL]))
        for h in range(N_Q_HEADS):
            lanes = slice(h * HEAD_DIM, (h + 1) * HEAD_DIM)
            term = gate[:, lanes] * br_ref[h].astype(F32)
            if br == 0:
                mg_ref[:, lanes] = term
            else:
                mg_ref[:, lanes] += term
    o_ref[...] = x + mod_ref[0, 5:6, :] * _dot(mg_ref[...].astype(BF16), wo_ref[...])


def _out_proj(x, mod, g_norm, oa, ob, oc, wgate, wo, *, layer, seq, tm):
    n = x.shape[0]
    o_spec = pl.BlockSpec((N_Q_HEADS, tm, HEAD_DIM), lambda i: (0, i, 0))
    return pl.pallas_call(
        _out_kernel,
        out_shape=jax.ShapeDtypeStruct((n, D_MODEL), F32),
        grid=(n // tm,),
        in_specs=[
            pl.BlockSpec((tm, D_MODEL), lambda i: (i, 0)),
            pl.BlockSpec((None, 1, 9, D_MODEL), lambda i: (layer, (i * tm) // seq, 0, 0)),
            pl.BlockSpec((None, 3, D_MODEL), lambda i: (layer, 0, 0)),
            o_spec, o_spec, o_spec,
            pl.BlockSpec((None, D_MODEL, 3 * D_MODEL), lambda i: (layer, 0, 0), pipeline_mode=pl.Buffered(1)),
            pl.BlockSpec((None, D_MODEL, D_MODEL), lambda i: (layer, 0, 0), pipeline_mode=pl.Buffered(1)),
        ],
        out_specs=pl.BlockSpec((tm, D_MODEL), lambda i: (i, 0)),
        scratch_shapes=[pltpu.VMEM((tm, D_MODEL), F32)],
        input_output_aliases={0: 0},
        compiler_params=_cparams("arbitrary"),
        name="out_proj",
    )(x, mod, g_norm, oa, ob, oc, wgate, wo)


MIN_DENOM = 2.0 ** -64
SAFE_SPAN = 60.0


def _key_slice(idx, tks):
    return pl.ds(pl.multiple_of(idx * tks, tks), tks)


def _score_bound(q, key_norm):
    qf = q.astype(F32)
    qn2 = _dot_nt(jnp.ones((F32_SUBLANES, q.shape[1]), BF16), (qf * qf).astype(BF16))[:1]
    return jnp.sqrt(qn2) * key_norm


def _bounded_iteration(t, last, q, k_ref, v_ref, acct_ref, s_ref, *, unroll, tks, shift, bias_fn=None):
    acct = acct_ref[...]
    s = s_ref[...]
    for u in range(unroll):
        idx = t * unroll + u
        s_next = None if (last and u == unroll - 1) else _dot_nt(k_ref[_key_slice(idx + 1, tks), :], q)
        if bias_fn is not None:
            s = bias_fn(idx, s)
        acct = acct + _dot(v_ref[:, _key_slice(idx, tks)], jnp.exp2(s - shift).astype(BF16))
        s = s_next
    acct_ref[...] = acct
    if not last:
        s_ref[...] = s


def _exact_pass(q, k_ref, v_ref, m_ref, acct_ref, *, n_sub, tks, bias_fn=None):
    m_ref[...] = jnp.full_like(m_ref, -jnp.inf)
    acct_ref[...] = jnp.zeros_like(acct_ref)

    def body(idx, carry):
        s = _dot_nt(k_ref[_key_slice(idx, tks), :], q)
        if bias_fn is not None:
            s = bias_fn(idx, s)
        m = m_ref[...]
        m_new = jnp.maximum(m, jnp.max(s, axis=0, keepdims=True))
        p = jnp.exp2(s - m_new).astype(BF16)
        acct_ref[...] = jnp.exp2(m - m_new) * acct_ref[...] + _dot(v_ref[:, _key_slice(idx, tks)], p)
        m_ref[...] = m_new
        return carry

    lax.fori_loop(0, n_sub, body, 0)


def _recompute_if_flushed(check, acct_ref, exact_pass):
    @pl.when(check != 0)
    def _():
        @pl.when(jnp.min(acct_ref[HEAD_DIM:HEAD_DIM + 1, :]) < MIN_DENOM)
        def _():
            exact_pass()


def _attn_a_kernel(check_ref, q_ref, k_ref, v_ref, kn_ref, o_ref, m_ref, acct_ref, s_ref,
                   *, layer, n_iter, unroll, tks):
    rows = acct_ref.shape[1]
    q = q_ref[...].reshape(rows, HEAD_DIM)
    shift = _score_bound(q, kn_ref[:, :1])
    acct_ref[...] = jnp.zeros_like(acct_ref)
    s_ref[...] = _dot_nt(k_ref[0:tks, :], q)
    refs = (q, k_ref, v_ref, acct_ref, s_ref)

    def body(t, carry):
        _bounded_iteration(t, False, *refs, unroll=unroll, tks=tks, shift=shift)
        return carry

    lax.fori_loop(0, n_iter - 1, body, 0)
    _bounded_iteration(n_iter - 1, True, *refs, unroll=unroll, tks=tks, shift=shift)

    _recompute_if_flushed(check_ref[layer], acct_ref, lambda: _exact_pass(
        q, k_ref, v_ref, m_ref, acct_ref, n_sub=n_iter * unroll, tks=tks))

    acct = acct_ref[...]
    o_t = acct[:HEAD_DIM] / acct[HEAD_DIM:HEAD_DIM + 1]
    o_ref[...] = o_t.T.astype(o_ref.dtype).reshape(o_ref.shape)


def _attn_a(check, q, k, v1t, key_norm, *, layer, batch, seq, tq, tks, unroll):
    n = q.shape[1]
    nq = seq // tq
    rows = Q_PER_KV * tq
    q_map = lambda b, g, i: (g, b * nq + i, 0)
    return pl.pallas_call(
        functools.partial(_attn_a_kernel, layer=layer, n_iter=seq // (tks * unroll), unroll=unroll, tks=tks),
        out_shape=jax.ShapeDtypeStruct((N_Q_HEADS, n, HEAD_DIM), ATTN_OUT),
        grid=(batch, N_KV_HEADS, nq),
        in_specs=[
            pl.BlockSpec(memory_space=pltpu.SMEM),
            pl.BlockSpec((Q_PER_KV, tq, HEAD_DIM), q_map),
            pl.BlockSpec((None, seq, HEAD_DIM), lambda b, g, i: (g, b, 0)),
            pl.BlockSpec((None, VT_ROWS, seq), lambda b, g, i: (g, 0, b)),
            pl.BlockSpec((None, 1, HEAD_DIM), lambda b, g, i: (layer, 0, 0)),
        ],
        out_specs=pl.BlockSpec((Q_PER_KV, tq, HEAD_DIM), q_map),
        scratch_shapes=[pltpu.VMEM((1, rows), F32), pltpu.VMEM((VT_ROWS, rows), F32),
                        pltpu.VMEM((tks, rows), F32)],
        compiler_params=_cparams("arbitrary", "arbitrary", "arbitrary"),
        name="attn_axial",
    )(check, q, k, v1t, key_norm)


def _attn_b_kernel(q_ref, kp_ref, kc_ref, kn_ref, vp_ref, vc_ref, vn_ref, bias_ref, sink_ref, o_ref,
                   *, nb, qb):
    i = pl.program_id(2)
    rows = Q_PER_KV * BLOCK_Q
    kwin = jnp.concatenate([kp_ref[...], kc_ref[...], kn_ref[...]], axis=0)
    vwin = jnp.concatenate([vp_ref[...], vc_ref[...], vn_ref[...]], axis=1)
    bias = bias_ref[...]
    sink = sink_ref[...]
    for u in range(qb):
        blk = i * qb + u
        q = q_ref[:, u * BLOCK_Q:(u + 1) * BLOCK_Q, :].reshape(rows, HEAD_DIM)
        s = _dot_nt(kwin[u * BLOCK_Q:u * BLOCK_Q + NEAR], q) + bias
        s = jnp.concatenate([jnp.where(blk > 0, s[:BLOCK_Q], NEG), s[BLOCK_Q:2 * BLOCK_Q],
                             jnp.where(blk < nb - 1, s[2 * BLOCK_Q:], NEG)], axis=0)
        m = jnp.maximum(jnp.max(s, axis=0, keepdims=True), sink)
        p = jnp.exp2(s - m).astype(BF16)
        acct = _dot(vwin[:, u * BLOCK_Q:u * BLOCK_Q + NEAR], p)
        den = acct[HEAD_DIM:HEAD_DIM + 1] + jnp.exp2(sink - m)
        o_t = acct[:HEAD_DIM] / den
        o_ref[:, u * BLOCK_Q:(u + 1) * BLOCK_Q, :] = o_t.T.astype(o_ref.dtype).reshape(
            Q_PER_KV, BLOCK_Q, HEAD_DIM)


def _attn_b(q, k, v1t, bias_t, sink_t, *, layer, batch, seq, qb):
    n = q.shape[1]
    nb = seq // BLOCK_Q
    steps = nb // qb
    rows = Q_PER_KV * BLOCK_Q
    cur_map = lambda b, g, i: (g, b * steps + i, 0)
    prev_map = lambda b, g, i: (g, b * nb + jnp.maximum(i * qb - 1, 0), 0)
    next_map = lambda b, g, i: (g, b * nb + jnp.minimum(i * qb + qb, nb - 1), 0)
    t = lambda index_map: (lambda b, g, i: (index_map(b, g, i)[0], 0, index_map(b, g, i)[1]))
    k_specs = [pl.BlockSpec((None, BLOCK_Q, HEAD_DIM), prev_map),
               pl.BlockSpec((None, qb * BLOCK_Q, HEAD_DIM), cur_map),
               pl.BlockSpec((None, BLOCK_Q, HEAD_DIM), next_map)]
    v_specs = [pl.BlockSpec((None, VT_ROWS, BLOCK_Q), t(prev_map)),
               pl.BlockSpec((None, VT_ROWS, qb * BLOCK_Q), t(cur_map)),
               pl.BlockSpec((None, VT_ROWS, BLOCK_Q), t(next_map))]
    return pl.pallas_call(
        functools.partial(_attn_b_kernel, nb=nb, qb=qb),
        out_shape=jax.ShapeDtypeStruct((N_Q_HEADS, n, HEAD_DIM), ATTN_OUT),
        grid=(batch, N_KV_HEADS, steps),
        in_specs=[pl.BlockSpec((Q_PER_KV, qb * BLOCK_Q, HEAD_DIM), cur_map)] + k_specs + v_specs + [
            pl.BlockSpec((None, NEAR, rows), lambda b, g, i: (g, 0, 0)),
            pl.BlockSpec((None, None, 1, rows), lambda b, g, i: (layer, g, 0, 0)),
        ],
        out_specs=pl.BlockSpec((Q_PER_KV, qb * BLOCK_Q, HEAD_DIM), cur_map),
        compiler_params=_cparams("arbitrary", "arbitrary", "arbitrary"),
        name="attn_window",
    )(q, k, k, k, v1t, v1t, v1t, bias_t, sink_t)


def _attn_c_kernel(check_ref, q_ref, k_ref, v_ref, bias_ref, cfar_ref, kn_ref, lam_ref, gs_ref, o_ref,
                   m_ref, acct_ref, s_ref, *, layer, n_iter, unroll, tks, nblk, out_scale):
    n_blk = pl.program_id(2)
    rows = Q_PER_KV * BLOCK_Q
    blocks_per_sub = tks // BLOCK_Q
    blocks_per_iter = unroll * blocks_per_sub

    q = q_ref[...].reshape(rows, HEAD_DIM)
    lo = lax.broadcasted_iota(jnp.int32, q.shape, 1) < DIFF_DIM
    zero = jnp.zeros_like(q)
    q2 = jnp.concatenate([jnp.where(lo, q, zero), jnp.where(lo, zero, q)], axis=0)
    acct_ref[...] = jnp.zeros_like(acct_ref)
    s_ref[...] = _dot_nt(k_ref[0:tks, :], q2)
    refs = (q2, k_ref, v_ref, acct_ref, s_ref)

    c_left, c_right, c_max = cfar_ref[0], cfar_ref[1], cfar_ref[2]
    t_lo = jnp.maximum(n_blk - 1, 0) // blocks_per_iter
    t_hi = jnp.minimum(n_blk + 1, nblk - 1) // blocks_per_iter
    two = lambda x: jnp.concatenate([x, x], axis=1)
    shift = _score_bound(q2, kn_ref[:, :1]) + two(c_max)

    def add_bias(idx, s):
        tile_of = lambda key_blk: jnp.clip(key_blk - n_blk + 2, 0, NEAR // BLOCK_Q + 1)
        bias = jnp.concatenate([bias_ref[tile_of(idx * blocks_per_sub + kb)] for kb in range(blocks_per_sub)],
                               axis=0)
        return jnp.concatenate([s[:, :rows] + bias, s[:, rows:] + bias], axis=1)

    def iteration(t, last):
        is_near = (t >= t_lo) & (t <= t_hi)

        @pl.when(jnp.logical_not(is_near))
        def _():
            side = jnp.where(t < t_lo, c_left, c_right)
            _bounded_iteration(t, last, *refs, unroll=unroll, tks=tks, shift=shift - two(side))

        @pl.when(is_near)
        def _():
            _bounded_iteration(t, last, *refs, unroll=unroll, tks=tks, shift=shift, bias_fn=add_bias)

    def body(t, carry):
        iteration(t, False)
        return carry

    lax.fori_loop(0, n_iter - 1, body, 0)
    iteration(n_iter - 1, True)

    _recompute_if_flushed(check_ref[layer], acct_ref, lambda: _exact_pass(
        q2, k_ref, v_ref, m_ref, acct_ref, n_sub=n_iter * unroll, tks=tks, bias_fn=add_bias))

    acct = acct_ref[...]
    o1_t = acct[:HEAD_DIM, :rows] / acct[HEAD_DIM:HEAD_DIM + 1, :rows]
    o2_t = acct[:HEAD_DIM, rows:] / acct[HEAD_DIM:HEAD_DIM + 1, rows:]
    o_t = o1_t - lam_ref[:, :1] * o2_t
    ms = jnp.mean(o_t * o_t, axis=0, keepdims=True)
    o = (o_t * lax.rsqrt(ms + EPS)).T * gs_ref[...] * out_scale
    o_ref[...] = o.astype(o_ref.dtype).reshape(o_ref.shape)


def _attn_c(check, q, k, v1t, bias_t, cfar_t, key_norm, lam, g_subln, *, layer, batch, seq, tks, unroll):
    n = q.shape[1]
    nq = seq // BLOCK_Q
    rows = Q_PER_KV * BLOCK_Q
    lam_init = 0.8 - 0.6 * math.exp(-0.3 * layer)
    q_map = lambda b, g, i: (g, b * nq + i, 0)
    return pl.pallas_call(
        functools.partial(_attn_c_kernel, layer=layer, n_iter=seq // (tks * unroll), unroll=unroll, tks=tks,
                          nblk=nq, out_scale=1.0 - lam_init),
        out_shape=jax.ShapeDtypeStruct((N_Q_HEADS, n, HEAD_DIM), ATTN_OUT),
        grid=(batch, N_KV_HEADS, nq),
        in_specs=[
            pl.BlockSpec(memory_space=pltpu.SMEM),
            pl.BlockSpec((Q_PER_KV, BLOCK_Q, HEAD_DIM), q_map),
            pl.BlockSpec((None, seq, HEAD_DIM), lambda b, g, i: (g, b, 0)),
            pl.BlockSpec((None, VT_ROWS, seq), lambda b, g, i: (g, 0, b)),
            pl.BlockSpec((None, NEAR // BLOCK_Q + 2, BLOCK_Q, rows), lambda b, g, i: (g, 0, 0, 0)),
            pl.BlockSpec((3, None, 1, rows), lambda b, g, i: (0, g, 0, 0)),
            pl.BlockSpec((None, 1, HEAD_DIM), lambda b, g, i: (layer, 0, 0)),
            pl.BlockSpec((None, 1, HEAD_DIM), lambda b, g, i: (layer, 0, 0)),
            pl.BlockSpec((None, 1, HEAD_DIM), lambda b, g, i: (layer, 0, 0)),
        ],
        out_specs=pl.BlockSpec((Q_PER_KV, BLOCK_Q, HEAD_DIM), q_map),
        scratch_shapes=[pltpu.VMEM((1, 2 * rows), F32), pltpu.VMEM((VT_ROWS, 2 * rows), F32),
                        pltpu.VMEM((tks, 2 * rows), F32)],
        compiler_params=_cparams("arbitrary", "arbitrary", "arbitrary"),
        name="attn_diff",
    )(check, q, k, v1t, bias_t, cfar_t, key_norm, lam, g_subln)


def _t5_bucket_np(rel):
    half = N_BUCKETS // 2
    max_exact = half // 2
    ret = np.where(rel > 0, half, 0)
    n = np.abs(rel)
    ratio = np.log(np.maximum(n, 1).astype(np.float32) / np.float32(max_exact)) / np.float32(
        math.log(MAX_DISTANCE / max_exact))
    large = max_exact + (ratio * np.float32(half - max_exact)).astype(np.int32)
    large = np.minimum(large, half - 1)
    return (ret + np.where(n < max_exact, n, large)).astype(np.int32)


def _near_bias(table):
    r = np.arange(BLOCK_Q)[:, None]
    c = np.arange(NEAR)[None, :]
    bucket = _t5_bucket_np(c - BLOCK_Q - r)
    onehot = (bucket.reshape(-1, 1) == np.arange(N_BUCKETS)[None, :]).astype(np.float32)
    rows = jnp.dot(jnp.asarray(onehot), table.astype(F32), precision=lax.Precision.HIGHEST)
    return rows.reshape(BLOCK_Q, NEAR, table.shape[1]).transpose(2, 0, 1)


def _rope_tables(seq):
    rows = seq // GRID_W
    nfreq = HEAD_DIM // 4
    inv = ROPE_THETA ** (-jnp.arange(nfreq, dtype=F32) / nfreq)
    ang_r = jnp.arange(rows).astype(F32)[:, None] * inv
    ang_c = jnp.arange(GRID_W).astype(F32)[:, None] * inv
    by_row = lambda t: jnp.broadcast_to(t[:, None, :], (rows, GRID_W, nfreq)).reshape(seq, nfreq)
    by_col = lambda t: jnp.broadcast_to(t[None, :, :], (rows, GRID_W, nfreq)).reshape(seq, nfreq)
    table = lambda fn: jnp.concatenate([by_row(fn(ang_r))] * 2 + [by_col(fn(ang_c))] * 2, axis=-1)
    cos, sin = table(jnp.cos), table(jnp.sin)
    first = (np.arange(HEAD_DIM) % (HEAD_DIM // 2)) < HEAD_DIM // 4
    return cos, jnp.where(first, -sin, 0.0), jnp.where(first, 0.0, sin)


def _trunk(x, mod, p, *, batch, seq):
    t = _tiles(seq)
    tm = t["tm"]
    rope_tabs = _rope_tables(seq)
    for l in range(DEPTH):
        x = _ffn(x, mod, p["g_norm"], p["w_ff_in"], p["w_ff_out"], layer=l, which=0, seq=seq, tm=tm)
        qa, ka, va, qb, kb, vb, qc, kc, vc = _qkv(x, mod, p["g_norm"], p["wqkv"], rope_tabs, p["gh"],
                                                  layer=l, seq=seq, tm=tm)
        oa = _attn_a(p["check_a"], qa, ka, va, p["kn_a"], layer=l, batch=batch, seq=seq, tq=t["tq_a"], tks=t["tks"],
                     unroll=t["unroll"])
        ob = _attn_b(qb, kb, vb, p["bias_b"], p["sink"], layer=l, batch=batch, seq=seq, qb=t["qb"])
        oc = _attn_c(p["check_c"], qc, kc, vc, p["bias_c"], p["cfar_c"], p["kn_c"], p["lam"], p["g_subln"],
                     layer=l, batch=batch, seq=seq, tks=t["tks"], unroll=t["unroll"])
        x = _out_proj(x, mod, p["g_norm"], oa, ob, oc, p["wgate"], p["wo"], layer=l, seq=seq,
                      tm=t["tm_out"])
        x = _ffn(x, mod, p["g_norm"], p["w_ff_in"], p["w_ff_out"], layer=l, which=1, seq=seq, tm=tm)
    return x


def _prepare(w_ff_in, w_ff_out, w_in, w_o, g_qa, g_ka, g_qb, g_kb, g_qc, g_kc, sink,
             lam_q1, lam_k1, lam_q2, lam_k2, g_subln, rel_bias):
    zeros = jnp.zeros_like(g_qa)
    gh = jnp.stack([g_qa, g_ka, g_qb, g_kb, jnp.tile(g_qc, (1, 2)), jnp.tile(g_kc, (1, 2)), zeros, zeros],
                   axis=1).astype(F32)
    table_b, table_c = rel_bias[:, :N_Q_HEADS], rel_bias[:, N_Q_HEADS:]
    half = N_BUCKETS // 2
    c_left, c_right = table_c[half - 1] * LOG2E, table_c[N_BUCKETS - 1] * LOG2E
    lam_init = jnp.asarray([0.8 - 0.6 * math.exp(-0.3 * l) for l in range(DEPTH)], F32)
    lam = (jnp.exp(jnp.sum(lam_q1.astype(F32) * lam_k1.astype(F32), axis=-1))
           - jnp.exp(jnp.sum(lam_q2.astype(F32) * lam_k2.astype(F32), axis=-1)) + lam_init)

    band = np.abs(np.arange(NEAR)[None, :] - BLOCK_Q - np.arange(BLOCK_Q)[:, None]) <= WINDOW
    near_c = (_near_bias(table_c) * LOG2E).reshape(
        N_KV_HEADS, Q_PER_KV, BLOCK_Q, NEAR // BLOCK_Q, BLOCK_Q).transpose(0, 3, 4, 1, 2).reshape(
        N_KV_HEADS, NEAR // BLOCK_Q, BLOCK_Q, Q_PER_KV * BLOCK_Q)
    cfar_c = jnp.repeat(jnp.stack([c_left, c_right, jnp.max(table_c, axis=0) * LOG2E]).astype(F32),
                        BLOCK_Q, axis=1).reshape(3, N_KV_HEADS, 1, Q_PER_KV * BLOCK_Q)

    def norm_bound(g, dim):
        return BOUND_MARGIN * math.sqrt(dim) * jnp.max(jnp.abs(g.astype(F32)), axis=-1)

    def key_norm(g, dim):
        return jnp.broadcast_to(norm_bound(g, dim)[:, None, None], (DEPTH, 1, HEAD_DIM))

    span_a = 2.0 * BOUND_MARGIN * QA_SCALE * norm_bound(g_qa, HEAD_DIM) * norm_bound(g_ka, HEAD_DIM)
    span_c = (2.0 * BOUND_MARGIN * QC_SCALE * norm_bound(g_qc, DIFF_DIM) * norm_bound(g_kc, DIFF_DIM)
              + (jnp.max(table_c) - jnp.min(table_c)) * LOG2E)

    return dict(
        w_ff_in=w_ff_in.astype(BF16), w_ff_out=w_ff_out.astype(BF16),
        wqkv=w_in[:, :, :QKV_W].astype(BF16), wgate=w_in[:, :, QKV_W:].astype(BF16), wo=w_o.astype(BF16),
        gh=gh,
        bias_b=jnp.where(band, _near_bias(table_b) * LOG2E, NEG).reshape(
            N_KV_HEADS, Q_PER_KV, BLOCK_Q, NEAR).transpose(0, 3, 1, 2).reshape(
            N_KV_HEADS, NEAR, Q_PER_KV * BLOCK_Q),
        sink=jnp.repeat(sink.astype(F32) * LOG2E, BLOCK_Q, axis=1).reshape(
            DEPTH, N_KV_HEADS, 1, Q_PER_KV * BLOCK_Q),
        bias_c=jnp.concatenate([cfar_c[0][:, None] + jnp.zeros((1, 1, BLOCK_Q, 1), F32), near_c,
                                cfar_c[1][:, None] + jnp.zeros((1, 1, BLOCK_Q, 1), F32)], axis=1),
        cfar_c=cfar_c,
        kn_a=key_norm(g_ka, HEAD_DIM), kn_c=key_norm(g_kc, DIFF_DIM),
        check_a=(span_a > SAFE_SPAN).astype(jnp.int32), check_c=(span_c > SAFE_SPAN).astype(jnp.int32),
        lam=jnp.broadcast_to(lam[:, None, None], (DEPTH, 1, HEAD_DIM)),
        g_subln=g_subln.astype(F32)[:, None, :],
    )


def kernel(x_prompt, x_sample, c_prompt, c_sample, w_ada, b_ada, g_norm, w_ff_in, w_ff_out, w_in, w_o,
           g_qa, g_ka, g_qb, g_kb, g_qc, g_kc, sink, lam_q1, lam_k1, lam_q2, lam_k2, g_subln, rel_bias):
    p = _prepare(w_ff_in, w_ff_out, w_in, w_o, g_qa, g_ka, g_qb, g_kb, g_qc, g_kc, sink,
                 lam_q1, lam_k1, lam_q2, lam_k2, g_subln, rel_bias)
    p["g_norm"] = g_norm.astype(F32)
    outs = []
    n_cond = 0
    conds = [c_prompt, c_sample]
    c_all = jnp.concatenate(conds + [jnp.zeros((ADA_ROWS - sum(c.shape[0] for c in conds), D_MODEL), F32)])
    mod_all = _ada(c_all, w_ada, b_ada)
    for x, c in ((x_prompt, c_prompt), (x_sample, c_sample)):
        batch, seq, _ = x.shape
        mod = mod_all[:, n_cond:n_cond + batch].reshape(DEPTH, batch, 9, D_MODEL)
        n_cond += batch
        y = _trunk(x.reshape(batch * seq, D_MODEL), mod, p, batch=batch, seq=seq)
        outs.append(y.reshape(batch, seq, D_MODEL))
    return tuple(outs)
```
